```python
import math
import jax
import jax.numpy as jnp
from jax import lax
import numpy as np

D_MODEL = 1024
BATCH = 32
SEQ = 256
DEPTH = 2
DEC_BATCH = 2
DEC_SEQ = 4096
PAST_LEN = 512

GRID_W = 64
N_BRANCH = 4
BR_W = 512
H_A = 4
DK_A = 128
DV_A = 128
CHUNK = 64
CONV_K = 3
HY_BANDS = 16
HY_EMB = 1 + 2 * HY_BANDS
HY_FF = 64
H_C = 4
DH_C = 128
WIN_R = 8
WIN_C = 16
H_D = 4
DQK_D = 64
DV_D = 128
Q_BLOCK = 128
ROPE_BASE = 10000.0
EPS = 1e-6

N_A = 4 * BR_W + 4 * H_A
N_B = 4 * BR_W
N_C = 4 * BR_W
N_D = 4 * BR_W
N_IN = N_A + N_B + N_C + N_D

kernel_name = 'hybrid_prefix_diffusion_trunk_step'


def rmsnorm(x, g):
    xf = x.astype(jnp.float32)
    xf = xf * lax.rsqrt(jnp.mean(xf * xf, axis=-1, keepdims=True) + EPS)
    return (xf * g.astype(jnp.float32)).astype(x.dtype)


def l2norm(x):
    xf = x.astype(jnp.float32)
    return xf * lax.rsqrt(jnp.sum(xf * xf, axis=-1, keepdims=True) + EPS)


def dwconv(x, w):
    k = w.shape[0]
    return lax.conv_general_dilated(x, w[:, None, :].astype(x.dtype), window_strides=(1,),
                                    padding=[(k // 2, k // 2)],
                                    dimension_numbers=('NWC', 'WIO', 'NWC'),
                                    feature_group_count=x.shape[-1])


def heads(x, n):
    b, l, _ = x.shape
    return x.reshape(b, l, n, -1).transpose(0, 2, 1, 3)


def merge_heads(x):
    b, n, l, d = x.shape
    return x.transpose(0, 2, 1, 3).reshape(b, l, n * d)


def split_maps(t):
    b, h, l, _ = t.shape
    return t.reshape(b, h, l, 2, DQK_D).transpose(0, 1, 3, 2, 4)


def axial_rope(x):
    l, d = x.shape[-2], x.shape[-1]
    half = d // 2
    nf = half // 2
    t = jnp.arange(l)
    row = (t // GRID_W).astype(jnp.float32)
    col = (t % GRID_W).astype(jnp.float32)
    inv = ROPE_BASE ** (-jnp.arange(nf, dtype=jnp.float32) / nf)
    ang = jnp.concatenate([row[:, None] * inv, col[:, None] * inv], axis=-1)
    cos, sin = jnp.cos(ang), jnp.sin(ang)
    xf = x.astype(jnp.float32)
    x1, x2 = xf[..., :half], xf[..., half:]
    return jnp.concatenate([x1 * cos - x2 * sin, x2 * cos + x1 * sin], axis=-1).astype(x.dtype)


def gdn_chunked(q, k, v, g, beta, s0):
    b, h, l, dk = q.shape
    dv = v.shape[-1]
    n = l // CHUNK
    q = q.astype(jnp.float32).reshape(b, h, n, CHUNK, dk)
    k = k.astype(jnp.float32).reshape(b, h, n, CHUNK, dk)
    v = v.astype(jnp.float32).reshape(b, h, n, CHUNK, dv)
    beta = beta.astype(jnp.float32).reshape(b, h, n, CHUNK, 1)
    gc = jnp.cumsum(g.astype(jnp.float32).reshape(b, h, n, CHUNK), axis=-1)
    incl = jnp.tril(jnp.ones((CHUNK, CHUNK), dtype=bool))
    strict = jnp.tril(jnp.ones((CHUNK, CHUNK), dtype=bool), -1)
    rel = gc[..., :, None] - gc[..., None, :]
    decay = jnp.where(incl, jnp.exp(jnp.where(incl, rel, 0.0)), 0.0)
    kb = k * beta
    lower = jnp.where(strict, jnp.einsum('bhnid,bhnjd->bhnij', kb, k) * decay, 0.0)
    rhs = jnp.concatenate([v * beta, kb * jnp.exp(gc)[..., None]], axis=-1)
    sol = lax.linalg.triangular_solve(lower + jnp.eye(CHUNK, dtype=jnp.float32), rhs,
                                      left_side=True, lower=True, unit_diagonal=True)
    u, w = sol[..., :dv], sol[..., dv:]
    a_intra = jnp.where(incl, jnp.einsum('bhnid,bhnjd->bhnij', q, k) * decay, 0.0)
    q_dec = q * jnp.exp(gc)[..., None]
    k_dec = k * jnp.exp(gc[..., -1:] - gc)[..., None]
    g_last = jnp.exp(gc[..., -1])

    def step(s, xs):
        u_i, w_i, q_i, k_i, a_i, gl_i = xs
        v_new = u_i - jnp.einsum('bhik,bhkv->bhiv', w_i, s)
        o_i = jnp.einsum('bhik,bhkv->bhiv', q_i, s) + jnp.einsum('bhij,bhjv->bhiv', a_i, v_new)
        s = s * gl_i[..., None, None] + jnp.einsum('bhik,bhiv->bhkv', k_i, v_new)
        return s, o_i

    xs = tuple(jnp.moveaxis(t, 2, 0) for t in (u, w, q_dec, k_dec, a_intra, g_last))
    s_fin, o = lax.scan(step, s0.astype(jnp.float32), xs)
    return jnp.moveaxis(o, 0, 2).reshape(b, h, l, dv), s_fin


def gdn_branch(pa, conv_w, a_log, dt_bias, norm_g, s0):
    b, l, _ = pa.shape
    qkv = jax.nn.silu(dwconv(pa[..., :3 * BR_W], conv_w))
    q = l2norm(heads(qkv[..., :BR_W], H_A)) * (DK_A ** -0.5)
    k = l2norm(heads(qkv[..., BR_W:2 * BR_W], H_A))
    v = heads(qkv[..., 2 * BR_W:], H_A)
    z = pa[..., 3 * BR_W:4 * BR_W]
    ab = pa[..., 4 * BR_W:].astype(jnp.float32).reshape(b, l, 2, 2, H_A).transpose(2, 3, 0, 4, 1)
    g_all = -jnp.exp(a_log.astype(jnp.float32))[:, None, :, None] * jax.nn.softplus(
        ab[:, 0] + dt_bias.astype(jnp.float32)[:, None, :, None])
    beta_all = jax.nn.sigmoid(ab[:, 1])
    o_f, s_f = gdn_chunked(q, k, v, g_all[0], beta_all[0], s0[:, 0])
    rev = lambda t: jnp.flip(t, axis=2)
    o_b, s_b = gdn_chunked(rev(q), rev(k), rev(v), rev(g_all[1]), rev(beta_all[1]), s0[:, 1])
    o = rmsnorm(o_f + rev(o_b), norm_g)
    y = merge_heads(o) * jax.nn.silu(z.astype(jnp.float32))
    return y.astype(pa.dtype), jnp.stack([s_f, s_b], axis=1)


def hyena_filters(l, w1, b1, w2, b2, w3, b3, decay):
    pos = jnp.arange(l, dtype=jnp.float32)
    t = pos / l
    ang = (2.0 * math.pi) * t[:, None] * jnp.arange(1, HY_BANDS + 1, dtype=jnp.float32)
    feat = jnp.concatenate([t[:, None], jnp.cos(ang), jnp.sin(ang)], axis=-1)
    hid = jnp.sin(feat @ w1 + b1)
    hid = jnp.sin(hid @ w2 + b2)
    filt = (hid @ w3 + b3).astype(jnp.float32)
    dist = jnp.abs(pos - l // 2) / l
    return filt * jnp.exp(-dist[:, None] * jnp.abs(decay.astype(jnp.float32)))


def fft_conv_centred(u, h, skip):
    l = u.shape[1]
    uf = jnp.fft.rfft(u.astype(jnp.float32), n=2 * l, axis=1)
    hf = jnp.fft.rfft(h, n=2 * l, axis=0)
    y = jnp.fft.irfft(uf * hf[None], n=2 * l, axis=1)[:, l // 2:l // 2 + l]
    return y + u.astype(jnp.float32) * skip.astype(jnp.float32)


def hyena_branch(pb, conv_w, w1, b1, w2, b2, w3, b3, decay, skip):
    l = pb.shape[1]
    proj = dwconv(pb[..., :3 * BR_W], conv_w)
    v, x1, x2 = proj[..., :BR_W], proj[..., BR_W:2 * BR_W], proj[..., 2 * BR_W:]
    filt = hyena_filters(l, w1, b1, w2, b2, w3, b3, decay)
    z = x1.astype(jnp.float32) * fft_conv_centred(v, filt[:, :BR_W], skip[0])
    z = x2.astype(jnp.float32) * fft_conv_centred(z, filt[:, BR_W:], skip[1])
    return (z * jax.nn.silu(pb[..., 3 * BR_W:].astype(jnp.float32))).astype(pb.dtype)


def blocked_attn(q, k, v):
    b, h, lq, d = q.shape
    qb = jnp.moveaxis(q.reshape(b, h, lq // Q_BLOCK, Q_BLOCK, d), 2, 0)

    def block(q_i):
        s = jnp.einsum('bhqd,bhkd->bhqk', q_i, k).astype(jnp.float32) * (d ** -0.5)
        pr = jax.nn.softmax(s, axis=-1).astype(v.dtype)
        return jnp.einsum('bhqk,bhkd->bhqd', pr, v)

    o = lax.map(block, qb)
    return jnp.moveaxis(o, 0, 2).reshape(b, h, lq, v.shape[-1])


def nat_latent(q, k, v, ck, cv, rpb):
    b, h, l, d = q.shape
    rows = l // GRID_W
    kr = min(WIN_R, rows)
    cols = jnp.arange(GRID_W)
    col_idx = jnp.clip(cols - WIN_C // 2, 0, GRID_W - WIN_C)[:, None] + jnp.arange(WIN_C)[None, :]
    dc = col_idx - cols[:, None] + (WIN_C - 1)
    qg = q.reshape(b, h, rows, GRID_W, d)
    kg = k.reshape(b, h, rows, GRID_W, d)
    vg = v.reshape(b, h, rows, GRID_W, d)
    scale = d ** -0.5

    def row_block(r):
        rs = jnp.clip(r - kr // 2, 0, rows - kr)
        q_r = lax.dynamic_index_in_dim(qg, r, axis=2, keepdims=False)
        k_w = lax.dynamic_slice_in_dim(kg, rs, kr, axis=2)[:, :, :, col_idx]
        v_w = lax.dynamic_slice_in_dim(vg, rs, kr, axis=2)[:, :, :, col_idx]
        dr = rs + jnp.arange(kr) - r + (WIN_R - 1)
        bias = rpb[:, dr[None, :, None], dc[:, None, :]].astype(jnp.float32)
        s_lat = jnp.einsum('bhqd,bhjqcd->bhqjc', q_r, k_w).astype(jnp.float32) * scale + bias[None]
        s_ctx = jnp.einsum('bhqd,bhsd->bhqs', q_r, ck).astype(jnp.float32) * scale
        pr = jax.nn.softmax(jnp.concatenate([s_lat.reshape(b, h, GRID_W, kr * WIN_C), s_ctx], axis=-1),
                            axis=-1).astype(v.dtype)
        p_lat = pr[..., :kr * WIN_C].reshape(b, h, GRID_W, kr, WIN_C)
        return (jnp.einsum('bhqjc,bhjqcd->bhqd', p_lat, v_w)
                + jnp.einsum('bhqs,bhsd->bhqd', pr[..., kr * WIN_C:], cv))

    o = lax.map(row_block, jnp.arange(rows))
    return jnp.transpose(o, (1, 2, 0, 3, 4)).reshape(b, h, l, d)


def diff_attn(q, k, v, lam):
    b, h, _, lq, dq = q.shape
    qb = jnp.moveaxis(q.reshape(b, h, 2, lq // Q_BLOCK, Q_BLOCK, dq), 3, 0)

    def block(q_i):
        s = jnp.einsum('bhmqd,bhmkd->bhmqk', q_i, k).astype(jnp.float32) * (dq ** -0.5)
        pr = jax.nn.softmax(s, axis=-1)
        a = (pr[:, :, 0] - lam * pr[:, :, 1]).astype(v.dtype)
        return jnp.einsum('bhqk,bhkd->bhqd', a, v)

    o = lax.map(block, qb)
    return jnp.moveaxis(o, 0, 2).reshape(b, h, lq, v.shape[-1])


def trunk_layer(x, mod, p, layer, ctx):
    b, l, _ = x.shape
    latent = ctx is not None
    shift, scale, gate = jnp.split(mod, 3, axis=-1)
    h = rmsnorm(x, p['g_pre']) * (1.0 + scale) + shift
    proj = h @ p['w_in']
    pa = proj[..., :N_A]
    pb = proj[..., N_A:N_A + N_B]
    pc = proj[..., N_A + N_B:N_A + N_B + N_C]
    pd = proj[..., N_A + N_B + N_C:]

    s0 = ctx['gdn'] if latent else jnp.zeros((b, 2, H_A, DK_A, DV_A), jnp.float32)
    ya, s_fin = gdn_branch(pa, p['gdn_conv'], p['gdn_a_log'], p['gdn_dt_bias'], p['gdn_norm'], s0)

    yb = hyena_branch(pb, p['hy_conv'], p['hy_w1'], p['hy_b1'], p['hy_w2'], p['hy_b2'],
                      p['hy_w3'], p['hy_b3'], p['hy_decay'], p['hy_skip'])

    qc = heads(pc[..., :BR_W], H_C)
    kc = heads(pc[..., BR_W:2 * BR_W], H_C)
    vc = heads(pc[..., 2 * BR_W:3 * BR_W], H_C)
    if latent:
        oc = nat_latent(qc, kc, vc, ctx['nat_k'], ctx['nat_v'], p['nat_rpb'])
    else:
        oc = blocked_attn(qc, kc, vc)
    yc = merge_heads(oc) * jax.nn.silu(pc[..., 3 * BR_W:])

    qd = pd[..., :BR_W].reshape(b, l, H_D, 2, DQK_D).transpose(0, 2, 3, 1, 4)
    kd_flat = heads(pd[..., BR_W:2 * BR_W], H_D)
    vd = heads(pd[..., 2 * BR_W:3 * BR_W], H_D)
    lam_init = 0.8 - 0.6 * math.exp(-0.3 * layer)
    lam_p = p['diff_lam'].astype(jnp.float32)
    lam = jnp.exp(jnp.sum(lam_p[0] * lam_p[1])) - jnp.exp(jnp.sum(lam_p[2] * lam_p[3])) + lam_init
    if latent:
        keys = jnp.concatenate([axial_rope(split_maps(kd_flat)), split_maps(ctx['diff_k'])], axis=3)
        vals = jnp.concatenate([vd, ctx['diff_v']], axis=2)
        od = diff_attn(axial_rope(qd), keys, vals, lam)
    else:
        od = diff_attn(qd, split_maps(kd_flat), vd, lam)
    yd = merge_heads(rmsnorm(od, p['diff_norm']) * (1.0 - lam_init)) * jax.nn.silu(pd[..., 3 * BR_W:])

    ys = jnp.stack([t.astype(x.dtype) for t in (ya, yb, yc, yd)], axis=2)
    br = jnp.einsum('blkw,kwd->blkd', ys, p['w_branch'])
    gates = jax.nn.sigmoid((h @ p['w_merge'] + p['b_merge']).astype(jnp.float32)).reshape(b, l, N_BRANCH, D_MODEL)
    y = jnp.sum(gates * br.astype(jnp.float32), axis=2).astype(x.dtype) @ p['w_out']
    x = x + gate * rmsnorm(y, p['g_post'])
    if latent:
        return x, None
    return x, (s_fin, jnp.stack([kc, vc], axis=1), jnp.stack([kd_flat, vd], axis=1))


def setup_inputs(seed: int = 0) -> dict:
    key = jax.random.key(seed)
    ks = jax.random.split(key, 32)
    f32 = jnp.float32
    nrm = lambda i, shape, s: jax.random.normal(ks[i], shape, f32) * s
    dt = jnp.exp(jax.random.uniform(ks[14], (DEPTH, 2, H_A), f32, math.log(1e-3), math.log(1e-1)))
    return {
        'x_prompt': nrm(0, (BATCH, SEQ, D_MODEL), 1.0),
        'x_sample': nrm(1, (DEC_BATCH, DEC_SEQ, D_MODEL), 1.0),
        'state_gdn': nrm(2, (DEC_BATCH, DEPTH, 2, H_A, DK_A, DV_A), 0.5),
        'cache_nat_kv': nrm(3, (DEC_BATCH, DEPTH, 2, H_C, PAST_LEN, DH_C), 1.0),
        'cache_diff_kv': nrm(4, (DEC_BATCH, DEPTH, 2, H_D, PAST_LEN, 2 * DQK_D), 1.0),
        'c': nrm(5, (DEC_BATCH, D_MODEL), 1.0),
        'c_ctx': nrm(6, (D_MODEL,), 1.0),
        'w_mod': nrm(7, (DEPTH, D_MODEL, 3 * D_MODEL), D_MODEL ** -0.5),
        'b_mod': nrm(8, (DEPTH, 3 * D_MODEL), 0.02),
        'g_pre': 1.0 + nrm(9, (DEPTH, D_MODEL), 0.02),
        'g_post': 1.0 + nrm(10, (DEPTH, D_MODEL), 0.02),
        'w_in': nrm(11, (DEPTH, D_MODEL, N_IN), D_MODEL ** -0.5),
        'gdn_conv': nrm(12, (DEPTH, CONV_K, 3 * BR_W), CONV_K ** -0.5),
        'gdn_a_log': jnp.log(jax.random.uniform(ks[13], (DEPTH, 2, H_A), f32, 1.0, 16.0)),
        'gdn_dt_bias': dt + jnp.log(-jnp.expm1(-dt)),
        'gdn_norm': 1.0 + nrm(15, (DEPTH, DV_A), 0.02),
        'hy_conv': nrm(16, (DEPTH, CONV_K, 3 * BR_W), CONV_K ** -0.5),
        'hy_w1': nrm(17, (DEPTH, HY_EMB, HY_FF), HY_EMB ** -0.5),
        'hy_b1': nrm(18, (DEPTH, HY_FF), 0.1),
        'hy_w2': nrm(19, (DEPTH, HY_FF, HY_FF), HY_FF ** -0.5),
        'hy_b2': nrm(20, (DEPTH, HY_FF), 0.1),
        'hy_w3': nrm(21, (DEPTH, HY_FF, 2 * BR_W), 0.1 * HY_FF ** -0.5),
        'hy_b3': nrm(22, (DEPTH, 2 * BR_W), 0.01),
        'hy_decay': jax.random.uniform(ks[23], (DEPTH, 2 * BR_W), f32, 3.0, 30.0),
        'hy_skip': nrm(24, (DEPTH, 2, BR_W), 0.5),
        'nat_rpb': nrm(25, (DEPTH, H_C, 2 * WIN_R - 1, 2 * WIN_C - 1), 0.1),
        'diff_lam': nrm(26, (DEPTH, 4, DQK_D), 0.1),
        'diff_norm': 1.0 + nrm(27, (DEPTH, DV_D), 0.02),
        'w_branch': nrm(28, (DEPTH, N_BRANCH, BR_W, D_MODEL), BR_W ** -0.5),
        'w_merge': nrm(29, (DEPTH, D_MODEL, N_BRANCH * D_MODEL), D_MODEL ** -0.5),
        'b_merge': nrm(30, (DEPTH, N_BRANCH * D_MODEL), 0.02),
        'w_out': nrm(31, (DEPTH, D_MODEL, D_MODEL), D_MODEL ** -0.5),
    }


def reference(x_prompt, x_sample, state_gdn, cache_nat_kv, cache_diff_kv, c, c_ctx,
              w_mod, b_mod, g_pre, g_post, w_in, gdn_conv, gdn_a_log, gdn_dt_bias, gdn_norm,
              hy_conv, hy_w1, hy_b1, hy_w2, hy_b2, hy_w3, hy_b3, hy_decay, hy_skip,
              nat_rpb, diff_lam, diff_norm, w_branch, w_merge, b_merge, w_out):
    stacked = {
        'w_mod': w_mod, 'b_mod': b_mod, 'g_pre': g_pre, 'g_post': g_post, 'w_in': w_in,
        'gdn_conv': gdn_conv, 'gdn_a_log': gdn_a_log, 'gdn_dt_bias': gdn_dt_bias, 'gdn_norm': gdn_norm,
        'hy_conv': hy_conv, 'hy_w1': hy_w1, 'hy_b1': hy_b1, 'hy_w2': hy_w2, 'hy_b2': hy_b2,
        'hy_w3': hy_w3, 'hy_b3': hy_b3, 'hy_decay': hy_decay, 'hy_skip': hy_skip,
        'nat_rpb': nat_rpb, 'diff_lam': diff_lam, 'diff_norm': diff_norm,
        'w_branch': w_branch, 'w_merge': w_merge, 'b_merge': b_merge, 'w_out': w_out,
    }

    y_prompt = x_prompt
    gdn_states, nat_kvs, diff_kvs = [], [], []
    for layer in range(DEPTH):
        p = {name: arr[layer] for name, arr in stacked.items()}
        mod = jax.nn.silu(c_ctx) @ p['w_mod'] + p['b_mod']
        y_prompt, (s_l, nat_l, diff_l) = trunk_layer(y_prompt, mod, p, layer, None)
        gdn_states.append(s_l)
        nat_kvs.append(nat_l)
        diff_kvs.append(diff_l)
    new_state_gdn = jnp.stack(gdn_states, axis=1)
    new_cache_nat_kv = jnp.stack(nat_kvs, axis=1)
    new_cache_diff_kv = jnp.stack(diff_kvs, axis=1)

    y_sample = x_sample
    for layer in range(DEPTH):
        p = {name: arr[layer] for name, arr in stacked.items()}
        mod = (jax.nn.silu(c) @ p['w_mod'] + p['b_mod'])[:, None, :]
        ctx = {
            'gdn': state_gdn[:, layer],
            'nat_k': cache_nat_kv[:, layer, 0], 'nat_v': cache_nat_kv[:, layer, 1],
            'diff_k': cache_diff_kv[:, layer, 0], 'diff_v': cache_diff_kv[:, layer, 1],
        }
        y_sample, _ = trunk_layer(y_sample, mod, p, layer, ctx)

    return (y_prompt, y_sample, new_state_gdn, new_cache_nat_kv, new_cache_diff_kv)
```

```python
import functools
import math

import jax
import jax.numpy as jnp
from jax import lax
from jax.experimental import pallas as pl
from jax.experimental.pallas import tpu as pltpu

F32 = jnp.float32
BF16 = jnp.bfloat16

D_MODEL = 1024
DEPTH = 2
GRID_W = 64
N_BRANCH = 4
BR_W = 512
N_HEAD = 4
D_HEAD = 128
CHUNK = 64
HY_BANDS = 16
WIN_R = 8
WIN_C = 16
DQK_D = 64
ROPE_BASE = 10000.0
EPS = 1e-6
N_MAIN = 4 * 4 * BR_W
AB_PAD = 128
NEG_INF = -1e30

VMEM_LIMIT = 48 * 1024 * 1024


def _cparams(*sem):
    return pltpu.CompilerParams(dimension_semantics=sem, vmem_limit_bytes=VMEM_LIMIT)


def _silu(x):
    return x * (1.0 / (1.0 + jnp.exp(-x)))


def _sigmoid(x):
    return 1.0 / (1.0 + jnp.exp(-x))


def _rms(x, g):
    return x * lax.rsqrt(jnp.mean(x * x, axis=-1, keepdims=True) + EPS) * g


def _dot(a, b):
    return jnp.dot(a.astype(BF16), b.astype(BF16), preferred_element_type=F32)


def _dot_nt(a, b):
    return lax.dot_general(a.astype(BF16), b.astype(BF16), (((1,), (1,)), ((), ())),
                           preferred_element_type=F32)


def _dot_tn(a, b):
    return lax.dot_general(a.astype(BF16), b.astype(BF16), (((0,), (0,)), ((), ())),
                           preferred_element_type=F32)


def _prenorm(x, g_pre, mod_ref):
    return _rms(x, g_pre) * (1.0 + mod_ref[0, 1:2, :]) + mod_ref[0, 0:1, :]


def _inproj_kernel(x_ref, mod_ref, gpre_ref, w_ref, wab_ref, proj_ref, ab_ref, h_scr):
    @pl.when(pl.program_id(1) == 0)
    def _():
        h = _prenorm(x_ref[...], gpre_ref[...], mod_ref).astype(BF16)
        h_scr[...] = h
        ab_ref[...] = jnp.dot(h, wab_ref[...], preferred_element_type=F32)

    proj_ref[...] = jnp.dot(h_scr[...], w_ref[...], preferred_element_type=F32)


def _inproj(x2, mod, g_pre, w_main, w_ab, rows_per_mod, tm=1024, tn=1024):
    m = x2.shape[0]
    return pl.pallas_call(
        _inproj_kernel,
        grid=(m // tm, N_MAIN // tn),
        in_specs=[
            pl.BlockSpec((tm, D_MODEL), lambda i, j: (i, 0)),
            pl.BlockSpec((1, 3, D_MODEL), lambda i, j: ((i * tm) // rows_per_mod, 0, 0)),
            pl.BlockSpec((1, D_MODEL), lambda i, j: (0, 0)),
            pl.BlockSpec((D_MODEL, tn), lambda i, j: (0, j)),
            pl.BlockSpec((D_MODEL, AB_PAD), lambda i, j: (0, 0)),
        ],
        out_specs=[
            pl.BlockSpec((tm, tn), lambda i, j: (i, j)),
            pl.BlockSpec((tm, AB_PAD), lambda i, j: (i, 0)),
        ],
        out_shape=[jax.ShapeDtypeStruct((m, N_MAIN), F32),
                   jax.ShapeDtypeStruct((m, AB_PAD), F32)],
        scratch_shapes=[pltpu.VMEM((tm, D_MODEL), BF16)],
        compiler_params=_cparams("parallel", "arbitrary"),
        name="inproj",
    )(x2, mod, g_pre, w_main, w_ab)


def _merge_kernel(x_ref, mod_ref, gpre_ref, gpost_ref, ya_ref, yb_ref, yc_ref, yd_ref,
                  wbr_ref, wmg_ref, bmg_ref, wout_ref, o_ref):
    x = x_ref[...]
    h = _prenorm(x, gpre_ref[...], mod_ref).astype(BF16)
    acc = None
    for k, y_ref in enumerate((ya_ref, yb_ref, yc_ref, yd_ref)):
        cols = slice(k * D_MODEL, (k + 1) * D_MODEL)
        gate = _sigmoid(jnp.dot(h, wmg_ref[:, cols], preferred_element_type=F32) + bmg_ref[:, cols])
        br = jnp.dot(y_ref[...], wbr_ref[k], preferred_element_type=F32)
        acc = gate * br if acc is None else acc + gate * br
    y = jnp.dot(acc.astype(BF16), wout_ref[...], preferred_element_type=F32)
    o_ref[...] = x + mod_ref[0, 2:3, :] * _rms(y, gpost_ref[...])


def _merge(x2, mod, g_pre, g_post, ys, w_branch, w_merge, b_merge, w_out, rows_per_mod, tm=256):
    m = x2.shape[0]
    row = lambda i: (i, 0)
    fixed2 = lambda i: (0, 0)
    return pl.pallas_call(
        _merge_kernel,
        grid=(m // tm,),
        in_specs=[
            pl.BlockSpec((tm, D_MODEL), row),
            pl.BlockSpec((1, 3, D_MODEL), lambda i: ((i * tm) // rows_per_mod, 0, 0)),
            pl.BlockSpec((1, D_MODEL), fixed2),
            pl.BlockSpec((1, D_MODEL), fixed2),
            pl.BlockSpec((tm, BR_W), row),
            pl.BlockSpec((tm, BR_W), row),
            pl.BlockSpec((tm, BR_W), row),
            pl.BlockSpec((tm, BR_W), row),
            pl.BlockSpec((N_BRANCH, BR_W, D_MODEL), lambda i: (0, 0, 0)),
            pl.BlockSpec((D_MODEL, N_BRANCH * D_MODEL), fixed2),
            pl.BlockSpec((1, N_BRANCH * D_MODEL), fixed2),
            pl.BlockSpec((D_MODEL, D_MODEL), fixed2),
        ],
        out_specs=pl.BlockSpec((tm, D_MODEL), row),
        out_shape=jax.ShapeDtypeStruct((m, D_MODEL), F32),
        compiler_params=_cparams("parallel"),
        name="merge",
    )(x2, mod, g_pre, g_post, *ys, w_branch, w_merge, b_merge, w_out)


def _softmax_rows(s):
    p = jnp.exp(s - jnp.max(s, axis=-1, keepdims=True))
    return p, jnp.sum(p, axis=-1, keepdims=True)


def _ctx_nat_kernel(q_ref, k_ref, v_ref, g_ref, y_ref, kv_ref):
    scale = D_HEAD ** -0.5
    for h in range(N_HEAD):
        sl = slice(h * D_HEAD, (h + 1) * D_HEAD)
        k = k_ref[0, :, sl]
        v = v_ref[0, :, sl]
        p, l = _softmax_rows(_dot_nt(q_ref[0, :, sl], k) * scale)
        o = _dot(p, v) / l
        y_ref[0, :, sl] = (o * _silu(g_ref[0, :, sl])).astype(y_ref.dtype)
        kv_ref[0, 0, h] = k
        kv_ref[0, 1, h] = v


def _map_masks():
    lane = lax.broadcasted_iota(jnp.int32, (1, D_HEAD), 1)
    first = (lane < DQK_D).astype(F32)
    return first, 1.0 - first


def _ctx_diff_kernel(lam_ref, q_ref, k_ref, v_ref, g_ref, gn_ref, y_ref, kv_ref, *, out_scale):
    scale = DQK_D ** -0.5
    m1, m2 = _map_masks()
    lam = lam_ref[...]
    for h in range(N_HEAD):
        sl = slice(h * D_HEAD, (h + 1) * D_HEAD)
        q = q_ref[0, :, sl]
        k = k_ref[0, :, sl]
        v = v_ref[0, :, sl]
        p1, l1 = _softmax_rows(_dot_nt(q * m1, k) * scale)
        p2, l2 = _softmax_rows(_dot_nt(q * m2, k) * scale)
        a = p1 / l1 - lam * (p2 / l2)
        o = _rms(_dot(a, v), gn_ref[...]) * out_scale
        y_ref[0, :, sl] = (o * _silu(g_ref[0, :, sl])).astype(y_ref.dtype)
        kv_ref[0, 0, h] = k
        kv_ref[0, 1, h] = v


def _ctx_attention(proj3, lam, diff_norm, lam_init):
    b, l, _ = proj3.shape
    blk = lambda c: pl.BlockSpec((1, l, BR_W), lambda i, c=c: (i, 0, c))
    y_spec = pl.BlockSpec((1, l, BR_W), lambda i: (i, 0, 0))
    kv_spec = pl.BlockSpec((1, 2, N_HEAD, l, D_HEAD), lambda i: (i, 0, 0, 0, 0))
    out_shape = [jax.ShapeDtypeStruct((b, l, BR_W), BF16),
                 jax.ShapeDtypeStruct((b, 2, N_HEAD, l, D_HEAD), F32)]
    yc, nat_kv = pl.pallas_call(
        _ctx_nat_kernel,
        grid=(b,),
        in_specs=[blk(8), blk(9), blk(10), blk(11)],
        out_specs=[y_spec, kv_spec],
        out_shape=out_shape,
        compiler_params=_cparams("parallel"),
        name="ctx_nat",
    )(proj3, proj3, proj3, proj3)
    yd, diff_kv = pl.pallas_call(
        functools.partial(_ctx_diff_kernel, out_scale=1.0 - lam_init),
        grid=(b,),
        in_specs=[pl.BlockSpec((1, 1), lambda i: (0, 0)),
                  blk(12), blk(13), blk(14), blk(15),
                  pl.BlockSpec((1, D_HEAD), lambda i: (0, 0))],
        out_specs=[y_spec, kv_spec],
        out_shape=out_shape,
        compiler_params=_cparams("parallel"),
        name="ctx_diff",
    )(lam, proj3, proj3, proj3, proj3, diff_norm)
    return yc, yd, nat_kv, diff_kv


def _nat_bias_table(rpb):
    off = jnp.arange(WIN_R)
    dr = jnp.arange(WIN_R)[None, :] - off[:, None] + (WIN_R - 1)
    cols = jnp.arange(GRID_W)
    start = jnp.clip(cols - WIN_C // 2, 0, GRID_W - WIN_C)
    rel = cols[None, :] - cols[:, None]
    inside = (cols[None, :] >= start[:, None]) & (cols[None, :] < start[:, None] + WIN_C)
    dc = jnp.clip(rel + (WIN_C - 1), 0, 2 * WIN_C - 2)
    tab = rpb.astype(F32)[:, dr[:, None, :, None], dc[None, :, None, :]]
    tab = jnp.where(inside[None, None, :, None, :], tab, NEG_INF)
    return tab.reshape(rpb.shape[0], WIN_R, GRID_W, WIN_R * GRID_W)


def _lat_nat_kernel(q_ref, k_ref, v_ref, g_ref, ckv_ref, bias_ref, y_ref, kb_scr, vb_scr):
    scale = D_HEAD ** -0.5
    rows = q_ref.shape[1] // GRID_W
    win = WIN_R * GRID_W
    kb_scr[...] = k_ref[0].astype(BF16)
    vb_scr[...] = v_ref[0].astype(BF16)
    ck = ckv_ref[0, 0, 0, 0].astype(BF16)
    cv = ckv_ref[0, 0, 1, 0].astype(BF16)

    def row(r, carry):
        rs = jnp.clip(r - WIN_R // 2, 0, rows - WIN_R)
        q0 = pl.multiple_of(r * GRID_W, GRID_W)
        k0 = pl.multiple_of(rs * GRID_W, GRID_W)
        q = q_ref[0, pl.ds(q0, GRID_W), :].astype(BF16)
        s_lat = _dot_nt(q, kb_scr[pl.ds(k0, win), :]) * scale + bias_ref[0, r - rs]
        s_ctx = _dot_nt(q, ck) * scale
        m = jnp.maximum(jnp.max(s_lat, axis=-1, keepdims=True), jnp.max(s_ctx, axis=-1, keepdims=True))
        p_lat = jnp.exp(s_lat - m)
        p_ctx = jnp.exp(s_ctx - m)
        l = jnp.sum(p_lat, axis=-1, keepdims=True) + jnp.sum(p_ctx, axis=-1, keepdims=True)
        o = (_dot(p_lat, vb_scr[pl.ds(k0, win), :]) + _dot(p_ctx, cv)) / l
        g = g_ref[0, pl.ds(q0, GRID_W), :]
        y_ref[0, pl.ds(q0, GRID_W), :] = (o * _silu(g)).astype(y_ref.dtype)
        return carry

    lax.fori_loop(0, rows, row, 0)


def _lat_nat(proj3, cache_nat_kv, layer, bias_tab):
    b, l, _ = proj3.shape
    past = cache_nat_kv.shape[4]
    blk = lambda c: pl.BlockSpec((1, l, D_HEAD), lambda i, h, c=c: (i, 0, c + h))
    return pl.pallas_call(
        _lat_nat_kernel,
        grid=(b, N_HEAD),
        in_specs=[blk(32), blk(36), blk(40), blk(44),
                  pl.BlockSpec((1, 1, 2, 1, past, D_HEAD), lambda i, h: (i, layer, 0, h, 0, 0)),
                  pl.BlockSpec((1, WIN_R, GRID_W, WIN_R * GRID_W), lambda i, h: (h, 0, 0, 0))],
        out_specs=pl.BlockSpec((1, l, D_HEAD), lambda i, h: (i, 0, h)),
        out_shape=jax.ShapeDtypeStruct((b, l, BR_W), BF16),
        scratch_shapes=[pltpu.VMEM((l, D_HEAD), BF16), pltpu.VMEM((l, D_HEAD), BF16)],
        compiler_params=_cparams("parallel", "parallel"),
        name="lat_nat",
    )(proj3, proj3, proj3, proj3, cache_nat_kv, bias_tab)


def _rope_tables(l):
    half = DQK_D // 2
    nf = half // 2
    t = jnp.arange(l)
    row = (t // GRID_W).astype(F32)
    col = (t % GRID_W).astype(F32)
    inv = ROPE_BASE ** (-jnp.arange(nf, dtype=F32) / nf)
    ang = jnp.concatenate([row[:, None] * inv, col[:, None] * inv], axis=-1)
    cos, sin = jnp.cos(ang), jnp.sin(ang)
    zero = jnp.zeros_like(sin)
    tile2 = lambda a, b: jnp.concatenate([a, b, a, b], axis=-1)
    return tile2(cos, cos), tile2(-sin, zero), tile2(zero, sin)


def _rope(x, cos, sin_a, sin_b):
    return x * cos + pltpu.roll(x, 96, 1) * sin_a + pltpu.roll(x, 32, 1) * sin_b


def _lat_diff_kernel(lam_ref, q_ref, k_ref, v_ref, g_ref, ckv_ref, gn_ref,
                     cq_ref, saq_ref, sbq_ref, ck_ref, sak_ref, sbk_ref,
                     y_ref, ks_scr, vs_scr, *, out_scale, prep_rows):
    scale = DQK_D ** -0.5
    l = k_ref.shape[1]

    @pl.when(pl.program_id(2) == 0)
    def _():
        def prep(i, carry):
            rows = pl.ds(pl.multiple_of(i * prep_rows, prep_rows), prep_rows)
            kr = _rope(k_ref[0, rows, :], ck_ref[rows, :], sak_ref[rows, :], sbk_ref[rows, :])
            ks_scr[rows, :] = kr.astype(BF16)
            vs_scr[rows, :] = v_ref[0, rows, :].astype(BF16)
            return carry

        lax.fori_loop(0, l // prep_rows, prep, 0)
        ks_scr[l:, :] = ckv_ref[0, 0, 0, 0].astype(BF16)
        vs_scr[l:, :] = ckv_ref[0, 0, 1, 0].astype(BF16)

    m1, m2 = _map_masks()
    q = _rope(q_ref[0], cq_ref[...], saq_ref[...], sbq_ref[...])
    ks = ks_scr[...]
    p1, l1 = _softmax_rows(_dot_nt(q * m1, ks) * scale)
    p2, l2 = _softmax_rows(_dot_nt(q * m2, ks) * scale)
    a = p1 * (1.0 / l1) - p2 * (lam_ref[...] / l2)
    o = _rms(_dot(a, vs_scr[...]), gn_ref[...]) * out_scale
    y_ref[0] = (o * _silu(g_ref[0])).astype(y_ref.dtype)


def _lat_diff(proj3, cache_diff_kv, layer, lam, diff_norm, lam_init, rope_tabs, tq=128):
    b, l, _ = proj3.shape
    past = cache_diff_kv.shape[4]
    qblk = lambda c: pl.BlockSpec((1, tq, D_HEAD), lambda i, h, j, c=c: (i, j, c + h))
    full = lambda c: pl.BlockSpec((1, l, D_HEAD), lambda i, h, j, c=c: (i, 0, c + h))
    tq_tab = pl.BlockSpec((tq, D_HEAD), lambda i, h, j: (j, 0))
    full_tab = pl.BlockSpec((l, D_HEAD), lambda i, h, j: (0, 0))
    return pl.pallas_call(
        functools.partial(_lat_diff_kernel, out_scale=1.0 - lam_init, prep_rows=512),
        grid=(b, N_HEAD, l // tq),
        in_specs=[pl.BlockSpec((1, 1), lambda i, h, j: (0, 0)),
                  qblk(48), full(52), full(56), qblk(60),
                  pl.BlockSpec((1, 1, 2, 1, past, D_HEAD), lambda i, h, j: (i, layer, 0, h, 0, 0)),
                  pl.BlockSpec((1, D_HEAD), lambda i, h, j: (0, 0)),
                  tq_tab, tq_tab, tq_tab, full_tab, full_tab, full_tab],
        out_specs=pl.BlockSpec((1, tq, D_HEAD), lambda i, h, j: (i, j, h)),
        out_shape=jax.ShapeDtypeStruct((b, l, BR_W), BF16),
        scratch_shapes=[pltpu.VMEM((l + past, D_HEAD), BF16), pltpu.VMEM((l + past, D_HEAD), BF16)],
        compiler_params=_cparams("parallel", "parallel", "arbitrary"),
        name="lat_diff",
    )(lam, proj3, proj3, proj3, proj3, cache_diff_kv, diff_norm, *rope_tabs, *rope_tabs)


def _dwconv3(x, w_ref):
    l = x.shape[0]
    row = lax.broadcasted_iota(jnp.int32, x.shape, 0)
    prev = jnp.where(row == 0, 0.0, pltpu.roll(x, 1, 0))
    nxt = jnp.where(row == l - 1, 0.0, pltpu.roll(x, l - 1, 0))
    return prev * w_ref[0:1, :] + x * w_ref[1:2, :] + nxt * w_ref[2:3, :]


def _hy_pre_kernel(x_ref, w_ref, o_ref, ob_ref):
    y = _dwconv3(x_ref[0], w_ref)
    o_ref[0] = y
    ob_ref[0] = y.astype(BF16)


def _hy_pre(proj3, conv_w):
    b, l, _ = proj3.shape
    n = 3 * BR_W // D_HEAD
    spec = pl.BlockSpec((1, l, D_HEAD), lambda i, j: (i, 0, j))
    return pl.pallas_call(
        _hy_pre_kernel,
        grid=(b, n),
        in_specs=[pl.BlockSpec((1, l, D_HEAD), lambda i, j: (i, 0, 4 * BR_W // D_HEAD + j)),
                  pl.BlockSpec((3, D_HEAD), lambda i, j: (0, j))],
        out_specs=[spec, spec],
        out_shape=[jax.ShapeDtypeStruct((b, l, 3 * BR_W), F32),
                   jax.ShapeDtypeStruct((b, l, 3 * BR_W), BF16)],
        compiler_params=_cparams("parallel", "parallel"),
        name="hy_pre",
    )(proj3, conv_w)


def _dot_hi(a, b):
    return jnp.dot(a, b, preferred_element_type=F32, precision=lax.Precision.HIGHEST)


def _hy_filter_kernel(feat_ref, dist_ref, w1_ref, b1_ref, w2_ref, b2_ref, w3_ref, b3_ref, dec_ref, o_ref):
    hid = jnp.sin(_dot_hi(feat_ref[...], w1_ref[...]) + b1_ref[...])
    hid = jnp.sin(_dot_hi(hid, w2_ref[...]) + b2_ref[...])
    filt = _dot_hi(hid, w3_ref[...]) + b3_ref[...]
    o_ref[...] = filt * jnp.exp(-dist_ref[...] * jnp.abs(dec_ref[...]))


def _hy_filter(l, w1, b1, w2, b2, w3, b3, decay):
    pos = jnp.arange(l, dtype=F32)
    t = pos / l
    ang = (2.0 * math.pi) * t[:, None] * jnp.arange(1, HY_BANDS + 1, dtype=F32)
    feat = jnp.concatenate([t[:, None], jnp.cos(ang), jnp.sin(ang)], axis=-1)
    dist = jnp.broadcast_to((jnp.abs(pos - l // 2) / l)[:, None], (l, D_HEAD))
    pad = D_HEAD
    emb, ff = w1.shape
    feat = jnp.pad(feat, ((0, 0), (0, pad - emb)))
    w1p = jnp.pad(w1, ((0, pad - emb), (0, pad - ff)))
    w2p = jnp.pad(w2, ((0, pad - ff), (0, pad - ff)))
    w3p = jnp.pad(w3, ((0, pad - ff), (0, 0)))
    b1p = jnp.pad(b1, (0, pad - ff)).reshape(1, pad)
    b2p = jnp.pad(b2, (0, pad - ff)).reshape(1, pad)
    tl = min(l, 512)
    n = 2 * BR_W
    fixed = lambda shape: pl.BlockSpec(shape, lambda i, j: (0, 0))
    return pl.pallas_call(
        _hy_filter_kernel,
        grid=(l // tl, n // D_HEAD),
        in_specs=[pl.BlockSpec((tl, pad), lambda i, j: (i, 0)),
                  pl.BlockSpec((tl, D_HEAD), lambda i, j: (i, 0)),
                  fixed((pad, pad)), fixed((1, pad)), fixed((pad, pad)), fixed((1, pad)),
                  pl.BlockSpec((pad, D_HEAD), lambda i, j: (0, j)),
                  pl.BlockSpec((1, D_HEAD), lambda i, j: (0, j)),
                  pl.BlockSpec((1, D_HEAD), lambda i, j: (0, j))],
        out_specs=pl.BlockSpec((tl, D_HEAD), lambda i, j: (i, j)),
        out_shape=jax.ShapeDtypeStruct((l, n), F32),
        compiler_params=_cparams("parallel", "parallel"),
        name="hy_filter",
    )(feat, dist, w1p, b1p, w2p, b2p, w3p, b3.reshape(1, n), decay.reshape(1, n))


def _dft_matrices(l):
    n = 2 * l
    k = jnp.arange(l, dtype=jnp.int32)
    t = jnp.arange(l, dtype=jnp.int32)
    ang = (2.0 * math.pi / n) * ((k[:, None] * t[None, :]) % n).astype(F32)
    alt = jnp.where(t % 2 == 0, 1.0, -1.0).astype(F32)
    fc = jnp.cos(ang)
    fs = jnp.where(k[:, None] == 0, alt[None, :], -jnp.sin(ang))
    tp = t + l // 2
    ang_i = (2.0 * math.pi / n) * ((tp[:, None] * k[None, :]) % n).astype(F32)
    wk = jnp.where(k == 0, 1.0, 2.0).astype(F32) / n
    alt_i = jnp.where(tp % 2 == 0, 1.0, -1.0).astype(F32) / n
    ic = jnp.cos(ang_i) * wk[None, :]
    is_ = jnp.where(k[None, :] == 0, alt_i[:, None], -jnp.sin(ang_i) * wk[None, :])
    return jnp.stack([fc, fs]).astype(BF16), jnp.stack([ic, is_]).astype(BF16)


def _dft_fwd_kernel(f_ref, x_ref, *rest, with_filter, tm):
    x = x_ref[0]
    ur = jnp.dot(f_ref[0], x, preferred_element_type=F32)
    ui = jnp.dot(f_ref[1], x, preferred_element_type=F32)
    if not with_filter:
        zr_ref, zi_ref = rest
        zr_ref[0] = ur
        zi_ref[0] = ui
        return
    hr_ref, hi_ref, zr_ref, zi_ref = rest
    hr, hi = hr_ref[0], hi_ref[0]
    row0 = (lax.broadcasted_iota(jnp.int32, ur.shape, 0) + pl.program_id(0) * tm) == 0
    zr_ref[0] = (ur * hr - jnp.where(row0, 0.0, ui * hi)).astype(zr_ref.dtype)
    zi_ref[0] = jnp.where(row0, ui * hi, ur * hi + ui * hr).astype(zi_ref.dtype)


def _dft_fwd(fwd, x, x_col0, c, spec_h=None, h_col0=0, tm=512, tn=512):
    b, l, _ = x.shape
    tm = min(tm, l)
    xo, ho = x_col0 // tn, h_col0 // tn
    out_dtype = F32 if spec_h is None else BF16
    in_specs = [pl.BlockSpec((2, tm, l), lambda i, bb, j: (0, i, 0)),
                pl.BlockSpec((1, l, tn), lambda i, bb, j: (bb, 0, xo + j))]
    args = [fwd, x]
    if spec_h is not None:
        hspec = pl.BlockSpec((1, tm, tn), lambda i, bb, j: (0, i, ho + j))
        in_specs += [hspec, hspec]
        args += list(spec_h)
    ospec = pl.BlockSpec((1, tm, tn), lambda i, bb, j: (bb, i, j))
    return pl.pallas_call(
        functools.partial(_dft_fwd_kernel, with_filter=spec_h is not None, tm=tm),
        grid=(l // tm, b, c // tn),
        in_specs=in_specs,
        out_specs=[ospec, ospec],
        out_shape=[jax.ShapeDtypeStruct((b, l, c), out_dtype)] * 2,
        compiler_params=_cparams("parallel", "parallel", "parallel"),
        name="dft_fwd",
    )(*args)


def _dft_inv_kernel(f_ref, zr_ref, zi_ref, u_ref, m_ref, skip_ref, *rest, with_gate):
    y = (jnp.dot(f_ref[0], zr_ref[0], preferred_element_type=F32)
         + jnp.dot(f_ref[1], zi_ref[0], preferred_element_type=F32))
    z = m_ref[0] * (y + u_ref[0] * skip_ref[...])
    if with_gate:
        g_ref, o_ref = rest
        o_ref[0] = (z * _silu(g_ref[0])).astype(o_ref.dtype)
    else:
        o_ref, ob_ref = rest
        o_ref[0] = z
        ob_ref[0] = z.astype(BF16)


def _dft_inv(inv, zr, zi, u, u_col0, mul, mul_col0, skip, gate=None, gate_col0=0, tm=512, tn=512):
    b, l, c = zr.shape
    tm = min(tm, l)
    win = lambda col0: pl.BlockSpec((1, tm, tn), lambda i, bb, j, o=col0 // tn: (bb, i, o + j))
    zspec = pl.BlockSpec((1, l, tn), lambda i, bb, j: (bb, 0, j))
    in_specs = [pl.BlockSpec((2, tm, l), lambda i, bb, j: (0, i, 0)), zspec, zspec,
                win(u_col0), win(mul_col0), pl.BlockSpec((1, tn), lambda i, bb, j: (0, j))]
    args = [inv, zr, zi, u, mul, skip]
    ospec = pl.BlockSpec((1, tm, tn), lambda i, bb, j: (bb, i, j))
    if gate is not None:
        in_specs.append(win(gate_col0))
        args.append(gate)
        out_specs, out_shape = ospec, jax.ShapeDtypeStruct((b, l, c), BF16)
    else:
        out_specs = [ospec, ospec]
        out_shape = [jax.ShapeDtypeStruct((b, l, c), F32), jax.ShapeDtypeStruct((b, l, c), BF16)]
    return pl.pallas_call(
        functools.partial(_dft_inv_kernel, with_gate=gate is not None),
        grid=(l // tm, b, c // tn),
        in_specs=in_specs,
        out_specs=out_specs,
        out_shape=out_shape,
        compiler_params=_cparams("parallel", "parallel", "parallel"),
        name="dft_inv",
    )(*args)


def _hyena(proj3, p, dft):
    l = proj3.shape[1]
    fwd, inv = dft
    filt = _hy_filter(l, p['hy_w1'], p['hy_b1'], p['hy_w2'], p['hy_b2'], p['hy_w3'], p['hy_b3'], p['hy_decay'])
    filt_b = filt.astype(BF16)[None]
    spec_h = _dft_fwd(fwd, filt_b, 0, 2 * BR_W)
    pre, pre_b = _hy_pre(proj3, p['hy_conv'])
    skip = p['hy_skip'].astype(F32)
    zr, zi = _dft_fwd(fwd, pre_b, 0, BR_W, spec_h, 0)
    z1, z1_b = _dft_inv(inv, zr, zi, pre, 0, pre, BR_W, skip[0:1])
    zr, zi = _dft_fwd(fwd, z1_b, 0, BR_W, spec_h, BR_W)
    return _dft_inv(inv, zr, zi, z1, 0, pre, 2 * BR_W, skip[1:2], gate=proj3, gate_col0=7 * BR_W)


def _softplus(x):
    return jnp.maximum(x, 0.0) + jnp.log1p(jnp.exp(-jnp.abs(x)))


def _unit_tri_inverse(a, blk16, blk32):
    eye = (lax.broadcasted_iota(jnp.int32, a.shape, 0) == lax.broadcasted_iota(jnp.int32, a.shape, 1)).astype(F32)
    x = -jnp.where(blk16, a, 0.0)
    p = eye + x
    for _ in range(3):
        x = _dot_hi(x, x)
        p = p + _dot_hi(p, x)
    for off in (jnp.where(blk32 & ~blk16, a, 0.0), jnp.where(blk32, 0.0, a)):
        p = p - _dot_hi(p, _dot_hi(off, p))
    return p


def _gdn_chunk(q, k, v, ab, a_row, dt_row, h, s, *, backward):
    n = q.shape[0]
    lane = lax.broadcasted_iota(jnp.int32, ab.shape, 1)
    base = (2 * N_HEAD if backward else 0) + h
    g_all = -a_row * _softplus(ab + dt_row)
    g = jnp.sum(jnp.where(lane == base, g_all, 0.0), axis=1, keepdims=True)
    beta = jnp.sum(jnp.where(lane == base + N_HEAD, _sigmoid(ab), 0.0), axis=1, keepdims=True)

    ri = lax.broadcasted_iota(jnp.int32, (n, n), 0)
    ci = lax.broadcasted_iota(jnp.int32, (n, n), 1)
    incl = (ci >= ri) if backward else (ci <= ri)
    strict = (ci > ri) if backward else (ci < ri)
    gc = _dot_hi(incl.astype(F32), jnp.broadcast_to(g, q.shape))
    gc_row = gc.T[:n, :]
    total = gc[0:1, :] if backward else gc[n - 1:n, :]
    decay = jnp.where(incl, jnp.exp(jnp.where(incl, gc[:, :n] - gc_row, 0.0)), 0.0)

    kb = k * beta
    a = jnp.where(strict, _dot_nt(kb, k) * decay, 0.0)
    t = _unit_tri_inverse(a, (ri // 16) == (ci // 16), (ri // 32) == (ci // 32))
    e = jnp.exp(gc)
    u = _dot_hi(t, v * beta)
    w = _dot_hi(t, kb * e)
    a_intra = jnp.where(incl, _dot_nt(q, k) * decay, 0.0)

    v_new = u - _dot(w, s)
    o = _dot(q * e, s) + _dot(a_intra, v_new)
    s = s * jnp.exp(total) + _dot_tn(k * jnp.exp(total - gc), v_new)
    return o, s


def _gdn_kernel(*refs, has_s0):
    if has_s0:
        (q_ref, k_ref, v_ref, z_ref, ab_ref, wq_ref, wk_ref, wv_ref, arow_ref, dt_ref, gn_ref, s0_ref,
         y_ref, sf_ref, qn, kn, vn, of, ob) = refs
    else:
        (q_ref, k_ref, v_ref, z_ref, ab_ref, wq_ref, wk_ref, wv_ref, arow_ref, dt_ref, gn_ref,
         y_ref, sf_ref, qn, kn, vn, of, ob) = refs
    h = pl.program_id(1)
    l = q_ref.shape[1]
    n_chunks = l // CHUNK

    def l2n(x):
        return x * lax.rsqrt(jnp.sum(x * x, axis=-1, keepdims=True) + EPS)

    qn[...] = l2n(_silu(_dwconv3(q_ref[0], wq_ref))) * (D_HEAD ** -0.5)
    kn[...] = l2n(_silu(_dwconv3(k_ref[0], wk_ref)))
    vn[...] = _silu(_dwconv3(v_ref[0], wv_ref))

    a_row, dt_row = arow_ref[...], dt_ref[...]

    def step(i, carry):
        s_f, s_b = carry
        rf = pl.ds(pl.multiple_of(i * CHUNK, CHUNK), CHUNK)
        rb = pl.ds(pl.multiple_of((n_chunks - 1 - i) * CHUNK, CHUNK), CHUNK)
        o_f, s_f = _gdn_chunk(qn[rf, :], kn[rf, :], vn[rf, :], ab_ref[0, rf, :], a_row, dt_row, h, s_f,
                              backward=False)
        o_b, s_b = _gdn_chunk(qn[rb, :], kn[rb, :], vn[rb, :], ab_ref[0, rb, :], a_row, dt_row, h, s_b,
                              backward=True)
        of[rf, :] = o_f
        ob[rb, :] = o_b
        return s_f, s_b

    if has_s0:
        init = (s0_ref[0, 0, 0, 0], s0_ref[0, 0, 1, 0])
    else:
        init = (jnp.zeros((D_HEAD, D_HEAD), F32),) * 2
    s_f, s_b = lax.fori_loop(0, n_chunks, step, init)
    sf_ref[0, 0, 0] = s_f
    sf_ref[0, 1, 0] = s_b
    y_ref[0] = (_rms(of[...] + ob[...], gn_ref[...]) * _silu(z_ref[0])).astype(y_ref.dtype)


def _gdn(proj3, ab3, conv_w, a_log, dt_bias, norm_g, state=None, layer=0):
    b, l, _ = proj3.shape
    lanes = jnp.zeros((2, 2 * N_HEAD), F32).at[:, :N_HEAD].set(1.0)
    a_row = jnp.pad((jnp.exp(a_log.astype(F32))[:, None, :] * lanes.reshape(2, 2, N_HEAD)).reshape(1, -1),
                    ((0, 0), (0, AB_PAD - 4 * N_HEAD)))
    dt_row = jnp.pad((dt_bias.astype(F32)[:, None, :] * lanes.reshape(2, 2, N_HEAD)).reshape(1, -1),
                     ((0, 0), (0, AB_PAD - 4 * N_HEAD)))
    blk = lambda c: pl.BlockSpec((1, l, D_HEAD), lambda i, h, c=c: (i, 0, c + h))
    wblk = lambda c: pl.BlockSpec((3, D_HEAD), lambda i, h, c=c: (0, c + h))
    row = pl.BlockSpec((1, D_HEAD), lambda i, h: (0, 0))
    in_specs = [blk(0), blk(4), blk(8), blk(12),
                pl.BlockSpec((1, l, AB_PAD), lambda i, h: (i, 0, 0)),
                wblk(0), wblk(4), wblk(8), row, row, row]
    args = [proj3, proj3, proj3, proj3, ab3, conv_w, conv_w, conv_w, a_row, dt_row, norm_g]
    if state is not None:
        in_specs.append(pl.BlockSpec((1, 1, 2, 1, D_HEAD, D_HEAD), lambda i, h: (i, layer, 0, h, 0, 0)))
        args.append(state)
    return pl.pallas_call(
        functools.partial(_gdn_kernel, has_s0=state is not None),
        grid=(b, N_HEAD),
        in_specs=in_specs,
        out_specs=[pl.BlockSpec((1, l, D_HEAD), lambda i, h: (i, 0, h)),
                   pl.BlockSpec((1, 2, 1, D_HEAD, D_HEAD), lambda i, h: (i, 0, h, 0, 0))],
        out_shape=[jax.ShapeDtypeStruct((b, l, BR_W), BF16),
                   jax.ShapeDtypeStruct((b, 2, N_HEAD, D_HEAD, D_HEAD), F32)],
        scratch_shapes=[pltpu.VMEM((l, D_HEAD), F32)] * 5,
        compiler_params=_cparams("parallel", "parallel"),
        name="gdn",
    )(*args)


def _mod_kernel(c_ref, w_ref, b_ref, o_ref):
    o_ref[...] = _dot_hi(_silu(c_ref[...]), w_ref[...]) + b_ref[...]


def _modulation(cond, w_mod, b_mod, tn=512):
    n = cond.shape[0]
    rows = 8
    out = pl.pallas_call(
        _mod_kernel,
        grid=(3 * D_MODEL // tn,),
        in_specs=[pl.BlockSpec((rows, D_MODEL), lambda j: (0, 0)),
                  pl.BlockSpec((D_MODEL, tn), lambda j: (0, j)),
                  pl.BlockSpec((1, tn), lambda j: (0, j))],
        out_specs=pl.BlockSpec((rows, tn), lambda j: (0, j)),
        out_shape=jax.ShapeDtypeStruct((rows, 3 * D_MODEL), F32),
        compiler_params=_cparams("parallel"),
        name="modulation",
    )(jnp.pad(cond.astype(F32), ((0, rows - n), (0, 0))), w_mod, b_mod.reshape(1, -1))
    return out[:n].reshape(n, 3, D_MODEL)


def _split_w_in(w_in):
    n_a = 4 * BR_W + 4 * N_HEAD
    main = jnp.concatenate([w_in[:, :4 * BR_W], w_in[:, n_a:]], axis=1).astype(BF16)
    ab = jnp.pad(w_in[:, 4 * BR_W:n_a], ((0, 0), (0, AB_PAD - 4 * N_HEAD))).astype(BF16)
    return main, ab


def _trunk_layer(x3, cond, p, layer, dft, latent):
    b, l, _ = x3.shape
    x2 = x3.reshape(b * l, D_MODEL)
    mod = _modulation(cond, p['w_mod'], p['b_mod'])
    rows_per_mod = l if mod.shape[0] == b else b * l
    g_pre = p['g_pre'].reshape(1, D_MODEL)
    w_main, w_ab = _split_w_in(p['w_in'])
    proj, ab = _inproj(x2, mod, g_pre, w_main, w_ab, rows_per_mod)
    proj3 = proj.reshape(b, l, N_MAIN)
    ab3 = ab.reshape(b, l, AB_PAD)

    lam_init = 0.8 - 0.6 * math.exp(-0.3 * layer)
    lam_p = p['diff_lam'].astype(F32)
    lam = (jnp.exp(jnp.sum(lam_p[0] * lam_p[1])) - jnp.exp(jnp.sum(lam_p[2] * lam_p[3])) + lam_init).reshape(1, 1)
    diff_norm = p['diff_norm'].reshape(1, D_HEAD)
    gdn_norm = p['gdn_norm'].reshape(1, D_HEAD)

    yb = _hyena(proj3, p, dft)
    if latent is None:
        ya, s_fin = _gdn(proj3, ab3, p['gdn_conv'], p['gdn_a_log'], p['gdn_dt_bias'], gdn_norm)
        yc, yd, nat_kv, diff_kv = _ctx_attention(proj3, lam, diff_norm, lam_init)
        extras = (s_fin, nat_kv, diff_kv)
    else:
        ya, _ = _gdn(proj3, ab3, p['gdn_conv'], p['gdn_a_log'], p['gdn_dt_bias'], gdn_norm,
                     latent['state_gdn'], layer)
        yc = _lat_nat(proj3, latent['cache_nat_kv'], layer, _nat_bias_table(p['nat_rpb']))
        yd = _lat_diff(proj3, latent['cache_diff_kv'], layer, lam, diff_norm, lam_init, latent['rope'])
        extras = None

    ys = [t.reshape(b * l, BR_W) for t in (ya, yb, yc, yd)]
    out = _merge(x2, mod, g_pre, p['g_post'].reshape(1, D_MODEL), ys,
                 p['w_branch'].astype(BF16), p['w_merge'].astype(BF16),
                 p['b_merge'].reshape(1, -1).astype(F32), p['w_out'].astype(BF16), rows_per_mod)
    return out.reshape(b, l, D_MODEL), extras


def kernel(x_prompt, x_sample, state_gdn, cache_nat_kv, cache_diff_kv, c, c_ctx,
           w_mod, b_mod, g_pre, g_post, w_in, gdn_conv, gdn_a_log, gdn_dt_bias, gdn_norm,
           hy_conv, hy_w1, hy_b1, hy_w2, hy_b2, hy_w3, hy_b3, hy_decay, hy_skip,
           nat_rpb, diff_lam, diff_norm, w_branch, w_merge, b_merge, w_out):
    stacked = {
        'w_mod': w_mod, 'b_mod': b_mod, 'g_pre': g_pre, 'g_post': g_post, 'w_in': w_in,
        'gdn_conv': gdn_conv, 'gdn_a_log': gdn_a_log, 'gdn_dt_bias': gdn_dt_bias, 'gdn_norm': gdn_norm,
        'hy_conv': hy_conv, 'hy_w1': hy_w1, 'hy_b1': hy_b1, 'hy_w2': hy_w2, 'hy_b2': hy_b2,
        'hy_w3': hy_w3, 'hy_b3': hy_b3, 'hy_decay': hy_decay, 'hy_skip': hy_skip,
        'nat_rpb': nat_rpb, 'diff_lam': diff_lam, 'diff_norm': diff_norm,
        'w_branch': w_branch, 'w_merge': w_merge, 'b_merge': b_merge, 'w_out': w_out,
    }
    layers = [{name: arr[i] for name, arr in stacked.items()} for i in range(DEPTH)]

    y_prompt = x_prompt
    dft_ctx = _dft_matrices(x_prompt.shape[1])
    states, nat_kvs, diff_kvs = [], [], []
    for i, p in enumerate(layers):
        y_prompt, (s_l, nat_l, diff_l) = _trunk_layer(y_prompt, c_ctx.reshape(1, D_MODEL), p, i, dft_ctx, None)
        states.append(s_l)
        nat_kvs.append(nat_l)
        diff_kvs.append(diff_l)

    y_sample = x_sample
    dft_lat = _dft_matrices(x_sample.shape[1])
    latent = {'state_gdn': state_gdn, 'cache_nat_kv': cache_nat_kv, 'cache_diff_kv': cache_diff_kv,
              'rope': _rope_tables(x_sample.shape[1])}
    for i, p in enumerate(layers):
        y_sample, _ = _trunk_layer(y_sample, c, p, i, dft_lat, latent)

    return (y_prompt, y_sample, jnp.stack(states, axis=1), jnp.stack(nat_kvs, axis=1),
            jnp.stack(diff_kvs, axis=1))
```

```python
import functools
import math

import jax
import jax.numpy as jnp
import numpy as np
from jax import lax
from jax.experimental import pallas as pl
from jax.experimental.pallas import tpu as pltpu

F32 = jnp.float32
BF16 = jnp.bfloat16

D_MODEL = 1024
DEPTH = 2
GRID_W = 64
N_BRANCH = 4
BR_W = 512
N_HEAD = 4
D_HEAD = 128
CHUNK = 64
HY_BANDS = 16
WIN_R = 8
WIN_C = 16
DQK_D = 64
ROPE_BASE = 10000.0
EPS = 1e-6
N_MAIN = 4 * 4 * BR_W
AB_PAD = 128
NEG_INF = -1e30

VMEM_LIMIT = 48 * 1024 * 1024


def _cparams(*sem):
    return pltpu.CompilerParams(dimension_semantics=sem, vmem_limit_bytes=VMEM_LIMIT)


def _silu(x):
    return x * (1.0 / (1.0 + jnp.exp(-x)))


def _sigmoid(x):
    return 1.0 / (1.0 + jnp.exp(-x))


def _rms(x, g):
    return x * lax.rsqrt(jnp.mean(x * x, axis=-1, keepdims=True) + EPS) * g


def _dot(a, b):
    return jnp.dot(a.astype(BF16), b.astype(BF16), preferred_element_type=F32)


def _dot_nt(a, b):
    return lax.dot_general(a.astype(BF16), b.astype(BF16), (((1,), (1,)), ((), ())),
                           preferred_element_type=F32)


def _dot_tn(a, b):
    return lax.dot_general(a.astype(BF16), b.astype(BF16), (((0,), (0,)), ((), ())),
                           preferred_element_type=F32)


def _prenorm(x, g_pre, mod_ref):
    return _rms(x, g_pre) * (1.0 + mod_ref[0, 1:2, :]) + mod_ref[0, 0:1, :]


def _inproj_kernel(x_ref, mod_ref, gpre_ref, w_ref, wab_ref, proj_ref, ab_ref, h_scr):
    @pl.when(pl.program_id(1) == 0)
    def _():
        h = _prenorm(x_ref[...], gpre_ref[...], mod_ref).astype(BF16)
        h_scr[...] = h
        ab_ref[...] = jnp.dot(h, wab_ref[...], preferred_element_type=F32)

    proj_ref[...] = jnp.dot(h_scr[...], w_ref[...], preferred_element_type=F32)


def _inproj(x2, mod, g_pre, w_main, w_ab, rows_per_mod, tm=1024, tn=1024):
    m = x2.shape[0]
    return pl.pallas_call(
        _inproj_kernel,
        grid=(m // tm, N_MAIN // tn),
        in_specs=[
            pl.BlockSpec((tm, D_MODEL), lambda i, j: (i, 0)),
            pl.BlockSpec((1, 3, D_MODEL), lambda i, j: ((i * tm) // rows_per_mod, 0, 0)),
            pl.BlockSpec((1, D_MODEL), lambda i, j: (0, 0)),
            pl.BlockSpec((D_MODEL, tn), lambda i, j: (0, j)),
            pl.BlockSpec((D_MODEL, AB_PAD), lambda i, j: (0, 0)),
        ],
        out_specs=[
            pl.BlockSpec((tm, tn), lambda i, j: (i, j)),
            pl.BlockSpec((tm, AB_PAD), lambda i, j: (i, 0)),
        ],
        out_shape=[jax.ShapeDtypeStruct((m, N_MAIN), F32),
                   jax.ShapeDtypeStruct((m, AB_PAD), F32)],
        scratch_shapes=[pltpu.VMEM((tm, D_MODEL), BF16)],
        compiler_params=_cparams("parallel", "arbitrary"),
        name="inproj",
    )(x2, mod, g_pre, w_main, w_ab)


def _merge_kernel(x_ref, mod_ref, gpre_ref, gpost_ref, ya_ref, yb_ref, yc_ref, yd_ref,
                  wbr_ref, wmg_ref, bmg_ref, wout_ref, o_ref):
    x = x_ref[...]
    h = _prenorm(x, gpre_ref[...], mod_ref).astype(BF16)
    acc = None
    for k, y_ref in enumerate((ya_ref, yb_ref, yc_ref, yd_ref)):
        cols = slice(k * D_MODEL, (k + 1) * D_MODEL)
        gate = _sigmoid(jnp.dot(h, wmg_ref[:, cols], preferred_element_type=F32) + bmg_ref[:, cols])
        br = jnp.dot(y_ref[...], wbr_ref[k], preferred_element_type=F32)
        acc = gate * br if acc is None else acc + gate * br
    y = jnp.dot(acc.astype(BF16), wout_ref[...], preferred_element_type=F32)
    o_ref[...] = x + mod_ref[0, 2:3, :] * _rms(y, gpost_ref[...])


def _merge(x2, mod, g_pre, g_post, ys, w_branch, w_merge, b_merge, w_out, rows_per_mod, tm=256):
    m = x2.shape[0]
    row = lambda i: (i, 0)
    fixed2 = lambda i: (0, 0)
    return pl.pallas_call(
        _merge_kernel,
        grid=(m // tm,),
        in_specs=[
            pl.BlockSpec((tm, D_MODEL), row),
            pl.BlockSpec((1, 3, D_MODEL), lambda i: ((i * tm) // rows_per_mod, 0, 0)),
            pl.BlockSpec((1, D_MODEL), fixed2),
            pl.BlockSpec((1, D_MODEL), fixed2),
            pl.BlockSpec((tm, BR_W), row),
            pl.BlockSpec((tm, BR_W), row),
            pl.BlockSpec((tm, BR_W), row),
            pl.BlockSpec((tm, BR_W), row),
            pl.BlockSpec((N_BRANCH, BR_W, D_MODEL), lambda i: (0, 0, 0)),
            pl.BlockSpec((D_MODEL, N_BRANCH * D_MODEL), fixed2),
            pl.BlockSpec((1, N_BRANCH * D_MODEL), fixed2),
            pl.BlockSpec((D_MODEL, D_MODEL), fixed2),
        ],
        out_specs=pl.BlockSpec((tm, D_MODEL), row),
        out_shape=jax.ShapeDtypeStruct((m, D_MODEL), F32),
        compiler_params=_cparams("parallel"),
        name="merge",
    )(x2, mod, g_pre, g_post, *ys, w_branch, w_merge, b_merge, w_out)


def _softmax_rows(s):
    p = jnp.exp(s - jnp.max(s, axis=-1, keepdims=True))
    return p, jnp.sum(p, axis=-1, keepdims=True)


def _ctx_nat_kernel(q_ref, k_ref, v_ref, g_ref, y_ref, kv_ref):
    scale = D_HEAD ** -0.5
    for h in range(N_HEAD):
        sl = slice(h * D_HEAD, (h + 1) * D_HEAD)
        k = k_ref[0, :, sl]
        v = v_ref[0, :, sl]
        p, l = _softmax_rows(_dot_nt(q_ref[0, :, sl], k) * scale)
        o = _dot(p, v) / l
        y_ref[0, :, sl] = (o * _silu(g_ref[0, :, sl])).astype(y_ref.dtype)
        kv_ref[0, 0, h] = k
        kv_ref[0, 1, h] = v


def _map_masks():
    lane = lax.broadcasted_iota(jnp.int32, (1, D_HEAD), 1)
    first = (lane < DQK_D).astype(F32)
    return first, 1.0 - first


def _ctx_diff_kernel(lam_ref, q_ref, k_ref, v_ref, g_ref, gn_ref, y_ref, kv_ref, *, out_scale):
    scale = DQK_D ** -0.5
    m1, m2 = _map_masks()
    lam = lam_ref[...]
    for h in range(N_HEAD):
        sl = slice(h * D_HEAD, (h + 1) * D_HEAD)
        q = q_ref[0, :, sl]
        k = k_ref[0, :, sl]
        v = v_ref[0, :, sl]
        p1, l1 = _softmax_rows(_dot_nt(q * m1, k) * scale)
        p2, l2 = _softmax_rows(_dot_nt(q * m2, k) * scale)
        a = p1 / l1 - lam * (p2 / l2)
        o = _rms(_dot(a, v), gn_ref[...]) * out_scale
        y_ref[0, :, sl] = (o * _silu(g_ref[0, :, sl])).astype(y_ref.dtype)
        kv_ref[0, 0, h] = k
        kv_ref[0, 1, h] = v


def _ctx_attention(proj3, lam, diff_norm, lam_init):
    b, l, _ = proj3.shape
    blk = lambda c: pl.BlockSpec((1, l, BR_W), lambda i, c=c: (i, 0, c))
    y_spec = pl.BlockSpec((1, l, BR_W), lambda i: (i, 0, 0))
    kv_spec = pl.BlockSpec((1, 2, N_HEAD, l, D_HEAD), lambda i: (i, 0, 0, 0, 0))
    out_shape = [jax.ShapeDtypeStruct((b, l, BR_W), BF16),
                 jax.ShapeDtypeStruct((b, 2, N_HEAD, l, D_HEAD), F32)]
    yc, nat_kv = pl.pallas_call(
        _ctx_nat_kernel,
        grid=(b,),
        in_specs=[blk(8), blk(9), blk(10), blk(11)],
        out_specs=[y_spec, kv_spec],
        out_shape=out_shape,
        compiler_params=_cparams("parallel"),
        name="ctx_nat",
    )(proj3, proj3, proj3, proj3)
    yd, diff_kv = pl.pallas_call(
        functools.partial(_ctx_diff_kernel, out_scale=1.0 - lam_init),
        grid=(b,),
        in_specs=[pl.BlockSpec((1, 1), lambda i: (0, 0)),
                  blk(12), blk(13), blk(14), blk(15),
                  pl.BlockSpec((1, D_HEAD), lambda i: (0, 0))],
        out_specs=[y_spec, kv_spec],
        out_shape=out_shape,
        compiler_params=_cparams("parallel"),
        name="ctx_diff",
    )(lam, proj3, proj3, proj3, proj3, diff_norm)
    return yc, yd, nat_kv, diff_kv


def _nat_bias_table(rpb):
    cols = np.arange(GRID_W)
    start = np.clip(cols - WIN_C // 2, 0, GRID_W - WIN_C)
    inside = (cols[None, :] >= start[:, None]) & (cols[None, :] < start[:, None] + WIN_C)
    dc = cols[None, :] - cols[:, None] + (WIN_C - 1)
    onehot = ((dc[None] == np.arange(2 * WIN_C - 1)[:, None, None]) & inside[None]).astype(np.float32)
    t = jnp.einsum('hdx,xck->hdck', rpb.astype(F32), jnp.asarray(onehot), precision=lax.Precision.HIGHEST)
    t = jnp.where(jnp.asarray(inside)[None, None], t, NEG_INF)
    tab = jnp.stack([t[:, WIN_R - 1 - off:2 * WIN_R - 1 - off] for off in range(WIN_R)], axis=1)
    return tab.transpose(0, 1, 3, 2, 4).reshape(rpb.shape[0], WIN_R, GRID_W, WIN_R * GRID_W)


def _lat_nat_kernel(q_ref, k_ref, v_ref, g_ref, ckv_ref, bias_ref, y_ref, kb_scr, vb_scr):
    scale = D_HEAD ** -0.5
    rows = q_ref.shape[1] // GRID_W
    win = WIN_R * GRID_W
    kb_scr[...] = k_ref[0].astype(BF16)
    vb_scr[...] = v_ref[0].astype(BF16)
    ck = ckv_ref[0, 0, 0, 0].astype(BF16)
    cv = ckv_ref[0, 0, 1, 0].astype(BF16)

    def row(r, carry):
        rs = jnp.clip(r - WIN_R // 2, 0, rows - WIN_R)
        q0 = pl.multiple_of(r * GRID_W, GRID_W)
        k0 = pl.multiple_of(rs * GRID_W, GRID_W)
        q = q_ref[0, pl.ds(q0, GRID_W), :].astype(BF16)
        s_lat = _dot_nt(q, kb_scr[pl.ds(k0, win), :]) * scale + bias_ref[0, r - rs]
        s_ctx = _dot_nt(q, ck) * scale
        m = jnp.maximum(jnp.max(s_lat, axis=-1, keepdims=True), jnp.max(s_ctx, axis=-1, keepdims=True))
        p_lat = jnp.exp(s_lat - m)
        p_ctx = jnp.exp(s_ctx - m)
        l = jnp.sum(p_lat, axis=-1, keepdims=True) + jnp.sum(p_ctx, axis=-1, keepdims=True)
        o = (_dot(p_lat, vb_scr[pl.ds(k0, win), :]) + _dot(p_ctx, cv)) / l
        g = g_ref[0, pl.ds(q0, GRID_W), :]
        y_ref[0, pl.ds(q0, GRID_W), :] = (o * _silu(g)).astype(y_ref.dtype)
        return carry

    lax.fori_loop(0, rows, row, 0)


def _lat_nat(proj3, cache_nat_kv, layer, bias_tab):
    b, l, _ = proj3.shape
    past = cache_nat_kv.shape[4]
    blk = lambda c: pl.BlockSpec((1, l, D_HEAD), lambda i, h, c=c: (i, 0, c + h))
    return pl.pallas_call(
        _lat_nat_kernel,
        grid=(b, N_HEAD),
        in_specs=[blk(32), blk(36), blk(40), blk(44),
                  pl.BlockSpec((1, 1, 2, 1, past, D_HEAD), lambda i, h: (i, layer, 0, h, 0, 0)),
                  pl.BlockSpec((1, WIN_R, GRID_W, WIN_R * GRID_W), lambda i, h: (h, 0, 0, 0))],
        out_specs=pl.BlockSpec((1, l, D_HEAD), lambda i, h: (i, 0, h)),
        out_shape=jax.ShapeDtypeStruct((b, l, BR_W), BF16),
        scratch_shapes=[pltpu.VMEM((l, D_HEAD), BF16), pltpu.VMEM((l, D_HEAD), BF16)],
        compiler_params=_cparams("parallel", "parallel"),
        name="lat_nat",
    )(proj3, proj3, proj3, proj3, cache_nat_kv, bias_tab)


def _rope_tables(l):
    half = DQK_D // 2
    nf = half // 2
    t = jnp.arange(l)
    row = (t // GRID_W).astype(F32)
    col = (t % GRID_W).astype(F32)
    inv = ROPE_BASE ** (-jnp.arange(nf, dtype=F32) / nf)
    ang = jnp.concatenate([row[:, None] * inv, col[:, None] * inv], axis=-1)
    cos, sin = jnp.cos(ang), jnp.sin(ang)
    zero = jnp.zeros_like(sin)
    tile2 = lambda a, b: jnp.concatenate([a, b, a, b], axis=-1)
    return tile2(cos, cos), tile2(-sin, zero), tile2(zero, sin)


def _rope(x, cos, sin_a, sin_b):
    return x * cos + pltpu.roll(x, 96, 1) * sin_a + pltpu.roll(x, 32, 1) * sin_b


def _lat_diff_kernel(lam_ref, q_ref, k_ref, v_ref, g_ref, ckv_ref, gn_ref,
                     cq_ref, saq_ref, sbq_ref, ck_ref, sak_ref, sbk_ref,
                     y_ref, ks_scr, vs_scr, *, out_scale, prep_rows):
    scale = DQK_D ** -0.5
    l = k_ref.shape[1]

    @pl.when(pl.program_id(2) == 0)
    def _():
        def prep(i, carry):
            rows = pl.ds(pl.multiple_of(i * prep_rows, prep_rows), prep_rows)
            kr = _rope(k_ref[0, rows, :], ck_ref[rows, :], sak_ref[rows, :], sbk_ref[rows, :])
            ks_scr[rows, :] = kr.astype(BF16)
            vs_scr[rows, :] = v_ref[0, rows, :].astype(BF16)
            return carry

        lax.fori_loop(0, l // prep_rows, prep, 0)
        ks_scr[l:, :] = ckv_ref[0, 0, 0, 0].astype(BF16)
        vs_scr[l:, :] = ckv_ref[0, 0, 1, 0].astype(BF16)

    m1, m2 = _map_masks()
    q = _rope(q_ref[0], cq_ref[...], saq_ref[...], sbq_ref[...])
    ks = ks_scr[...]
    p1, l1 = _softmax_rows(_dot_nt(q * m1, ks) * scale)
    p2, l2 = _softmax_rows(_dot_nt(q * m2, ks) * scale)
    a = p1 * (1.0 / l1) - p2 * (lam_ref[...] / l2)
    o = _rms(_dot(a, vs_scr[...]), gn_ref[...]) * out_scale
    y_ref[0] = (o * _silu(g_ref[0])).astype(y_ref.dtype)


def _lat_diff(proj3, cache_diff_kv, layer, lam, diff_norm, lam_init, rope_tabs, tq=128):
    b, l, _ = proj3.shape
    past = cache_diff_kv.shape[4]
    qblk = lambda c: pl.BlockSpec((1, tq, D_HEAD), lambda i, h, j, c=c: (i, j, c + h))
    full = lambda c: pl.BlockSpec((1, l, D_HEAD), lambda i, h, j, c=c: (i, 0, c + h))
    tq_tab = pl.BlockSpec((tq, D_HEAD), lambda i, h, j: (j, 0))
    full_tab = pl.BlockSpec((l, D_HEAD), lambda i, h, j: (0, 0))
    return pl.pallas_call(
        functools.partial(_lat_diff_kernel, out_scale=1.0 - lam_init, prep_rows=512),
        grid=(b, N_HEAD, l // tq),
        in_specs=[pl.BlockSpec((1, 1), lambda i, h, j: (0, 0)),
                  qblk(48), full(52), full(56), qblk(60),
                  pl.BlockSpec((1, 1, 2, 1, past, D_HEAD), lambda i, h, j: (i, layer, 0, h, 0, 0)),
                  pl.BlockSpec((1, D_HEAD), lambda i, h, j: (0, 0)),
                  tq_tab, tq_tab, tq_tab, full_tab, full_tab, full_tab],
        out_specs=pl.BlockSpec((1, tq, D_HEAD), lambda i, h, j: (i, j, h)),
        out_shape=jax.ShapeDtypeStruct((b, l, BR_W), BF16),
        scratch_shapes=[pltpu.VMEM((l + past, D_HEAD), BF16), pltpu.VMEM((l + past, D_HEAD), BF16)],
        compiler_params=_cparams("parallel", "parallel", "arbitrary"),
        name="lat_diff",
    )(lam, proj3, proj3, proj3, proj3, cache_diff_kv, diff_norm, *rope_tabs, *rope_tabs)


def _dwconv3(x, w_ref):
    l = x.shape[0]
    row = lax.broadcasted_iota(jnp.int32, x.shape, 0)
    prev = jnp.where(row == 0, 0.0, pltpu.roll(x, 1, 0))
    nxt = jnp.where(row == l - 1, 0.0, pltpu.roll(x, l - 1, 0))
    return prev * w_ref[0:1, :] + x * w_ref[1:2, :] + nxt * w_ref[2:3, :]


def _hy_pre_kernel(x_ref, w_ref, o_ref, ob_ref):
    y = _dwconv3(x_ref[0], w_ref)
    o_ref[0] = y
    ob_ref[0] = y.astype(BF16)


def _hy_pre(proj3, conv_w):
    b, l, _ = proj3.shape
    n = 3 * BR_W // D_HEAD
    spec = pl.BlockSpec((1, l, D_HEAD), lambda i, j: (i, 0, j))
    return pl.pallas_call(
        _hy_pre_kernel,
        grid=(b, n),
        in_specs=[pl.BlockSpec((1, l, D_HEAD), lambda i, j: (i, 0, 4 * BR_W // D_HEAD + j)),
                  pl.BlockSpec((3, D_HEAD), lambda i, j: (0, j))],
        out_specs=[spec, spec],
        out_shape=[jax.ShapeDtypeStruct((b, l, 3 * BR_W), F32),
                   jax.ShapeDtypeStruct((b, l, 3 * BR_W), BF16)],
        compiler_params=_cparams("parallel", "parallel"),
        name="hy_pre",
    )(proj3, conv_w)


def _dot_hi(a, b):
    return jnp.dot(a, b, preferred_element_type=F32, precision=lax.Precision.HIGHEST)


def _hy_filter_kernel(feat_ref, dist_ref, w1_ref, b1_ref, w2_ref, b2_ref, w3_ref, b3_ref, dec_ref, o_ref):
    hid = jnp.sin(_dot_hi(feat_ref[...], w1_ref[...]) + b1_ref[...])
    hid = jnp.sin(_dot_hi(hid, w2_ref[...]) + b2_ref[...])
    dist = dist_ref[...]
    for j in range(o_ref.shape[1] // D_HEAD):
        cols = slice(j * D_HEAD, (j + 1) * D_HEAD)
        filt = _dot_hi(hid, w3_ref[:, cols]) + b3_ref[:, cols]
        o_ref[:, cols] = filt * jnp.exp(-dist * jnp.abs(dec_ref[:, cols]))


def _hy_filter(l, w1, b1, w2, b2, w3, b3, decay):
    pos = jnp.arange(l, dtype=F32)
    t = pos / l
    ang = (2.0 * math.pi) * t[:, None] * jnp.arange(1, HY_BANDS + 1, dtype=F32)
    feat = jnp.concatenate([t[:, None], jnp.cos(ang), jnp.sin(ang)], axis=-1)
    dist = jnp.broadcast_to((jnp.abs(pos - l // 2) / l)[:, None], (l, D_HEAD))
    pad = D_HEAD
    emb, ff = w1.shape
    feat = jnp.pad(feat, ((0, 0), (0, pad - emb)))
    w1p = jnp.pad(w1, ((0, pad - emb), (0, pad - ff)))
    w2p = jnp.pad(w2, ((0, pad - ff), (0, pad - ff)))
    w3p = jnp.pad(w3, ((0, pad - ff), (0, 0)))
    b1p = jnp.pad(b1, (0, pad - ff)).reshape(1, pad)
    b2p = jnp.pad(b2, (0, pad - ff)).reshape(1, pad)
    tl = min(l, 256)
    n = 2 * BR_W
    fixed = lambda shape: pl.BlockSpec(shape, lambda i: (0, 0))
    return pl.pallas_call(
        _hy_filter_kernel,
        grid=(l // tl,),
        in_specs=[pl.BlockSpec((tl, pad), lambda i: (i, 0)),
                  pl.BlockSpec((tl, D_HEAD), lambda i: (i, 0)),
                  fixed((pad, pad)), fixed((1, pad)), fixed((pad, pad)), fixed((1, pad)),
                  fixed((pad, n)), fixed((1, n)), fixed((1, n))],
        out_specs=pl.BlockSpec((tl, n), lambda i: (i, 0)),
        out_shape=jax.ShapeDtypeStruct((l, n), F32),
        compiler_params=_cparams("parallel"),
        name="hy_filter",
    )(feat, dist, w1p, b1p, w2p, b2p, w3p, b3.reshape(1, n), decay.reshape(1, n))


def _dft_matrices(l):
    n = 2 * l
    k = jnp.arange(l, dtype=jnp.int32)
    t = jnp.arange(l, dtype=jnp.int32)
    ang = (2.0 * math.pi / n) * ((k[:, None] * t[None, :]) % n).astype(F32)
    alt = jnp.where(t % 2 == 0, 1.0, -1.0).astype(F32)
    fc = jnp.cos(ang)
    fs = jnp.where(k[:, None] == 0, alt[None, :], -jnp.sin(ang))
    tp = t + l // 2
    ang_i = (2.0 * math.pi / n) * ((tp[:, None] * k[None, :]) % n).astype(F32)
    wk = jnp.where(k == 0, 1.0, 2.0).astype(F32) / n
    alt_i = jnp.where(tp % 2 == 0, 1.0, -1.0).astype(F32) / n
    ic = jnp.cos(ang_i) * wk[None, :]
    is_ = jnp.where(k[None, :] == 0, alt_i[:, None], -jnp.sin(ang_i) * wk[None, :])
    return jnp.stack([fc, fs]).astype(BF16), jnp.stack([ic, is_]).astype(BF16)


def _dft_fwd_kernel(f_ref, x_ref, *rest, with_filter, tm):
    x = x_ref[0]
    ur = jnp.dot(f_ref[0], x, preferred_element_type=F32)
    ui = jnp.dot(f_ref[1], x, preferred_element_type=F32)
    if not with_filter:
        zr_ref, zi_ref = rest
        zr_ref[0] = ur
        zi_ref[0] = ui
        return
    hr_ref, hi_ref, zr_ref, zi_ref = rest
    hr, hi = hr_ref[0], hi_ref[0]
    row0 = (lax.broadcasted_iota(jnp.int32, ur.shape, 0) + pl.program_id(0) * tm) == 0
    zr_ref[0] = (ur * hr - jnp.where(row0, 0.0, ui * hi)).astype(zr_ref.dtype)
    zi_ref[0] = jnp.where(row0, ui * hi, ur * hi + ui * hr).astype(zi_ref.dtype)


def _dft_fwd(fwd, x, x_col0, c, spec_h=None, h_col0=0, tm=512, tn=512):
    b, l, _ = x.shape
    tm = min(tm, l)
    xo, ho = x_col0 // tn, h_col0 // tn
    out_dtype = F32 if spec_h is None else BF16
    in_specs = [pl.BlockSpec((2, tm, l), lambda i, bb, j: (0, i, 0)),
                pl.BlockSpec((1, l, tn), lambda i, bb, j: (bb, 0, xo + j))]
    args = [fwd, x]
    if spec_h is not None:
        hspec = pl.BlockSpec((1, tm, tn), lambda i, bb, j: (0, i, ho + j))
        in_specs += [hspec, hspec]
        args += list(spec_h)
    ospec = pl.BlockSpec((1, tm, tn), lambda i, bb, j: (bb, i, j))
    return pl.pallas_call(
        functools.partial(_dft_fwd_kernel, with_filter=spec_h is not None, tm=tm),
        grid=(l // tm, b, c // tn),
        in_specs=in_specs,
        out_specs=[ospec, ospec],
        out_shape=[jax.ShapeDtypeStruct((b, l, c), out_dtype)] * 2,
        compiler_params=_cparams("parallel", "parallel", "parallel"),
        name="dft_fwd",
    )(*args)


def _dft_inv_kernel(f_ref, zr_ref, zi_ref, u_ref, m_ref, skip_ref, *rest, with_gate):
    y = (jnp.dot(f_ref[0], zr_ref[0], preferred_element_type=F32)
         + jnp.dot(f_ref[1], zi_ref[0], preferred_element_type=F32))
    z = m_ref[0] * (y + u_ref[0] * skip_ref[...])
    if with_gate:
        g_ref, o_ref = rest
        o_ref[0] = (z * _silu(g_ref[0])).astype(o_ref.dtype)
    else:
        o_ref, ob_ref = rest
        o_ref[0] = z
        ob_ref[0] = z.astype(BF16)


def _dft_inv(inv, zr, zi, u, u_col0, mul, mul_col0, skip, gate=None, gate_col0=0, tm=512, tn=512):
    b, l, c = zr.shape
    tm = min(tm, l)
    win = lambda col0: pl.BlockSpec((1, tm, tn), lambda i, bb, j, o=col0 // tn: (bb, i, o + j))
    zspec = pl.BlockSpec((1, l, tn), lambda i, bb, j: (bb, 0, j))
    in_specs = [pl.BlockSpec((2, tm, l), lambda i, bb, j: (0, i, 0)), zspec, zspec,
                win(u_col0), win(mul_col0), pl.BlockSpec((1, tn), lambda i, bb, j: (0, j))]
    args = [inv, zr, zi, u, mul, skip]
    ospec = pl.BlockSpec((1, tm, tn), lambda i, bb, j: (bb, i, j))
    if gate is not None:
        in_specs.append(win(gate_col0))
        args.append(gate)
        out_specs, out_shape = ospec, jax.ShapeDtypeStruct((b, l, c), BF16)
    else:
        out_specs = [ospec, ospec]
        out_shape = [jax.ShapeDtypeStruct((b, l, c), F32), jax.ShapeDtypeStruct((b, l, c), BF16)]
    return pl.pallas_call(
        functools.partial(_dft_inv_kernel, with_gate=gate is not None),
        grid=(l // tm, b, c // tn),
        in_specs=in_specs,
        out_specs=out_specs,
        out_shape=out_shape,
        compiler_params=_cparams("parallel", "parallel", "parallel"),
        name="dft_inv",
    )(*args)


def _hyena(proj3, p, dft):
    l = proj3.shape[1]
    fwd, inv = dft
    filt = _hy_filter(l, p['hy_w1'], p['hy_b1'], p['hy_w2'], p['hy_b2'], p['hy_w3'], p['hy_b3'], p['hy_decay'])
    filt_b = filt.astype(BF16)[None]
    spec_h = _dft_fwd(fwd, filt_b, 0, 2 * BR_W)
    pre, pre_b = _hy_pre(proj3, p['hy_conv'])
    skip = p['hy_skip'].astype(F32)
    zr, zi = _dft_fwd(fwd, pre_b, 0, BR_W, spec_h, 0)
    z1, z1_b = _dft_inv(inv, zr, zi, pre, 0, pre, BR_W, skip[0:1])
    zr, zi = _dft_fwd(fwd, z1_b, 0, BR_W, spec_h, BR_W)
    return _dft_inv(inv, zr, zi, z1, 0, pre, 2 * BR_W, skip[1:2], gate=proj3, gate_col0=7 * BR_W)


def _softplus(x):
    return jnp.maximum(x, 0.0) + jnp.log1p(jnp.exp(-jnp.abs(x)))


def _split_bf16(x, parts):
    out = []
    for _ in range(parts - 1):
        piece = x.astype(BF16)
        out.append(piece)
        x = x - piece.astype(F32)
    out.append(x.astype(BF16))
    return out


def _bmm(a, b, hi=False):
    mm = lambda x, y: jnp.einsum('nij,njk->nik', x, y, preferred_element_type=F32)
    if not hi:
        return mm(a.astype(BF16), b.astype(BF16))
    (a1, a2), (b1, b2) = _split_bf16(a, 2), _split_bf16(b, 2)
    return mm(a1, b1) + (mm(a1, b2) + mm(a2, b1))


def _bmm_nt(a, b):
    return jnp.einsum('nid,njd->nij', a.astype(BF16), b.astype(BF16), preferred_element_type=F32)


def _unit_tri_inverse(a, blk16, blk32):
    eye = (lax.broadcasted_iota(jnp.int32, a.shape, 1) == lax.broadcasted_iota(jnp.int32, a.shape, 2)).astype(F32)
    x = -jnp.where(blk16, a, 0.0)
    p = eye + x
    for _ in range(3):
        x = _bmm(x, x, hi=True)
        p = p + _bmm(p, x, hi=True)
    for off in (jnp.where(blk32 & ~blk16, a, 0.0), jnp.where(blk32, 0.0, a)):
        p = p - _bmm(p, _bmm(off, p, hi=True), hi=True)
    return p


def _gdn_prepare(q, k, v, ab, a_row, dt_row, h):
    n, c, _ = q.shape
    two = lambda x: jnp.concatenate([x, x], axis=0)
    q, k, v, ab = two(q), two(k), two(v), two(ab)
    back3 = lambda shape: lax.broadcasted_iota(jnp.int32, shape, 0) >= n
    lane = lax.broadcasted_iota(jnp.int32, ab.shape, 2)
    base = jnp.where(back3(ab.shape), 2 * N_HEAD, 0) + h
    g_all = -a_row * _softplus(ab + dt_row)
    g = jnp.sum(jnp.where(lane == base, g_all, 0.0), axis=2, keepdims=True)
    beta = jnp.sum(jnp.where(lane == base + N_HEAD, _sigmoid(ab), 0.0), axis=2, keepdims=True)

    sq = (2 * n, c, c)
    ri = lax.broadcasted_iota(jnp.int32, sq, 1)
    ci = lax.broadcasted_iota(jnp.int32, sq, 2)
    ahead = jnp.where(back3(sq), ci - ri, ri - ci)
    incl = ahead >= 0
    strict = ahead > 0
    tri = jnp.where(incl, 1.0, 0.0).astype(BF16)
    gc = sum(jnp.einsum('nij,njk->nik', tri, piece, preferred_element_type=F32)
             for piece in _split_bf16(jnp.broadcast_to(g, q.shape), 3))
    gc_row = jnp.swapaxes(gc, 1, 2)[:, :c, :]
    total = jnp.where(back3((2 * n, 1, D_HEAD)), gc[:, 0:1, :], gc[:, c - 1:c, :])
    decay = jnp.where(incl, jnp.exp(jnp.where(incl, gc[:, :, :c] - gc_row, 0.0)), 0.0)

    kb = k * beta
    a = jnp.where(strict, _bmm_nt(kb, k) * decay, 0.0)
    t = _unit_tri_inverse(a, (ri // 16) == (ci // 16), (ri // 32) == (ci // 32))
    e = jnp.exp(gc)
    u = _bmm(t, v * beta, hi=True)
    w = _bmm(t, kb * e, hi=True)
    a_intra = jnp.where(incl, _bmm_nt(q, k) * decay, 0.0)
    return (u, w.astype(BF16), (q * e).astype(BF16), (k * jnp.exp(total - gc)).astype(BF16),
            a_intra.astype(BF16), jnp.exp(total))


def _gdn_kernel(*refs, has_s0, group):
    if has_s0:
        (q_ref, k_ref, v_ref, z_ref, ab_ref, wq_ref, wk_ref, wv_ref, arow_ref, dt_ref, gn_ref, s0_ref,
         y_ref, sf_ref, qn, kn, vn, u_s, w_s, qd_s, kd_s, ai_s, gl_s) = refs
    else:
        (q_ref, k_ref, v_ref, z_ref, ab_ref, wq_ref, wk_ref, wv_ref, arow_ref, dt_ref, gn_ref,
         y_ref, sf_ref, qn, kn, vn, u_s, w_s, qd_s, kd_s, ai_s, gl_s) = refs
    l = q_ref.shape[1]
    heads = q_ref.shape[2] // D_HEAD
    head0 = pl.program_id(1) * heads
    n_chunks = l // CHUNK
    hcols = lambda hh: slice(hh * D_HEAD, (hh + 1) * D_HEAD)

    def l2n(x):
        return x * lax.rsqrt(jnp.sum(x * x, axis=-1, keepdims=True) + EPS)

    for hh in range(heads):
        cols = hcols(hh)
        qn[:, cols] = l2n(_silu(_dwconv3(q_ref[0, :, cols], wq_ref.at[:, cols]))) * (D_HEAD ** -0.5)
        kn[:, cols] = l2n(_silu(_dwconv3(k_ref[0, :, cols], wk_ref.at[:, cols])))
        vn[:, cols] = _silu(_dwconv3(v_ref[0, :, cols], wv_ref.at[:, cols]))

    a_row, dt_row = arow_ref[...], dt_ref[...]

    def prepare(gi, carry):
        span = group * CHUNK
        rows = pl.ds(pl.multiple_of(gi * span, span), span)
        chunks = lambda x: x.reshape(group, CHUNK, x.shape[-1])
        ab = chunks(ab_ref[0, rows, :])
        for hh in range(heads):
            cols = hcols(hh)
            u, w, qd, kd, ai, gl = _gdn_prepare(chunks(qn[rows, cols]), chunks(kn[rows, cols]),
                                                chunks(vn[rows, cols]), ab, a_row, dt_row, head0 + hh)
            for d in range(2):
                part = slice(d * group, (d + 1) * group)
                u_s[d, rows, cols] = u[part].reshape(span, D_HEAD)
                w_s[d, rows, cols] = w[part].reshape(span, D_HEAD)
                qd_s[d, rows, cols] = qd[part].reshape(span, D_HEAD)
                kd_s[d, rows, cols] = kd[part].reshape(span, D_HEAD)
                ai_s[d, hh, rows, :] = ai[part].reshape(span, CHUNK)
                gl_s[d, hh, pl.ds(gi * group, group)] = jnp.broadcast_to(gl[part], (group,) + gl_s.shape[3:])
        return carry

    lax.fori_loop(0, n_chunks // group, prepare, 0)

    def scan(i, carry):
        new = []
        for hh in range(heads):
            cols = hcols(hh)
            for d, chunk in ((0, i), (1, n_chunks - 1 - i)):
                rows = pl.ds(pl.multiple_of(chunk * CHUNK, CHUNK), CHUNK)
                s = carry[2 * hh + d]
                v_new = u_s[d, rows, cols] - _dot(w_s[d, rows, cols], s)
                o = _dot(qd_s[d, rows, cols], s) + _dot(ai_s[d, hh, rows, :], v_new)
                new.append(s * gl_s[d, hh, chunk][0:1, :] + _dot_tn(kd_s[d, rows, cols], v_new))
                u_s[d, rows, cols] = o
        return tuple(new)

    if has_s0:
        init = tuple(s0_ref[0, 0, d, hh] for hh in range(heads) for d in range(2))
    else:
        init = (jnp.zeros((D_HEAD, D_HEAD), F32),) * (2 * heads)
    final = lax.fori_loop(0, n_chunks, scan, init)
    for hh in range(heads):
        cols = hcols(hh)
        sf_ref[0, 0, hh] = final[2 * hh]
        sf_ref[0, 1, hh] = final[2 * hh + 1]
        y_ref[0, :, cols] = (_rms(u_s[0, :, cols] + u_s[1, :, cols], gn_ref[...])
                             * _silu(z_ref[0, :, cols])).astype(y_ref.dtype)


def _gdn(proj3, ab3, conv_w, a_log, dt_bias, norm_g, state=None, layer=0):
    b, l, _ = proj3.shape
    lanes = jnp.zeros((2, 2 * N_HEAD), F32).at[:, :N_HEAD].set(1.0)
    a_row = jnp.pad((jnp.exp(a_log.astype(F32))[:, None, :] * lanes.reshape(2, 2, N_HEAD)).reshape(1, -1),
                    ((0, 0), (0, AB_PAD - 4 * N_HEAD)))
    dt_row = jnp.pad((dt_bias.astype(F32)[:, None, :] * lanes.reshape(2, 2, N_HEAD)).reshape(1, -1),
                     ((0, 0), (0, AB_PAD - 4 * N_HEAD)))
    hps = N_HEAD if l <= 512 else 1
    wid = hps * D_HEAD
    n_hb = N_HEAD // hps
    blk = lambda c: pl.BlockSpec((1, l, wid), lambda i, h, c=c: (i, 0, c * n_hb + h))
    wblk = lambda c: pl.BlockSpec((3, wid), lambda i, h, c=c: (0, c * n_hb + h))
    row = pl.BlockSpec((1, D_HEAD), lambda i, h: (0, 0))
    in_specs = [blk(0), blk(1), blk(2), blk(3),
                pl.BlockSpec((1, l, AB_PAD), lambda i, h: (i, 0, 0)),
                wblk(0), wblk(1), wblk(2), row, row, row]
    args = [proj3, proj3, proj3, proj3, ab3, conv_w, conv_w, conv_w, a_row, dt_row, norm_g]
    if state is not None:
        in_specs.append(pl.BlockSpec((1, 1, 2, hps, D_HEAD, D_HEAD), lambda i, h: (i, layer, 0, h, 0, 0)))
        args.append(state)
    return pl.pallas_call(
        functools.partial(_gdn_kernel, has_s0=state is not None, group=4),
        grid=(b, n_hb),
        in_specs=in_specs,
        out_specs=[pl.BlockSpec((1, l, wid), lambda i, h: (i, 0, h)),
                   pl.BlockSpec((1, 2, hps, D_HEAD, D_HEAD), lambda i, h: (i, 0, h, 0, 0))],
        out_shape=[jax.ShapeDtypeStruct((b, l, BR_W), BF16),
                   jax.ShapeDtypeStruct((b, 2, N_HEAD, D_HEAD, D_HEAD), F32)],
        scratch_shapes=[pltpu.VMEM((l, wid), F32)] * 3
        + [pltpu.VMEM((2, l, wid), F32)] + [pltpu.VMEM((2, l, wid), BF16)] * 3
        + [pltpu.VMEM((2, hps, l, CHUNK), BF16), pltpu.VMEM((2, hps, l // CHUNK, 8, D_HEAD), F32)],
        compiler_params=_cparams("parallel", "parallel"),
        name="gdn",
    )(*args)


def _mod_kernel(c_ref, w_ref, b_ref, o_ref):
    o_ref[...] = _dot_hi(_silu(c_ref[...]), w_ref[...]) + b_ref[...]


def _modulation(cond, w_mod, b_mod, tn=512):
    n = cond.shape[0]
    rows = 8
    out = pl.pallas_call(
        _mod_kernel,
        grid=(3 * D_MODEL // tn,),
        in_specs=[pl.BlockSpec((rows, D_MODEL), lambda j: (0, 0)),
                  pl.BlockSpec((D_MODEL, tn), lambda j: (0, j)),
                  pl.BlockSpec((1, tn), lambda j: (0, j))],
        out_specs=pl.BlockSpec((rows, tn), lambda j: (0, j)),
        out_shape=jax.ShapeDtypeStruct((rows, 3 * D_MODEL), F32),
        compiler_params=_cparams("parallel"),
        name="modulation",
    )(jnp.pad(cond.astype(F32), ((0, rows - n), (0, 0))), w_mod, b_mod.reshape(1, -1))
    return out[:n].reshape(n, 3, D_MODEL)


def _split_w_in(w_in):
    n_a = 4 * BR_W + 4 * N_HEAD
    main = jnp.concatenate([w_in[:, :4 * BR_W], w_in[:, n_a:]], axis=1).astype(BF16)
    ab = jnp.pad(w_in[:, 4 * BR_W:n_a], ((0, 0), (0, AB_PAD - 4 * N_HEAD))).astype(BF16)
    return main, ab


def _trunk_layer(x3, cond, p, layer, dft, latent):
    b, l, _ = x3.shape
    x2 = x3.reshape(b * l, D_MODEL)
    mod = _modulation(cond, p['w_mod'], p['b_mod'])
    rows_per_mod = l if mod.shape[0] == b else b * l
    g_pre = p['g_pre'].reshape(1, D_MODEL)
    w_main, w_ab = _split_w_in(p['w_in'])
    proj, ab = _inproj(x2, mod, g_pre, w_main, w_ab, rows_per_mod)
    proj3 = proj.reshape(b, l, N_MAIN)
    ab3 = ab.reshape(b, l, AB_PAD)

    lam_init = 0.8 - 0.6 * math.exp(-0.3 * layer)
    lam_p = p['diff_lam'].astype(F32)
    lam = (jnp.exp(jnp.sum(lam_p[0] * lam_p[1])) - jnp.exp(jnp.sum(lam_p[2] * lam_p[3])) + lam_init).reshape(1, 1)
    diff_norm = p['diff_norm'].reshape(1, D_HEAD)
    gdn_norm = p['gdn_norm'].reshape(1, D_HEAD)

    yb = _hyena(proj3, p, dft)
    if latent is None:
        ya, s_fin = _gdn(proj3, ab3, p['gdn_conv'], p['gdn_a_log'], p['gdn_dt_bias'], gdn_norm)
        yc, yd, nat_kv, diff_kv = _ctx_attention(proj3, lam, diff_norm, lam_init)
        extras = (s_fin, nat_kv, diff_kv)
    else:
        ya, _ = _gdn(proj3, ab3, p['gdn_conv'], p['gdn_a_log'], p['gdn_dt_bias'], gdn_norm,
                     latent['state_gdn'], layer)
        yc = _lat_nat(proj3, latent['cache_nat_kv'], layer, _nat_bias_table(p['nat_rpb']))
        yd = _lat_diff(proj3, latent['cache_diff_kv'], layer, lam, diff_norm, lam_init, latent['rope'])
        extras = None

    ys = [t.reshape(b * l, BR_W) for t in (ya, yb, yc, yd)]
    out = _merge(x2, mod, g_pre, p['g_post'].reshape(1, D_MODEL), ys,
                 p['w_branch'].astype(BF16), p['w_merge'].astype(BF16),
                 p['b_merge'].reshape(1, -1).astype(F32), p['w_out'].astype(BF16), rows_per_mod)
    return out.reshape(b, l, D_MODEL), extras


def kernel(x_prompt, x_sample, state_gdn, cache_nat_kv, cache_diff_kv, c, c_ctx,
           w_mod, b_mod, g_pre, g_post, w_in, gdn_conv, gdn_a_log, gdn_dt_bias, gdn_norm,
           hy_conv, hy_w1, hy_b1, hy_w2, hy_b2, hy_w3, hy_b3, hy_decay, hy_skip,
           nat_rpb, diff_lam, diff_norm, w_branch, w_merge, b_merge, w_out):
    stacked = {
        'w_mod': w_mod, 'b_mod': b_mod, 'g_pre': g_pre, 'g_post': g_post, 'w_in': w_in,
        'gdn_conv': gdn_conv, 'gdn_a_log': gdn_a_log, 'gdn_dt_bias': gdn_dt_bias, 'gdn_norm': gdn_norm,
        'hy_conv': hy_conv, 'hy_w1': hy_w1, 'hy_b1': hy_b1, 'hy_w2': hy_w2, 'hy_b2': hy_b2,
        'hy_w3': hy_w3, 'hy_b3': hy_b3, 'hy_decay': hy_decay, 'hy_skip': hy_skip,
        'nat_rpb': nat_rpb, 'diff_lam': diff_lam, 'diff_norm': diff_norm,
        'w_branch': w_branch, 'w_merge': w_merge, 'b_merge': b_merge, 'w_out': w_out,
    }
    layers = [{name: arr[i] for name, arr in stacked.items()} for i in range(DEPTH)]

    y_prompt = x_prompt
    dft_ctx = _dft_matrices(x_prompt.shape[1])
    states, nat_kvs, diff_kvs = [], [], []
    for i, p in enumerate(layers):
        y_prompt, (s_l, nat_l, diff_l) = _trunk_layer(y_prompt, c_ctx.reshape(1, D_MODEL), p, i, dft_ctx, None)
        states.append(s_l)
        nat_kvs.append(nat_l)
        diff_kvs.append(diff_l)

    y_sample = x_sample
    dft_lat = _dft_matrices(x_sample.shape[1])
    latent = {'state_gdn': state_gdn, 'cache_nat_kv': cache_nat_kv, 'cache_diff_kv': cache_diff_kv,
              'rope': _rope_tables(x_sample.shape[1])}
    for i, p in enumerate(layers):
        y_sample, _ = _trunk_layer(y_sample, c, p, i, dft_lat, latent)

    return (y_prompt, y_sample, jnp.stack(states, axis=1), jnp.stack(nat_kvs, axis=1),
            jnp.stack(diff_kvs, axis=1))
```

```python
import functools
import math

import jax
import jax.numpy as jnp
import numpy as np
from jax import lax
from jax.experimental import pallas as pl
from jax.experimental.pallas import tpu as pltpu

F32 = jnp.float32
BF16 = jnp.bfloat16

D_MODEL = 1024
DEPTH = 2
GRID_W = 64
N_BRANCH = 4
BR_W = 512
N_HEAD = 4
D_HEAD = 128
SUBLANES = 8
CHUNK = 64
HY_BANDS = 16
WIN_R = 8
WIN_C = 16
DQK_D = 64
ROPE_BASE = 10000.0
EPS = 1e-6
N_MAIN = 4 * 4 * BR_W
AB_PAD = 128
NEG_INF = -1e30

VMEM_LIMIT = 48 * 1024 * 1024


def _cparams(*sem):
    return pltpu.CompilerParams(dimension_semantics=sem, vmem_limit_bytes=VMEM_LIMIT)


def _silu(x):
    return x * (1.0 / (1.0 + jnp.exp(-x)))


def _sigmoid(x):
    return 1.0 / (1.0 + jnp.exp(-x))


def _rms(x, g):
    return x * lax.rsqrt(jnp.mean(x * x, axis=-1, keepdims=True) + EPS) * g


def _dot(a, b):
    return jnp.dot(a.astype(BF16), b.astype(BF16), preferred_element_type=F32)


def _dot_nt(a, b):
    return lax.dot_general(a.astype(BF16), b.astype(BF16), (((1,), (1,)), ((), ())),
                           preferred_element_type=F32)


def _dot_tn(a, b):
    return lax.dot_general(a.astype(BF16), b.astype(BF16), (((0,), (0,)), ((), ())),
                           preferred_element_type=F32)


def _prenorm(x, g_pre, mod_ref):
    return _rms(x, g_pre) * (1.0 + mod_ref[0, 1:2, :]) + mod_ref[0, 0:1, :]


def _inproj_kernel(x_ref, mod_ref, gpre_ref, w_ref, wab_ref, proj_ref, ab_ref, h_scr):
    @pl.when(pl.program_id(1) == 0)
    def _():
        h = _prenorm(x_ref[...], gpre_ref[...], mod_ref).astype(BF16)
        h_scr[...] = h
        ab_ref[...] = jnp.dot(h, wab_ref[...], preferred_element_type=F32)

    proj_ref[...] = jnp.dot(h_scr[...], w_ref[...], preferred_element_type=F32)


def _inproj(x2, mod, g_pre, w_main, w_ab, rows_per_mod, tm=1024, tn=1024):
    m = x2.shape[0]
    return pl.pallas_call(
        _inproj_kernel,
        grid=(m // tm, N_MAIN // tn),
        in_specs=[
            pl.BlockSpec((tm, D_MODEL), lambda i, j: (i, 0)),
            pl.BlockSpec((1, 3, D_MODEL), lambda i, j: ((i * tm) // rows_per_mod, 0, 0)),
            pl.BlockSpec((1, D_MODEL), lambda i, j: (0, 0)),
            pl.BlockSpec((D_MODEL, tn), lambda i, j: (0, j)),
            pl.BlockSpec((D_MODEL, AB_PAD), lambda i, j: (0, 0)),
        ],
        out_specs=[
            pl.BlockSpec((tm, tn), lambda i, j: (i, j)),
            pl.BlockSpec((tm, AB_PAD), lambda i, j: (i, 0)),
        ],
        out_shape=[jax.ShapeDtypeStruct((m, N_MAIN), F32),
                   jax.ShapeDtypeStruct((m, AB_PAD), F32)],
        scratch_shapes=[pltpu.VMEM((tm, D_MODEL), BF16)],
        compiler_params=_cparams("parallel", "arbitrary"),
        name="inproj",
    )(x2, mod, g_pre, w_main, w_ab)


def _merge_kernel(x_ref, mod_ref, gpre_ref, gpost_ref, ya_ref, yb_ref, yc_ref, yd_ref,
                  wbr_ref, wmg_ref, bmg_ref, wout_ref, o_ref):
    x = x_ref[...]
    h = _prenorm(x, gpre_ref[...], mod_ref).astype(BF16)
    acc = None
    for k, y_ref in enumerate((ya_ref, yb_ref, yc_ref, yd_ref)):
        cols = slice(k * D_MODEL, (k + 1) * D_MODEL)
        gate = _sigmoid(jnp.dot(h, wmg_ref[:, cols], preferred_element_type=F32) + bmg_ref[:, cols])
        br = jnp.dot(y_ref[...], wbr_ref[k], preferred_element_type=F32)
        acc = gate * br if acc is None else acc + gate * br
    y = jnp.dot(acc.astype(BF16), wout_ref[...], preferred_element_type=F32)
    o_ref[...] = x + mod_ref[0, 2:3, :] * _rms(y, gpost_ref[...])


def _merge(x2, mod, g_pre, g_post, ys, w_branch, w_merge, b_merge, w_out, rows_per_mod, tm=256):
    m = x2.shape[0]
    row = lambda i: (i, 0)
    fixed2 = lambda i: (0, 0)
    return pl.pallas_call(
        _merge_kernel,
        grid=(m // tm,),
        in_specs=[
            pl.BlockSpec((tm, D_MODEL), row),
            pl.BlockSpec((1, 3, D_MODEL), lambda i: ((i * tm) // rows_per_mod, 0, 0)),
            pl.BlockSpec((1, D_MODEL), fixed2),
            pl.BlockSpec((1, D_MODEL), fixed2),
            pl.BlockSpec((tm, BR_W), row),
            pl.BlockSpec((tm, BR_W), row),
            pl.BlockSpec((tm, BR_W), row),
            pl.BlockSpec((tm, BR_W), row),
            pl.BlockSpec((N_BRANCH, BR_W, D_MODEL), lambda i: (0, 0, 0)),
            pl.BlockSpec((D_MODEL, N_BRANCH * D_MODEL), fixed2),
            pl.BlockSpec((1, N_BRANCH * D_MODEL), fixed2),
            pl.BlockSpec((D_MODEL, D_MODEL), fixed2),
        ],
        out_specs=pl.BlockSpec((tm, D_MODEL), row),
        out_shape=jax.ShapeDtypeStruct((m, D_MODEL), F32),
        compiler_params=_cparams("parallel"),
        name="merge",
    )(x2, mod, g_pre, g_post, *ys, w_branch, w_merge, b_merge, w_out)


def _softmax_rows(s):
    p = jnp.exp(s - jnp.max(s, axis=-1, keepdims=True))
    return p, jnp.sum(p, axis=-1, keepdims=True)


def _ctx_nat_kernel(*refs, aliased):
    q_ref, k_ref, v_ref, g_ref, y_ref, kv_ref = refs[1:] if aliased else refs
    scale = D_HEAD ** -0.5
    for h in range(N_HEAD):
        sl = slice(h * D_HEAD, (h + 1) * D_HEAD)
        k = k_ref[0, :, sl]
        v = v_ref[0, :, sl]
        p, l = _softmax_rows(_dot_nt(q_ref[0, :, sl], k) * scale)
        o = _dot(p, v) / l
        y_ref[0, :, sl] = (o * _silu(g_ref[0, :, sl])).astype(y_ref.dtype)
        kv_ref[0, 0, 0, h] = k
        kv_ref[0, 0, 1, h] = v


def _map_masks():
    lane = lax.broadcasted_iota(jnp.int32, (1, D_HEAD), 1)
    first = (lane < DQK_D).astype(F32)
    return first, 1.0 - first


def _ctx_diff_kernel(*refs, aliased, out_scale):
    lam_ref, q_ref, k_ref, v_ref, g_ref, gn_ref, y_ref, kv_ref = refs[1:] if aliased else refs
    scale = DQK_D ** -0.5
    m1, m2 = _map_masks()
    lam = lam_ref[...]
    for h in range(N_HEAD):
        sl = slice(h * D_HEAD, (h + 1) * D_HEAD)
        q = q_ref[0, :, sl]
        k = k_ref[0, :, sl]
        v = v_ref[0, :, sl]
        p1, l1 = _softmax_rows(_dot_nt(q * m1, k) * scale)
        p2, l2 = _softmax_rows(_dot_nt(q * m2, k) * scale)
        a = p1 / l1 - lam * (p2 / l2)
        o = _rms(_dot(a, v), gn_ref[...]) * out_scale
        y_ref[0, :, sl] = (o * _silu(g_ref[0, :, sl])).astype(y_ref.dtype)
        kv_ref[0, 0, 0, h] = k
        kv_ref[0, 0, 1, h] = v


def _ctx_attention(proj3, lam, diff_norm, lam_init, layer, nat_cache, diff_cache):
    b, l, _ = proj3.shape
    blk = lambda c: pl.BlockSpec((1, l, BR_W), lambda i, c=c: (i, 0, c))
    y_spec = pl.BlockSpec((1, l, BR_W), lambda i: (i, 0, 0))
    kv_spec = pl.BlockSpec((1, 1, 2, N_HEAD, l, D_HEAD), lambda i: (i, layer, 0, 0, 0, 0))
    out_shape = [jax.ShapeDtypeStruct((b, l, BR_W), BF16),
                 jax.ShapeDtypeStruct((b, DEPTH, 2, N_HEAD, l, D_HEAD), F32)]
    aliased = nat_cache is not None
    cache_specs = [pl.BlockSpec(memory_space=pl.ANY)] if aliased else []
    aliases = {0: 1} if aliased else {}
    yc, nat_cache = pl.pallas_call(
        functools.partial(_ctx_nat_kernel, aliased=aliased),
        grid=(b,),
        in_specs=cache_specs + [blk(8), blk(9), blk(10), blk(11)],
        out_specs=[y_spec, kv_spec],
        out_shape=out_shape,
        input_output_aliases=aliases,
        compiler_params=_cparams("parallel"),
        name="ctx_nat",
    )(*([nat_cache] if aliased else []), proj3, proj3, proj3, proj3)
    yd, diff_cache = pl.pallas_call(
        functools.partial(_ctx_diff_kernel, aliased=aliased, out_scale=1.0 - lam_init),
        grid=(b,),
        in_specs=cache_specs + [pl.BlockSpec((1, 1), lambda i: (0, 0)),
                                blk(12), blk(13), blk(14), blk(15),
                                pl.BlockSpec((1, D_HEAD), lambda i: (0, 0))],
        out_specs=[y_spec, kv_spec],
        out_shape=out_shape,
        input_output_aliases=aliases,
        compiler_params=_cparams("parallel"),
        name="ctx_diff",
    )(*([diff_cache] if aliased else []), lam, proj3, proj3, proj3, proj3, diff_norm)
    return yc, yd, nat_cache, diff_cache


def _nat_bias_table(rpb):
    cols = np.arange(GRID_W)
    start = np.clip(cols - WIN_C // 2, 0, GRID_W - WIN_C)
    inside = (cols[None, :] >= start[:, None]) & (cols[None, :] < start[:, None] + WIN_C)
    dc = cols[None, :] - cols[:, None] + (WIN_C - 1)
    onehot = ((dc[None] == np.arange(2 * WIN_C - 1)[:, None, None]) & inside[None]).astype(np.float32)
    t = jnp.einsum('hdx,xck->hdck', rpb.astype(F32), jnp.asarray(onehot), precision=lax.Precision.HIGHEST)
    t = jnp.where(jnp.asarray(inside)[None, None], t, NEG_INF)
    tab = jnp.stack([t[:, WIN_R - 1 - off:2 * WIN_R - 1 - off] for off in range(WIN_R)], axis=1)
    return tab.transpose(0, 1, 3, 2, 4).reshape(rpb.shape[0], WIN_R, GRID_W, WIN_R * GRID_W)


def _lat_nat_kernel(q_ref, k_ref, v_ref, g_ref, ckv_ref, bias_ref, y_ref, kb_scr, vb_scr):
    scale = D_HEAD ** -0.5
    rows = q_ref.shape[1] // GRID_W
    win = WIN_R * GRID_W
    kb_scr[...] = k_ref[0].astype(BF16)
    vb_scr[...] = v_ref[0].astype(BF16)
    ck = ckv_ref[0, 0, 0, 0].astype(BF16)
    cv = ckv_ref[0, 0, 1, 0].astype(BF16)

    def row(r, carry):
        rs = jnp.clip(r - WIN_R // 2, 0, rows - WIN_R)
        q0 = pl.multiple_of(r * GRID_W, GRID_W)
        k0 = pl.multiple_of(rs * GRID_W, GRID_W)
        q = q_ref[0, pl.ds(q0, GRID_W), :].astype(BF16)
        s_lat = _dot_nt(q, kb_scr[pl.ds(k0, win), :]) * scale + bias_ref[0, r - rs]
        s_ctx = _dot_nt(q, ck) * scale
        m = jnp.maximum(jnp.max(s_lat, axis=-1, keepdims=True), jnp.max(s_ctx, axis=-1, keepdims=True))
        p_lat = jnp.exp(s_lat - m)
        p_ctx = jnp.exp(s_ctx - m)
        l = jnp.sum(p_lat, axis=-1, keepdims=True) + jnp.sum(p_ctx, axis=-1, keepdims=True)
        o = (_dot(p_lat, vb_scr[pl.ds(k0, win), :]) + _dot(p_ctx, cv)) / l
        g = g_ref[0, pl.ds(q0, GRID_W), :]
        y_ref[0, pl.ds(q0, GRID_W), :] = (o * _silu(g)).astype(y_ref.dtype)
        return carry

    lax.fori_loop(0, rows, row, 0)


def _lat_nat(proj3, cache_nat_kv, layer, bias_tab):
    b, l, _ = proj3.shape
    past = cache_nat_kv.shape[4]
    blk = lambda c: pl.BlockSpec((1, l, D_HEAD), lambda i, h, c=c: (i, 0, c + h))
    return pl.pallas_call(
        _lat_nat_kernel,
        grid=(b, N_HEAD),
        in_specs=[blk(32), blk(36), blk(40), blk(44),
                  pl.BlockSpec((1, 1, 2, 1, past, D_HEAD), lambda i, h: (i, layer, 0, h, 0, 0)),
                  pl.BlockSpec((1, WIN_R, GRID_W, WIN_R * GRID_W), lambda i, h: (h, 0, 0, 0))],
        out_specs=pl.BlockSpec((1, l, D_HEAD), lambda i, h: (i, 0, h)),
        out_shape=jax.ShapeDtypeStruct((b, l, BR_W), BF16),
        scratch_shapes=[pltpu.VMEM((l, D_HEAD), BF16), pltpu.VMEM((l, D_HEAD), BF16)],
        compiler_params=_cparams("parallel", "parallel"),
        name="lat_nat",
    )(proj3, proj3, proj3, proj3, cache_nat_kv, bias_tab)


def _rope_tables(l):
    half = DQK_D // 2
    nf = half // 2
    t = jnp.arange(l)
    row = (t // GRID_W).astype(F32)
    col = (t % GRID_W).astype(F32)
    inv = ROPE_BASE ** (-jnp.arange(nf, dtype=F32) / nf)
    ang = jnp.concatenate([row[:, None] * inv, col[:, None] * inv], axis=-1)
    cos, sin = jnp.cos(ang), jnp.sin(ang)
    zero = jnp.zeros_like(sin)
    tile2 = lambda a, b: jnp.concatenate([a, b, a, b], axis=-1)
    return tile2(cos, cos), tile2(-sin, zero), tile2(zero, sin)


def _rope(x, cos, sin_a, sin_b):
    return x * cos + pltpu.roll(x, 96, 1) * sin_a + pltpu.roll(x, 32, 1) * sin_b


def _lat_diff_kernel(lam_ref, q_ref, k_ref, v_ref, g_ref, ckv_ref, gn_ref,
                     cq_ref, saq_ref, sbq_ref, ck_ref, sak_ref, sbk_ref,
                     y_ref, ks_scr, vs_scr, *, out_scale, prep_rows, key_block):
    scale = DQK_D ** -0.5
    l = k_ref.shape[1]

    @pl.when(pl.program_id(2) == 0)
    def _():
        def prep(i, carry):
            rows = pl.ds(pl.multiple_of(i * prep_rows, prep_rows), prep_rows)
            kr = _rope(k_ref[0, rows, :], ck_ref[rows, :], sak_ref[rows, :], sbk_ref[rows, :])
            ks_scr[rows, :] = kr.astype(BF16)
            vs_scr[rows, :] = v_ref[0, rows, :].astype(BF16)
            return carry

        lax.fori_loop(0, l // prep_rows, prep, 0)
        ks_scr[l:, :] = ckv_ref[0, 0, 0, 0].astype(BF16)
        vs_scr[l:, :] = ckv_ref[0, 0, 1, 0].astype(BF16)

    q = _rope(q_ref[0], cq_ref[...], saq_ref[...], sbq_ref[...]) * scale
    outs = []
    for mask in _map_masks():
        qm = (q * mask).astype(BF16)
        m = l_sum = acc = None
        for blk in range(ks_scr.shape[0] // key_block):
            rows = slice(blk * key_block, (blk + 1) * key_block)
            s = _dot_nt(qm, ks_scr[rows, :])
            m_blk = jnp.max(s, axis=-1, keepdims=True)
            if blk == 0:
                m = m_blk
                p = jnp.exp(s - m)
                l_sum = jnp.sum(p, axis=-1, keepdims=True)
                acc = _dot(p, vs_scr[rows, :])
            else:
                m_new = jnp.maximum(m, m_blk)
                alpha = jnp.exp(m - m_new)
                p = jnp.exp(s - m_new)
                l_sum = alpha * l_sum + jnp.sum(p, axis=-1, keepdims=True)
                acc = alpha * acc + _dot(p, vs_scr[rows, :])
                m = m_new
        outs.append(acc / l_sum)
    o = _rms(outs[0] - lam_ref[...] * outs[1], gn_ref[...]) * out_scale
    y_ref[0] = (o * _silu(g_ref[0])).astype(y_ref.dtype)


def _lat_diff(proj3, cache_diff_kv, layer, lam, diff_norm, lam_init, rope_tabs, tq=256):
    b, l, _ = proj3.shape
    past = cache_diff_kv.shape[4]
    qblk = lambda c: pl.BlockSpec((1, tq, D_HEAD), lambda i, h, j, c=c: (i, j, c + h))
    full = lambda c: pl.BlockSpec((1, l, D_HEAD), lambda i, h, j, c=c: (i, 0, c + h))
    tq_tab = pl.BlockSpec((tq, D_HEAD), lambda i, h, j: (j, 0))
    full_tab = pl.BlockSpec((l, D_HEAD), lambda i, h, j: (0, 0))
    return pl.pallas_call(
        functools.partial(_lat_diff_kernel, out_scale=1.0 - lam_init, prep_rows=512, key_block=256),
        grid=(b, N_HEAD, l // tq),
        in_specs=[pl.BlockSpec((1, 1), lambda i, h, j: (0, 0)),
                  qblk(48), full(52), full(56), qblk(60),
                  pl.BlockSpec((1, 1, 2, 1, past, D_HEAD), lambda i, h, j: (i, layer, 0, h, 0, 0)),
                  pl.BlockSpec((1, D_HEAD), lambda i, h, j: (0, 0)),
                  tq_tab, tq_tab, tq_tab, full_tab, full_tab, full_tab],
        out_specs=pl.BlockSpec((1, tq, D_HEAD), lambda i, h, j: (i, j, h)),
        out_shape=jax.ShapeDtypeStruct((b, l, BR_W), BF16),
        scratch_shapes=[pltpu.VMEM((l + past, D_HEAD), BF16), pltpu.VMEM((l + past, D_HEAD), BF16)],
        compiler_params=_cparams("parallel", "parallel", "arbitrary"),
        name="lat_diff",
    )(lam, proj3, proj3, proj3, proj3, cache_diff_kv, diff_norm, *rope_tabs, *rope_tabs)


def _dwconv3(x, w_ref):
    l = x.shape[0]
    row = lax.broadcasted_iota(jnp.int32, x.shape, 0)
    prev = jnp.where(row == 0, 0.0, pltpu.roll(x, 1, 0))
    nxt = jnp.where(row == l - 1, 0.0, pltpu.roll(x, l - 1, 0))
    return prev * w_ref[0:1, :] + x * w_ref[1:2, :] + nxt * w_ref[2:3, :]


def _hy_pre_kernel(x_ref, above_ref, below_ref, w_ref, o_ref, ob_ref):
    t, n_t = pl.program_id(1), pl.num_programs(1)
    x = x_ref[0]
    rows = x.shape[0]
    row = lax.broadcasted_iota(jnp.int32, x.shape, 0)
    before = jnp.where(t == 0, 0.0, above_ref[0, SUBLANES - 1:SUBLANES, :])
    after = jnp.where(t == n_t - 1, 0.0, below_ref[0, 0:1, :])
    prev = jnp.where(row == 0, before, pltpu.roll(x, 1, 0))
    nxt = jnp.where(row == rows - 1, after, pltpu.roll(x, rows - 1, 0))
    y = prev * w_ref[0:1, :] + x * w_ref[1:2, :] + nxt * w_ref[2:3, :]
    o_ref[0] = y
    ob_ref[0] = y.astype(BF16)


def _hy_pre(proj3, conv_w, tl=512):
    b, l, _ = proj3.shape
    tl = min(tl, l)
    n = 3
    col0 = 4
    groups = tl // SUBLANES
    last_group = l // SUBLANES - 1
    spec = pl.BlockSpec((1, tl, BR_W), lambda i, t, j: (i, t, j))
    return pl.pallas_call(
        _hy_pre_kernel,
        grid=(b, l // tl, n),
        in_specs=[pl.BlockSpec((1, tl, BR_W), lambda i, t, j: (i, t, col0 + j)),
                  pl.BlockSpec((1, SUBLANES, BR_W),
                               lambda i, t, j: (i, jnp.maximum(t * groups - 1, 0), col0 + j)),
                  pl.BlockSpec((1, SUBLANES, BR_W),
                               lambda i, t, j: (i, jnp.minimum((t + 1) * groups, last_group), col0 + j)),
                  pl.BlockSpec((3, BR_W), lambda i, t, j: (0, j))],
        out_specs=[spec, spec],
        out_shape=[jax.ShapeDtypeStruct((b, l, 3 * BR_W), F32),
                   jax.ShapeDtypeStruct((b, l, 3 * BR_W), BF16)],
        compiler_params=_cparams("parallel", "parallel", "parallel"),
        name="hy_pre",
    )(proj3, proj3, proj3, conv_w)


def _dot_hi(a, b):
    return jnp.dot(a, b, preferred_element_type=F32, precision=lax.Precision.HIGHEST)


def _hy_filter_kernel(feat_ref, dist_ref, w1_ref, b1_ref, w2_ref, b2_ref, w3_ref, b3_ref, dec_ref, o_ref):
    hid = jnp.sin(_dot_hi(feat_ref[...], w1_ref[...]) + b1_ref[...])
    hid = jnp.sin(_dot_hi(hid, w2_ref[...]) + b2_ref[...])
    dist = dist_ref[...]
    for j in range(o_ref.shape[1] // D_HEAD):
        cols = slice(j * D_HEAD, (j + 1) * D_HEAD)
        filt = _dot_hi(hid, w3_ref[:, cols]) + b3_ref[:, cols]
        o_ref[:, cols] = filt * jnp.exp(-dist * jnp.abs(dec_ref[:, cols]))


def _hy_filter(l, w1, b1, w2, b2, w3, b3, decay):
    pos = jnp.arange(l, dtype=F32)
    t = pos / l
    ang = (2.0 * math.pi) * t[:, None] * jnp.arange(1, HY_BANDS + 1, dtype=F32)
    feat = jnp.concatenate([t[:, None], jnp.cos(ang), jnp.sin(ang)], axis=-1)
    dist = jnp.broadcast_to((jnp.abs(pos - l // 2) / l)[:, None], (l, D_HEAD))
    pad = D_HEAD
    emb, ff = w1.shape
    feat = jnp.pad(feat, ((0, 0), (0, pad - emb)))
    w1p = jnp.pad(w1, ((0, pad - emb), (0, pad - ff)))
    w2p = jnp.pad(w2, ((0, pad - ff), (0, pad - ff)))
    w3p = jnp.pad(w3, ((0, pad - ff), (0, 0)))
    b1p = jnp.pad(b1, (0, pad - ff)).reshape(1, pad)
    b2p = jnp.pad(b2, (0, pad - ff)).reshape(1, pad)
    tl = min(l, 256)
    n = 2 * BR_W
    fixed = lambda shape: pl.BlockSpec(shape, lambda i: (0, 0))
    return pl.pallas_call(
        _hy_filter_kernel,
        grid=(l // tl,),
        in_specs=[pl.BlockSpec((tl, pad), lambda i: (i, 0)),
                  pl.BlockSpec((tl, D_HEAD), lambda i: (i, 0)),
                  fixed((pad, pad)), fixed((1, pad)), fixed((pad, pad)), fixed((1, pad)),
                  fixed((pad, n)), fixed((1, n)), fixed((1, n))],
        out_specs=pl.BlockSpec((tl, n), lambda i: (i, 0)),
        out_shape=jax.ShapeDtypeStruct((l, n), F32),
        compiler_params=_cparams("parallel"),
        name="hy_filter",
    )(feat, dist, w1p, b1p, w2p, b2p, w3p, b3.reshape(1, n), decay.reshape(1, n))


def _dft_matrices(l):
    n = 2 * l
    k = jnp.arange(l, dtype=jnp.int32)
    t = jnp.arange(l, dtype=jnp.int32)
    split = 1 << (max(l.bit_length() - 1, 0) // 2)

    def cos_sin(rows, cols):
        def table(r):
            ang = (2.0 * math.pi / n) * ((r[:, None] * cols[None, :]) % n).astype(F32)
            return jnp.cos(ang), jnp.sin(ang)
        lo = rows[:split] - rows[0]
        (ch, sh), (cl, sl) = table(rows[::split]), table(lo)
        c = ch[:, None, :] * cl[None, :, :] - sh[:, None, :] * sl[None, :, :]
        s = sh[:, None, :] * cl[None, :, :] + ch[:, None, :] * sl[None, :, :]
        return c.reshape(l, l), s.reshape(l, l)

    alt = jnp.where(t % 2 == 0, 1.0, -1.0).astype(F32)
    fc, fs = cos_sin(k, t)
    fs = jnp.where(k[:, None] == 0, alt[None, :], -fs)
    tp = t + l // 2
    wk = jnp.where(k == 0, 1.0, 2.0).astype(F32) / n
    alt_i = jnp.where(tp % 2 == 0, 1.0, -1.0).astype(F32) / n
    ic, is_ = cos_sin(tp, k)
    ic = ic * wk[None, :]
    is_ = jnp.where(k[None, :] == 0, alt_i[:, None], -is_ * wk[None, :])
    return jnp.stack([fc, fs]).astype(BF16), jnp.stack([ic, is_]).astype(BF16)


def _dft_fwd_kernel(f_ref, x_ref, *rest, with_filter, tm):
    x = x_ref[0]
    ur = jnp.dot(f_ref[0], x, preferred_element_type=F32)
    ui = jnp.dot(f_ref[1], x, preferred_element_type=F32)
    if not with_filter:
        zr_ref, zi_ref = rest
        zr_ref[0] = ur
        zi_ref[0] = ui
        return
    hr_ref, hi_ref, zr_ref, zi_ref = rest
    hr, hi = hr_ref[0], hi_ref[0]
    row0 = (lax.broadcasted_iota(jnp.int32, ur.shape, 0) + pl.program_id(0) * tm) == 0
    zr_ref[0] = (ur * hr - jnp.where(row0, 0.0, ui * hi)).astype(zr_ref.dtype)
    zi_ref[0] = jnp.where(row0, ui * hi, ur * hi + ui * hr).astype(zi_ref.dtype)


def _dft_fwd(fwd, x, x_col0, c, spec_h=None, h_col0=0, tm=512, tn=512):
    b, l, _ = x.shape
    tm = min(tm, l)
    xo, ho = x_col0 // tn, h_col0 // tn
    out_dtype = F32 if spec_h is None else BF16
    in_specs = [pl.BlockSpec((2, tm, l), lambda i, bb, j: (0, i, 0)),
                pl.BlockSpec((1, l, tn), lambda i, bb, j: (bb, 0, xo + j))]
    args = [fwd, x]
    if spec_h is not None:
        hspec = pl.BlockSpec((1, tm, tn), lambda i, bb, j: (0, i, ho + j))
        in_specs += [hspec, hspec]
        args += list(spec_h)
    ospec = pl.BlockSpec((1, tm, tn), lambda i, bb, j: (bb, i, j))
    return pl.pallas_call(
        functools.partial(_dft_fwd_kernel, with_filter=spec_h is not None, tm=tm),
        grid=(l // tm, b, c // tn),
        in_specs=in_specs,
        out_specs=[ospec, ospec],
        out_shape=[jax.ShapeDtypeStruct((b, l, c), out_dtype)] * 2,
        compiler_params=_cparams("parallel", "parallel", "parallel"),
        name="dft_fwd",
    )(*args)


def _dft_inv_kernel(f_ref, zr_ref, zi_ref, u_ref, m_ref, skip_ref, *rest, with_gate):
    y = (jnp.dot(f_ref[0], zr_ref[0], preferred_element_type=F32)
         + jnp.dot(f_ref[1], zi_ref[0], preferred_element_type=F32))
    z = m_ref[0] * (y + u_ref[0] * skip_ref[...])
    if with_gate:
        g_ref, o_ref = rest
        o_ref[0] = (z * _silu(g_ref[0])).astype(o_ref.dtype)
    else:
        o_ref, ob_ref = rest
        o_ref[0] = z
        ob_ref[0] = z.astype(BF16)


def _dft_inv(inv, zr, zi, u, u_col0, mul, mul_col0, skip, gate=None, gate_col0=0, tm=512, tn=512):
    b, l, c = zr.shape
    tm = min(tm, l)
    win = lambda col0: pl.BlockSpec((1, tm, tn), lambda i, bb, j, o=col0 // tn: (bb, i, o + j))
    zspec = pl.BlockSpec((1, l, tn), lambda i, bb, j: (bb, 0, j))
    in_specs = [pl.BlockSpec((2, tm, l), lambda i, bb, j: (0, i, 0)), zspec, zspec,
                win(u_col0), win(mul_col0), pl.BlockSpec((1, tn), lambda i, bb, j: (0, j))]
    args = [inv, zr, zi, u, mul, skip]
    ospec = pl.BlockSpec((1, tm, tn), lambda i, bb, j: (bb, i, j))
    if gate is not None:
        in_specs.append(win(gate_col0))
        args.append(gate)
        out_specs, out_shape = ospec, jax.ShapeDtypeStruct((b, l, c), BF16)
    else:
        out_specs = [ospec, ospec]
        out_shape = [jax.ShapeDtypeStruct((b, l, c), F32), jax.ShapeDtypeStruct((b, l, c), BF16)]
    return pl.pallas_call(
        functools.partial(_dft_inv_kernel, with_gate=gate is not None),
        grid=(l // tm, b, c // tn),
        in_specs=in_specs,
        out_specs=out_specs,
        out_shape=out_shape,
        compiler_params=_cparams("parallel", "parallel", "parallel"),
        name="dft_inv",
    )(*args)


def _hyena(proj3, p, dft):
    l = proj3.shape[1]
    fwd, inv = dft
    filt = _hy_filter(l, p['hy_w1'], p['hy_b1'], p['hy_w2'], p['hy_b2'], p['hy_w3'], p['hy_b3'], p['hy_decay'])
    filt_b = filt.astype(BF16)[None]
    spec_h = _dft_fwd(fwd, filt_b, 0, 2 * BR_W)
    pre, pre_b = _hy_pre(proj3, p['hy_conv'])
    skip = p['hy_skip'].astype(F32)
    zr, zi = _dft_fwd(fwd, pre_b, 0, BR_W, spec_h, 0)
    z1, z1_b = _dft_inv(inv, zr, zi, pre, 0, pre, BR_W, skip[0:1])
    zr, zi = _dft_fwd(fwd, z1_b, 0, BR_W, spec_h, BR_W)
    return _dft_inv(inv, zr, zi, z1, 0, pre, 2 * BR_W, skip[1:2], gate=proj3, gate_col0=7 * BR_W)


def _softplus(x):
    return jnp.maximum(x, 0.0) + jnp.log1p(jnp.exp(-jnp.abs(x)))


def _split_bf16(x, parts):
    out = []
    for _ in range(parts - 1):
        piece = x.astype(BF16)
        out.append(piece)
        x = x - piece.astype(F32)
    out.append(x.astype(BF16))
    return out


def _bmm(a, b, hi=False):
    mm = lambda x, y: jnp.einsum('nij,njk->nik', x, y, preferred_element_type=F32)
    if not hi:
        return mm(a.astype(BF16), b.astype(BF16))
    (a1, a2), (b1, b2) = _split_bf16(a, 2), _split_bf16(b, 2)
    return mm(a1, b1) + (mm(a1, b2) + mm(a2, b1))


def _bmm_nt(a, b):
    return jnp.einsum('nid,njd->nij', a.astype(BF16), b.astype(BF16), preferred_element_type=F32)


def _unit_tri_inverse(a, blk16, blk32):
    eye = (lax.broadcasted_iota(jnp.int32, a.shape, 1) == lax.broadcasted_iota(jnp.int32, a.shape, 2)).astype(F32)
    x = -jnp.where(blk16, a, 0.0)
    p = eye + x
    for _ in range(3):
        x = _bmm(x, x, hi=True)
        p = p + _bmm(p, x, hi=True)
    for off in (jnp.where(blk32 & ~blk16, a, 0.0), jnp.where(blk32, 0.0, a)):
        p = p - _bmm(p, _bmm(off, p, hi=True), hi=True)
    return p


def _gdn_prepare(q, k, v, ab, a_row, dt_row, h):
    n, c, _ = q.shape
    two = lambda x: jnp.concatenate([x, x], axis=0)
    q, k, v, ab = two(q), two(k), two(v), two(ab)
    back3 = lambda shape: lax.broadcasted_iota(jnp.int32, shape, 0) >= n
    lane = lax.broadcasted_iota(jnp.int32, ab.shape, 2)
    base = jnp.where(back3(ab.shape), 2 * N_HEAD, 0) + h
    g_all = -a_row * _softplus(ab + dt_row)
    g = jnp.sum(jnp.where(lane == base, g_all, 0.0), axis=2, keepdims=True)
    beta = jnp.sum(jnp.where(lane == base + N_HEAD, _sigmoid(ab), 0.0), axis=2, keepdims=True)

    sq = (2 * n, c, c)
    ri = lax.broadcasted_iota(jnp.int32, sq, 1)
    ci = lax.broadcasted_iota(jnp.int32, sq, 2)
    ahead = jnp.where(back3(sq), ci - ri, ri - ci)
    incl = ahead >= 0
    strict = ahead > 0
    tri = jnp.where(incl, 1.0, 0.0).astype(BF16)
    gc = sum(jnp.einsum('nij,njk->nik', tri, piece, preferred_element_type=F32)
             for piece in _split_bf16(jnp.broadcast_to(g, q.shape), 3))
    gc_row = jnp.swapaxes(gc, 1, 2)[:, :c, :]
    total = jnp.where(back3((2 * n, 1, D_HEAD)), gc[:, 0:1, :], gc[:, c - 1:c, :])
    decay = jnp.where(incl, jnp.exp(jnp.where(incl, gc[:, :, :c] - gc_row, 0.0)), 0.0)

    kb = k * beta
    a = jnp.where(strict, _bmm_nt(kb, k) * decay, 0.0)
    t = _unit_tri_inverse(a, (ri // 16) == (ci // 16), (ri // 32) == (ci // 32))
    e = jnp.exp(gc)
    u = _bmm(t, v * beta, hi=True)
    w = _bmm(t, kb * e, hi=True)
    a_intra = jnp.where(incl, _bmm_nt(q, k) * decay, 0.0)
    return (u, w.astype(BF16), (q * e).astype(BF16), (k * jnp.exp(total - gc)).astype(BF16),
            a_intra.astype(BF16), jnp.exp(total))


def _gdn_kernel(*refs, has_s0, group):
    if has_s0:
        (q_ref, k_ref, v_ref, z_ref, ab_ref, wq_ref, wk_ref, wv_ref, arow_ref, dt_ref, gn_ref, s0_ref,
         y_ref, sf_ref, qn, kn, vn, u_s, w_s, qd_s, kd_s, ai_s, gl_s) = refs
    else:
        (q_ref, k_ref, v_ref, z_ref, ab_ref, wq_ref, wk_ref, wv_ref, arow_ref, dt_ref, gn_ref,
         y_ref, sf_ref, qn, kn, vn, u_s, w_s, qd_s, kd_s, ai_s, gl_s) = refs
    l = q_ref.shape[1]
    heads = q_ref.shape[2] // D_HEAD
    head0 = pl.program_id(1) * heads
    n_chunks = l // CHUNK
    hcols = lambda hh: slice(hh * D_HEAD, (hh + 1) * D_HEAD)

    def l2n(x):
        return x * lax.rsqrt(jnp.sum(x * x, axis=-1, keepdims=True) + EPS)

    for hh in range(heads):
        cols = hcols(hh)
        qn[:, cols] = l2n(_silu(_dwconv3(q_ref[0, :, cols], wq_ref.at[:, cols]))) * (D_HEAD ** -0.5)
        kn[:, cols] = l2n(_silu(_dwconv3(k_ref[0, :, cols], wk_ref.at[:, cols])))
        vn[:, cols] = _silu(_dwconv3(v_ref[0, :, cols], wv_ref.at[:, cols]))

    a_row, dt_row = arow_ref[...], dt_ref[...]

    def prepare(gi, carry):
        span = group * CHUNK
        rows = pl.ds(pl.multiple_of(gi * span, span), span)
        chunks = lambda x: x.reshape(group, CHUNK, x.shape[-1])
        ab = chunks(ab_ref[0, rows, :])
        for hh in range(heads):
            cols = hcols(hh)
            u, w, qd, kd, ai, gl = _gdn_prepare(chunks(qn[rows, cols]), chunks(kn[rows, cols]),
                                                chunks(vn[rows, cols]), ab, a_row, dt_row, head0 + hh)
            for d in range(2):
                part = slice(d * group, (d + 1) * group)
                u_s[d, rows, cols] = u[part].reshape(span, D_HEAD)
                w_s[d, rows, cols] = w[part].reshape(span, D_HEAD)
                qd_s[d, rows, cols] = qd[part].reshape(span, D_HEAD)
                kd_s[d, rows, cols] = kd[part].reshape(span, D_HEAD)
                ai_s[d, hh, rows, :] = ai[part].reshape(span, CHUNK)
                gl_s[d, hh, pl.ds(gi * group, group)] = jnp.broadcast_to(gl[part], (group,) + gl_s.shape[3:])
        return carry

    lax.fori_loop(0, n_chunks // group, prepare, 0)

    def scan(i, carry):
        new = []
        for hh in range(heads):
            cols = hcols(hh)
            for d, chunk in ((0, i), (1, n_chunks - 1 - i)):
                rows = pl.ds(pl.multiple_of(chunk * CHUNK, CHUNK), CHUNK)
                s = carry[2 * hh + d]
                v_new = u_s[d, rows, cols] - _dot(w_s[d, rows, cols], s)
                o = _dot(qd_s[d, rows, cols], s) + _dot(ai_s[d, hh, rows, :], v_new)
                new.append(s * gl_s[d, hh, chunk][0:1, :] + _dot_tn(kd_s[d, rows, cols], v_new))
                u_s[d, rows, cols] = o
        return tuple(new)

    if has_s0:
        init = tuple(s0_ref[0, 0, d, hh] for hh in range(heads) for d in range(2))
    else:
        init = (jnp.zeros((D_HEAD, D_HEAD), F32),) * (2 * heads)
    final = lax.fori_loop(0, n_chunks, scan, init)
    for hh in range(heads):
        cols = hcols(hh)
        sf_ref[0, 0, hh] = final[2 * hh]
        sf_ref[0, 1, hh] = final[2 * hh + 1]
        y_ref[0, :, cols] = (_rms(u_s[0, :, cols] + u_s[1, :, cols], gn_ref[...])
                             * _silu(z_ref[0, :, cols])).astype(y_ref.dtype)


def _gdn(proj3, ab3, conv_w, a_log, dt_bias, norm_g, state=None, layer=0):
    b, l, _ = proj3.shape
    lanes = jnp.zeros((2, 2 * N_HEAD), F32).at[:, :N_HEAD].set(1.0)
    a_row = jnp.pad((jnp.exp(a_log.astype(F32))[:, None, :] * lanes.reshape(2, 2, N_HEAD)).reshape(1, -1),
                    ((0, 0), (0, AB_PAD - 4 * N_HEAD)))
    dt_row = jnp.pad((dt_bias.astype(F32)[:, None, :] * lanes.reshape(2, 2, N_HEAD)).reshape(1, -1),
                     ((0, 0), (0, AB_PAD - 4 * N_HEAD)))
    hps = N_HEAD if l <= 512 else 1
    wid = hps * D_HEAD
    n_hb = N_HEAD // hps
    blk = lambda c: pl.BlockSpec((1, l, wid), lambda i, h, c=c: (i, 0, c * n_hb + h))
    wblk = lambda c: pl.BlockSpec((3, wid), lambda i, h, c=c: (0, c * n_hb + h))
    row = pl.BlockSpec((1, D_HEAD), lambda i, h: (0, 0))
    in_specs = [blk(0), blk(1), blk(2), blk(3),
                pl.BlockSpec((1, l, AB_PAD), lambda i, h: (i, 0, 0)),
                wblk(0), wblk(1), wblk(2), row, row, row]
    args = [proj3, proj3, proj3, proj3, ab3, conv_w, conv_w, conv_w, a_row, dt_row, norm_g]
    if state is not None:
        in_specs.append(pl.BlockSpec((1, 1, 2, hps, D_HEAD, D_HEAD), lambda i, h: (i, layer, 0, h, 0, 0)))
        args.append(state)
    return pl.pallas_call(
        functools.partial(_gdn_kernel, has_s0=state is not None, group=4),
        grid=(b, n_hb),
        in_specs=in_specs,
        out_specs=[pl.BlockSpec((1, l, wid), lambda i, h: (i, 0, h)),
                   pl.BlockSpec((1, 2, hps, D_HEAD, D_HEAD), lambda i, h: (i, 0, h, 0, 0))],
        out_shape=[jax.ShapeDtypeStruct((b, l, BR_W), BF16),
                   jax.ShapeDtypeStruct((b, 2, N_HEAD, D_HEAD, D_HEAD), F32)],
        scratch_shapes=[pltpu.VMEM((l, wid), F32)] * 3
        + [pltpu.VMEM((2, l, wid), F32)] + [pltpu.VMEM((2, l, wid), BF16)] * 3
        + [pltpu.VMEM((2, hps, l, CHUNK), BF16), pltpu.VMEM((2, hps, l // CHUNK, 8, D_HEAD), F32)],
        compiler_params=_cparams("parallel", "parallel"),
        name="gdn",
    )(*args)


def _mod_kernel(c_ref, w_ref, b_ref, o_ref):
    o_ref[...] = _dot_hi(_silu(c_ref[...]), w_ref[...]) + b_ref[...]


def _modulation(cond, w_mod, b_mod, tn=512):
    n = cond.shape[0]
    rows = 8
    out = pl.pallas_call(
        _mod_kernel,
        grid=(3 * D_MODEL // tn,),
        in_specs=[pl.BlockSpec((rows, D_MODEL), lambda j: (0, 0)),
                  pl.BlockSpec((D_MODEL, tn), lambda j: (0, j)),
                  pl.BlockSpec((1, tn), lambda j: (0, j))],
        out_specs=pl.BlockSpec((rows, tn), lambda j: (0, j)),
        out_shape=jax.ShapeDtypeStruct((rows, 3 * D_MODEL), F32),
        compiler_params=_cparams("parallel"),
        name="modulation",
    )(jnp.pad(cond.astype(F32), ((0, rows - n), (0, 0))), w_mod, b_mod.reshape(1, -1))
    return out[:n].reshape(n, 3, D_MODEL)


def _split_w_in(w_in):
    n_a = 4 * BR_W + 4 * N_HEAD
    main = jnp.concatenate([w_in[:, :4 * BR_W], w_in[:, n_a:]], axis=1).astype(BF16)
    ab = jnp.pad(w_in[:, 4 * BR_W:n_a], ((0, 0), (0, AB_PAD - 4 * N_HEAD))).astype(BF16)
    return main, ab


def _trunk_layer(x3, cond, p, layer, dft, latent, new_caches=(None, None)):
    b, l, _ = x3.shape
    x2 = x3.reshape(b * l, D_MODEL)
    mod = _modulation(cond, p['w_mod'], p['b_mod'])
    rows_per_mod = l if mod.shape[0] == b else b * l
    g_pre = p['g_pre'].reshape(1, D_MODEL)
    w_main, w_ab = _split_w_in(p['w_in'])
    proj, ab = _inproj(x2, mod, g_pre, w_main, w_ab, rows_per_mod)
    proj3 = proj.reshape(b, l, N_MAIN)
    ab3 = ab.reshape(b, l, AB_PAD)

    lam_init = 0.8 - 0.6 * math.exp(-0.3 * layer)
    lam_p = p['diff_lam'].astype(F32)
    lam = (jnp.exp(jnp.sum(lam_p[0] * lam_p[1])) - jnp.exp(jnp.sum(lam_p[2] * lam_p[3])) + lam_init).reshape(1, 1)
    diff_norm = p['diff_norm'].reshape(1, D_HEAD)
    gdn_norm = p['gdn_norm'].reshape(1, D_HEAD)

    yb = _hyena(proj3, p, dft)
    if latent is None:
        ya, s_fin = _gdn(proj3, ab3, p['gdn_conv'], p['gdn_a_log'], p['gdn_dt_bias'], gdn_norm)
        yc, yd, nat_kv, diff_kv = _ctx_attention(proj3, lam, diff_norm, lam_init, layer, *new_caches)
        extras = (s_fin, nat_kv, diff_kv)
    else:
        ya, _ = _gdn(proj3, ab3, p['gdn_conv'], p['gdn_a_log'], p['gdn_dt_bias'], gdn_norm,
                     latent['state_gdn'], layer)
        yc = _lat_nat(proj3, latent['cache_nat_kv'], layer, _nat_bias_table(p['nat_rpb']))
        yd = _lat_diff(proj3, latent['cache_diff_kv'], layer, lam, diff_norm, lam_init, latent['rope'])
        extras = None

    ys = [t.reshape(b * l, BR_W) for t in (ya, yb, yc, yd)]
    out = _merge(x2, mod, g_pre, p['g_post'].reshape(1, D_MODEL), ys,
                 p['w_branch'].astype(BF16), p['w_merge'].astype(BF16),
                 p['b_merge'].reshape(1, -1).astype(F32), p['w_out'].astype(BF16), rows_per_mod)
    return out.reshape(b, l, D_MODEL), extras


def kernel(x_prompt, x_sample, state_gdn, cache_nat_kv, cache_diff_kv, c, c_ctx,
           w_mod, b_mod, g_pre, g_post, w_in, gdn_conv, gdn_a_log, gdn_dt_bias, gdn_norm,
           hy_conv, hy_w1, hy_b1, hy_w2, hy_b2, hy_w3, hy_b3, hy_decay, hy_skip,
           nat_rpb, diff_lam, diff_norm, w_branch, w_merge, b_merge, w_out):
    stacked = {
        'w_mod': w_mod, 'b_mod': b_mod, 'g_pre': g_pre, 'g_post': g_post, 'w_in': w_in,
        'gdn_conv': gdn_conv, 'gdn_a_log': gdn_a_log, 'gdn_dt_bias': gdn_dt_bias, 'gdn_norm': gdn_norm,
        'hy_conv': hy_conv, 'hy_w1': hy_w1, 'hy_b1': hy_b1, 'hy_w2': hy_w2, 'hy_b2': hy_b2,
        'hy_w3': hy_w3, 'hy_b3': hy_b3, 'hy_decay': hy_decay, 'hy_skip': hy_skip,
        'nat_rpb': nat_rpb, 'diff_lam': diff_lam, 'diff_norm': diff_norm,
        'w_branch': w_branch, 'w_merge': w_merge, 'b_merge': b_merge, 'w_out': w_out,
    }
    layers = [{name: arr[i] for name, arr in stacked.items()} for i in range(DEPTH)]

    y_prompt = x_prompt
    dft_ctx = _dft_matrices(x_prompt.shape[1])
    states, nat_cache, diff_cache = [], None, None
    for i, p in enumerate(layers):
        y_prompt, (s_l, nat_cache, diff_cache) = _trunk_layer(
            y_prompt, c_ctx.reshape(1, D_MODEL), p, i, dft_ctx, None, (nat_cache, diff_cache))
        states.append(s_l)

    y_sample = x_sample
    dft_lat = _dft_matrices(x_sample.shape[1])
    latent = {'state_gdn': state_gdn, 'cache_nat_kv': cache_nat_kv, 'cache_diff_kv': cache_diff_kv,
              'rope': _rope_tables(x_sample.shape[1])}
    for i, p in enumerate(layers):
        y_sample, _ = _trunk_layer(y_sample, c, p, i, dft_lat, latent)

    return (y_prompt, y_sample, jnp.stack(states, axis=1), nat_cache, diff_cache)
```

```python
import functools
import math

import jax
import jax.numpy as jnp
import numpy as np
from jax import lax
from jax.experimental import pallas as pl
from jax.experimental.pallas import tpu as pltpu

F32 = jnp.float32
BF16 = jnp.bfloat16

D_MODEL = 1024
DEPTH = 2
GRID_W = 64
N_BRANCH = 4
BR_W = 512
N_HEAD = 4
D_HEAD = 128
SUBLANES = 8
CHUNK = 64
HY_BANDS = 16
WIN_R = 8
WIN_C = 16
DQK_D = 64
ROPE_BASE = 10000.0
EPS = 1e-6
N_MAIN = 4 * 4 * BR_W
AB_PAD = 128
NEG_INF = -1e30

VMEM_LIMIT = 48 * 1024 * 1024


def _cparams(*sem):
    return pltpu.CompilerParams(dimension_semantics=sem, vmem_limit_bytes=VMEM_LIMIT)


def _silu(x):
    return x * (1.0 / (1.0 + jnp.exp(-x)))


def _sigmoid(x):
    return 1.0 / (1.0 + jnp.exp(-x))


def _rms(x, g):
    return x * lax.rsqrt(jnp.mean(x * x, axis=-1, keepdims=True) + EPS) * g


def _dot(a, b):
    return jnp.dot(a.astype(BF16), b.astype(BF16), preferred_element_type=F32)


def _dot_nt(a, b):
    return lax.dot_general(a.astype(BF16), b.astype(BF16), (((1,), (1,)), ((), ())),
                           preferred_element_type=F32)


def _dot_tn(a, b):
    return lax.dot_general(a.astype(BF16), b.astype(BF16), (((0,), (0,)), ((), ())),
                           preferred_element_type=F32)


def _prenorm(x, g_pre, mod_ref):
    return _rms(x, g_pre) * (1.0 + mod_ref[0, 1:2, :]) + mod_ref[0, 0:1, :]


def _inproj_kernel(x_ref, mod_ref, gpre_ref, w_ref, wab_ref, proj_ref, ab_ref, h_scr):
    @pl.when(pl.program_id(1) == 0)
    def _():
        h = _prenorm(x_ref[...], gpre_ref[...], mod_ref).astype(BF16)
        h_scr[...] = h
        ab_ref[...] = jnp.dot(h, wab_ref[...], preferred_element_type=F32)

    proj_ref[...] = jnp.dot(h_scr[...], w_ref[...], preferred_element_type=F32)


def _inproj(x2, mod, g_pre, w_main, w_ab, layer, rows_per_mod, tm=1024, tn=1024):
    m = x2.shape[0]
    return pl.pallas_call(
        _inproj_kernel,
        grid=(m // tm, N_MAIN // tn),
        in_specs=[
            pl.BlockSpec((tm, D_MODEL), lambda i, j: (i, 0)),
            pl.BlockSpec((1, 3, D_MODEL), lambda i, j: ((i * tm) // rows_per_mod, 0, 0)),
            pl.BlockSpec((1, D_MODEL), lambda i, j: (0, 0)),
            pl.BlockSpec((None, D_MODEL, tn), lambda i, j: (layer, 0, j)),
            pl.BlockSpec((None, D_MODEL, AB_PAD), lambda i, j: (layer, 0, 0)),
        ],
        out_specs=[
            pl.BlockSpec((tm, tn), lambda i, j: (i, j)),
            pl.BlockSpec((tm, AB_PAD), lambda i, j: (i, 0)),
        ],
        out_shape=[jax.ShapeDtypeStruct((m, N_MAIN), F32),
                   jax.ShapeDtypeStruct((m, AB_PAD), F32)],
        scratch_shapes=[pltpu.VMEM((tm, D_MODEL), BF16)],
        compiler_params=_cparams("parallel", "arbitrary"),
        name="inproj",
    )(x2, mod, g_pre, w_main, w_ab)


def _merge_kernel(x_ref, mod_ref, gpre_ref, gpost_ref, ya_ref, yb_ref, yc_ref, yd_ref,
                  wbr_ref, wmg_ref, bmg_ref, wout_ref, o_ref):
    x = x_ref[...]
    h = _prenorm(x, gpre_ref[...], mod_ref).astype(BF16)
    acc = None
    for k, y_ref in enumerate((ya_ref, yb_ref, yc_ref, yd_ref)):
        cols = slice(k * D_MODEL, (k + 1) * D_MODEL)
        gate = _sigmoid(jnp.dot(h, wmg_ref[:, cols], preferred_element_type=F32) + bmg_ref[:, cols])
        br = jnp.dot(y_ref[...], wbr_ref[k], preferred_element_type=F32)
        acc = gate * br if acc is None else acc + gate * br
    y = jnp.dot(acc.astype(BF16), wout_ref[...], preferred_element_type=F32)
    o_ref[...] = x + mod_ref[0, 2:3, :] * _rms(y, gpost_ref[...])


def _merge(x2, mod, g_pre, g_post, ys, w_branch, w_merge, b_merge, w_out, layer, rows_per_mod, tm=256):
    m = x2.shape[0]
    row = lambda i: (i, 0)
    fixed2 = lambda i: (0, 0)
    return pl.pallas_call(
        _merge_kernel,
        grid=(m // tm,),
        in_specs=[
            pl.BlockSpec((tm, D_MODEL), row),
            pl.BlockSpec((1, 3, D_MODEL), lambda i: ((i * tm) // rows_per_mod, 0, 0)),
            pl.BlockSpec((1, D_MODEL), fixed2),
            pl.BlockSpec((1, D_MODEL), fixed2),
            pl.BlockSpec((tm, BR_W), row),
            pl.BlockSpec((tm, BR_W), row),
            pl.BlockSpec((tm, BR_W), row),
            pl.BlockSpec((tm, BR_W), row),
            pl.BlockSpec((None, N_BRANCH, BR_W, D_MODEL), lambda i: (layer, 0, 0, 0)),
            pl.BlockSpec((None, D_MODEL, N_BRANCH * D_MODEL), lambda i: (layer, 0, 0)),
            pl.BlockSpec((1, N_BRANCH * D_MODEL), fixed2),
            pl.BlockSpec((None, D_MODEL, D_MODEL), lambda i: (layer, 0, 0)),
        ],
        out_specs=pl.BlockSpec((tm, D_MODEL), row),
        out_shape=jax.ShapeDtypeStruct((m, D_MODEL), F32),
        compiler_params=_cparams("parallel"),
        name="merge",
    )(x2, mod, g_pre, g_post, *ys, w_branch, w_merge, b_merge, w_out)


def _softmax_rows(s):
    p = jnp.exp(s - jnp.max(s, axis=-1, keepdims=True))
    return p, jnp.sum(p, axis=-1, keepdims=True)


def _ctx_nat_kernel(*refs, aliased):
    q_ref, k_ref, v_ref, g_ref, y_ref, kv_ref = refs[1:] if aliased else refs
    scale = D_HEAD ** -0.5
    for h in range(N_HEAD):
        sl = slice(h * D_HEAD, (h + 1) * D_HEAD)
        k = k_ref[0, :, sl]
        v = v_ref[0, :, sl]
        p, l = _softmax_rows(_dot_nt(q_ref[0, :, sl], k) * scale)
        o = _dot(p, v) / l
        y_ref[0, :, sl] = (o * _silu(g_ref[0, :, sl])).astype(y_ref.dtype)
        kv_ref[0, 0, 0, h] = k
        kv_ref[0, 0, 1, h] = v


def _map_masks():
    lane = lax.broadcasted_iota(jnp.int32, (1, D_HEAD), 1)
    first = (lane < DQK_D).astype(F32)
    return first, 1.0 - first


def _ctx_diff_kernel(*refs, aliased, out_scale):
    lam_ref, q_ref, k_ref, v_ref, g_ref, gn_ref, y_ref, kv_ref = refs[1:] if aliased else refs
    scale = DQK_D ** -0.5
    m1, m2 = _map_masks()
    lam = lam_ref[...]
    for h in range(N_HEAD):
        sl = slice(h * D_HEAD, (h + 1) * D_HEAD)
        q = q_ref[0, :, sl]
        k = k_ref[0, :, sl]
        v = v_ref[0, :, sl]
        p1, l1 = _softmax_rows(_dot_nt(q * m1, k) * scale)
        p2, l2 = _softmax_rows(_dot_nt(q * m2, k) * scale)
        a = p1 / l1 - lam * (p2 / l2)
        o = _rms(_dot(a, v), gn_ref[...]) * out_scale
        y_ref[0, :, sl] = (o * _silu(g_ref[0, :, sl])).astype(y_ref.dtype)
        kv_ref[0, 0, 0, h] = k
        kv_ref[0, 0, 1, h] = v


def _ctx_attention(proj3, lam, diff_norm, lam_init, layer, nat_cache, diff_cache):
    b, l, _ = proj3.shape
    blk = lambda c: pl.BlockSpec((1, l, BR_W), lambda i, c=c: (i, 0, c))
    y_spec = pl.BlockSpec((1, l, BR_W), lambda i: (i, 0, 0))
    kv_spec = pl.BlockSpec((1, 1, 2, N_HEAD, l, D_HEAD), lambda i: (i, layer, 0, 0, 0, 0))
    out_shape = [jax.ShapeDtypeStruct((b, l, BR_W), BF16),
                 jax.ShapeDtypeStruct((b, DEPTH, 2, N_HEAD, l, D_HEAD), F32)]
    aliased = nat_cache is not None
    cache_specs = [pl.BlockSpec(memory_space=pl.ANY)] if aliased else []
    aliases = {0: 1} if aliased else {}
    yc, nat_cache = pl.pallas_call(
        functools.partial(_ctx_nat_kernel, aliased=aliased),
        grid=(b,),
        in_specs=cache_specs + [blk(8), blk(9), blk(10), blk(11)],
        out_specs=[y_spec, kv_spec],
        out_shape=out_shape,
        input_output_aliases=aliases,
        compiler_params=_cparams("parallel"),
        name="ctx_nat",
    )(*([nat_cache] if aliased else []), proj3, proj3, proj3, proj3)
    yd, diff_cache = pl.pallas_call(
        functools.partial(_ctx_diff_kernel, aliased=aliased, out_scale=1.0 - lam_init),
        grid=(b,),
        in_specs=cache_specs + [pl.BlockSpec((1, 1), lambda i: (0, 0)),
                                blk(12), blk(13), blk(14), blk(15),
                                pl.BlockSpec((1, D_HEAD), lambda i: (0, 0))],
        out_specs=[y_spec, kv_spec],
        out_shape=out_shape,
        input_output_aliases=aliases,
        compiler_params=_cparams("parallel"),
        name="ctx_diff",
    )(*([diff_cache] if aliased else []), lam, proj3, proj3, proj3, proj3, diff_norm)
    return yc, yd, nat_cache, diff_cache


def _nat_bias_table(rpb):
    cols = np.arange(GRID_W)
    start = np.clip(cols - WIN_C // 2, 0, GRID_W - WIN_C)
    inside = (cols[None, :] >= start[:, None]) & (cols[None, :] < start[:, None] + WIN_C)
    dc = cols[None, :] - cols[:, None] + (WIN_C - 1)
    onehot = ((dc[None] == np.arange(2 * WIN_C - 1)[:, None, None]) & inside[None]).astype(np.float32)
    t = jnp.einsum('hdx,xck->hdck', rpb.astype(F32), jnp.asarray(onehot), precision=lax.Precision.HIGHEST)
    t = jnp.where(jnp.asarray(inside)[None, None], t, NEG_INF)
    tab = jnp.stack([t[:, WIN_R - 1 - off:2 * WIN_R - 1 - off] for off in range(WIN_R)], axis=1)
    return tab.transpose(0, 1, 3, 2, 4).reshape(rpb.shape[0], WIN_R, GRID_W, WIN_R * GRID_W)


def _lat_nat_kernel(q_ref, k_ref, v_ref, g_ref, ckv_ref, bias_ref, y_ref, kb_scr, vb_scr, *, rb):
    scale = D_HEAD ** -0.5
    rows = q_ref.shape[1] // GRID_W
    win = WIN_R * GRID_W
    kb_scr[...] = k_ref[0].astype(BF16)
    vb_scr[...] = v_ref[0].astype(BF16)
    ck = ckv_ref[0, 0, 0, 0].astype(BF16)
    cv = ckv_ref[0, 0, 1, 0].astype(BF16)

    def row_block(i, carry):
        q0 = pl.multiple_of(i * (rb * GRID_W), rb * GRID_W)
        qrows = pl.ds(q0, rb * GRID_W)
        q = q_ref[0, qrows, :].astype(BF16)
        kw, vw, bias = [], [], []
        for j in range(rb):
            r = i * rb + j
            rs = jnp.clip(r - WIN_R // 2, 0, rows - WIN_R)
            wrows = pl.ds(pl.multiple_of(rs * GRID_W, GRID_W), win)
            kw.append(kb_scr[wrows, :])
            vw.append(vb_scr[wrows, :])
            bias.append(bias_ref[0, r - rs])
        q3 = q.reshape(rb, GRID_W, D_HEAD)
        s_lat = _bmm_nt(q3, jnp.stack(kw)) * scale + jnp.stack(bias)
        s_ctx = (_dot_nt(q, ck) * scale).reshape(rb, GRID_W, ck.shape[0])
        m = jnp.maximum(jnp.max(s_lat, axis=-1, keepdims=True), jnp.max(s_ctx, axis=-1, keepdims=True))
        p_lat = jnp.exp(s_lat - m)
        p_ctx = jnp.exp(s_ctx - m)
        l = jnp.sum(p_lat, axis=-1, keepdims=True) + jnp.sum(p_ctx, axis=-1, keepdims=True)
        o_ctx = _dot(p_ctx.reshape(rb * GRID_W, ck.shape[0]), cv).reshape(rb, GRID_W, D_HEAD)
        o = ((_bmm(p_lat, jnp.stack(vw)) + o_ctx) / l).reshape(rb * GRID_W, D_HEAD)
        y_ref[0, qrows, :] = (o * _silu(g_ref[0, qrows, :])).astype(y_ref.dtype)
        return carry

    lax.fori_loop(0, rows // rb, row_block, 0)


def _lat_nat(proj3, cache_nat_kv, layer, bias_tab):
    b, l, _ = proj3.shape
    past = cache_nat_kv.shape[4]
    blk = lambda c: pl.BlockSpec((1, l, D_HEAD), lambda i, h, c=c: (i, 0, c + h))
    return pl.pallas_call(
        functools.partial(_lat_nat_kernel, rb=8),
        grid=(b, N_HEAD),
        in_specs=[blk(32), blk(36), blk(40), blk(44),
                  pl.BlockSpec((1, 1, 2, 1, past, D_HEAD), lambda i, h: (i, layer, 0, h, 0, 0)),
                  pl.BlockSpec((1, WIN_R, GRID_W, WIN_R * GRID_W), lambda i, h: (h, 0, 0, 0))],
        out_specs=pl.BlockSpec((1, l, D_HEAD), lambda i, h: (i, 0, h)),
        out_shape=jax.ShapeDtypeStruct((b, l, BR_W), BF16),
        scratch_shapes=[pltpu.VMEM((l, D_HEAD), BF16), pltpu.VMEM((l, D_HEAD), BF16)],
        compiler_params=_cparams("parallel", "parallel"),
        name="lat_nat",
    )(proj3, proj3, proj3, proj3, cache_nat_kv, bias_tab)


def _rope_tables(l):
    half = DQK_D // 2
    nf = half // 2
    t = jnp.arange(l)
    row = (t // GRID_W).astype(F32)
    col = (t % GRID_W).astype(F32)
    inv = ROPE_BASE ** (-jnp.arange(nf, dtype=F32) / nf)
    ang = jnp.concatenate([row[:, None] * inv, col[:, None] * inv], axis=-1)
    cos, sin = jnp.cos(ang), jnp.sin(ang)
    zero = jnp.zeros_like(sin)
    tile2 = lambda a, b: jnp.concatenate([a, b, a, b], axis=-1)
    return tile2(cos, cos), tile2(-sin, zero), tile2(zero, sin)


def _rope(x, cos, sin_a, sin_b):
    return x * cos + pltpu.roll(x, 96, 1) * sin_a + pltpu.roll(x, 32, 1) * sin_b


def _lat_diff_kernel(lam_ref, q_ref, k_ref, v_ref, g_ref, ckv_ref, gn_ref,
                     cq_ref, saq_ref, sbq_ref, ck_ref, sak_ref, sbk_ref,
                     y_ref, ks_scr, vs_scr, *, out_scale, prep_rows, key_block):
    scale = DQK_D ** -0.5
    l = k_ref.shape[1]

    @pl.when(pl.program_id(2) == 0)
    def _():
        def prep(i, carry):
            rows = pl.ds(pl.multiple_of(i * prep_rows, prep_rows), prep_rows)
            kr = _rope(k_ref[0, rows, :], ck_ref[rows, :], sak_ref[rows, :], sbk_ref[rows, :])
            ks_scr[rows, :] = kr.astype(BF16)
            vs_scr[rows, :] = v_ref[0, rows, :].astype(BF16)
            return carry

        lax.fori_loop(0, l // prep_rows, prep, 0)
        ks_scr[l:, :] = ckv_ref[0, 0, 0, 0].astype(BF16)
        vs_scr[l:, :] = ckv_ref[0, 0, 1, 0].astype(BF16)

    q = _rope(q_ref[0], cq_ref[...], saq_ref[...], sbq_ref[...]) * scale
    m1, m2 = _map_masks()
    tq = q.shape[0]
    qm = jnp.concatenate([q * m1, q * m2], axis=0).astype(BF16)
    m = l_sum = acc = None
    for blk in range(ks_scr.shape[0] // key_block):
        rows = slice(blk * key_block, (blk + 1) * key_block)
        s = _dot_nt(qm, ks_scr[rows, :])
        m_blk = jnp.max(s, axis=-1, keepdims=True)
        if blk == 0:
            m = m_blk
            p = jnp.exp(s - m)
            l_sum = jnp.sum(p, axis=-1, keepdims=True)
            acc = _dot(p, vs_scr[rows, :])
        else:
            m_new = jnp.maximum(m, m_blk)
            alpha = jnp.exp(m - m_new)
            p = jnp.exp(s - m_new)
            l_sum = alpha * l_sum + jnp.sum(p, axis=-1, keepdims=True)
            acc = alpha * acc + _dot(p, vs_scr[rows, :])
            m = m_new
    out = acc / l_sum
    o = _rms(out[:tq] - lam_ref[...] * out[tq:], gn_ref[...]) * out_scale
    y_ref[0] = (o * _silu(g_ref[0])).astype(y_ref.dtype)


def _lat_diff(proj3, cache_diff_kv, layer, lam, diff_norm, lam_init, rope_tabs, tq=256):
    b, l, _ = proj3.shape
    past = cache_diff_kv.shape[4]
    qblk = lambda c: pl.BlockSpec((1, tq, D_HEAD), lambda i, h, j, c=c: (i, j, c + h))
    full = lambda c: pl.BlockSpec((1, l, D_HEAD), lambda i, h, j, c=c: (i, 0, c + h))
    tq_tab = pl.BlockSpec((tq, D_HEAD), lambda i, h, j: (j, 0))
    full_tab = pl.BlockSpec((l, D_HEAD), lambda i, h, j: (0, 0))
    return pl.pallas_call(
        functools.partial(_lat_diff_kernel, out_scale=1.0 - lam_init, prep_rows=512, key_block=256),
        grid=(b, N_HEAD, l // tq),
        in_specs=[pl.BlockSpec((1, 1), lambda i, h, j: (0, 0)),
                  qblk(48), full(52), full(56), qblk(60),
                  pl.BlockSpec((1, 1, 2, 1, past, D_HEAD), lambda i, h, j: (i, layer, 0, h, 0, 0)),
                  pl.BlockSpec((1, D_HEAD), lambda i, h, j: (0, 0)),
                  tq_tab, tq_tab, tq_tab, full_tab, full_tab, full_tab],
        out_specs=pl.BlockSpec((1, tq, D_HEAD), lambda i, h, j: (i, j, h)),
        out_shape=jax.ShapeDtypeStruct((b, l, BR_W), BF16),
        scratch_shapes=[pltpu.VMEM((l + past, D_HEAD), BF16), pltpu.VMEM((l + past, D_HEAD), BF16)],
        compiler_params=_cparams("parallel", "parallel", "arbitrary"),
        name="lat_diff",
    )(lam, proj3, proj3, proj3, proj3, cache_diff_kv, diff_norm, *rope_tabs, *rope_tabs)


def _dwconv3(x, w_ref):
    l = x.shape[0]
    row = lax.broadcasted_iota(jnp.int32, x.shape, 0)
    prev = jnp.where(row == 0, 0.0, pltpu.roll(x, 1, 0))
    nxt = jnp.where(row == l - 1, 0.0, pltpu.roll(x, l - 1, 0))
    return prev * w_ref[0:1, :] + x * w_ref[1:2, :] + nxt * w_ref[2:3, :]


def _hy_pre_kernel(x_ref, above_ref, below_ref, w_ref, o_ref, ob_ref):
    t, n_t = pl.program_id(1), pl.num_programs(1)
    x = x_ref[0]
    rows = x.shape[0]
    row = lax.broadcasted_iota(jnp.int32, x.shape, 0)
    before = jnp.where(t == 0, 0.0, above_ref[0, SUBLANES - 1:SUBLANES, :])
    after = jnp.where(t == n_t - 1, 0.0, below_ref[0, 0:1, :])
    prev = jnp.where(row == 0, before, pltpu.roll(x, 1, 0))
    nxt = jnp.where(row == rows - 1, after, pltpu.roll(x, rows - 1, 0))
    y = prev * w_ref[0:1, :] + x * w_ref[1:2, :] + nxt * w_ref[2:3, :]
    o_ref[0] = y
    ob_ref[0] = y.astype(BF16)


def _hy_pre(proj3, conv_w, tl=512):
    b, l, _ = proj3.shape
    tl = min(tl, l)
    n = 3
    col0 = 4
    groups = tl // SUBLANES
    last_group = l // SUBLANES - 1
    spec = pl.BlockSpec((1, tl, BR_W), lambda i, t, j: (i, t, j))
    return pl.pallas_call(
        _hy_pre_kernel,
        grid=(b, l // tl, n),
        in_specs=[pl.BlockSpec((1, tl, BR_W), lambda i, t, j: (i, t, col0 + j)),
                  pl.BlockSpec((1, SUBLANES, BR_W),
                               lambda i, t, j: (i, jnp.maximum(t * groups - 1, 0), col0 + j)),
                  pl.BlockSpec((1, SUBLANES, BR_W),
                               lambda i, t, j: (i, jnp.minimum((t + 1) * groups, last_group), col0 + j)),
                  pl.BlockSpec((3, BR_W), lambda i, t, j: (0, j))],
        out_specs=[spec, spec],
        out_shape=[jax.ShapeDtypeStruct((b, l, 3 * BR_W), F32),
                   jax.ShapeDtypeStruct((b, l, 3 * BR_W), BF16)],
        compiler_params=_cparams("parallel", "parallel", "parallel"),
        name="hy_pre",
    )(proj3, proj3, proj3, conv_w)


def _dot_hi(a, b):
    return jnp.dot(a, b, preferred_element_type=F32, precision=lax.Precision.HIGHEST)


def _hy_filter_kernel(feat_ref, dist_ref, w1_ref, b1_ref, w2_ref, b2_ref, w3_ref, b3_ref, dec_ref, o_ref):
    hid = jnp.sin(_dot_hi(feat_ref[...], w1_ref[...]) + b1_ref[...])
    hid = jnp.sin(_dot_hi(hid, w2_ref[...]) + b2_ref[...])
    dist = dist_ref[...]
    for j in range(o_ref.shape[1] // D_HEAD):
        cols = slice(j * D_HEAD, (j + 1) * D_HEAD)
        filt = _dot_hi(hid, w3_ref[:, cols]) + b3_ref[:, cols]
        o_ref[:, cols] = (filt * jnp.exp(-dist * jnp.abs(dec_ref[:, cols]))).astype(o_ref.dtype)


def _hy_filter(l, w1, b1, w2, b2, w3, b3, decay):
    pos = jnp.arange(l, dtype=F32)
    t = pos / l
    ang = (2.0 * math.pi) * t[:, None] * jnp.arange(1, HY_BANDS + 1, dtype=F32)
    feat = jnp.concatenate([t[:, None], jnp.cos(ang), jnp.sin(ang)], axis=-1)
    dist = jnp.broadcast_to((jnp.abs(pos - l // 2) / l)[:, None], (l, D_HEAD))
    pad = D_HEAD
    emb, ff = w1.shape
    feat = jnp.pad(feat, ((0, 0), (0, pad - emb)))
    w1p = jnp.pad(w1, ((0, pad - emb), (0, pad - ff)))
    w2p = jnp.pad(w2, ((0, pad - ff), (0, pad - ff)))
    w3p = jnp.pad(w3, ((0, pad - ff), (0, 0)))
    b1p = jnp.pad(b1, (0, pad - ff)).reshape(1, pad)
    b2p = jnp.pad(b2, (0, pad - ff)).reshape(1, pad)
    tl = min(l, 256)
    n = 2 * BR_W
    fixed = lambda shape: pl.BlockSpec(shape, lambda i: (0, 0))
    return pl.pallas_call(
        _hy_filter_kernel,
        grid=(l // tl,),
        in_specs=[pl.BlockSpec((tl, pad), lambda i: (i, 0)),
                  pl.BlockSpec((tl, D_HEAD), lambda i: (i, 0)),
                  fixed((pad, pad)), fixed((1, pad)), fixed((pad, pad)), fixed((1, pad)),
                  fixed((pad, n)), fixed((1, n)), fixed((1, n))],
        out_specs=pl.BlockSpec((tl, n), lambda i: (i, 0)),
        out_shape=jax.ShapeDtypeStruct((l, n), BF16),
        compiler_params=_cparams("parallel"),
        name="hy_filter",
    )(feat, dist, w1p, b1p, w2p, b2p, w3p, b3.reshape(1, n), decay.reshape(1, n))


def _dft_matrices(l):
    n = 2 * l
    k = jnp.arange(l, dtype=jnp.int32)
    t = jnp.arange(l, dtype=jnp.int32)
    split = 1 << (max(l.bit_length() - 1, 0) // 2)

    def cos_sin(rows, cols):
        def table(r):
            ang = (2.0 * math.pi / n) * ((r[:, None] * cols[None, :]) % n).astype(F32)
            return jnp.cos(ang), jnp.sin(ang)
        lo = rows[:split] - rows[0]
        (ch, sh), (cl, sl) = table(rows[::split]), table(lo)
        c = ch[:, None, :] * cl[None, :, :] - sh[:, None, :] * sl[None, :, :]
        s = sh[:, None, :] * cl[None, :, :] + ch[:, None, :] * sl[None, :, :]
        return c.reshape(l, l), s.reshape(l, l)

    alt = jnp.where(t % 2 == 0, 1.0, -1.0).astype(F32)
    fc, fs = cos_sin(k, t)
    fs = jnp.where(k[:, None] == 0, alt[None, :], -fs)
    tp = t + l // 2
    wk = jnp.where(k == 0, 1.0, 2.0).astype(F32) / n
    alt_i = jnp.where(tp % 2 == 0, 1.0, -1.0).astype(F32) / n
    ic, is_ = cos_sin(tp, k)
    ic = ic * wk[None, :]
    is_ = jnp.where(k[None, :] == 0, alt_i[:, None], -is_ * wk[None, :])
    return (fc.astype(BF16), fs.astype(BF16)), (ic.astype(BF16), is_.astype(BF16))


def _dft_fwd_kernel(fc_ref, fs_ref, x_ref, *rest, with_filter, tm):
    x = x_ref[0]
    ur = jnp.dot(fc_ref[...], x, preferred_element_type=F32)
    ui = jnp.dot(fs_ref[...], x, preferred_element_type=F32)
    if not with_filter:
        zr_ref, zi_ref = rest
        zr_ref[0] = ur
        zi_ref[0] = ui
        return
    hr_ref, hi_ref, zr_ref, zi_ref = rest
    hr, hi = hr_ref[0], hi_ref[0]
    row0 = (lax.broadcasted_iota(jnp.int32, ur.shape, 0) + pl.program_id(0) * tm) == 0
    zr_ref[0] = (ur * hr - jnp.where(row0, 0.0, ui * hi)).astype(zr_ref.dtype)
    zi_ref[0] = jnp.where(row0, ui * hi, ur * hi + ui * hr).astype(zi_ref.dtype)


def _dft_fwd(fwd, x, x_col0, c, spec_h=None, h_col0=0, tm=512, tn=512):
    b, l, _ = x.shape
    tm = min(tm, l)
    xo, ho = x_col0 // tn, h_col0 // tn
    out_dtype = F32 if spec_h is None else BF16
    fspec = pl.BlockSpec((tm, l), lambda i, bb, j: (i, 0))
    in_specs = [fspec, fspec, pl.BlockSpec((1, l, tn), lambda i, bb, j: (bb, 0, xo + j))]
    args = [*fwd, x]
    if spec_h is not None:
        hspec = pl.BlockSpec((1, tm, tn), lambda i, bb, j: (0, i, ho + j))
        in_specs += [hspec, hspec]
        args += list(spec_h)
    ospec = pl.BlockSpec((1, tm, tn), lambda i, bb, j: (bb, i, j))
    return pl.pallas_call(
        functools.partial(_dft_fwd_kernel, with_filter=spec_h is not None, tm=tm),
        grid=(l // tm, b, c // tn),
        in_specs=in_specs,
        out_specs=[ospec, ospec],
        out_shape=[jax.ShapeDtypeStruct((b, l, c), out_dtype)] * 2,
        compiler_params=_cparams("parallel", "parallel", "parallel"),
        name="dft_fwd",
    )(*args)


def _dft_inv_kernel(ic_ref, is_ref, zr_ref, zi_ref, u_ref, m_ref, skip_ref, *rest, with_gate):
    y = (jnp.dot(ic_ref[...], zr_ref[0], preferred_element_type=F32)
         + jnp.dot(is_ref[...], zi_ref[0], preferred_element_type=F32))
    z = m_ref[0] * (y + u_ref[0] * skip_ref[...])
    if with_gate:
        g_ref, o_ref = rest
        o_ref[0] = (z * _silu(g_ref[0])).astype(o_ref.dtype)
    else:
        o_ref, ob_ref = rest
        o_ref[0] = z
        ob_ref[0] = z.astype(BF16)


def _dft_inv(inv, zr, zi, u, u_col0, mul, mul_col0, skip, gate=None, gate_col0=0, tm=512, tn=512):
    b, l, c = zr.shape
    tm = min(tm, l)
    win = lambda col0: pl.BlockSpec((1, tm, tn), lambda i, bb, j, o=col0 // tn: (bb, i, o + j))
    zspec = pl.BlockSpec((1, l, tn), lambda i, bb, j: (bb, 0, j))
    fspec = pl.BlockSpec((tm, l), lambda i, bb, j: (i, 0))
    in_specs = [fspec, fspec, zspec, zspec,
                win(u_col0), win(mul_col0), pl.BlockSpec((1, tn), lambda i, bb, j: (0, j))]
    args = [*inv, zr, zi, u, mul, skip]
    ospec = pl.BlockSpec((1, tm, tn), lambda i, bb, j: (bb, i, j))
    if gate is not None:
        in_specs.append(win(gate_col0))
        args.append(gate)
        out_specs, out_shape = ospec, jax.ShapeDtypeStruct((b, l, c), BF16)
    else:
        out_specs = [ospec, ospec]
        out_shape = [jax.ShapeDtypeStruct((b, l, c), F32), jax.ShapeDtypeStruct((b, l, c), BF16)]
    return pl.pallas_call(
        functools.partial(_dft_inv_kernel, with_gate=gate is not None),
        grid=(l // tm, b, c // tn),
        in_specs=in_specs,
        out_specs=out_specs,
        out_shape=out_shape,
        compiler_params=_cparams("parallel", "parallel", "parallel"),
        name="dft_inv",
    )(*args)


def _hyena(proj3, p, dft):
    l = proj3.shape[1]
    fwd, inv = dft
    filt = _hy_filter(l, p['hy_w1'], p['hy_b1'], p['hy_w2'], p['hy_b2'], p['hy_w3'], p['hy_b3'], p['hy_decay'])
    filt_b = filt[None]
    spec_h = _dft_fwd(fwd, filt_b, 0, 2 * BR_W)
    pre, pre_b = _hy_pre(proj3, p['hy_conv'])
    skip = p['hy_skip'].astype(F32)
    zr, zi = _dft_fwd(fwd, pre_b, 0, BR_W, spec_h, 0)
    z1, z1_b = _dft_inv(inv, zr, zi, pre, 0, pre, BR_W, skip[0:1])
    zr, zi = _dft_fwd(fwd, z1_b, 0, BR_W, spec_h, BR_W)
    return _dft_inv(inv, zr, zi, z1, 0, pre, 2 * BR_W, skip[1:2], gate=proj3, gate_col0=7 * BR_W)


def _softplus(x):
    return jnp.maximum(x, 0.0) + jnp.log1p(jnp.exp(-jnp.abs(x)))


def _split_bf16(x, parts):
    out = []
    for _ in range(parts - 1):
        piece = x.astype(BF16)
        out.append(piece)
        x = x - piece.astype(F32)
    out.append(x.astype(BF16))
    return out


def _bmm(a, b, hi=False):
    mm = lambda x, y: jnp.einsum('nij,njk->nik', x, y, preferred_element_type=F32)
    if not hi:
        return mm(a.astype(BF16), b.astype(BF16))
    (a1, a2), (b1, b2) = _split_bf16(a, 2), _split_bf16(b, 2)
    return mm(a1, b1) + (mm(a1, b2) + mm(a2, b1))


def _bmm_nt(a, b):
    return jnp.einsum('nid,njd->nij', a.astype(BF16), b.astype(BF16), preferred_element_type=F32)


def _unit_tri_inverse(a, blk16, blk32):
    eye = (lax.broadcasted_iota(jnp.int32, a.shape, 1) == lax.broadcasted_iota(jnp.int32, a.shape, 2)).astype(F32)
    x = -jnp.where(blk16, a, 0.0)
    p = eye + x
    for _ in range(3):
        x = _bmm(x, x, hi=True)
        p = p + _bmm(p, x, hi=True)
    for off in (jnp.where(blk32 & ~blk16, a, 0.0), jnp.where(blk32, 0.0, a)):
        p = p - _bmm(p, _bmm(off, p))
    return p


def _gdn_prepare(q, k, v, ab, a_row, dt_row, head0, group):
    n, c, _ = q.shape
    two = lambda x: jnp.concatenate([x, x], axis=0)
    q, k, v, ab = two(q), two(k), two(v), two(ab)
    back3 = lambda shape: lax.broadcasted_iota(jnp.int32, shape, 0) >= n
    lane = lax.broadcasted_iota(jnp.int32, ab.shape, 2)
    bidx = lax.broadcasted_iota(jnp.int32, ab.shape, 0)
    head = head0 + jnp.where(bidx >= n, bidx - n, bidx) // group
    base = jnp.where(bidx >= n, 2 * N_HEAD, 0) + head
    g_all = -a_row * _softplus(ab + dt_row)
    g = jnp.sum(jnp.where(lane == base, g_all, 0.0), axis=2, keepdims=True)
    beta = jnp.sum(jnp.where(lane == base + N_HEAD, _sigmoid(ab), 0.0), axis=2, keepdims=True)

    sq = (2 * n, c, c)
    ri = lax.broadcasted_iota(jnp.int32, sq, 1)
    ci = lax.broadcasted_iota(jnp.int32, sq, 2)
    ahead = jnp.where(back3(sq), ci - ri, ri - ci)
    incl = ahead >= 0
    strict = ahead > 0
    tri = jnp.where(incl, 1.0, 0.0).astype(BF16)
    gc = sum(jnp.einsum('nij,njk->nik', tri, piece, preferred_element_type=F32)
             for piece in _split_bf16(jnp.broadcast_to(g, q.shape), 3))
    gc_row = jnp.swapaxes(gc, 1, 2)[:, :c, :]
    total = jnp.where(back3((2 * n, 1, D_HEAD)), gc[:, 0:1, :], gc[:, c - 1:c, :])
    decay = jnp.where(incl, jnp.exp(jnp.where(incl, gc[:, :, :c] - gc_row, 0.0)), 0.0)

    kb = k * beta
    a = jnp.where(strict, _bmm_nt(kb, k) * decay, 0.0)
    t = _unit_tri_inverse(a, (ri // 16) == (ci // 16), (ri // 32) == (ci // 32))
    e = jnp.exp(gc)
    u = _bmm(t, v * beta)
    w = _bmm(t, kb * e)
    a_intra = jnp.where(incl, _bmm_nt(q, k) * decay, 0.0)
    return (u, w.astype(BF16), (q * e).astype(BF16), (k * jnp.exp(total - gc)).astype(BF16),
            a_intra.astype(BF16), jnp.exp(total))


def _gdn_kernel(*refs, aliased, has_s0, group):
    if aliased:
        refs = refs[1:]
    if has_s0:
        (q_ref, k_ref, v_ref, z_ref, ab_ref, wq_ref, wk_ref, wv_ref, arow_ref, dt_ref, gn_ref, s0_ref,
         y_ref, sf_ref, qn, kn, vn, u_s, w_s, qd_s, kd_s, ai_s, gl_s) = refs
    else:
        (q_ref, k_ref, v_ref, z_ref, ab_ref, wq_ref, wk_ref, wv_ref, arow_ref, dt_ref, gn_ref,
         y_ref, sf_ref, qn, kn, vn, u_s, w_s, qd_s, kd_s, ai_s, gl_s) = refs
    l = q_ref.shape[1]
    heads = q_ref.shape[2] // D_HEAD
    head0 = pl.program_id(1) * heads
    n_chunks = l // CHUNK
    hcols = lambda hh: slice(hh * D_HEAD, (hh + 1) * D_HEAD)

    def l2n(x):
        return x * lax.rsqrt(jnp.sum(x * x, axis=-1, keepdims=True) + EPS)

    for hh in range(heads):
        cols = hcols(hh)
        qn[:, cols] = l2n(_silu(_dwconv3(q_ref[0, :, cols], wq_ref.at[:, cols]))) * (D_HEAD ** -0.5)
        kn[:, cols] = l2n(_silu(_dwconv3(k_ref[0, :, cols], wk_ref.at[:, cols])))
        vn[:, cols] = _silu(_dwconv3(v_ref[0, :, cols], wv_ref.at[:, cols]))

    a_row, dt_row = arow_ref[...], dt_ref[...]

    def prepare(gi, carry):
        span = group * CHUNK
        rows = pl.ds(pl.multiple_of(gi * span, span), span)
        chunks = lambda x: x.reshape(group, CHUNK, x.shape[-1])
        per_head = lambda ref: jnp.concatenate([chunks(ref[rows, hcols(hh)]) for hh in range(heads)], axis=0)
        ab = chunks(ab_ref[0, rows, :])
        u, w, qd, kd, ai, gl = _gdn_prepare(per_head(qn), per_head(kn), per_head(vn),
                                            jnp.concatenate([ab] * heads, axis=0), a_row, dt_row, head0, group)
        for d in range(2):
            for hh in range(heads):
                cols = hcols(hh)
                part = slice((d * heads + hh) * group, (d * heads + hh + 1) * group)
                u_s[d, rows, cols] = u[part].reshape(span, D_HEAD)
                w_s[d, rows, cols] = w[part].reshape(span, D_HEAD)
                qd_s[d, rows, cols] = qd[part].reshape(span, D_HEAD)
                kd_s[d, rows, cols] = kd[part].reshape(span, D_HEAD)
                ai_s[d, hh, rows, :] = ai[part].reshape(span, CHUNK)
                gl_s[d, hh, pl.ds(gi * group, group)] = jnp.broadcast_to(gl[part], (group,) + gl_s.shape[3:])
        return carry

    lax.fori_loop(0, n_chunks // group, prepare, 0)

    def scan(i, s):
        where = [(hh, d, pl.ds(pl.multiple_of(chunk * CHUNK, CHUNK), CHUNK), chunk)
                 for hh in range(heads) for d, chunk in ((0, i), (1, n_chunks - 1 - i))]
        gather = lambda ref: jnp.stack([ref[d, rows, hcols(hh)] for hh, d, rows, _ in where])
        a_intra = jnp.stack([ai_s[d, hh, rows, :] for hh, d, rows, _ in where])
        decay = jnp.stack([gl_s[d, hh, chunk][0:1, :] for hh, d, _, chunk in where])
        sb = s.astype(BF16)
        v_new = gather(u_s) - _bmm(gather(w_s), sb)
        vb = v_new.astype(BF16)
        o = _bmm(gather(qd_s), sb) + _bmm(a_intra, vb)
        for idx, (hh, d, rows, _) in enumerate(where):
            u_s[d, rows, hcols(hh)] = o[idx]
        return s * decay + jnp.einsum('nik,niv->nkv', gather(kd_s), vb, preferred_element_type=F32)

    if has_s0:
        init = jnp.stack([s0_ref[0, 0, d, hh] for hh in range(heads) for d in range(2)])
    else:
        init = jnp.zeros((2 * heads, D_HEAD, D_HEAD), F32)
    final = lax.fori_loop(0, n_chunks, scan, init)
    for hh in range(heads):
        cols = hcols(hh)
        sf_ref[0, 0, 0, hh] = final[2 * hh]
        sf_ref[0, 0, 1, hh] = final[2 * hh + 1]
        y_ref[0, :, cols] = (_rms(u_s[0, :, cols] + u_s[1, :, cols], gn_ref[...])
                             * _silu(z_ref[0, :, cols])).astype(y_ref.dtype)


def _gdn(proj3, ab3, conv_w, a_log, dt_bias, norm_g, layer, state=None, new_state=None):
    b, l, _ = proj3.shape
    depth_out, layer_out = (1, 0) if state is not None else (DEPTH, layer)
    aliased = new_state is not None
    lanes = jnp.zeros((2, 2 * N_HEAD), F32).at[:, :N_HEAD].set(1.0)
    a_row = jnp.pad((jnp.exp(a_log.astype(F32))[:, None, :] * lanes.reshape(2, 2, N_HEAD)).reshape(1, -1),
                    ((0, 0), (0, AB_PAD - 4 * N_HEAD)))
    dt_row = jnp.pad((dt_bias.astype(F32)[:, None, :] * lanes.reshape(2, 2, N_HEAD)).reshape(1, -1),
                     ((0, 0), (0, AB_PAD - 4 * N_HEAD)))
    hps = N_HEAD if l <= 512 else 1
    wid = hps * D_HEAD
    n_hb = N_HEAD // hps
    blk = lambda c: pl.BlockSpec((1, l, wid), lambda i, h, c=c: (i, 0, c * n_hb + h))
    wblk = lambda c: pl.BlockSpec((3, wid), lambda i, h, c=c: (0, c * n_hb + h))
    row = pl.BlockSpec((1, D_HEAD), lambda i, h: (0, 0))
    in_specs = [blk(0), blk(1), blk(2), blk(3),
                pl.BlockSpec((1, l, AB_PAD), lambda i, h: (i, 0, 0)),
                wblk(0), wblk(1), wblk(2), row, row, row]
    args = [proj3, proj3, proj3, proj3, ab3, conv_w, conv_w, conv_w, a_row, dt_row, norm_g]
    if aliased:
        in_specs.insert(0, pl.BlockSpec(memory_space=pl.ANY))
        args.insert(0, new_state)
    if state is not None:
        in_specs.append(pl.BlockSpec((1, 1, 2, hps, D_HEAD, D_HEAD), lambda i, h: (i, layer, 0, h, 0, 0)))
        args.append(state)
    return pl.pallas_call(
        functools.partial(_gdn_kernel, aliased=aliased, has_s0=state is not None, group=min(8, l // CHUNK)),
        grid=(b, n_hb),
        in_specs=in_specs,
        out_specs=[pl.BlockSpec((1, l, wid), lambda i, h: (i, 0, h)),
                   pl.BlockSpec((1, 1, 2, hps, D_HEAD, D_HEAD), lambda i, h: (i, layer_out, 0, h, 0, 0))],
        out_shape=[jax.ShapeDtypeStruct((b, l, BR_W), BF16),
                   jax.ShapeDtypeStruct((b, depth_out, 2, N_HEAD, D_HEAD, D_HEAD), F32)],
        input_output_aliases={0: 1} if aliased else {},
        scratch_shapes=[pltpu.VMEM((l, wid), F32)] * 3
        + [pltpu.VMEM((2, l, wid), F32)] + [pltpu.VMEM((2, l, wid), BF16)] * 3
        + [pltpu.VMEM((2, hps, l, CHUNK), BF16), pltpu.VMEM((2, hps, l // CHUNK, 8, D_HEAD), F32)],
        compiler_params=_cparams("parallel", "parallel"),
        name="gdn",
    )(*args)


def _mod_kernel(c_ref, w_ref, b_ref, o_ref):
    o_ref[...] = _dot_hi(_silu(c_ref[...]), w_ref[...]) + b_ref[...]


def _modulation(cond, w_mod, b_mod, layer, tn=512):
    n = cond.shape[0]
    rows = 8
    out = pl.pallas_call(
        _mod_kernel,
        grid=(3 * D_MODEL // tn,),
        in_specs=[pl.BlockSpec((rows, D_MODEL), lambda j: (0, 0)),
                  pl.BlockSpec((None, D_MODEL, tn), lambda j: (layer, 0, j)),
                  pl.BlockSpec((1, tn), lambda j: (0, j))],
        out_specs=pl.BlockSpec((rows, tn), lambda j: (0, j)),
        out_shape=jax.ShapeDtypeStruct((rows, 3 * D_MODEL), F32),
        compiler_params=_cparams("parallel"),
        name="modulation",
    )(jnp.pad(cond.astype(F32), ((0, rows - n), (0, 0))), w_mod, b_mod.reshape(1, -1))
    return out[:n].reshape(n, 3, D_MODEL)


def _split_w_in(w_in):
    n_a = 4 * BR_W + 4 * N_HEAD
    main = jnp.concatenate([w_in[..., :4 * BR_W], w_in[..., n_a:]], axis=-1).astype(BF16)
    ab = jnp.pad(w_in[..., 4 * BR_W:n_a], ((0, 0),) * (w_in.ndim - 1) + ((0, AB_PAD - 4 * N_HEAD),)).astype(BF16)
    return main, ab


def _trunk_layer(x3, cond, p, big, layer, dft, latent, new_outputs=(None, None, None)):
    b, l, _ = x3.shape
    x2 = x3.reshape(b * l, D_MODEL)
    mod = _modulation(cond, big['w_mod'], p['b_mod'], layer)
    rows_per_mod = l if mod.shape[0] == b else b * l
    g_pre = p['g_pre'].reshape(1, D_MODEL)
    proj, ab = _inproj(x2, mod, g_pre, big['w_main'], big['w_ab'], layer, rows_per_mod)
    proj3 = proj.reshape(b, l, N_MAIN)
    ab3 = ab.reshape(b, l, AB_PAD)

    lam_init = 0.8 - 0.6 * math.exp(-0.3 * layer)
    lam_p = p['diff_lam'].astype(F32)
    lam = (jnp.exp(jnp.sum(lam_p[0] * lam_p[1])) - jnp.exp(jnp.sum(lam_p[2] * lam_p[3])) + lam_init).reshape(1, 1)
    diff_norm = p['diff_norm'].reshape(1, D_HEAD)
    gdn_args = (proj3, ab3, p['gdn_conv'], p['gdn_a_log'], p['gdn_dt_bias'], p['gdn_norm'].reshape(1, D_HEAD), layer)

    yb = _hyena(proj3, p, dft)
    if latent is None:
        new_state, nat_cache, diff_cache = new_outputs
        ya, new_state = _gdn(*gdn_args, new_state=new_state)
        yc, yd, nat_cache, diff_cache = _ctx_attention(proj3, lam, diff_norm, lam_init, layer, nat_cache, diff_cache)
        extras = (new_state, nat_cache, diff_cache)
    else:
        ya, _ = _gdn(*gdn_args, state=latent['state_gdn'])
        yc = _lat_nat(proj3, latent['cache_nat_kv'], layer, _nat_bias_table(p['nat_rpb']))
        yd = _lat_diff(proj3, latent['cache_diff_kv'], layer, lam, diff_norm, lam_init, latent['rope'])
        extras = None

    ys = [t.reshape(b * l, BR_W) for t in (ya, yb, yc, yd)]
    out = _merge(x2, mod, g_pre, p['g_post'].reshape(1, D_MODEL), ys, big['w_branch'], big['w_merge'],
                 p['b_merge'].reshape(1, -1).astype(F32), big['w_out'], layer, rows_per_mod)
    return out.reshape(b, l, D_MODEL), extras


def kernel(x_prompt, x_sample, state_gdn, cache_nat_kv, cache_diff_kv, c, c_ctx,
           w_mod, b_mod, g_pre, g_post, w_in, gdn_conv, gdn_a_log, gdn_dt_bias, gdn_norm,
           hy_conv, hy_w1, hy_b1, hy_w2, hy_b2, hy_w3, hy_b3, hy_decay, hy_skip,
           nat_rpb, diff_lam, diff_norm, w_branch, w_merge, b_merge, w_out):
    small = {
        'b_mod': b_mod, 'g_pre': g_pre, 'g_post': g_post,
        'gdn_conv': gdn_conv, 'gdn_a_log': gdn_a_log, 'gdn_dt_bias': gdn_dt_bias, 'gdn_norm': gdn_norm,
        'hy_conv': hy_conv, 'hy_w1': hy_w1, 'hy_b1': hy_b1, 'hy_w2': hy_w2, 'hy_b2': hy_b2,
        'hy_w3': hy_w3, 'hy_b3': hy_b3, 'hy_decay': hy_decay, 'hy_skip': hy_skip,
        'nat_rpb': nat_rpb, 'diff_lam': diff_lam, 'diff_norm': diff_norm, 'b_merge': b_merge,
    }
    layers = [{name: arr[i] for name, arr in small.items()} for i in range(DEPTH)]
    w_main, w_ab = _split_w_in(w_in)
    big = {'w_mod': w_mod.astype(F32), 'w_main': w_main, 'w_ab': w_ab, 'w_branch': w_branch.astype(BF16),
           'w_merge': w_merge.astype(BF16), 'w_out': w_out.astype(BF16)}

    y_prompt = x_prompt
    dft_ctx = _dft_matrices(x_prompt.shape[1])
    outputs = (None, None, None)
    for i, p in enumerate(layers):
        y_prompt, outputs = _trunk_layer(y_prompt, c_ctx.reshape(1, D_MODEL), p, big, i, dft_ctx, None, outputs)
    new_state, nat_cache, diff_cache = outputs

    y_sample = x_sample
    dft_lat = _dft_matrices(x_sample.shape[1])
    latent = {'state_gdn': state_gdn, 'cache_nat_kv': cache_nat_kv, 'cache_diff_kv': cache_diff_kv,
              'rope': _rope_tables(x_sample.shape[1])}
    for i, p in enumerate(layers):
        y_sample, _ = _trunk_layer(y_sample, c, p, big, i, dft_lat, latent)

    return (y_prompt, y_sample, new_state, nat_cache, diff_cache)
```

```python
import functools
import math

import jax
import jax.numpy as jnp
import numpy as np
from jax import lax
from jax.experimental import pallas as pl
from jax.experimental.pallas import tpu as pltpu

F32 = jnp.float32
BF16 = jnp.bfloat16

D_MODEL = 1024
DEPTH = 2
GRID_W = 64
N_BRANCH = 4
BR_W = 512
N_HEAD = 4
D_HEAD = 128
SUBLANES = 8
CHUNK = 64
HY_BANDS = 16
WIN_R = 8
WIN_C = 16
DQK_D = 64
ROPE_BASE = 10000.0
EPS = 1e-6
N_MAIN = 4 * 4 * BR_W
AB_PAD = 128
NEG_INF = -1e30

VMEM_LIMIT = 48 * 1024 * 1024


def _cparams(*sem):
    return pltpu.CompilerParams(dimension_semantics=sem, vmem_limit_bytes=VMEM_LIMIT)


def _silu(x):
    return x * (1.0 / (1.0 + jnp.exp(-x)))


def _sigmoid(x):
    return 1.0 / (1.0 + jnp.exp(-x))


def _rms(x, g):
    return x * lax.rsqrt(jnp.mean(x * x, axis=-1, keepdims=True) + EPS) * g


def _dot(a, b):
    return jnp.dot(a.astype(BF16), b.astype(BF16), preferred_element_type=F32)


def _dot_nt(a, b):
    return lax.dot_general(a.astype(BF16), b.astype(BF16), (((1,), (1,)), ((), ())),
                           preferred_element_type=F32)


def _dot_tn(a, b):
    return lax.dot_general(a.astype(BF16), b.astype(BF16), (((0,), (0,)), ((), ())),
                           preferred_element_type=F32)


def _prenorm(x, g_pre, mod_ref):
    return _rms(x, g_pre) * (1.0 + mod_ref[0, 1:2, :]) + mod_ref[0, 0:1, :]


def _inproj_kernel(x_ref, mod_ref, gpre_ref, w_ref, wab_ref, proj_ref, ab_ref, h_scr):
    @pl.when(pl.program_id(1) == 0)
    def _():
        h = _prenorm(x_ref[...], gpre_ref[...], mod_ref).astype(BF16)
        h_scr[...] = h
        ab_ref[...] = jnp.dot(h, wab_ref[...], preferred_element_type=F32)

    proj_ref[...] = jnp.dot(h_scr[...], w_ref[...], preferred_element_type=F32)


def _inproj(x2, mod, g_pre, w_main, w_ab, layer, rows_per_mod, tm=1024, tn=1024):
    m = x2.shape[0]
    tm = math.gcd(tm, rows_per_mod)
    return pl.pallas_call(
        _inproj_kernel,
        grid=(m // tm, N_MAIN // tn),
        in_specs=[
            pl.BlockSpec((tm, D_MODEL), lambda i, j: (i, 0)),
            pl.BlockSpec((1, 3, D_MODEL), lambda i, j: ((i * tm) // rows_per_mod, 0, 0)),
            pl.BlockSpec((1, D_MODEL), lambda i, j: (0, 0)),
            pl.BlockSpec((None, D_MODEL, tn), lambda i, j: (layer, 0, j)),
            pl.BlockSpec((None, D_MODEL, AB_PAD), lambda i, j: (layer, 0, 0)),
        ],
        out_specs=[
            pl.BlockSpec((tm, tn), lambda i, j: (i, j)),
            pl.BlockSpec((tm, AB_PAD), lambda i, j: (i, 0)),
        ],
        out_shape=[jax.ShapeDtypeStruct((m, N_MAIN), F32),
                   jax.ShapeDtypeStruct((m, AB_PAD), F32)],
        scratch_shapes=[pltpu.VMEM((tm, D_MODEL), BF16)],
        compiler_params=_cparams("parallel", "arbitrary"),
        name="inproj",
    )(x2, mod, g_pre, w_main, w_ab)


def _merge_kernel(x_ref, mod_ref, gpre_ref, gpost_ref, ya_ref, yb_ref, yc_ref, yd_ref,
                  wbr_ref, wmg_ref, bmg_ref, wout_ref, o_ref):
    x = x_ref[...]
    h = _prenorm(x, gpre_ref[...], mod_ref).astype(BF16)
    acc = None
    for k, y_ref in enumerate((ya_ref, yb_ref, yc_ref, yd_ref)):
        cols = slice(k * D_MODEL, (k + 1) * D_MODEL)
        gate = _sigmoid(jnp.dot(h, wmg_ref[:, cols], preferred_element_type=F32) + bmg_ref[:, cols])
        br = jnp.dot(y_ref[...], wbr_ref[k], preferred_element_type=F32)
        acc = gate * br if acc is None else acc + gate * br
    y = jnp.dot(acc.astype(BF16), wout_ref[...], preferred_element_type=F32)
    o_ref[...] = x + mod_ref[0, 2:3, :] * _rms(y, gpost_ref[...])


def _merge(x2, mod, g_pre, g_post, ys, w_branch, w_merge, b_merge, w_out, layer, rows_per_mod, tm=256):
    m = x2.shape[0]
    row = lambda i: (i, 0)
    fixed2 = lambda i: (0, 0)
    return pl.pallas_call(
        _merge_kernel,
        grid=(m // tm,),
        in_specs=[
            pl.BlockSpec((tm, D_MODEL), row),
            pl.BlockSpec((1, 3, D_MODEL), lambda i: ((i * tm) // rows_per_mod, 0, 0)),
            pl.BlockSpec((1, D_MODEL), fixed2),
            pl.BlockSpec((1, D_MODEL), fixed2),
            pl.BlockSpec((tm, BR_W), row),
            pl.BlockSpec((tm, BR_W), row),
            pl.BlockSpec((tm, BR_W), row),
            pl.BlockSpec((tm, BR_W), row),
            pl.BlockSpec((None, N_BRANCH, BR_W, D_MODEL), lambda i: (layer, 0, 0, 0)),
            pl.BlockSpec((None, D_MODEL, N_BRANCH * D_MODEL), lambda i: (layer, 0, 0)),
            pl.BlockSpec((1, N_BRANCH * D_MODEL), fixed2),
            pl.BlockSpec((None, D_MODEL, D_MODEL), lambda i: (layer, 0, 0)),
        ],
        out_specs=pl.BlockSpec((tm, D_MODEL), row),
        out_shape=jax.ShapeDtypeStruct((m, D_MODEL), F32),
        compiler_params=_cparams("parallel"),
        name="merge",
    )(x2, mod, g_pre, g_post, *ys, w_branch, w_merge, b_merge, w_out)


def _softmax_rows(s):
    p = jnp.exp(s - jnp.max(s, axis=-1, keepdims=True))
    return p, jnp.sum(p, axis=-1, keepdims=True)


def _ctx_nat_kernel(*refs, aliased):
    q_ref, k_ref, v_ref, g_ref, y_ref, kv_ref = refs[1:] if aliased else refs
    scale = D_HEAD ** -0.5
    for h in range(N_HEAD):
        sl = slice(h * D_HEAD, (h + 1) * D_HEAD)
        k = k_ref[0, :, sl]
        v = v_ref[0, :, sl]
        p, l = _softmax_rows(_dot_nt(q_ref[0, :, sl], k) * scale)
        o = _dot(p, v) / l
        y_ref[0, :, sl] = (o * _silu(g_ref[0, :, sl])).astype(y_ref.dtype)
        kv_ref[0, 0, 0, h] = k
        kv_ref[0, 0, 1, h] = v


def _map_masks():
    lane = lax.broadcasted_iota(jnp.int32, (1, D_HEAD), 1)
    first = (lane < DQK_D).astype(F32)
    return first, 1.0 - first


def _ctx_diff_kernel(*refs, aliased, out_scale):
    lam_ref, q_ref, k_ref, v_ref, g_ref, gn_ref, y_ref, kv_ref = refs[1:] if aliased else refs
    scale = DQK_D ** -0.5
    m1, m2 = _map_masks()
    lam = lam_ref[...]
    for h in range(N_HEAD):
        sl = slice(h * D_HEAD, (h + 1) * D_HEAD)
        q = q_ref[0, :, sl]
        k = k_ref[0, :, sl]
        v = v_ref[0, :, sl]
        p1, l1 = _softmax_rows(_dot_nt(q * m1, k) * scale)
        p2, l2 = _softmax_rows(_dot_nt(q * m2, k) * scale)
        a = p1 / l1 - lam * (p2 / l2)
        o = _rms(_dot(a, v), gn_ref[...]) * out_scale
        y_ref[0, :, sl] = (o * _silu(g_ref[0, :, sl])).astype(y_ref.dtype)
        kv_ref[0, 0, 0, h] = k
        kv_ref[0, 0, 1, h] = v


def _ctx_attention(proj3, lam, diff_norm, lam_init, layer, nat_cache, diff_cache):
    b, l, _ = proj3.shape
    blk = lambda c: pl.BlockSpec((1, l, BR_W), lambda i, c=c: (i, 0, c))
    y_spec = pl.BlockSpec((1, l, BR_W), lambda i: (i, 0, 0))
    kv_spec = pl.BlockSpec((1, 1, 2, N_HEAD, l, D_HEAD), lambda i: (i, layer, 0, 0, 0, 0))
    out_shape = [jax.ShapeDtypeStruct((b, l, BR_W), BF16),
                 jax.ShapeDtypeStruct((b, DEPTH, 2, N_HEAD, l, D_HEAD), F32)]
    aliased = nat_cache is not None
    cache_specs = [pl.BlockSpec(memory_space=pl.ANY)] if aliased else []
    aliases = {0: 1} if aliased else {}
    yc, nat_cache = pl.pallas_call(
        functools.partial(_ctx_nat_kernel, aliased=aliased),
        grid=(b,),
        in_specs=cache_specs + [blk(8), blk(9), blk(10), blk(11)],
        out_specs=[y_spec, kv_spec],
        out_shape=out_shape,
        input_output_aliases=aliases,
        compiler_params=_cparams("parallel"),
        name="ctx_nat",
    )(*([nat_cache] if aliased else []), proj3, proj3, proj3, proj3)
    yd, diff_cache = pl.pallas_call(
        functools.partial(_ctx_diff_kernel, aliased=aliased, out_scale=1.0 - lam_init),
        grid=(b,),
        in_specs=cache_specs + [pl.BlockSpec((1, 1), lambda i: (0, 0)),
                                blk(12), blk(13), blk(14), blk(15),
                                pl.BlockSpec((1, D_HEAD), lambda i: (0, 0))],
        out_specs=[y_spec, kv_spec],
        out_shape=out_shape,
        input_output_aliases=aliases,
        compiler_params=_cparams("parallel"),
        name="ctx_diff",
    )(*([diff_cache] if aliased else []), lam, proj3, proj3, proj3, proj3, diff_norm)
    return yc, yd, nat_cache, diff_cache


def _nat_bias_table(rpb):
    cols = np.arange(GRID_W)
    start = np.clip(cols - WIN_C // 2, 0, GRID_W - WIN_C)
    inside = (cols[None, :] >= start[:, None]) & (cols[None, :] < start[:, None] + WIN_C)
    dc = cols[None, :] - cols[:, None] + (WIN_C - 1)
    onehot = ((dc[None] == np.arange(2 * WIN_C - 1)[:, None, None]) & inside[None]).astype(np.float32)
    t = jnp.einsum('hdx,xck->hdck', rpb.astype(F32), jnp.asarray(onehot), precision=lax.Precision.HIGHEST)
    t = jnp.where(jnp.asarray(inside)[None, None], t, NEG_INF)
    tab = jnp.stack([t[:, WIN_R - 1 - off:2 * WIN_R - 1 - off] for off in range(WIN_R)], axis=1)
    return tab.transpose(0, 1, 3, 2, 4).reshape(rpb.shape[0], WIN_R, GRID_W, WIN_R * GRID_W)


def _lat_nat_kernel(q_ref, k_ref, v_ref, g_ref, ckv_ref, bias_ref, y_ref, kb_scr, vb_scr, *, rb):
    scale = D_HEAD ** -0.5
    rows = q_ref.shape[1] // GRID_W
    win = WIN_R * GRID_W
    kb_scr[...] = k_ref[0].astype(BF16)
    vb_scr[...] = v_ref[0].astype(BF16)
    ck = ckv_ref[0, 0, 0, 0].astype(BF16)
    cv = ckv_ref[0, 0, 1, 0].astype(BF16)

    def row_block(i, carry):
        q0 = pl.multiple_of(i * (rb * GRID_W), rb * GRID_W)
        qrows = pl.ds(q0, rb * GRID_W)
        q = q_ref[0, qrows, :].astype(BF16)
        kw, vw, bias = [], [], []
        for j in range(rb):
            r = i * rb + j
            rs = jnp.clip(r - WIN_R // 2, 0, rows - WIN_R)
            wrows = pl.ds(pl.multiple_of(rs * GRID_W, GRID_W), win)
            kw.append(kb_scr[wrows, :])
            vw.append(vb_scr[wrows, :])
            bias.append(bias_ref[0, r - rs])
        q3 = q.reshape(rb, GRID_W, D_HEAD)
        s_lat = _bmm_nt(q3, jnp.stack(kw)) * scale + jnp.stack(bias)
        s_ctx = (_dot_nt(q, ck) * scale).reshape(rb, GRID_W, ck.shape[0])
        m = jnp.maximum(jnp.max(s_lat, axis=-1, keepdims=True), jnp.max(s_ctx, axis=-1, keepdims=True))
        p_lat = jnp.exp(s_lat - m)
        p_ctx = jnp.exp(s_ctx - m)
        l = jnp.sum(p_lat, axis=-1, keepdims=True) + jnp.sum(p_ctx, axis=-1, keepdims=True)
        o_ctx = _dot(p_ctx.reshape(rb * GRID_W, ck.shape[0]), cv).reshape(rb, GRID_W, D_HEAD)
        o = ((_bmm(p_lat, jnp.stack(vw)) + o_ctx) / l).reshape(rb * GRID_W, D_HEAD)
        y_ref[0, qrows, :] = (o * _silu(g_ref[0, qrows, :])).astype(y_ref.dtype)
        return carry

    lax.fori_loop(0, rows // rb, row_block, 0)


def _lat_nat(proj3, cache_nat_kv, layer, bias_tab):
    b, l, _ = proj3.shape
    past = cache_nat_kv.shape[4]
    blk = lambda c: pl.BlockSpec((1, l, D_HEAD), lambda i, h, c=c: (i, 0, c + h))
    return pl.pallas_call(
        functools.partial(_lat_nat_kernel, rb=8),
        grid=(b, N_HEAD),
        in_specs=[blk(32), blk(36), blk(40), blk(44),
                  pl.BlockSpec((1, 1, 2, 1, past, D_HEAD), lambda i, h: (i, layer, 0, h, 0, 0)),
                  pl.BlockSpec((1, WIN_R, GRID_W, WIN_R * GRID_W), lambda i, h: (h, 0, 0, 0))],
        out_specs=pl.BlockSpec((1, l, D_HEAD), lambda i, h: (i, 0, h)),
        out_shape=jax.ShapeDtypeStruct((b, l, BR_W), BF16),
        scratch_shapes=[pltpu.VMEM((l, D_HEAD), BF16), pltpu.VMEM((l, D_HEAD), BF16)],
        compiler_params=_cparams("parallel", "parallel"),
        name="lat_nat",
    )(proj3, proj3, proj3, proj3, cache_nat_kv, bias_tab)


def _rope_tables(l):
    half = DQK_D // 2
    nf = half // 2
    t = jnp.arange(l)
    row = (t // GRID_W).astype(F32)
    col = (t % GRID_W).astype(F32)
    inv = ROPE_BASE ** (-jnp.arange(nf, dtype=F32) / nf)
    ang = jnp.concatenate([row[:, None] * inv, col[:, None] * inv], axis=-1)
    cos, sin = jnp.cos(ang), jnp.sin(ang)
    zero = jnp.zeros_like(sin)
    tile2 = lambda a, b: jnp.concatenate([a, b, a, b], axis=-1)
    return tile2(cos, cos), tile2(-sin, zero), tile2(zero, sin)


def _rope(x, cos, sin_a, sin_b):
    return x * cos + pltpu.roll(x, 96, 1) * sin_a + pltpu.roll(x, 32, 1) * sin_b


def _lat_diff_kernel(lam_ref, q_ref, k_ref, v_ref, g_ref, ckv_ref, gn_ref,
                     cq_ref, saq_ref, sbq_ref, ck_ref, sak_ref, sbk_ref,
                     y_ref, ks_scr, vs_scr, *, out_scale, prep_rows, key_block):
    scale = DQK_D ** -0.5
    l = k_ref.shape[1]

    @pl.when(pl.program_id(2) == 0)
    def _():
        def prep(i, carry):
            rows = pl.ds(pl.multiple_of(i * prep_rows, prep_rows), prep_rows)
            kr = _rope(k_ref[0, rows, :], ck_ref[rows, :], sak_ref[rows, :], sbk_ref[rows, :])
            ks_scr[rows, :] = kr.astype(BF16)
            vs_scr[rows, :] = v_ref[0, rows, :].astype(BF16)
            return carry

        lax.fori_loop(0, l // prep_rows, prep, 0)
        ks_scr[l:, :] = ckv_ref[0, 0, 0, 0].astype(BF16)
        vs_scr[l:, :] = ckv_ref[0, 0, 1, 0].astype(BF16)

    q = _rope(q_ref[0], cq_ref[...], saq_ref[...], sbq_ref[...]) * scale
    m1, m2 = _map_masks()
    tq = q.shape[0]
    qm = jnp.concatenate([q * m1, q * m2], axis=0).astype(BF16)
    m = l_sum = acc = None
    for blk in range(ks_scr.shape[0] // key_block):
        rows = slice(blk * key_block, (blk + 1) * key_block)
        s = _dot_nt(qm, ks_scr[rows, :])
        m_blk = jnp.max(s, axis=-1, keepdims=True)
        if blk == 0:
            m = m_blk
            p = jnp.exp(s - m)
            l_sum = jnp.sum(p, axis=-1, keepdims=True)
            acc = _dot(p, vs_scr[rows, :])
        else:
            m_new = jnp.maximum(m, m_blk)
            alpha = jnp.exp(m - m_new)
            p = jnp.exp(s - m_new)
            l_sum = alpha * l_sum + jnp.sum(p, axis=-1, keepdims=True)
            acc = alpha * acc + _dot(p, vs_scr[rows, :])
            m = m_new
    out = acc / l_sum
    o = _rms(out[:tq] - lam_ref[...] * out[tq:], gn_ref[...]) * out_scale
    y_ref[0] = (o * _silu(g_ref[0])).astype(y_ref.dtype)


def _lat_diff(proj3, cache_diff_kv, layer, lam, diff_norm, lam_init, rope_tabs, tq=256):
    b, l, _ = proj3.shape
    past = cache_diff_kv.shape[4]
    qblk = lambda c: pl.BlockSpec((1, tq, D_HEAD), lambda i, h, j, c=c: (i, j, c + h))
    full = lambda c: pl.BlockSpec((1, l, D_HEAD), lambda i, h, j, c=c: (i, 0, c + h))
    tq_tab = pl.BlockSpec((tq, D_HEAD), lambda i, h, j: (j, 0))
    full_tab = pl.BlockSpec((l, D_HEAD), lambda i, h, j: (0, 0))
    return pl.pallas_call(
        functools.partial(_lat_diff_kernel, out_scale=1.0 - lam_init, prep_rows=512, key_block=256),
        grid=(b, N_HEAD, l // tq),
        in_specs=[pl.BlockSpec((1, 1), lambda i, h, j: (0, 0)),
                  qblk(48), full(52), full(56), qblk(60),
                  pl.BlockSpec((1, 1, 2, 1, past, D_HEAD), lambda i, h, j: (i, layer, 0, h, 0, 0)),
                  pl.BlockSpec((1, D_HEAD), lambda i, h, j: (0, 0)),
                  tq_tab, tq_tab, tq_tab, full_tab, full_tab, full_tab],
        out_specs=pl.BlockSpec((1, tq, D_HEAD), lambda i, h, j: (i, j, h)),
        out_shape=jax.ShapeDtypeStruct((b, l, BR_W), BF16),
        scratch_shapes=[pltpu.VMEM((l + past, D_HEAD), BF16), pltpu.VMEM((l + past, D_HEAD), BF16)],
        compiler_params=_cparams("parallel", "parallel", "arbitrary"),
        name="lat_diff",
    )(lam, proj3, proj3, proj3, proj3, cache_diff_kv, diff_norm, *rope_tabs, *rope_tabs)


def _dwconv3(x, w_ref):
    l = x.shape[0]
    row = lax.broadcasted_iota(jnp.int32, x.shape, 0)
    prev = jnp.where(row == 0, 0.0, pltpu.roll(x, 1, 0))
    nxt = jnp.where(row == l - 1, 0.0, pltpu.roll(x, l - 1, 0))
    return prev * w_ref[0:1, :] + x * w_ref[1:2, :] + nxt * w_ref[2:3, :]


def _hy_pre_kernel(x_ref, above_ref, below_ref, w_ref, o_ref, ob_ref):
    t, n_t = pl.program_id(1), pl.num_programs(1)
    x = x_ref[0]
    rows = x.shape[0]
    row = lax.broadcasted_iota(jnp.int32, x.shape, 0)
    before = jnp.where(t == 0, 0.0, above_ref[0, SUBLANES - 1:SUBLANES, :])
    after = jnp.where(t == n_t - 1, 0.0, below_ref[0, 0:1, :])
    prev = jnp.where(row == 0, before, pltpu.roll(x, 1, 0))
    nxt = jnp.where(row == rows - 1, after, pltpu.roll(x, rows - 1, 0))
    y = prev * w_ref[0:1, :] + x * w_ref[1:2, :] + nxt * w_ref[2:3, :]
    o_ref[0] = y
    ob_ref[0] = y.astype(BF16)


def _hy_pre(proj3, conv_w, tl=512):
    b, l, _ = proj3.shape
    tl = min(tl, l)
    n = 3
    col0 = 4
    groups = tl // SUBLANES
    last_group = l // SUBLANES - 1
    spec = pl.BlockSpec((1, tl, BR_W), lambda i, t, j: (i, t, j))
    return pl.pallas_call(
        _hy_pre_kernel,
        grid=(b, l // tl, n),
        in_specs=[pl.BlockSpec((1, tl, BR_W), lambda i, t, j: (i, t, col0 + j)),
                  pl.BlockSpec((1, SUBLANES, BR_W),
                               lambda i, t, j: (i, jnp.maximum(t * groups - 1, 0), col0 + j)),
                  pl.BlockSpec((1, SUBLANES, BR_W),
                               lambda i, t, j: (i, jnp.minimum((t + 1) * groups, last_group), col0 + j)),
                  pl.BlockSpec((3, BR_W), lambda i, t, j: (0, j))],
        out_specs=[spec, spec],
        out_shape=[jax.ShapeDtypeStruct((b, l, 3 * BR_W), F32),
                   jax.ShapeDtypeStruct((b, l, 3 * BR_W), BF16)],
        compiler_params=_cparams("parallel", "parallel", "parallel"),
        name="hy_pre",
    )(proj3, proj3, proj3, conv_w)


def _dot_hi(a, b):
    return jnp.dot(a, b, preferred_element_type=F32, precision=lax.Precision.HIGHEST)


def _hy_filter_kernel(feat_ref, dist_ref, w1_ref, b1_ref, w2_ref, b2_ref, w3_ref, b3_ref, dec_ref, o_ref):
    hid = jnp.sin(_dot_hi(feat_ref[...], w1_ref[...]) + b1_ref[...])
    hid = jnp.sin(_dot_hi(hid, w2_ref[...]) + b2_ref[...])
    dist = dist_ref[...]
    for j in range(o_ref.shape[1] // D_HEAD):
        cols = slice(j * D_HEAD, (j + 1) * D_HEAD)
        filt = _dot_hi(hid, w3_ref[:, cols]) + b3_ref[:, cols]
        o_ref[:, cols] = (filt * jnp.exp(-dist * jnp.abs(dec_ref[:, cols]))).astype(o_ref.dtype)


def _hy_filter(l, w1, b1, w2, b2, w3, b3, decay):
    pos = jnp.arange(l, dtype=F32)
    t = pos / l
    ang = (2.0 * math.pi) * t[:, None] * jnp.arange(1, HY_BANDS + 1, dtype=F32)
    feat = jnp.concatenate([t[:, None], jnp.cos(ang), jnp.sin(ang)], axis=-1)
    dist = jnp.broadcast_to((jnp.abs(pos - l // 2) / l)[:, None], (l, D_HEAD))
    pad = D_HEAD
    emb, ff = w1.shape
    feat = jnp.pad(feat, ((0, 0), (0, pad - emb)))
    w1p = jnp.pad(w1, ((0, pad - emb), (0, pad - ff)))
    w2p = jnp.pad(w2, ((0, pad - ff), (0, pad - ff)))
    w3p = jnp.pad(w3, ((0, pad - ff), (0, 0)))
    b1p = jnp.pad(b1, (0, pad - ff)).reshape(1, pad)
    b2p = jnp.pad(b2, (0, pad - ff)).reshape(1, pad)
    tl = min(l, 256)
    n = 2 * BR_W
    fixed = lambda shape: pl.BlockSpec(shape, lambda i: (0, 0))
    return pl.pallas_call(
        _hy_filter_kernel,
        grid=(l // tl,),
        in_specs=[pl.BlockSpec((tl, pad), lambda i: (i, 0)),
                  pl.BlockSpec((tl, D_HEAD), lambda i: (i, 0)),
                  fixed((pad, pad)), fixed((1, pad)), fixed((pad, pad)), fixed((1, pad)),
                  fixed((pad, n)), fixed((1, n)), fixed((1, n))],
        out_specs=pl.BlockSpec((tl, n), lambda i: (i, 0)),
        out_shape=jax.ShapeDtypeStruct((l, n), BF16),
        compiler_params=_cparams("parallel"),
        name="hy_filter",
    )(feat, dist, w1p, b1p, w2p, b2p, w3p, b3.reshape(1, n), decay.reshape(1, n))


def _dft_matrices(l):
    n = 2 * l
    k = jnp.arange(l, dtype=jnp.int32)
    t = jnp.arange(l, dtype=jnp.int32)
    split = 1 << (max(l.bit_length() - 1, 0) // 2)

    def cos_sin(rows, cols):
        def table(r):
            ang = (2.0 * math.pi / n) * ((r[:, None] * cols[None, :]) % n).astype(F32)
            return jnp.cos(ang), jnp.sin(ang)
        lo = rows[:split] - rows[0]
        (ch, sh), (cl, sl) = table(rows[::split]), table(lo)
        c = ch[:, None, :] * cl[None, :, :] - sh[:, None, :] * sl[None, :, :]
        s = sh[:, None, :] * cl[None, :, :] + ch[:, None, :] * sl[None, :, :]
        return c.reshape(l, l), s.reshape(l, l)

    alt = jnp.where(t % 2 == 0, 1.0, -1.0).astype(F32)
    fc, fs = cos_sin(k, t)
    fs = jnp.where(k[:, None] == 0, alt[None, :], -fs)
    tp = t + l // 2
    wk = jnp.where(k == 0, 1.0, 2.0).astype(F32) / n
    alt_i = jnp.where(tp % 2 == 0, 1.0, -1.0).astype(F32) / n
    ic, is_ = cos_sin(tp, k)
    ic = ic * wk[None, :]
    is_ = jnp.where(k[None, :] == 0, alt_i[:, None], -is_ * wk[None, :])
    return (fc.astype(BF16), fs.astype(BF16)), (ic.astype(BF16), is_.astype(BF16))


def _seqs_per_step(b, l, rows=2048):
    bt = max(1, min(b, rows // l))
    while b % bt:
        bt -= 1
    return bt


def _dft_fwd_kernel(fc_ref, fs_ref, x_ref, *rest, with_filter, tm):
    for bb in range(x_ref.shape[0]):
        x = x_ref[bb]
        ur = jnp.dot(fc_ref[...], x, preferred_element_type=F32)
        ui = jnp.dot(fs_ref[...], x, preferred_element_type=F32)
        if not with_filter:
            zr_ref, zi_ref = rest
            zr_ref[bb] = ur
            zi_ref[bb] = ui
            continue
        hr_ref, hi_ref, zr_ref, zi_ref = rest
        hr, hi = hr_ref[0], hi_ref[0]
        row0 = (lax.broadcasted_iota(jnp.int32, ur.shape, 0) + pl.program_id(0) * tm) == 0
        zr_ref[bb] = (ur * hr - jnp.where(row0, 0.0, ui * hi)).astype(zr_ref.dtype)
        zi_ref[bb] = jnp.where(row0, ui * hi, ur * hi + ui * hr).astype(zi_ref.dtype)


def _dft_fwd(fwd, x, x_col0, c, spec_h=None, h_col0=0, tm=512, tn=512):
    b, l, _ = x.shape
    tm = min(tm, l)
    bt = _seqs_per_step(b, l)
    xo, ho = x_col0 // tn, h_col0 // tn
    out_dtype = F32 if spec_h is None else BF16
    fspec = pl.BlockSpec((tm, l), lambda i, bb, j: (i, 0))
    in_specs = [fspec, fspec, pl.BlockSpec((bt, l, tn), lambda i, bb, j: (bb, 0, xo + j))]
    args = [*fwd, x]
    if spec_h is not None:
        hspec = pl.BlockSpec((1, tm, tn), lambda i, bb, j: (0, i, ho + j))
        in_specs += [hspec, hspec]
        args += list(spec_h)
    ospec = pl.BlockSpec((bt, tm, tn), lambda i, bb, j: (bb, i, j))
    return pl.pallas_call(
        functools.partial(_dft_fwd_kernel, with_filter=spec_h is not None, tm=tm),
        grid=(l // tm, b // bt, c // tn),
        in_specs=in_specs,
        out_specs=[ospec, ospec],
        out_shape=[jax.ShapeDtypeStruct((b, l, c), out_dtype)] * 2,
        compiler_params=_cparams("parallel", "parallel", "parallel"),
        name="dft_fwd",
    )(*args)


def _dft_inv_kernel(ic_ref, is_ref, zr_ref, zi_ref, u_ref, m_ref, skip_ref, *rest, with_gate):
    for bb in range(zr_ref.shape[0]):
        y = (jnp.dot(ic_ref[...], zr_ref[bb], preferred_element_type=F32)
             + jnp.dot(is_ref[...], zi_ref[bb], preferred_element_type=F32))
        z = m_ref[bb] * (y + u_ref[bb] * skip_ref[...])
        if with_gate:
            g_ref, o_ref = rest
            o_ref[bb] = (z * _silu(g_ref[bb])).astype(o_ref.dtype)
        else:
            o_ref, ob_ref = rest
            o_ref[bb] = z
            ob_ref[bb] = z.astype(BF16)


def _dft_inv(inv, zr, zi, u, u_col0, mul, mul_col0, skip, gate=None, gate_col0=0, tm=512, tn=512):
    b, l, c = zr.shape
    tm = min(tm, l)
    bt = _seqs_per_step(b, l)
    win = lambda col0: pl.BlockSpec((bt, tm, tn), lambda i, bb, j, o=col0 // tn: (bb, i, o + j))
    zspec = pl.BlockSpec((bt, l, tn), lambda i, bb, j: (bb, 0, j))
    fspec = pl.BlockSpec((tm, l), lambda i, bb, j: (i, 0))
    in_specs = [fspec, fspec, zspec, zspec,
                win(u_col0), win(mul_col0), pl.BlockSpec((1, tn), lambda i, bb, j: (0, j))]
    args = [*inv, zr, zi, u, mul, skip]
    ospec = pl.BlockSpec((bt, tm, tn), lambda i, bb, j: (bb, i, j))
    if gate is not None:
        in_specs.append(win(gate_col0))
        args.append(gate)
        out_specs, out_shape = ospec, jax.ShapeDtypeStruct((b, l, c), BF16)
    else:
        out_specs = [ospec, ospec]
        out_shape = [jax.ShapeDtypeStruct((b, l, c), F32), jax.ShapeDtypeStruct((b, l, c), BF16)]
    return pl.pallas_call(
        functools.partial(_dft_inv_kernel, with_gate=gate is not None),
        grid=(l // tm, b // bt, c // tn),
        in_specs=in_specs,
        out_specs=out_specs,
        out_shape=out_shape,
        compiler_params=_cparams("parallel", "parallel", "parallel"),
        name="dft_inv",
    )(*args)


def _hyena(proj3, p, dft):
    l = proj3.shape[1]
    fwd, inv = dft
    filt = _hy_filter(l, p['hy_w1'], p['hy_b1'], p['hy_w2'], p['hy_b2'], p['hy_w3'], p['hy_b3'], p['hy_decay'])
    filt_b = filt[None]
    spec_h = _dft_fwd(fwd, filt_b, 0, 2 * BR_W)
    pre, pre_b = _hy_pre(proj3, p['hy_conv'])
    skip = p['hy_skip'].astype(F32)
    zr, zi = _dft_fwd(fwd, pre_b, 0, BR_W, spec_h, 0)
    z1, z1_b = _dft_inv(inv, zr, zi, pre, 0, pre, BR_W, skip[0:1])
    zr, zi = _dft_fwd(fwd, z1_b, 0, BR_W, spec_h, BR_W)
    return _dft_inv(inv, zr, zi, z1, 0, pre, 2 * BR_W, skip[1:2], gate=proj3, gate_col0=7 * BR_W)


def _softplus(x):
    return jnp.maximum(x, 0.0) + jnp.log1p(jnp.exp(-jnp.abs(x)))


def _split_bf16(x, parts):
    out = []
    for _ in range(parts - 1):
        piece = x.astype(BF16)
        out.append(piece)
        x = x - piece.astype(F32)
    out.append(x.astype(BF16))
    return out


def _bmm(a, b, hi=False):
    mm = lambda x, y: jnp.einsum('nij,njk->nik', x, y, preferred_element_type=F32)
    if not hi:
        return mm(a.astype(BF16), b.astype(BF16))
    (a1, a2), (b1, b2) = _split_bf16(a, 2), _split_bf16(b, 2)
    return mm(a1, b1) + (mm(a1, b2) + mm(a2, b1))


def _bmm_nt(a, b):
    return jnp.einsum('nid,njd->nij', a.astype(BF16), b.astype(BF16), preferred_element_type=F32)


TRI_BASE = 4


def _unit_tri_inverse(a, ri, ci):
    same = lambda w: (ri // w) == (ci // w)
    eye = (ri == ci).astype(F32)
    x = -jnp.where(same(TRI_BASE), a, 0.0)
    p = eye + x
    for _ in range(TRI_BASE.bit_length() - 2):
        x = _bmm(x, x, hi=True)
        p = p + _bmm(p, x, hi=True)
    w = TRI_BASE
    while w < a.shape[-1]:
        off = jnp.where(same(2 * w) & ~same(w), a, 0.0)
        p = p - _bmm(p, _bmm(off, p))
        w *= 2
    return p


def _gdn_prepare(q, k, v, ab, a_row, dt_row, head0, group):
    n, c, _ = q.shape
    two = lambda x: jnp.concatenate([x, x], axis=0)
    q, k, v, ab = two(q), two(k), two(v), two(ab)
    back3 = lambda shape: lax.broadcasted_iota(jnp.int32, shape, 0) >= n
    lane = lax.broadcasted_iota(jnp.int32, ab.shape, 2)
    bidx = lax.broadcasted_iota(jnp.int32, ab.shape, 0)
    head = head0 + jnp.where(bidx >= n, bidx - n, bidx) // group
    base = jnp.where(bidx >= n, 2 * N_HEAD, 0) + head
    g_all = -a_row * _softplus(ab + dt_row)
    g = jnp.sum(jnp.where(lane == base, g_all, 0.0), axis=2, keepdims=True)
    beta = jnp.sum(jnp.where(lane == base + N_HEAD, _sigmoid(ab), 0.0), axis=2, keepdims=True)

    sq = (2 * n, c, c)
    ri = lax.broadcasted_iota(jnp.int32, sq, 1)
    ci = lax.broadcasted_iota(jnp.int32, sq, 2)
    ahead = jnp.where(back3(sq), ci - ri, ri - ci)
    incl = ahead >= 0
    strict = ahead > 0
    tri = jnp.where(incl, 1.0, 0.0).astype(BF16)
    gc = sum(jnp.einsum('nij,njk->nik', tri, piece, preferred_element_type=F32)
             for piece in _split_bf16(jnp.broadcast_to(g, q.shape), 3))
    gc_row = jnp.swapaxes(gc, 1, 2)[:, :c, :]
    total = jnp.where(back3((2 * n, 1, D_HEAD)), gc[:, 0:1, :], gc[:, c - 1:c, :])
    decay = jnp.where(incl, jnp.exp(jnp.where(incl, gc[:, :, :c] - gc_row, 0.0)), 0.0)

    kb = k * beta
    a = jnp.where(strict, _bmm_nt(kb, k) * decay, 0.0)
    t = _unit_tri_inverse(a, ri, ci)
    e = jnp.exp(gc)
    u = _bmm(t, v * beta)
    w = _bmm(t, kb * e)
    a_intra = jnp.where(incl, _bmm_nt(q, k) * decay, 0.0)
    return (u, w.astype(BF16), (q * e).astype(BF16), (k * jnp.exp(total - gc)).astype(BF16),
            a_intra.astype(BF16), jnp.exp(total))


def _gdn_kernel(*refs, aliased, has_s0, group):
    if aliased:
        refs = refs[1:]
    if has_s0:
        (q_ref, k_ref, v_ref, z_ref, ab_ref, wq_ref, wk_ref, wv_ref, arow_ref, dt_ref, gn_ref, s0_ref,
         y_ref, sf_ref, qn, kn, vn, u_s, w_s, qd_s, kd_s, ai_s, gl_s) = refs
    else:
        (q_ref, k_ref, v_ref, z_ref, ab_ref, wq_ref, wk_ref, wv_ref, arow_ref, dt_ref, gn_ref,
         y_ref, sf_ref, qn, kn, vn, u_s, w_s, qd_s, kd_s, ai_s, gl_s) = refs
    l = q_ref.shape[1]
    heads = q_ref.shape[2] // D_HEAD
    head0 = pl.program_id(1) * heads
    n_chunks = l // CHUNK
    hcols = lambda hh: slice(hh * D_HEAD, (hh + 1) * D_HEAD)

    def l2n(x):
        return x * lax.rsqrt(jnp.sum(x * x, axis=-1, keepdims=True) + EPS)

    for hh in range(heads):
        cols = hcols(hh)
        qn[:, cols] = l2n(_silu(_dwconv3(q_ref[0, :, cols], wq_ref.at[:, cols]))) * (D_HEAD ** -0.5)
        kn[:, cols] = l2n(_silu(_dwconv3(k_ref[0, :, cols], wk_ref.at[:, cols])))
        vn[:, cols] = _silu(_dwconv3(v_ref[0, :, cols], wv_ref.at[:, cols]))

    a_row, dt_row = arow_ref[...], dt_ref[...]

    def prepare(gi, carry):
        span = group * CHUNK
        rows = pl.ds(pl.multiple_of(gi * span, span), span)
        chunks = lambda x: x.reshape(group, CHUNK, x.shape[-1])
        per_head = lambda ref: jnp.concatenate([chunks(ref[rows, hcols(hh)]) for hh in range(heads)], axis=0)
        ab = chunks(ab_ref[0, rows, :])
        u, w, qd, kd, ai, gl = _gdn_prepare(per_head(qn), per_head(kn), per_head(vn),
                                            jnp.concatenate([ab] * heads, axis=0), a_row, dt_row, head0, group)
        for d in range(2):
            for hh in range(heads):
                cols = hcols(hh)
                part = slice((d * heads + hh) * group, (d * heads + hh + 1) * group)
                u_s[d, rows, cols] = u[part].reshape(span, D_HEAD)
                w_s[d, rows, cols] = w[part].reshape(span, D_HEAD)
                qd_s[d, rows, cols] = qd[part].reshape(span, D_HEAD)
                kd_s[d, rows, cols] = kd[part].reshape(span, D_HEAD)
                ai_s[d, hh, rows, :] = ai[part].reshape(span, CHUNK)
                gl_s[d, hh, pl.ds(gi * group, group)] = jnp.broadcast_to(gl[part], (group,) + gl_s.shape[3:])
        return carry

    lax.fori_loop(0, n_chunks // group, prepare, 0)

    def scan(i, s):
        where = [(hh, d, pl.ds(pl.multiple_of(chunk * CHUNK, CHUNK), CHUNK), chunk)
                 for hh in range(heads) for d, chunk in ((0, i), (1, n_chunks - 1 - i))]
        gather = lambda ref: jnp.stack([ref[d, rows, hcols(hh)] for hh, d, rows, _ in where])
        a_intra = jnp.stack([ai_s[d, hh, rows, :] for hh, d, rows, _ in where])
        decay = jnp.stack([gl_s[d, hh, chunk][0:1, :] for hh, d, _, chunk in where])
        sb = s.astype(BF16)
        v_new = gather(u_s) - _bmm(gather(w_s), sb)
        vb = v_new.astype(BF16)
        o = _bmm(gather(qd_s), sb) + _bmm(a_intra, vb)
        for idx, (hh, d, rows, _) in enumerate(where):
            u_s[d, rows, hcols(hh)] = o[idx]
        return s * decay + jnp.einsum('nik,niv->nkv', gather(kd_s), vb, preferred_element_type=F32)

    if has_s0:
        init = jnp.stack([s0_ref[0, 0, d, hh] for hh in range(heads) for d in range(2)])
    else:
        init = jnp.zeros((2 * heads, D_HEAD, D_HEAD), F32)
    final = lax.fori_loop(0, n_chunks, scan, init)
    for hh in range(heads):
        cols = hcols(hh)
        sf_ref[0, 0, 0, hh] = final[2 * hh]
        sf_ref[0, 0, 1, hh] = final[2 * hh + 1]
        y_ref[0, :, cols] = (_rms(u_s[0, :, cols] + u_s[1, :, cols], gn_ref[...])
                             * _silu(z_ref[0, :, cols])).astype(y_ref.dtype)


def _gdn(proj3, ab3, conv_w, a_log, dt_bias, norm_g, layer, state=None, new_state=None):
    b, l, _ = proj3.shape
    depth_out, layer_out = (1, 0) if state is not None else (DEPTH, layer)
    aliased = new_state is not None
    lanes = jnp.zeros((2, 2 * N_HEAD), F32).at[:, :N_HEAD].set(1.0)
    a_row = jnp.pad((jnp.exp(a_log.astype(F32))[:, None, :] * lanes.reshape(2, 2, N_HEAD)).reshape(1, -1),
                    ((0, 0), (0, AB_PAD - 4 * N_HEAD)))
    dt_row = jnp.pad((dt_bias.astype(F32)[:, None, :] * lanes.reshape(2, 2, N_HEAD)).reshape(1, -1),
                     ((0, 0), (0, AB_PAD - 4 * N_HEAD)))
    hps = N_HEAD if l <= 512 else 1
    wid = hps * D_HEAD
    n_hb = N_HEAD // hps
    blk = lambda c: pl.BlockSpec((1, l, wid), lambda i, h, c=c: (i, 0, c * n_hb + h))
    wblk = lambda c: pl.BlockSpec((3, wid), lambda i, h, c=c: (0, c * n_hb + h))
    row = pl.BlockSpec((1, D_HEAD), lambda i, h: (0, 0))
    in_specs = [blk(0), blk(1), blk(2), blk(3),
                pl.BlockSpec((1, l, AB_PAD), lambda i, h: (i, 0, 0)),
                wblk(0), wblk(1), wblk(2), row, row, row]
    args = [proj3, proj3, proj3, proj3, ab3, conv_w, conv_w, conv_w, a_row, dt_row, norm_g]
    if aliased:
        in_specs.insert(0, pl.BlockSpec(memory_space=pl.ANY))
        args.insert(0, new_state)
    if state is not None:
        in_specs.append(pl.BlockSpec((1, 1, 2, hps, D_HEAD, D_HEAD), lambda i, h: (i, layer, 0, h, 0, 0)))
        args.append(state)
    return pl.pallas_call(
        functools.partial(_gdn_kernel, aliased=aliased, has_s0=state is not None, group=min(8, l // CHUNK)),
        grid=(b, n_hb),
        in_specs=in_specs,
        out_specs=[pl.BlockSpec((1, l, wid), lambda i, h: (i, 0, h)),
                   pl.BlockSpec((1, 1, 2, hps, D_HEAD, D_HEAD), lambda i, h: (i, layer_out, 0, h, 0, 0))],
        out_shape=[jax.ShapeDtypeStruct((b, l, BR_W), BF16),
                   jax.ShapeDtypeStruct((b, depth_out, 2, N_HEAD, D_HEAD, D_HEAD), F32)],
        input_output_aliases={0: 1} if aliased else {},
        scratch_shapes=[pltpu.VMEM((l, wid), F32)] * 3
        + [pltpu.VMEM((2, l, wid), F32)] + [pltpu.VMEM((2, l, wid), BF16)] * 3
        + [pltpu.VMEM((2, hps, l, CHUNK), BF16), pltpu.VMEM((2, hps, l // CHUNK, 8, D_HEAD), F32)],
        compiler_params=_cparams("parallel", "parallel"),
        name="gdn",
    )(*args)


def _mod_kernel(c_ref, w_ref, b_ref, o_ref):
    o_ref[...] = _dot_hi(_silu(c_ref[...]), w_ref[...]) + b_ref[...]


def _modulation(cond, w_mod, b_mod, layer, tn=512):
    n = cond.shape[0]
    rows = 8
    out = pl.pallas_call(
        _mod_kernel,
        grid=(3 * D_MODEL // tn,),
        in_specs=[pl.BlockSpec((rows, D_MODEL), lambda j: (0, 0)),
                  pl.BlockSpec((None, D_MODEL, tn), lambda j: (layer, 0, j)),
                  pl.BlockSpec((1, tn), lambda j: (0, j))],
        out_specs=pl.BlockSpec((rows, tn), lambda j: (0, j)),
        out_shape=jax.ShapeDtypeStruct((rows, 3 * D_MODEL), F32),
        compiler_params=_cparams("parallel"),
        name="modulation",
    )(jnp.pad(cond.astype(F32), ((0, rows - n), (0, 0))), w_mod, b_mod.reshape(1, -1))
    return out[:n].reshape(n, 3, D_MODEL)


def _split_w_in(w_in):
    n_a = 4 * BR_W + 4 * N_HEAD
    main = jnp.concatenate([w_in[..., :4 * BR_W], w_in[..., n_a:]], axis=-1).astype(BF16)
    ab = jnp.pad(w_in[..., 4 * BR_W:n_a], ((0, 0),) * (w_in.ndim - 1) + ((0, AB_PAD - 4 * N_HEAD),)).astype(BF16)
    return main, ab


def _trunk_layer(x3, cond, p, big, layer, dft, latent, new_outputs=(None, None, None)):
    b, l, _ = x3.shape
    x2 = x3.reshape(b * l, D_MODEL)
    mod = _modulation(cond, big['w_mod'], p['b_mod'], layer)
    rows_per_mod = l if mod.shape[0] == b else b * l
    g_pre = p['g_pre'].reshape(1, D_MODEL)
    proj, ab = _inproj(x2, mod, g_pre, big['w_main'], big['w_ab'], layer, rows_per_mod)
    proj3 = proj.reshape(b, l, N_MAIN)
    ab3 = ab.reshape(b, l, AB_PAD)

    lam_init = 0.8 - 0.6 * math.exp(-0.3 * layer)
    lam_p = p['diff_lam'].astype(F32)
    lam = (jnp.exp(jnp.sum(lam_p[0] * lam_p[1])) - jnp.exp(jnp.sum(lam_p[2] * lam_p[3])) + lam_init).reshape(1, 1)
    diff_norm = p['diff_norm'].reshape(1, D_HEAD)
    gdn_args = (proj3, ab3, p['gdn_conv'], p['gdn_a_log'], p['gdn_dt_bias'], p['gdn_norm'].reshape(1, D_HEAD), layer)

    yb = _hyena(proj3, p, dft)
    if latent is None:
        new_state, nat_cache, diff_cache = new_outputs
        ya, new_state = _gdn(*gdn_args, new_state=new_state)
        yc, yd, nat_cache, diff_cache = _ctx_attention(proj3, lam, diff_norm, lam_init, layer, nat_cache, diff_cache)
        extras = (new_state, nat_cache, diff_cache)
    else:
        ya, _ = _gdn(*gdn_args, state=latent['state_gdn'])
        yc = _lat_nat(proj3, latent['cache_nat_kv'], layer, _nat_bias_table(p['nat_rpb']))
        yd = _lat_diff(proj3, latent['cache_diff_kv'], layer, lam, diff_norm, lam_init, latent['rope'])
        extras = None

    ys = [t.reshape(b * l, BR_W) for t in (ya, yb, yc, yd)]
    out = _merge(x2, mod, g_pre, p['g_post'].reshape(1, D_MODEL), ys, big['w_branch'], big['w_merge'],
                 p['b_merge'].reshape(1, -1).astype(F32), big['w_out'], layer, rows_per_mod)
    return out.reshape(b, l, D_MODEL), extras


def kernel(x_prompt, x_sample, state_gdn, cache_nat_kv, cache_diff_kv, c, c_ctx,
           w_mod, b_mod, g_pre, g_post, w_in, gdn_conv, gdn_a_log, gdn_dt_bias, gdn_norm,
           hy_conv, hy_w1, hy_b1, hy_w2, hy_b2, hy_w3, hy_b3, hy_decay, hy_skip,
           nat_rpb, diff_lam, diff_norm, w_branch, w_merge, b_merge, w_out):
    small = {
        'b_mod': b_mod, 'g_pre': g_pre, 'g_post': g_post,
        'gdn_conv': gdn_conv, 'gdn_a_log': gdn_a_log, 'gdn_dt_bias': gdn_dt_bias, 'gdn_norm': gdn_norm,
        'hy_conv': hy_conv, 'hy_w1': hy_w1, 'hy_b1': hy_b1, 'hy_w2': hy_w2, 'hy_b2': hy_b2,
        'hy_w3': hy_w3, 'hy_b3': hy_b3, 'hy_decay': hy_decay, 'hy_skip': hy_skip,
        'nat_rpb': nat_rpb, 'diff_lam': diff_lam, 'diff_norm': diff_norm, 'b_merge': b_merge,
    }
    layers = [{name: arr[i] for name, arr in small.items()} for i in range(DEPTH)]
    w_main, w_ab = _split_w_in(w_in)
    big = {'w_mod': w_mod.astype(F32), 'w_main': w_main, 'w_ab': w_ab, 'w_branch': w_branch.astype(BF16),
           'w_merge': w_merge.astype(BF16), 'w_out': w_out.astype(BF16)}

    y_prompt = x_prompt
    dft_ctx = _dft_matrices(x_prompt.shape[1])
    outputs = (None, None, None)
    for i, p in enumerate(layers):
        y_prompt, outputs = _trunk_layer(y_prompt, c_ctx.reshape(1, D_MODEL), p, big, i, dft_ctx, None, outputs)
    new_state, nat_cache, diff_cache = outputs

    y_sample = x_sample
    dft_lat = _dft_matrices(x_sample.shape[1])
    latent = {'state_gdn': state_gdn, 'cache_nat_kv': cache_nat_kv, 'cache_diff_kv': cache_diff_kv,
              'rope': _rope_tables(x_sample.shape[1])}
    for i, p in enumerate(layers):
        y_sample, _ = _trunk_layer(y_sample, c, p, big, i, dft_lat, latent)

    return (y_prompt, y_sample, new_state, nat_cache, diff_cache)
```

```python
import functools
import math

import jax
import jax.numpy as jnp
import numpy as np
from jax import lax
from jax.experimental import pallas as pl
from jax.experimental.pallas import tpu as pltpu

F32 = jnp.float32
BF16 = jnp.bfloat16

D_MODEL = 1024
DEPTH = 2
GRID_W = 64
N_BRANCH = 4
BR_W = 512
N_HEAD = 4
D_HEAD = 128
SUBLANES = 8
CHUNK = 64
HY_BANDS = 16
WIN_R = 8
WIN_C = 16
DQK_D = 64
ROPE_BASE = 10000.0
EPS = 1e-6
N_MAIN = 4 * 4 * BR_W
AB_PAD = 128
NEG_INF = -1e30

VMEM_LIMIT = 48 * 1024 * 1024


def _cparams(*sem):
    return pltpu.CompilerParams(dimension_semantics=sem, vmem_limit_bytes=VMEM_LIMIT)


def _silu(x):
    return x * (1.0 / (1.0 + jnp.exp(-x)))


def _sigmoid(x):
    return 1.0 / (1.0 + jnp.exp(-x))


def _rms(x, g):
    return x * lax.rsqrt(jnp.mean(x * x, axis=-1, keepdims=True) + EPS) * g


def _dot(a, b):
    return jnp.dot(a.astype(BF16), b.astype(BF16), preferred_element_type=F32)


def _dot_nt(a, b):
    return lax.dot_general(a.astype(BF16), b.astype(BF16), (((1,), (1,)), ((), ())),
                           preferred_element_type=F32)


def _dot_tn(a, b):
    return lax.dot_general(a.astype(BF16), b.astype(BF16), (((0,), (0,)), ((), ())),
                           preferred_element_type=F32)


def _prenorm(x, g_pre, mod_ref):
    return _rms(x, g_pre) * (1.0 + mod_ref[0, 1:2, :]) + mod_ref[0, 0:1, :]


def _inproj_kernel(x_ref, mod_ref, gpre_ref, w_ref, wab_ref, proj_ref, ab_ref, h_scr):
    @pl.when(pl.program_id(1) == 0)
    def _():
        h = _prenorm(x_ref[...], gpre_ref[...], mod_ref).astype(BF16)
        h_scr[...] = h
        ab_ref[...] = jnp.dot(h, wab_ref[...], preferred_element_type=F32)

    proj_ref[...] = jnp.dot(h_scr[...], w_ref[...], preferred_element_type=F32)


def _inproj(x2, mod, g_pre, w_main, w_ab, layer, rows_per_mod, tm=1024, tn=1024):
    m = x2.shape[0]
    tm = math.gcd(tm, rows_per_mod)
    return pl.pallas_call(
        _inproj_kernel,
        grid=(m // tm, N_MAIN // tn),
        in_specs=[
            pl.BlockSpec((tm, D_MODEL), lambda i, j: (i, 0)),
            pl.BlockSpec((1, 3, D_MODEL), lambda i, j: ((i * tm) // rows_per_mod, 0, 0)),
            pl.BlockSpec((1, D_MODEL), lambda i, j: (0, 0)),
            pl.BlockSpec((None, D_MODEL, tn), lambda i, j: (layer, 0, j)),
            pl.BlockSpec((None, D_MODEL, AB_PAD), lambda i, j: (layer, 0, 0)),
        ],
        out_specs=[
            pl.BlockSpec((tm, tn), lambda i, j: (i, j)),
            pl.BlockSpec((tm, AB_PAD), lambda i, j: (i, 0)),
        ],
        out_shape=[jax.ShapeDtypeStruct((m, N_MAIN), F32),
                   jax.ShapeDtypeStruct((m, AB_PAD), F32)],
        scratch_shapes=[pltpu.VMEM((tm, D_MODEL), BF16)],
        compiler_params=_cparams("parallel", "arbitrary"),
        name="inproj",
    )(x2, mod, g_pre, w_main, w_ab)


def _merge_kernel(x_ref, mod_ref, gpre_ref, gpost_ref, ya_ref, yb_ref, yc_ref, yd_ref,
                  wbr_ref, wmg_ref, bmg_ref, wout_ref, o_ref):
    x = x_ref[...]
    h = _prenorm(x, gpre_ref[...], mod_ref).astype(BF16)
    acc = None
    for k, y_ref in enumerate((ya_ref, yb_ref, yc_ref, yd_ref)):
        cols = slice(k * D_MODEL, (k + 1) * D_MODEL)
        gate = _sigmoid(jnp.dot(h, wmg_ref[:, cols], preferred_element_type=F32) + bmg_ref[:, cols])
        br = jnp.dot(y_ref[...], wbr_ref[k], preferred_element_type=F32)
        acc = gate * br if acc is None else acc + gate * br
    y = jnp.dot(acc.astype(BF16), wout_ref[...], preferred_element_type=F32)
    o_ref[...] = x + mod_ref[0, 2:3, :] * _rms(y, gpost_ref[...])


def _merge(x2, mod, g_pre, g_post, ys, w_branch, w_merge, b_merge, w_out, layer, rows_per_mod, tm=256):
    m = x2.shape[0]
    row = lambda i: (i, 0)
    fixed2 = lambda i: (0, 0)
    return pl.pallas_call(
        _merge_kernel,
        grid=(m // tm,),
        in_specs=[
            pl.BlockSpec((tm, D_MODEL), row),
            pl.BlockSpec((1, 3, D_MODEL), lambda i: ((i * tm) // rows_per_mod, 0, 0)),
            pl.BlockSpec((1, D_MODEL), fixed2),
            pl.BlockSpec((1, D_MODEL), fixed2),
            pl.BlockSpec((tm, BR_W), row),
            pl.BlockSpec((tm, BR_W), row),
            pl.BlockSpec((tm, BR_W), row),
            pl.BlockSpec((tm, BR_W), row),
            pl.BlockSpec((None, N_BRANCH, BR_W, D_MODEL), lambda i: (layer, 0, 0, 0)),
            pl.BlockSpec((None, D_MODEL, N_BRANCH * D_MODEL), lambda i: (layer, 0, 0)),
            pl.BlockSpec((1, N_BRANCH * D_MODEL), fixed2),
            pl.BlockSpec((None, D_MODEL, D_MODEL), lambda i: (layer, 0, 0)),
        ],
        out_specs=pl.BlockSpec((tm, D_MODEL), row),
        out_shape=jax.ShapeDtypeStruct((m, D_MODEL), F32),
        compiler_params=_cparams("parallel"),
        name="merge",
    )(x2, mod, g_pre, g_post, *ys, w_branch, w_merge, b_merge, w_out)


def _softmax_rows(s):
    p = jnp.exp(s - jnp.max(s, axis=-1, keepdims=True))
    return p, jnp.sum(p, axis=-1, keepdims=True)


def _head_cols(h):
    return slice(h * D_HEAD, (h + 1) * D_HEAD)


def _stack_heads(ref):
    return jnp.stack([ref[0, :, _head_cols(h)] for h in range(N_HEAD)])


def _ctx_nat_kernel(*refs, aliased):
    q_ref, k_ref, v_ref, g_ref, y_ref, kv_ref = refs[1:] if aliased else refs
    scale = D_HEAD ** -0.5
    q, k, v = (_stack_heads(r) for r in (q_ref, k_ref, v_ref))
    p, l = _softmax_rows(_bmm_nt(q, k) * scale)
    o = _bmm(p, v) / l
    for h in range(N_HEAD):
        sl = _head_cols(h)
        y_ref[0, :, sl] = (o[h] * _silu(g_ref[0, :, sl])).astype(y_ref.dtype)
        kv_ref[0, 0, 0, h] = k[h]
        kv_ref[0, 0, 1, h] = v[h]


def _map_masks():
    lane = lax.broadcasted_iota(jnp.int32, (1, D_HEAD), 1)
    first = (lane < DQK_D).astype(F32)
    return first, 1.0 - first


def _ctx_diff_kernel(*refs, aliased, out_scale):
    lam_ref, q_ref, k_ref, v_ref, g_ref, gn_ref, y_ref, kv_ref = refs[1:] if aliased else refs
    scale = DQK_D ** -0.5
    m1, m2 = _map_masks()
    q, k, v = (_stack_heads(r) for r in (q_ref, k_ref, v_ref))
    p, l = _softmax_rows(_bmm_nt(jnp.concatenate([q * m1, q * m2], axis=0), jnp.concatenate([k, k], axis=0)) * scale)
    pn = p / l
    a = pn[:N_HEAD] - lam_ref[...] * pn[N_HEAD:]
    o = _rms(_bmm(a, v), gn_ref[...]) * out_scale
    for h in range(N_HEAD):
        sl = _head_cols(h)
        y_ref[0, :, sl] = (o[h] * _silu(g_ref[0, :, sl])).astype(y_ref.dtype)
        kv_ref[0, 0, 0, h] = k[h]
        kv_ref[0, 0, 1, h] = v[h]


def _ctx_attention(proj3, lam, diff_norm, lam_init, layer, nat_cache, diff_cache):
    b, l, _ = proj3.shape
    blk = lambda c: pl.BlockSpec((1, l, BR_W), lambda i, c=c: (i, 0, c))
    y_spec = pl.BlockSpec((1, l, BR_W), lambda i: (i, 0, 0))
    kv_spec = pl.BlockSpec((1, 1, 2, N_HEAD, l, D_HEAD), lambda i: (i, layer, 0, 0, 0, 0))
    out_shape = [jax.ShapeDtypeStruct((b, l, BR_W), BF16),
                 jax.ShapeDtypeStruct((b, DEPTH, 2, N_HEAD, l, D_HEAD), F32)]
    aliased = nat_cache is not None
    cache_specs = [pl.BlockSpec(memory_space=pl.ANY)] if aliased else []
    aliases = {0: 1} if aliased else {}
    yc, nat_cache = pl.pallas_call(
        functools.partial(_ctx_nat_kernel, aliased=aliased),
        grid=(b,),
        in_specs=cache_specs + [blk(8), blk(9), blk(10), blk(11)],
        out_specs=[y_spec, kv_spec],
        out_shape=out_shape,
        input_output_aliases=aliases,
        compiler_params=_cparams("parallel"),
        name="ctx_nat",
    )(*([nat_cache] if aliased else []), proj3, proj3, proj3, proj3)
    yd, diff_cache = pl.pallas_call(
        functools.partial(_ctx_diff_kernel, aliased=aliased, out_scale=1.0 - lam_init),
        grid=(b,),
        in_specs=cache_specs + [pl.BlockSpec((1, 1), lambda i: (0, 0)),
                                blk(12), blk(13), blk(14), blk(15),
                                pl.BlockSpec((1, D_HEAD), lambda i: (0, 0))],
        out_specs=[y_spec, kv_spec],
        out_shape=out_shape,
        input_output_aliases=aliases,
        compiler_params=_cparams("parallel"),
        name="ctx_diff",
    )(*([diff_cache] if aliased else []), lam, proj3, proj3, proj3, proj3, diff_norm)
    return yc, yd, nat_cache, diff_cache


def _nat_bias_table(rpb):
    cols = np.arange(GRID_W)
    start = np.clip(cols - WIN_C // 2, 0, GRID_W - WIN_C)
    inside = (cols[None, :] >= start[:, None]) & (cols[None, :] < start[:, None] + WIN_C)
    dc = cols[None, :] - cols[:, None] + (WIN_C - 1)
    onehot = ((dc[None] == np.arange(2 * WIN_C - 1)[:, None, None]) & inside[None]).astype(np.float32)
    t = jnp.einsum('hdx,xck->hdck', rpb.astype(F32), jnp.asarray(onehot), precision=lax.Precision.HIGHEST)
    t = jnp.where(jnp.asarray(inside)[None, None], t, NEG_INF)
    tab = jnp.stack([t[:, WIN_R - 1 - off:2 * WIN_R - 1 - off] for off in range(WIN_R)], axis=1)
    return tab.transpose(0, 1, 3, 2, 4).reshape(rpb.shape[0], WIN_R, GRID_W, WIN_R * GRID_W)


def _lat_nat_kernel(q_ref, k_ref, v_ref, g_ref, ckv_ref, bias_ref, y_ref, kb_scr, vb_scr, *, rb):
    scale = D_HEAD ** -0.5
    rows = q_ref.shape[1] // GRID_W
    win = WIN_R * GRID_W
    kb_scr[...] = k_ref[0].astype(BF16)
    vb_scr[...] = v_ref[0].astype(BF16)
    ck = ckv_ref[0, 0, 0, 0].astype(BF16)
    cv = ckv_ref[0, 0, 1, 0].astype(BF16)

    def row_block(i, carry):
        q0 = pl.multiple_of(i * (rb * GRID_W), rb * GRID_W)
        qrows = pl.ds(q0, rb * GRID_W)
        q = q_ref[0, qrows, :].astype(BF16)
        kw, vw, bias = [], [], []
        for j in range(rb):
            r = i * rb + j
            rs = jnp.clip(r - WIN_R // 2, 0, rows - WIN_R)
            wrows = pl.ds(pl.multiple_of(rs * GRID_W, GRID_W), win)
            kw.append(kb_scr[wrows, :])
            vw.append(vb_scr[wrows, :])
            bias.append(bias_ref[0, r - rs])
        q3 = q.reshape(rb, GRID_W, D_HEAD)
        s_lat = _bmm_nt(q3, jnp.stack(kw)) * scale + jnp.stack(bias)
        s_ctx = (_dot_nt(q, ck) * scale).reshape(rb, GRID_W, ck.shape[0])
        m = jnp.maximum(jnp.max(s_lat, axis=-1, keepdims=True), jnp.max(s_ctx, axis=-1, keepdims=True))
        p_lat = jnp.exp(s_lat - m)
        p_ctx = jnp.exp(s_ctx - m)
        l = jnp.sum(p_lat, axis=-1, keepdims=True) + jnp.sum(p_ctx, axis=-1, keepdims=True)
        o_ctx = _dot(p_ctx.reshape(rb * GRID_W, ck.shape[0]), cv).reshape(rb, GRID_W, D_HEAD)
        o = ((_bmm(p_lat, jnp.stack(vw)) + o_ctx) / l).reshape(rb * GRID_W, D_HEAD)
        y_ref[0, qrows, :] = (o * _silu(g_ref[0, qrows, :])).astype(y_ref.dtype)
        return carry

    lax.fori_loop(0, rows // rb, row_block, 0)


def _lat_nat(proj3, cache_nat_kv, layer, bias_tab):
    b, l, _ = proj3.shape
    past = cache_nat_kv.shape[4]
    blk = lambda c: pl.BlockSpec((1, l, D_HEAD), lambda i, h, c=c: (i, 0, c + h))
    return pl.pallas_call(
        functools.partial(_lat_nat_kernel, rb=8),
        grid=(b, N_HEAD),
        in_specs=[blk(32), blk(36), blk(40), blk(44),
                  pl.BlockSpec((1, 1, 2, 1, past, D_HEAD), lambda i, h: (i, layer, 0, h, 0, 0)),
                  pl.BlockSpec((1, WIN_R, GRID_W, WIN_R * GRID_W), lambda i, h: (h, 0, 0, 0))],
        out_specs=pl.BlockSpec((1, l, D_HEAD), lambda i, h: (i, 0, h)),
        out_shape=jax.ShapeDtypeStruct((b, l, BR_W), BF16),
        scratch_shapes=[pltpu.VMEM((l, D_HEAD), BF16), pltpu.VMEM((l, D_HEAD), BF16)],
        compiler_params=_cparams("parallel", "parallel"),
        name="lat_nat",
    )(proj3, proj3, proj3, proj3, cache_nat_kv, bias_tab)


def _rope_tables(l):
    half = DQK_D // 2
    nf = half // 2
    t = jnp.arange(l)
    row = (t // GRID_W).astype(F32)
    col = (t % GRID_W).astype(F32)
    inv = ROPE_BASE ** (-jnp.arange(nf, dtype=F32) / nf)
    ang = jnp.concatenate([row[:, None] * inv, col[:, None] * inv], axis=-1)
    cos, sin = jnp.cos(ang), jnp.sin(ang)
    zero = jnp.zeros_like(sin)
    tile2 = lambda a, b: jnp.concatenate([a, b, a, b], axis=-1)
    return tile2(cos, cos), tile2(-sin, zero), tile2(zero, sin)


def _rope(x, cos, sin_a, sin_b):
    return x * cos + pltpu.roll(x, 96, 1) * sin_a + pltpu.roll(x, 32, 1) * sin_b


def _lat_diff_kernel(lam_ref, q_ref, k_ref, v_ref, g_ref, ckv_ref, gn_ref,
                     cq_ref, saq_ref, sbq_ref, ck_ref, sak_ref, sbk_ref,
                     y_ref, ks_scr, vs_scr, *, out_scale, prep_rows, key_block):
    scale = DQK_D ** -0.5
    l = k_ref.shape[1]

    @pl.when(pl.program_id(2) == 0)
    def _():
        def prep(i, carry):
            rows = pl.ds(pl.multiple_of(i * prep_rows, prep_rows), prep_rows)
            kr = _rope(k_ref[0, rows, :], ck_ref[rows, :], sak_ref[rows, :], sbk_ref[rows, :])
            ks_scr[rows, :] = kr.astype(BF16)
            vs_scr[rows, :] = v_ref[0, rows, :].astype(BF16)
            return carry

        lax.fori_loop(0, l // prep_rows, prep, 0)
        ks_scr[l:, :] = ckv_ref[0, 0, 0, 0].astype(BF16)
        vs_scr[l:, :] = ckv_ref[0, 0, 1, 0].astype(BF16)

    q = _rope(q_ref[0], cq_ref[...], saq_ref[...], sbq_ref[...]) * scale
    m1, m2 = _map_masks()
    tq = q.shape[0]
    qm = jnp.concatenate([q * m1, q * m2], axis=0).astype(BF16)
    m = l_sum = acc = None
    for blk in range(ks_scr.shape[0] // key_block):
        rows = slice(blk * key_block, (blk + 1) * key_block)
        s = _dot_nt(qm, ks_scr[rows, :])
        m_blk = jnp.max(s, axis=-1, keepdims=True)
        if blk == 0:
            m = m_blk
            p = jnp.exp(s - m)
            l_sum = jnp.sum(p, axis=-1, keepdims=True)
            acc = _dot(p, vs_scr[rows, :])
        else:
            m_new = jnp.maximum(m, m_blk)
            alpha = jnp.exp(m - m_new)
            p = jnp.exp(s - m_new)
            l_sum = alpha * l_sum + jnp.sum(p, axis=-1, keepdims=True)
            acc = alpha * acc + _dot(p, vs_scr[rows, :])
            m = m_new
    out = acc / l_sum
    o = _rms(out[:tq] - lam_ref[...] * out[tq:], gn_ref[...]) * out_scale
    y_ref[0] = (o * _silu(g_ref[0])).astype(y_ref.dtype)


def _lat_diff(proj3, cache_diff_kv, layer, lam, diff_norm, lam_init, rope_tabs, tq=256):
    b, l, _ = proj3.shape
    past = cache_diff_kv.shape[4]
    qblk = lambda c: pl.BlockSpec((1, tq, D_HEAD), lambda i, h, j, c=c: (i, j, c + h))
    full = lambda c: pl.BlockSpec((1, l, D_HEAD), lambda i, h, j, c=c: (i, 0, c + h))
    tq_tab = pl.BlockSpec((tq, D_HEAD), lambda i, h, j: (j, 0))
    full_tab = pl.BlockSpec((l, D_HEAD), lambda i, h, j: (0, 0))
    return pl.pallas_call(
        functools.partial(_lat_diff_kernel, out_scale=1.0 - lam_init, prep_rows=512, key_block=256),
        grid=(b, N_HEAD, l // tq),
        in_specs=[pl.BlockSpec((1, 1), lambda i, h, j: (0, 0)),
                  qblk(48), full(52), full(56), qblk(60),
                  pl.BlockSpec((1, 1, 2, 1, past, D_HEAD), lambda i, h, j: (i, layer, 0, h, 0, 0)),
                  pl.BlockSpec((1, D_HEAD), lambda i, h, j: (0, 0)),
                  tq_tab, tq_tab, tq_tab, full_tab, full_tab, full_tab],
        out_specs=pl.BlockSpec((1, tq, D_HEAD), lambda i, h, j: (i, j, h)),
        out_shape=jax.ShapeDtypeStruct((b, l, BR_W), BF16),
        scratch_shapes=[pltpu.VMEM((l + past, D_HEAD), BF16), pltpu.VMEM((l + past, D_HEAD), BF16)],
        compiler_params=_cparams("parallel", "parallel", "arbitrary"),
        name="lat_diff",
    )(lam, proj3, proj3, proj3, proj3, cache_diff_kv, diff_norm, *rope_tabs, *rope_tabs)


def _dwconv3(x, w_ref):
    l = x.shape[0]
    row = lax.broadcasted_iota(jnp.int32, x.shape, 0)
    prev = jnp.where(row == 0, 0.0, pltpu.roll(x, 1, 0))
    nxt = jnp.where(row == l - 1, 0.0, pltpu.roll(x, l - 1, 0))
    return prev * w_ref[0:1, :] + x * w_ref[1:2, :] + nxt * w_ref[2:3, :]


def _hy_pre_kernel(x_ref, above_ref, below_ref, w_ref, o_ref, ob_ref):
    t, n_t = pl.program_id(1), pl.num_programs(1)
    rows = x_ref.shape[1]
    row = lax.broadcasted_iota(jnp.int32, x_ref.shape[1:], 0)
    for bb in range(x_ref.shape[0]):
        x = x_ref[bb]
        before = jnp.where(t == 0, 0.0, above_ref[bb, SUBLANES - 1:SUBLANES, :])
        after = jnp.where(t == n_t - 1, 0.0, below_ref[bb, 0:1, :])
        prev = jnp.where(row == 0, before, pltpu.roll(x, 1, 0))
        nxt = jnp.where(row == rows - 1, after, pltpu.roll(x, rows - 1, 0))
        y = prev * w_ref[0:1, :] + x * w_ref[1:2, :] + nxt * w_ref[2:3, :]
        o_ref[bb] = y
        ob_ref[bb] = y.astype(BF16)


def _hy_pre(proj3, conv_w, tl=1024):
    b, l, _ = proj3.shape
    tl = min(tl, l)
    bt = _seqs_per_step(b, l)
    n = 3
    col0 = 4
    groups = tl // SUBLANES
    last_group = l // SUBLANES - 1
    spec = pl.BlockSpec((bt, tl, BR_W), lambda i, t, j: (i, t, j))
    return pl.pallas_call(
        _hy_pre_kernel,
        grid=(b // bt, l // tl, n),
        in_specs=[pl.BlockSpec((bt, tl, BR_W), lambda i, t, j: (i, t, col0 + j)),
                  pl.BlockSpec((bt, SUBLANES, BR_W),
                               lambda i, t, j: (i, jnp.maximum(t * groups - 1, 0), col0 + j)),
                  pl.BlockSpec((bt, SUBLANES, BR_W),
                               lambda i, t, j: (i, jnp.minimum((t + 1) * groups, last_group), col0 + j)),
                  pl.BlockSpec((3, BR_W), lambda i, t, j: (0, j))],
        out_specs=[spec, spec],
        out_shape=[jax.ShapeDtypeStruct((b, l, 3 * BR_W), F32),
                   jax.ShapeDtypeStruct((b, l, 3 * BR_W), BF16)],
        compiler_params=_cparams("parallel", "parallel", "parallel"),
        name="hy_pre",
    )(proj3, proj3, proj3, conv_w)


def _dot_hi(a, b):
    return jnp.dot(a, b, preferred_element_type=F32, precision=lax.Precision.HIGHEST)


def _hy_filter_kernel(feat_ref, dist_ref, w1_ref, b1_ref, w2_ref, b2_ref, w3_ref, b3_ref, dec_ref, o_ref):
    hid = jnp.sin(_dot_hi(feat_ref[...], w1_ref[...]) + b1_ref[...])
    hid = jnp.sin(_dot_hi(hid, w2_ref[...]) + b2_ref[...])
    dist = dist_ref[...]
    for j in range(o_ref.shape[1] // D_HEAD):
        cols = slice(j * D_HEAD, (j + 1) * D_HEAD)
        filt = _dot_hi(hid, w3_ref[:, cols]) + b3_ref[:, cols]
        o_ref[:, cols] = (filt * jnp.exp(-dist * jnp.abs(dec_ref[:, cols]))).astype(o_ref.dtype)


def _hy_filter(l, w1, b1, w2, b2, w3, b3, decay):
    pos = jnp.arange(l, dtype=F32)
    t = pos / l
    ang = (2.0 * math.pi) * t[:, None] * jnp.arange(1, HY_BANDS + 1, dtype=F32)
    feat = jnp.concatenate([t[:, None], jnp.cos(ang), jnp.sin(ang)], axis=-1)
    dist = jnp.broadcast_to((jnp.abs(pos - l // 2) / l)[:, None], (l, D_HEAD))
    pad = D_HEAD
    emb, ff = w1.shape
    feat = jnp.pad(feat, ((0, 0), (0, pad - emb)))
    w1p = jnp.pad(w1, ((0, pad - emb), (0, pad - ff)))
    w2p = jnp.pad(w2, ((0, pad - ff), (0, pad - ff)))
    w3p = jnp.pad(w3, ((0, pad - ff), (0, 0)))
    b1p = jnp.pad(b1, (0, pad - ff)).reshape(1, pad)
    b2p = jnp.pad(b2, (0, pad - ff)).reshape(1, pad)
    tl = min(l, 256)
    n = 2 * BR_W
    fixed = lambda shape: pl.BlockSpec(shape, lambda i: (0, 0))
    return pl.pallas_call(
        _hy_filter_kernel,
        grid=(l // tl,),
        in_specs=[pl.BlockSpec((tl, pad), lambda i: (i, 0)),
                  pl.BlockSpec((tl, D_HEAD), lambda i: (i, 0)),
                  fixed((pad, pad)), fixed((1, pad)), fixed((pad, pad)), fixed((1, pad)),
                  fixed((pad, n)), fixed((1, n)), fixed((1, n))],
        out_specs=pl.BlockSpec((tl, n), lambda i: (i, 0)),
        out_shape=jax.ShapeDtypeStruct((l, n), BF16),
        compiler_params=_cparams("parallel"),
        name="hy_filter",
    )(feat, dist, w1p, b1p, w2p, b2p, w3p, b3.reshape(1, n), decay.reshape(1, n))


def _dft_matrices(l):
    n = 2 * l
    k = jnp.arange(l, dtype=jnp.int32)
    t = jnp.arange(l, dtype=jnp.int32)
    tp = t + l // 2
    split = 1 << (max(l.bit_length() - 1, 0) // 2)

    def tables(rows, cols):
        def table(r):
            ang = (2.0 * math.pi / n) * ((r[:, None] * cols[None, :]) % n).astype(F32)
            return jnp.cos(ang), jnp.sin(ang)
        return (*table(rows[::split]), *table(rows[:split] - rows[0]))

    alt = jnp.where(t % 2 == 0, 1.0, -1.0).astype(F32).reshape(1, l)
    wk = (jnp.where(k == 0, 1.0, 2.0).astype(F32) / n).reshape(1, l)
    out = pl.pallas_call(
        functools.partial(_dft_gen_kernel, split=split, l=l),
        grid=(l // split,),
        in_specs=[pl.BlockSpec((l // split, l), lambda i: (0, 0))] * 2 + [pl.BlockSpec((split, l), lambda i: (0, 0))] * 2
        + [pl.BlockSpec((l // split, l), lambda i: (0, 0))] * 2 + [pl.BlockSpec((split, l), lambda i: (0, 0))] * 2
        + [pl.BlockSpec((1, l), lambda i: (0, 0))] * 2,
        out_specs=[pl.BlockSpec((split, l), lambda i: (i, 0))] * 4,
        out_shape=[jax.ShapeDtypeStruct((l, l), BF16)] * 4,
        compiler_params=_cparams("parallel"),
        name="dft_gen",
    )(*tables(k, t), *tables(tp, k), alt, wk)
    return (out[0], out[1]), (out[2], out[3])


def _dft_gen_kernel(ch_ref, sh_ref, cl_ref, sl_ref, chi_ref, shi_ref, cli_ref, sli_ref, alt_ref, wk_ref,
                    fc_ref, fs_ref, ic_ref, is_ref, *, split, l):
    i = pl.program_id(0)

    def cos_sin(c_hi, s_hi, c_lo, s_lo):
        ch, sh = c_hi[pl.ds(i, 1), :], s_hi[pl.ds(i, 1), :]
        return ch * c_lo[...] - sh * s_lo[...], sh * c_lo[...] + ch * s_lo[...]

    row = lax.broadcasted_iota(jnp.int32, (split, l), 0) + i * split
    col = lax.broadcasted_iota(jnp.int32, (split, l), 1)
    c, s = cos_sin(ch_ref, sh_ref, cl_ref, sl_ref)
    fc_ref[...] = c.astype(BF16)
    fs_ref[...] = jnp.where(row == 0, alt_ref[...], -s).astype(BF16)
    c, s = cos_sin(chi_ref, shi_ref, cli_ref, sli_ref)
    wk = wk_ref[...]
    alt_i = jnp.where(row % 2 == 0, 1.0, -1.0) * (1.0 / (2 * l))
    ic_ref[...] = (c * wk).astype(BF16)
    is_ref[...] = jnp.where(col == 0, alt_i, -s * wk).astype(BF16)


def _seqs_per_step(b, l, rows=2048):
    bt = max(1, min(b, rows // l))
    while b % bt:
        bt -= 1
    return bt


def _dft_fwd_kernel(fc_ref, fs_ref, x_ref, *rest, with_filter, tm):
    for bb in range(x_ref.shape[0]):
        x = x_ref[bb]
        ur = jnp.dot(fc_ref[...], x, preferred_element_type=F32)
        ui = jnp.dot(fs_ref[...], x, preferred_element_type=F32)
        if not with_filter:
            zr_ref, zi_ref = rest
            zr_ref[bb] = ur
            zi_ref[bb] = ui
            continue
        hr_ref, hi_ref, zr_ref, zi_ref = rest
        hr, hi = hr_ref[0], hi_ref[0]
        row0 = (lax.broadcasted_iota(jnp.int32, ur.shape, 0) + pl.program_id(0) * tm) == 0
        zr_ref[bb] = (ur * hr - jnp.where(row0, 0.0, ui * hi)).astype(zr_ref.dtype)
        zi_ref[bb] = jnp.where(row0, ui * hi, ur * hi + ui * hr).astype(zi_ref.dtype)


def _dft_fwd(fwd, x, x_col0, c, spec_h=None, h_col0=0, tm=512, tn=512):
    b, l, _ = x.shape
    tm = min(tm, l)
    bt = _seqs_per_step(b, l)
    xo, ho = x_col0 // tn, h_col0 // tn
    out_dtype = F32 if spec_h is None else BF16
    fspec = pl.BlockSpec((tm, l), lambda i, bb, j: (i, 0))
    in_specs = [fspec, fspec, pl.BlockSpec((bt, l, tn), lambda i, bb, j: (bb, 0, xo + j))]
    args = [*fwd, x]
    if spec_h is not None:
        hspec = pl.BlockSpec((1, tm, tn), lambda i, bb, j: (0, i, ho + j))
        in_specs += [hspec, hspec]
        args += list(spec_h)
    ospec = pl.BlockSpec((bt, tm, tn), lambda i, bb, j: (bb, i, j))
    return pl.pallas_call(
        functools.partial(_dft_fwd_kernel, with_filter=spec_h is not None, tm=tm),
        grid=(l // tm, b // bt, c // tn),
        in_specs=in_specs,
        out_specs=[ospec, ospec],
        out_shape=[jax.ShapeDtypeStruct((b, l, c), out_dtype)] * 2,
        compiler_params=_cparams("parallel", "parallel", "parallel"),
        name="dft_fwd",
    )(*args)


def _dft_inv_kernel(ic_ref, is_ref, zr_ref, zi_ref, u_ref, m_ref, skip_ref, *rest, with_gate):
    for bb in range(zr_ref.shape[0]):
        y = (jnp.dot(ic_ref[...], zr_ref[bb], preferred_element_type=F32)
             + jnp.dot(is_ref[...], zi_ref[bb], preferred_element_type=F32))
        z = m_ref[bb] * (y + u_ref[bb] * skip_ref[...])
        if with_gate:
            g_ref, o_ref = rest
            o_ref[bb] = (z * _silu(g_ref[bb])).astype(o_ref.dtype)
        else:
            o_ref, ob_ref = rest
            o_ref[bb] = z
            ob_ref[bb] = z.astype(BF16)


def _dft_inv(inv, zr, zi, u, u_col0, mul, mul_col0, skip, gate=None, gate_col0=0, tm=512, tn=512):
    b, l, c = zr.shape
    tm = min(tm, l)
    bt = _seqs_per_step(b, l)
    win = lambda col0: pl.BlockSpec((bt, tm, tn), lambda i, bb, j, o=col0 // tn: (bb, i, o + j))
    zspec = pl.BlockSpec((bt, l, tn), lambda i, bb, j: (bb, 0, j))
    fspec = pl.BlockSpec((tm, l), lambda i, bb, j: (i, 0))
    in_specs = [fspec, fspec, zspec, zspec,
                win(u_col0), win(mul_col0), pl.BlockSpec((1, tn), lambda i, bb, j: (0, j))]
    args = [*inv, zr, zi, u, mul, skip]
    ospec = pl.BlockSpec((bt, tm, tn), lambda i, bb, j: (bb, i, j))
    if gate is not None:
        in_specs.append(win(gate_col0))
        args.append(gate)
        out_specs, out_shape = ospec, jax.ShapeDtypeStruct((b, l, c), BF16)
    else:
        out_specs = [ospec, ospec]
        out_shape = [jax.ShapeDtypeStruct((b, l, c), F32), jax.ShapeDtypeStruct((b, l, c), BF16)]
    return pl.pallas_call(
        functools.partial(_dft_inv_kernel, with_gate=gate is not None),
        grid=(l // tm, b // bt, c // tn),
        in_specs=in_specs,
        out_specs=out_specs,
        out_shape=out_shape,
        compiler_params=_cparams("parallel", "parallel", "parallel"),
        name="dft_inv",
    )(*args)


def _hyena(proj3, p, dft):
    l = proj3.shape[1]
    fwd, inv = dft
    filt = _hy_filter(l, p['hy_w1'], p['hy_b1'], p['hy_w2'], p['hy_b2'], p['hy_w3'], p['hy_b3'], p['hy_decay'])
    filt_b = filt[None]
    spec_h = _dft_fwd(fwd, filt_b, 0, 2 * BR_W)
    pre, pre_b = _hy_pre(proj3, p['hy_conv'])
    skip = p['hy_skip'].astype(F32)
    zr, zi = _dft_fwd(fwd, pre_b, 0, BR_W, spec_h, 0)
    z1, z1_b = _dft_inv(inv, zr, zi, pre, 0, pre, BR_W, skip[0:1])
    zr, zi = _dft_fwd(fwd, z1_b, 0, BR_W, spec_h, BR_W)
    return _dft_inv(inv, zr, zi, z1, 0, pre, 2 * BR_W, skip[1:2], gate=proj3, gate_col0=7 * BR_W)


def _softplus(x):
    return jnp.maximum(x, 0.0) + jnp.log1p(jnp.exp(-jnp.abs(x)))


def _split_bf16(x, parts):
    out = []
    for _ in range(parts - 1):
        piece = x.astype(BF16)
        out.append(piece)
        x = x - piece.astype(F32)
    out.append(x.astype(BF16))
    return out


def _bmm(a, b, hi=False):
    mm = lambda x, y: jnp.einsum('nij,njk->nik', x, y, preferred_element_type=F32)
    if not hi:
        return mm(a.astype(BF16), b.astype(BF16))
    (a1, a2), (b1, b2) = _split_bf16(a, 2), _split_bf16(b, 2)
    return mm(a1, b1) + (mm(a1, b2) + mm(a2, b1))


def _bmm_nt(a, b):
    return jnp.einsum('nid,njd->nij', a.astype(BF16), b.astype(BF16), preferred_element_type=F32)


TRI_BASE = 4


def _unit_tri_inverse(a, ri, ci):
    same = lambda w: (ri // w) == (ci // w)
    eye = (ri == ci).astype(F32)
    x = -jnp.where(same(TRI_BASE), a, 0.0)
    p = eye + x
    for _ in range(TRI_BASE.bit_length() - 2):
        x = _bmm(x, x, hi=True)
        p = p + _bmm(p, x, hi=True)
    w = TRI_BASE
    while w < a.shape[-1]:
        off = jnp.where(same(2 * w) & ~same(w), a, 0.0)
        p = p - _bmm(p, _bmm(off, p))
        w *= 2
    return p


def _gdn_prepare(q, k, v, ab, a_row, dt_row, head0, group):
    n, c, _ = q.shape
    two = lambda x: jnp.concatenate([x, x], axis=0)
    q, k, v, ab = two(q), two(k), two(v), two(ab)
    back3 = lambda shape: lax.broadcasted_iota(jnp.int32, shape, 0) >= n
    lane = lax.broadcasted_iota(jnp.int32, ab.shape, 2)
    bidx = lax.broadcasted_iota(jnp.int32, ab.shape, 0)
    head = head0 + jnp.where(bidx >= n, bidx - n, bidx) // group
    base = jnp.where(bidx >= n, 2 * N_HEAD, 0) + head
    g_all = -a_row * _softplus(ab + dt_row)
    g = jnp.sum(jnp.where(lane == base, g_all, 0.0), axis=2, keepdims=True)
    beta = jnp.sum(jnp.where(lane == base + N_HEAD, _sigmoid(ab), 0.0), axis=2, keepdims=True)

    sq = (2 * n, c, c)
    ri = lax.broadcasted_iota(jnp.int32, sq, 1)
    ci = lax.broadcasted_iota(jnp.int32, sq, 2)
    ahead = jnp.where(back3(sq), ci - ri, ri - ci)
    incl = ahead >= 0
    strict = ahead > 0
    tri = jnp.where(incl, 1.0, 0.0).astype(BF16)
    gc = sum(jnp.einsum('nij,njk->nik', tri, piece, preferred_element_type=F32)
             for piece in _split_bf16(jnp.broadcast_to(g, q.shape), 3))
    gc_row = jnp.swapaxes(gc, 1, 2)[:, :c, :]
    total = jnp.where(back3((2 * n, 1, D_HEAD)), gc[:, 0:1, :], gc[:, c - 1:c, :])
    decay = jnp.where(incl, jnp.exp(jnp.where(incl, gc[:, :, :c] - gc_row, 0.0)), 0.0)

    kb = k * beta
    a = jnp.where(strict, _bmm_nt(kb, k) * decay, 0.0)
    t = _unit_tri_inverse(a, ri, ci)
    e = jnp.exp(gc)
    u = _bmm(t, v * beta)
    w = _bmm(t, kb * e)
    a_intra = jnp.where(incl, _bmm_nt(q, k) * decay, 0.0)
    return (u, w.astype(BF16), (q * e).astype(BF16), (k * jnp.exp(total - gc)).astype(BF16),
            a_intra.astype(BF16), jnp.exp(total))


def _gdn_kernel(*refs, aliased, has_s0, group):
    if aliased:
        refs = refs[1:]
    if has_s0:
        (q_ref, k_ref, v_ref, z_ref, ab_ref, wq_ref, wk_ref, wv_ref, arow_ref, dt_ref, gn_ref, s0_ref,
         y_ref, sf_ref, qn, kn, vn, u_s, w_s, qd_s, kd_s, ai_s, gl_s) = refs
    else:
        (q_ref, k_ref, v_ref, z_ref, ab_ref, wq_ref, wk_ref, wv_ref, arow_ref, dt_ref, gn_ref,
         y_ref, sf_ref, qn, kn, vn, u_s, w_s, qd_s, kd_s, ai_s, gl_s) = refs
    l = q_ref.shape[1]
    heads = q_ref.shape[2] // D_HEAD
    head0 = pl.program_id(1) * heads
    n_chunks = l // CHUNK
    hcols = lambda hh: slice(hh * D_HEAD, (hh + 1) * D_HEAD)

    def l2n(x):
        return x * lax.rsqrt(jnp.sum(x * x, axis=-1, keepdims=True) + EPS)

    for hh in range(heads):
        cols = hcols(hh)
        qn[:, cols] = l2n(_silu(_dwconv3(q_ref[0, :, cols], wq_ref.at[:, cols]))) * (D_HEAD ** -0.5)
        kn[:, cols] = l2n(_silu(_dwconv3(k_ref[0, :, cols], wk_ref.at[:, cols])))
        vn[:, cols] = _silu(_dwconv3(v_ref[0, :, cols], wv_ref.at[:, cols]))

    a_row, dt_row = arow_ref[...], dt_ref[...]

    def prepare(gi, carry):
        span = group * CHUNK
        rows = pl.ds(pl.multiple_of(gi * span, span), span)
        chunks = lambda x: x.reshape(group, CHUNK, x.shape[-1])
        per_head = lambda ref: jnp.concatenate([chunks(ref[rows, hcols(hh)]) for hh in range(heads)], axis=0)
        ab = chunks(ab_ref[0, rows, :])
        u, w, qd, kd, ai, gl = _gdn_prepare(per_head(qn), per_head(kn), per_head(vn),
                                            jnp.concatenate([ab] * heads, axis=0), a_row, dt_row, head0, group)
        for d in range(2):
            for hh in range(heads):
                cols = hcols(hh)
                part = slice((d * heads + hh) * group, (d * heads + hh + 1) * group)
                u_s[d, rows, cols] = u[part].reshape(span, D_HEAD)
                w_s[d, rows, cols] = w[part].reshape(span, D_HEAD)
                qd_s[d, rows, cols] = qd[part].reshape(span, D_HEAD)
                kd_s[d, rows, cols] = kd[part].reshape(span, D_HEAD)
                ai_s[d, hh, rows, :] = ai[part].reshape(span, CHUNK)
                gl_s[d, hh, pl.ds(gi * group, group)] = jnp.broadcast_to(gl[part], (group,) + gl_s.shape[3:])
        return carry

    lax.fori_loop(0, n_chunks // group, prepare, 0)

    def scan(i, s):
        where = [(hh, d, pl.ds(pl.multiple_of(chunk * CHUNK, CHUNK), CHUNK), chunk)
                 for hh in range(heads) for d, chunk in ((0, i), (1, n_chunks - 1 - i))]
        gather = lambda ref: jnp.stack([ref[d, rows, hcols(hh)] for hh, d, rows, _ in where])
        a_intra = jnp.stack([ai_s[d, hh, rows, :] for hh, d, rows, _ in where])
        decay = jnp.stack([gl_s[d, hh, chunk][0:1, :] for hh, d, _, chunk in where])
        sb = s.astype(BF16)
        v_new = gather(u_s) - _bmm(gather(w_s), sb)
        vb = v_new.astype(BF16)
        o = _bmm(gather(qd_s), sb) + _bmm(a_intra, vb)
        for idx, (hh, d, rows, _) in enumerate(where):
            u_s[d, rows, hcols(hh)] = o[idx]
        return s * decay + jnp.einsum('nik,niv->nkv', gather(kd_s), vb, preferred_element_type=F32)

    if has_s0:
        init = jnp.stack([s0_ref[0, 0, d, hh] for hh in range(heads) for d in range(2)])
    else:
        init = jnp.zeros((2 * heads, D_HEAD, D_HEAD), F32)
    final = lax.fori_loop(0, n_chunks, scan, init)
    for hh in range(heads):
        cols = hcols(hh)
        sf_ref[0, 0, 0, hh] = final[2 * hh]
        sf_ref[0, 0, 1, hh] = final[2 * hh + 1]
        y_ref[0, :, cols] = (_rms(u_s[0, :, cols] + u_s[1, :, cols], gn_ref[...])
                             * _silu(z_ref[0, :, cols])).astype(y_ref.dtype)


def _gdn(proj3, ab3, conv_w, a_log, dt_bias, norm_g, layer, state=None, new_state=None):
    b, l, _ = proj3.shape
    depth_out, layer_out = (1, 0) if state is not None else (DEPTH, layer)
    aliased = new_state is not None
    lanes = jnp.zeros((2, 2 * N_HEAD), F32).at[:, :N_HEAD].set(1.0)
    a_row = jnp.pad((jnp.exp(a_log.astype(F32))[:, None, :] * lanes.reshape(2, 2, N_HEAD)).reshape(1, -1),
                    ((0, 0), (0, AB_PAD - 4 * N_HEAD)))
    dt_row = jnp.pad((dt_bias.astype(F32)[:, None, :] * lanes.reshape(2, 2, N_HEAD)).reshape(1, -1),
                     ((0, 0), (0, AB_PAD - 4 * N_HEAD)))
    hps = N_HEAD if l <= 512 else 1
    wid = hps * D_HEAD
    n_hb = N_HEAD // hps
    blk = lambda c: pl.BlockSpec((1, l, wid), lambda i, h, c=c: (i, 0, c * n_hb + h))
    wblk = lambda c: pl.BlockSpec((3, wid), lambda i, h, c=c: (0, c * n_hb + h))
    row = pl.BlockSpec((1, D_HEAD), lambda i, h: (0, 0))
    in_specs = [blk(0), blk(1), blk(2), blk(3),
                pl.BlockSpec((1, l, AB_PAD), lambda i, h: (i, 0, 0)),
                wblk(0), wblk(1), wblk(2), row, row, row]
    args = [proj3, proj3, proj3, proj3, ab3, conv_w, conv_w, conv_w, a_row, dt_row, norm_g]
    if aliased:
        in_specs.insert(0, pl.BlockSpec(memory_space=pl.ANY))
        args.insert(0, new_state)
    if state is not None:
        in_specs.append(pl.BlockSpec((1, 1, 2, hps, D_HEAD, D_HEAD), lambda i, h: (i, layer, 0, h, 0, 0)))
        args.append(state)
    return pl.pallas_call(
        functools.partial(_gdn_kernel, aliased=aliased, has_s0=state is not None, group=min(8, l // CHUNK)),
        grid=(b, n_hb),
        in_specs=in_specs,
        out_specs=[pl.BlockSpec((1, l, wid), lambda i, h: (i, 0, h)),
                   pl.BlockSpec((1, 1, 2, hps, D_HEAD, D_HEAD), lambda i, h: (i, layer_out, 0, h, 0, 0))],
        out_shape=[jax.ShapeDtypeStruct((b, l, BR_W), BF16),
                   jax.ShapeDtypeStruct((b, depth_out, 2, N_HEAD, D_HEAD, D_HEAD), F32)],
        input_output_aliases={0: 1} if aliased else {},
        scratch_shapes=[pltpu.VMEM((l, wid), F32)] * 3
        + [pltpu.VMEM((2, l, wid), F32)] + [pltpu.VMEM((2, l, wid), BF16)] * 3
        + [pltpu.VMEM((2, hps, l, CHUNK), BF16), pltpu.VMEM((2, hps, l // CHUNK, 8, D_HEAD), F32)],
        compiler_params=_cparams("parallel", "parallel"),
        name="gdn",
    )(*args)


def _mod_kernel(c_ref, w_ref, b_ref, o_ref):
    o_ref[...] = _dot_hi(_silu(c_ref[...]), w_ref[...]) + b_ref[...]


def _modulation(cond, w_mod, b_mod, layer, tn=512):
    n = cond.shape[0]
    rows = 8
    out = pl.pallas_call(
        _mod_kernel,
        grid=(3 * D_MODEL // tn,),
        in_specs=[pl.BlockSpec((rows, D_MODEL), lambda j: (0, 0)),
                  pl.BlockSpec((None, D_MODEL, tn), lambda j: (layer, 0, j)),
                  pl.BlockSpec((1, tn), lambda j: (0, j))],
        out_specs=pl.BlockSpec((rows, tn), lambda j: (0, j)),
        out_shape=jax.ShapeDtypeStruct((rows, 3 * D_MODEL), F32),
        compiler_params=_cparams("parallel"),
        name="modulation",
    )(jnp.pad(cond.astype(F32), ((0, rows - n), (0, 0))), w_mod, b_mod.reshape(1, -1))
    return out[:n].reshape(n, 3, D_MODEL)


def _split_w_in(w_in):
    n_a = 4 * BR_W + 4 * N_HEAD
    w_in = w_in.astype(BF16)
    main = jnp.concatenate([w_in[..., :4 * BR_W], w_in[..., n_a:]], axis=-1)
    ab = jnp.pad(w_in[..., 4 * BR_W:n_a], ((0, 0),) * (w_in.ndim - 1) + ((0, AB_PAD - 4 * N_HEAD),))
    return main, ab


def _trunk_layer(x3, cond, p, big, layer, dft, latent, new_outputs=(None, None, None)):
    b, l, _ = x3.shape
    x2 = x3.reshape(b * l, D_MODEL)
    mod = _modulation(cond, big['w_mod'], p['b_mod'], layer)
    rows_per_mod = l if mod.shape[0] == b else b * l
    g_pre = p['g_pre'].reshape(1, D_MODEL)
    proj, ab = _inproj(x2, mod, g_pre, big['w_main'], big['w_ab'], layer, rows_per_mod)
    proj3 = proj.reshape(b, l, N_MAIN)
    ab3 = ab.reshape(b, l, AB_PAD)

    lam_init = 0.8 - 0.6 * math.exp(-0.3 * layer)
    lam_p = p['diff_lam'].astype(F32)
    lam = (jnp.exp(jnp.sum(lam_p[0] * lam_p[1])) - jnp.exp(jnp.sum(lam_p[2] * lam_p[3])) + lam_init).reshape(1, 1)
    diff_norm = p['diff_norm'].reshape(1, D_HEAD)
    gdn_args = (proj3, ab3, p['gdn_conv'], p['gdn_a_log'], p['gdn_dt_bias'], p['gdn_norm'].reshape(1, D_HEAD), layer)

    yb = _hyena(proj3, p, dft)
    if latent is None:
        new_state, nat_cache, diff_cache = new_outputs
        ya, new_state = _gdn(*gdn_args, new_state=new_state)
        yc, yd, nat_cache, diff_cache = _ctx_attention(proj3, lam, diff_norm, lam_init, layer, nat_cache, diff_cache)
        extras = (new_state, nat_cache, diff_cache)
    else:
        ya, _ = _gdn(*gdn_args, state=latent['state_gdn'])
        yc = _lat_nat(proj3, latent['cache_nat_kv'], layer, _nat_bias_table(p['nat_rpb']))
        yd = _lat_diff(proj3, latent['cache_diff_kv'], layer, lam, diff_norm, lam_init, latent['rope'])
        extras = None

    ys = [t.reshape(b * l, BR_W) for t in (ya, yb, yc, yd)]
    out = _merge(x2, mod, g_pre, p['g_post'].reshape(1, D_MODEL), ys, big['w_branch'], big['w_merge'],
                 p['b_merge'].reshape(1, -1).astype(F32), big['w_out'], layer, rows_per_mod)
    return out.reshape(b, l, D_MODEL), extras


def kernel(x_prompt, x_sample, state_gdn, cache_nat_kv, cache_diff_kv, c, c_ctx,
           w_mod, b_mod, g_pre, g_post, w_in, gdn_conv, gdn_a_log, gdn_dt_bias, gdn_norm,
           hy_conv, hy_w1, hy_b1, hy_w2, hy_b2, hy_w3, hy_b3, hy_decay, hy_skip,
           nat_rpb, diff_lam, diff_norm, w_branch, w_merge, b_merge, w_out):
    small = {
        'b_mod': b_mod, 'g_pre': g_pre, 'g_post': g_post,
        'gdn_conv': gdn_conv, 'gdn_a_log': gdn_a_log, 'gdn_dt_bias': gdn_dt_bias, 'gdn_norm': gdn_norm,
        'hy_conv': hy_conv, 'hy_w1': hy_w1, 'hy_b1': hy_b1, 'hy_w2': hy_w2, 'hy_b2': hy_b2,
        'hy_w3': hy_w3, 'hy_b3': hy_b3, 'hy_decay': hy_decay, 'hy_skip': hy_skip,
        'nat_rpb': nat_rpb, 'diff_lam': diff_lam, 'diff_norm': diff_norm, 'b_merge': b_merge,
    }
    layers = [{name: arr[i] for name, arr in small.items()} for i in range(DEPTH)]
    w_main, w_ab = _split_w_in(w_in)
    big = {'w_mod': w_mod.astype(F32), 'w_main': w_main, 'w_ab': w_ab, 'w_branch': w_branch.astype(BF16),
           'w_merge': w_merge.astype(BF16), 'w_out': w_out.astype(BF16)}

    y_prompt = x_prompt
    dft_ctx = _dft_matrices(x_prompt.shape[1])
    outputs = (None, None, None)
    for i, p in enumerate(layers):
        y_prompt, outputs = _trunk_layer(y_prompt, c_ctx.reshape(1, D_MODEL), p, big, i, dft_ctx, None, outputs)
    new_state, nat_cache, diff_cache = outputs

    y_sample = x_sample
    dft_lat = _dft_matrices(x_sample.shape[1])
    latent = {'state_gdn': state_gdn, 'cache_nat_kv': cache_nat_kv, 'cache_diff_kv': cache_diff_kv,
              'rope': _rope_tables(x_sample.shape[1])}
    for i, p in enumerate(layers):
        y_sample, _ = _trunk_layer(y_sample, c, p, big, i, dft_lat, latent)

    return (y_prompt, y_sample, new_state, nat_cache, diff_cache)
```

```python
import functools
import math

import jax
import jax.numpy as jnp
import numpy as np
from jax import lax
from jax.experimental import pallas as pl
from jax.experimental.pallas import tpu as pltpu

F32 = jnp.float32
BF16 = jnp.bfloat16

D_MODEL = 1024
DEPTH = 2
GRID_W = 64
N_BRANCH = 4
BR_W = 512
N_HEAD = 4
D_HEAD = 128
SUBLANES = 8
CHUNK = 64
HY_BANDS = 16
WIN_R = 8
WIN_C = 16
DQK_D = 64
ROPE_BASE = 10000.0
EPS = 1e-6
N_MAIN = 4 * 4 * BR_W
AB_PAD = 128
NEG_INF = -1e30

VMEM_LIMIT = 48 * 1024 * 1024


def _cparams(*sem):
    return pltpu.CompilerParams(dimension_semantics=sem, vmem_limit_bytes=VMEM_LIMIT)


def _silu(x):
    return x * (1.0 / (1.0 + jnp.exp(-x)))


def _sigmoid(x):
    return 1.0 / (1.0 + jnp.exp(-x))


def _rms(x, g):
    return x * lax.rsqrt(jnp.mean(x * x, axis=-1, keepdims=True) + EPS) * g


def _dot(a, b):
    return jnp.dot(a.astype(BF16), b.astype(BF16), preferred_element_type=F32)


def _dot_nt(a, b):
    return lax.dot_general(a.astype(BF16), b.astype(BF16), (((1,), (1,)), ((), ())),
                           preferred_element_type=F32)


def _dot_tn(a, b):
    return lax.dot_general(a.astype(BF16), b.astype(BF16), (((0,), (0,)), ((), ())),
                           preferred_element_type=F32)


def _prenorm(x, g_pre, mod_ref):
    return _rms(x, g_pre) * (1.0 + mod_ref[0, 1:2, :]) + mod_ref[0, 0:1, :]


def _inproj_kernel(x_ref, mod_ref, gpre_ref, w_ref, wab_ref, proj_ref, ab_ref, h_scr):
    @pl.when(pl.program_id(1) == 0)
    def _():
        h = _prenorm(x_ref[...], gpre_ref[...], mod_ref).astype(BF16)
        h_scr[...] = h
        ab_ref[...] = jnp.dot(h, wab_ref[...], preferred_element_type=F32)

    proj_ref[...] = jnp.dot(h_scr[...], w_ref[...], preferred_element_type=F32)


def _inproj(x2, mod, g_pre, w_packed, layer, rows_per_mod, tm=1024, tn=1024):
    m = x2.shape[0]
    tm = math.gcd(tm, rows_per_mod)
    return pl.pallas_call(
        _inproj_kernel,
        grid=(m // tm, N_MAIN // tn),
        in_specs=[
            pl.BlockSpec((tm, D_MODEL), lambda i, j: (i, 0)),
            pl.BlockSpec((1, 3, D_MODEL), lambda i, j: ((i * tm) // rows_per_mod, 0, 0)),
            pl.BlockSpec((1, D_MODEL), lambda i, j: (0, 0)),
            pl.BlockSpec((None, D_MODEL, tn), lambda i, j: (layer, 0, j)),
            pl.BlockSpec((None, D_MODEL, AB_PAD), lambda i, j: (layer, 0, N_MAIN // AB_PAD)),
        ],
        out_specs=[
            pl.BlockSpec((tm, tn), lambda i, j: (i, j)),
            pl.BlockSpec((tm, AB_PAD), lambda i, j: (i, 0)),
        ],
        out_shape=[jax.ShapeDtypeStruct((m, N_MAIN), F32),
                   jax.ShapeDtypeStruct((m, AB_PAD), F32)],
        scratch_shapes=[pltpu.VMEM((tm, D_MODEL), BF16)],
        compiler_params=_cparams("parallel", "arbitrary"),
        name="inproj",
    )(x2, mod, g_pre, w_packed, w_packed)


def _merge_kernel(x_ref, mod_ref, gpre_ref, gpost_ref, ya_ref, yb_ref, yc_ref, yd_ref,
                  wbr_ref, wmg_ref, bmg_ref, wout_ref, o_ref):
    x = x_ref[...]
    h = _prenorm(x, gpre_ref[...], mod_ref).astype(BF16)
    acc = None
    for k, y_ref in enumerate((ya_ref, yb_ref, yc_ref, yd_ref)):
        cols = slice(k * D_MODEL, (k + 1) * D_MODEL)
        gate = _sigmoid(jnp.dot(h, wmg_ref[:, cols], preferred_element_type=F32) + bmg_ref[:, cols])
        br = jnp.dot(y_ref[...], wbr_ref[k], preferred_element_type=F32)
        acc = gate * br if acc is None else acc + gate * br
    y = jnp.dot(acc.astype(BF16), wout_ref[...], preferred_element_type=F32)
    o_ref[...] = x + mod_ref[0, 2:3, :] * _rms(y, gpost_ref[...])


def _merge(x2, mod, g_pre, g_post, ys, w_branch, w_merge, b_merge, w_out, layer, rows_per_mod, tm=256):
    m = x2.shape[0]
    row = lambda i: (i, 0)
    fixed2 = lambda i: (0, 0)
    return pl.pallas_call(
        _merge_kernel,
        grid=(m // tm,),
        in_specs=[
            pl.BlockSpec((tm, D_MODEL), row),
            pl.BlockSpec((1, 3, D_MODEL), lambda i: ((i * tm) // rows_per_mod, 0, 0)),
            pl.BlockSpec((1, D_MODEL), fixed2),
            pl.BlockSpec((1, D_MODEL), fixed2),
            pl.BlockSpec((tm, BR_W), row),
            pl.BlockSpec((tm, BR_W), row),
            pl.BlockSpec((tm, BR_W), row),
            pl.BlockSpec((tm, BR_W), row),
            pl.BlockSpec((None, N_BRANCH, BR_W, D_MODEL), lambda i: (layer, 0, 0, 0)),
            pl.BlockSpec((None, D_MODEL, N_BRANCH * D_MODEL), lambda i: (layer, 0, 0)),
            pl.BlockSpec((1, N_BRANCH * D_MODEL), fixed2),
            pl.BlockSpec((None, D_MODEL, D_MODEL), lambda i: (layer, 0, 0)),
        ],
        out_specs=pl.BlockSpec((tm, D_MODEL), row),
        out_shape=jax.ShapeDtypeStruct((m, D_MODEL), F32),
        compiler_params=_cparams("parallel"),
        name="merge",
    )(x2, mod, g_pre, g_post, *ys, w_branch, w_merge, b_merge, w_out)


def _softmax_rows(s):
    p = jnp.exp(s - jnp.max(s, axis=-1, keepdims=True))
    return p, jnp.sum(p, axis=-1, keepdims=True)


def _head_cols(h):
    return slice(h * D_HEAD, (h + 1) * D_HEAD)


def _stack_heads(ref):
    return jnp.stack([ref[0, :, _head_cols(h)] for h in range(N_HEAD)])


def _ctx_nat_kernel(*refs, aliased):
    q_ref, k_ref, v_ref, g_ref, y_ref, kv_ref = refs[1:] if aliased else refs
    scale = D_HEAD ** -0.5
    q, k, v = (_stack_heads(r) for r in (q_ref, k_ref, v_ref))
    p, l = _softmax_rows(_bmm_nt(q, k) * scale)
    o = _bmm(p, v) / l
    for h in range(N_HEAD):
        sl = _head_cols(h)
        y_ref[0, :, sl] = (o[h] * _silu(g_ref[0, :, sl])).astype(y_ref.dtype)
        kv_ref[0, 0, 0, h] = k[h]
        kv_ref[0, 0, 1, h] = v[h]


def _map_masks():
    lane = lax.broadcasted_iota(jnp.int32, (1, D_HEAD), 1)
    first = (lane < DQK_D).astype(F32)
    return first, 1.0 - first


def _ctx_diff_kernel(*refs, aliased, out_scale):
    lam_ref, q_ref, k_ref, v_ref, g_ref, gn_ref, y_ref, kv_ref = refs[1:] if aliased else refs
    scale = DQK_D ** -0.5
    m1, m2 = _map_masks()
    q, k, v = (_stack_heads(r) for r in (q_ref, k_ref, v_ref))
    p, l = _softmax_rows(_bmm_nt(jnp.concatenate([q * m1, q * m2], axis=0), jnp.concatenate([k, k], axis=0)) * scale)
    pn = p / l
    a = pn[:N_HEAD] - lam_ref[...] * pn[N_HEAD:]
    o = _rms(_bmm(a, v), gn_ref[...]) * out_scale
    for h in range(N_HEAD):
        sl = _head_cols(h)
        y_ref[0, :, sl] = (o[h] * _silu(g_ref[0, :, sl])).astype(y_ref.dtype)
        kv_ref[0, 0, 0, h] = k[h]
        kv_ref[0, 0, 1, h] = v[h]


def _ctx_attention(proj3, lam, diff_norm, lam_init, layer, nat_cache, diff_cache):
    b, l, _ = proj3.shape
    blk = lambda c: pl.BlockSpec((1, l, BR_W), lambda i, c=c: (i, 0, c))
    y_spec = pl.BlockSpec((1, l, BR_W), lambda i: (i, 0, 0))
    kv_spec = pl.BlockSpec((1, 1, 2, N_HEAD, l, D_HEAD), lambda i: (i, layer, 0, 0, 0, 0))
    out_shape = [jax.ShapeDtypeStruct((b, l, BR_W), BF16),
                 jax.ShapeDtypeStruct((b, DEPTH, 2, N_HEAD, l, D_HEAD), F32)]
    aliased = nat_cache is not None
    cache_specs = [pl.BlockSpec(memory_space=pl.ANY)] if aliased else []
    aliases = {0: 1} if aliased else {}
    yc, nat_cache = pl.pallas_call(
        functools.partial(_ctx_nat_kernel, aliased=aliased),
        grid=(b,),
        in_specs=cache_specs + [blk(8), blk(9), blk(10), blk(11)],
        out_specs=[y_spec, kv_spec],
        out_shape=out_shape,
        input_output_aliases=aliases,
        compiler_params=_cparams("parallel"),
        name="ctx_nat",
    )(*([nat_cache] if aliased else []), proj3, proj3, proj3, proj3)
    yd, diff_cache = pl.pallas_call(
        functools.partial(_ctx_diff_kernel, aliased=aliased, out_scale=1.0 - lam_init),
        grid=(b,),
        in_specs=cache_specs + [pl.BlockSpec((1, 1), lambda i: (0, 0)),
                                blk(12), blk(13), blk(14), blk(15),
                                pl.BlockSpec((1, D_HEAD), lambda i: (0, 0))],
        out_specs=[y_spec, kv_spec],
        out_shape=out_shape,
        input_output_aliases=aliases,
        compiler_params=_cparams("parallel"),
        name="ctx_diff",
    )(*([diff_cache] if aliased else []), lam, proj3, proj3, proj3, proj3, diff_norm)
    return yc, yd, nat_cache, diff_cache


def _nat_bias_table(rpb):
    cols = np.arange(GRID_W)
    start = np.clip(cols - WIN_C // 2, 0, GRID_W - WIN_C)
    inside = (cols[None, :] >= start[:, None]) & (cols[None, :] < start[:, None] + WIN_C)
    dc = cols[None, :] - cols[:, None] + (WIN_C - 1)
    onehot = ((dc[None] == np.arange(2 * WIN_C - 1)[:, None, None]) & inside[None]).astype(np.float32)
    t = jnp.einsum('hdx,xck->hdck', rpb.astype(F32), jnp.asarray(onehot), precision=lax.Precision.HIGHEST)
    t = jnp.where(jnp.asarray(inside)[None, None], t, NEG_INF)
    tab = jnp.stack([t[:, WIN_R - 1 - off:2 * WIN_R - 1 - off] for off in range(WIN_R)], axis=1)
    return tab.transpose(0, 1, 3, 2, 4).reshape(rpb.shape[0], WIN_R, GRID_W, WIN_R * GRID_W)


def _lat_nat_kernel(q_ref, k_ref, v_ref, g_ref, ckv_ref, bias_ref, y_ref, kb_scr, vb_scr, *, rb):
    scale = D_HEAD ** -0.5
    rows = q_ref.shape[1] // GRID_W
    win = WIN_R * GRID_W
    kb_scr[...] = k_ref[0].astype(BF16)
    vb_scr[...] = v_ref[0].astype(BF16)
    ck = ckv_ref[0, 0, 0, 0].astype(BF16)
    cv = ckv_ref[0, 0, 1, 0].astype(BF16)

    def row_block(i, carry):
        q0 = pl.multiple_of(i * (rb * GRID_W), rb * GRID_W)
        qrows = pl.ds(q0, rb * GRID_W)
        q = q_ref[0, qrows, :].astype(BF16)
        kw, vw, bias = [], [], []
        for j in range(rb):
            r = i * rb + j
            rs = jnp.clip(r - WIN_R // 2, 0, rows - WIN_R)
            wrows = pl.ds(pl.multiple_of(rs * GRID_W, GRID_W), win)
            kw.append(kb_scr[wrows, :])
            vw.append(vb_scr[wrows, :])
            bias.append(bias_ref[0, r - rs])
        q3 = q.reshape(rb, GRID_W, D_HEAD)
        s_lat = _bmm_nt(q3, jnp.stack(kw)) * scale + jnp.stack(bias)
        s_ctx = (_dot_nt(q, ck) * scale).reshape(rb, GRID_W, ck.shape[0])
        m = jnp.maximum(jnp.max(s_lat, axis=-1, keepdims=True), jnp.max(s_ctx, axis=-1, keepdims=True))
        p_lat = jnp.exp(s_lat - m)
        p_ctx = jnp.exp(s_ctx - m)
        l = jnp.sum(p_lat, axis=-1, keepdims=True) + jnp.sum(p_ctx, axis=-1, keepdims=True)
        o_ctx = _dot(p_ctx.reshape(rb * GRID_W, ck.shape[0]), cv).reshape(rb, GRID_W, D_HEAD)
        o = ((_bmm(p_lat, jnp.stack(vw)) + o_ctx) / l).reshape(rb * GRID_W, D_HEAD)
        y_ref[0, qrows, :] = (o * _silu(g_ref[0, qrows, :])).astype(y_ref.dtype)
        return carry

    lax.fori_loop(0, rows // rb, row_block, 0)


def _lat_nat(proj3, cache_nat_kv, layer, bias_tab):
    b, l, _ = proj3.shape
    past = cache_nat_kv.shape[4]
    blk = lambda c: pl.BlockSpec((1, l, D_HEAD), lambda i, h, c=c: (i, 0, c + h))
    return pl.pallas_call(
        functools.partial(_lat_nat_kernel, rb=8),
        grid=(b, N_HEAD),
        in_specs=[blk(32), blk(36), blk(40), blk(44),
                  pl.BlockSpec((1, 1, 2, 1, past, D_HEAD), lambda i, h: (i, layer, 0, h, 0, 0)),
                  pl.BlockSpec((1, WIN_R, GRID_W, WIN_R * GRID_W), lambda i, h: (h, 0, 0, 0))],
        out_specs=pl.BlockSpec((1, l, D_HEAD), lambda i, h: (i, 0, h)),
        out_shape=jax.ShapeDtypeStruct((b, l, BR_W), BF16),
        scratch_shapes=[pltpu.VMEM((l, D_HEAD), BF16), pltpu.VMEM((l, D_HEAD), BF16)],
        compiler_params=_cparams("parallel", "parallel"),
        name="lat_nat",
    )(proj3, proj3, proj3, proj3, cache_nat_kv, bias_tab)


def _rope_tables(l):
    half = DQK_D // 2
    nf = half // 2
    t = jnp.arange(l)
    row = (t // GRID_W).astype(F32)
    col = (t % GRID_W).astype(F32)
    inv = ROPE_BASE ** (-jnp.arange(nf, dtype=F32) / nf)
    ang = jnp.concatenate([row[:, None] * inv, col[:, None] * inv], axis=-1)
    cos, sin = jnp.cos(ang), jnp.sin(ang)
    zero = jnp.zeros_like(sin)
    tile2 = lambda a, b: jnp.concatenate([a, b, a, b], axis=-1)
    return tile2(cos, cos), tile2(-sin, zero), tile2(zero, sin)


def _rope(x, cos, sin_a, sin_b):
    return x * cos + pltpu.roll(x, 96, 1) * sin_a + pltpu.roll(x, 32, 1) * sin_b


def _lat_diff_kernel(lam_ref, q_ref, k_ref, v_ref, g_ref, ckv_ref, gn_ref,
                     cq_ref, saq_ref, sbq_ref, ck_ref, sak_ref, sbk_ref,
                     y_ref, ks_scr, vt_scr, *, out_scale, prep_rows, key_block, ahead):
    scale = DQK_D ** -0.5
    l = k_ref.shape[1]

    @pl.when(pl.program_id(2) == 0)
    def _():
        def prep(i, carry):
            rows = pl.ds(pl.multiple_of(i * prep_rows, prep_rows), prep_rows)
            kr = _rope(k_ref[0, rows, :], ck_ref[rows, :], sak_ref[rows, :], sbk_ref[rows, :])
            ks_scr[rows, :] = kr.astype(BF16)
            vt_scr[:, rows] = v_ref[0, rows, :].T.astype(BF16)
            return carry

        lax.fori_loop(0, l // prep_rows, prep, 0)
        ks_scr[l:, :] = ckv_ref[0, 0, 0, 0].astype(BF16)
        vt_scr[:, l:] = ckv_ref[0, 0, 1, 0].T.astype(BF16)

    q = _rope(q_ref[0], cq_ref[...], saq_ref[...], sbq_ref[...]) * (scale * math.log2(math.e))
    m1, m2 = _map_masks()
    tq = q.shape[0]
    qm = jnp.concatenate([q * m1, q * m2], axis=0).astype(BF16)
    m = l_sum = acc = None
    n_blk = ks_scr.shape[0] // key_block
    block = lambda blk: slice(blk * key_block, (blk + 1) * key_block)
    scores = [_dot_nt(ks_scr[block(b), :], qm) for b in range(min(ahead, n_blk))]
    for blk in range(n_blk):
        rows = block(blk)
        s = scores.pop(0)
        if blk + ahead < n_blk:
            scores.append(_dot_nt(ks_scr[block(blk + ahead), :], qm))
        m_blk = jnp.max(s, axis=0, keepdims=True)
        if blk == 0:
            m = m_blk
            p = jnp.exp2(s - m)
            l_sum = jnp.sum(p, axis=0, keepdims=True)
            acc = _dot(vt_scr[:, rows], p)
        else:
            m_new = jnp.maximum(m, m_blk)
            alpha = jnp.exp2(m - m_new)
            p = jnp.exp2(s - m_new)
            l_sum = alpha * l_sum + jnp.sum(p, axis=0, keepdims=True)
            acc = alpha * acc + _dot(vt_scr[:, rows], p)
            m = m_new
    out = acc / l_sum
    d = out[:, :tq] - lam_ref[...] * out[:, tq:]
    d = d * lax.rsqrt(jnp.mean(d * d, axis=0, keepdims=True) + EPS)
    o = d.T * gn_ref[...] * out_scale
    y_ref[0] = (o * _silu(g_ref[0])).astype(y_ref.dtype)


def _lat_diff(proj3, cache_diff_kv, layer, lam, diff_norm, lam_init, rope_tabs, tq=256):
    b, l, _ = proj3.shape
    past = cache_diff_kv.shape[4]
    qblk = lambda c: pl.BlockSpec((1, tq, D_HEAD), lambda i, h, j, c=c: (i, j, c + h))
    full = lambda c: pl.BlockSpec((1, l, D_HEAD), lambda i, h, j, c=c: (i, 0, c + h))
    tq_tab = pl.BlockSpec((tq, D_HEAD), lambda i, h, j: (j, 0))
    full_tab = pl.BlockSpec((l, D_HEAD), lambda i, h, j: (0, 0))
    return pl.pallas_call(
        functools.partial(_lat_diff_kernel, out_scale=1.0 - lam_init, prep_rows=512, key_block=512, ahead=2),
        grid=(b, N_HEAD, l // tq),
        in_specs=[pl.BlockSpec((1, 1), lambda i, h, j: (0, 0)),
                  qblk(48), full(52), full(56), qblk(60),
                  pl.BlockSpec((1, 1, 2, 1, past, D_HEAD), lambda i, h, j: (i, layer, 0, h, 0, 0)),
                  pl.BlockSpec((1, D_HEAD), lambda i, h, j: (0, 0)),
                  tq_tab, tq_tab, tq_tab, full_tab, full_tab, full_tab],
        out_specs=pl.BlockSpec((1, tq, D_HEAD), lambda i, h, j: (i, j, h)),
        out_shape=jax.ShapeDtypeStruct((b, l, BR_W), BF16),
        scratch_shapes=[pltpu.VMEM((l + past, D_HEAD), BF16), pltpu.VMEM((D_HEAD, l + past), BF16)],
        compiler_params=_cparams("parallel", "parallel", "arbitrary"),
        name="lat_diff",
    )(lam, proj3, proj3, proj3, proj3, cache_diff_kv, diff_norm, *rope_tabs, *rope_tabs)


def _dwconv3(x, w_ref):
    l = x.shape[0]
    row = lax.broadcasted_iota(jnp.int32, x.shape, 0)
    prev = jnp.where(row == 0, 0.0, pltpu.roll(x, 1, 0))
    nxt = jnp.where(row == l - 1, 0.0, pltpu.roll(x, l - 1, 0))
    return prev * w_ref[0:1, :] + x * w_ref[1:2, :] + nxt * w_ref[2:3, :]


def _hy_pre_kernel(x_ref, above_ref, below_ref, w_ref, o_ref, ob_ref):
    t, n_t = pl.program_id(1), pl.num_programs(1)
    rows = x_ref.shape[1]
    row = lax.broadcasted_iota(jnp.int32, x_ref.shape[1:], 0)
    for bb in range(x_ref.shape[0]):
        x = x_ref[bb]
        before = jnp.where(t == 0, 0.0, above_ref[bb, SUBLANES - 1:SUBLANES, :])
        after = jnp.where(t == n_t - 1, 0.0, below_ref[bb, 0:1, :])
        prev = jnp.where(row == 0, before, pltpu.roll(x, 1, 0))
        nxt = jnp.where(row == rows - 1, after, pltpu.roll(x, rows - 1, 0))
        y = prev * w_ref[0:1, :] + x * w_ref[1:2, :] + nxt * w_ref[2:3, :]
        o_ref[bb] = y
        ob_ref[bb] = y.astype(BF16)


def _hy_pre(proj3, conv_w, tl=1024):
    b, l, _ = proj3.shape
    tl = min(tl, l)
    bt = _seqs_per_step(b, l)
    n = 3
    col0 = 4
    groups = tl // SUBLANES
    last_group = l // SUBLANES - 1
    spec = pl.BlockSpec((bt, tl, BR_W), lambda i, t, j: (i, t, j))
    return pl.pallas_call(
        _hy_pre_kernel,
        grid=(b // bt, l // tl, n),
        in_specs=[pl.BlockSpec((bt, tl, BR_W), lambda i, t, j: (i, t, col0 + j)),
                  pl.BlockSpec((bt, SUBLANES, BR_W),
                               lambda i, t, j: (i, jnp.maximum(t * groups - 1, 0), col0 + j)),
                  pl.BlockSpec((bt, SUBLANES, BR_W),
                               lambda i, t, j: (i, jnp.minimum((t + 1) * groups, last_group), col0 + j)),
                  pl.BlockSpec((3, BR_W), lambda i, t, j: (0, j))],
        out_specs=[spec, spec],
        out_shape=[jax.ShapeDtypeStruct((b, l, 3 * BR_W), F32),
                   jax.ShapeDtypeStruct((b, l, 3 * BR_W), BF16)],
        compiler_params=_cparams("parallel", "parallel", "parallel"),
        name="hy_pre",
    )(proj3, proj3, proj3, conv_w)


def _dot_hi(a, b):
    return jnp.dot(a, b, preferred_element_type=F32, precision=lax.Precision.HIGHEST)


def _hy_filter_kernel(feat_ref, dist_ref, w1_ref, b1_ref, w2_ref, b2_ref, w3_ref, b3_ref, dec_ref, o_ref):
    hid = jnp.sin(_dot_hi(feat_ref[...], w1_ref[...]) + b1_ref[...])
    hid = jnp.sin(_dot_hi(hid, w2_ref[...]) + b2_ref[...])
    dist = dist_ref[...]
    for j in range(o_ref.shape[1] // D_HEAD):
        cols = slice(j * D_HEAD, (j + 1) * D_HEAD)
        filt = _dot_hi(hid, w3_ref[:, cols]) + b3_ref[:, cols]
        o_ref[:, cols] = (filt * jnp.exp(-dist * jnp.abs(dec_ref[:, cols]))).astype(o_ref.dtype)


def _hy_filter(l, w1, b1, w2, b2, w3, b3, decay):
    pos = jnp.arange(l, dtype=F32)
    t = pos / l
    ang = (2.0 * math.pi) * t[:, None] * jnp.arange(1, HY_BANDS + 1, dtype=F32)
    feat = jnp.concatenate([t[:, None], jnp.cos(ang), jnp.sin(ang)], axis=-1)
    dist = jnp.broadcast_to((jnp.abs(pos - l // 2) / l)[:, None], (l, D_HEAD))
    pad = D_HEAD
    emb, ff = w1.shape
    feat = jnp.pad(feat, ((0, 0), (0, pad - emb)))
    w1p = jnp.pad(w1, ((0, pad - emb), (0, pad - ff)))
    w2p = jnp.pad(w2, ((0, pad - ff), (0, pad - ff)))
    w3p = jnp.pad(w3, ((0, pad - ff), (0, 0)))
    b1p = jnp.pad(b1, (0, pad - ff)).reshape(1, pad)
    b2p = jnp.pad(b2, (0, pad - ff)).reshape(1, pad)
    tl = min(l, 256)
    n = 2 * BR_W
    fixed = lambda shape: pl.BlockSpec(shape, lambda i: (0, 0))
    return pl.pallas_call(
        _hy_filter_kernel,
        grid=(l // tl,),
        in_specs=[pl.BlockSpec((tl, pad), lambda i: (i, 0)),
                  pl.BlockSpec((tl, D_HEAD), lambda i: (i, 0)),
                  fixed((pad, pad)), fixed((1, pad)), fixed((pad, pad)), fixed((1, pad)),
                  fixed((pad, n)), fixed((1, n)), fixed((1, n))],
        out_specs=pl.BlockSpec((tl, n), lambda i: (i, 0)),
        out_shape=jax.ShapeDtypeStruct((l, n), BF16),
        compiler_params=_cparams("parallel"),
        name="hy_filter",
    )(feat, dist, w1p, b1p, w2p, b2p, w3p, b3.reshape(1, n), decay.reshape(1, n))


def _dft_matrices(l):
    n = 2 * l
    k = jnp.arange(l, dtype=jnp.int32)
    t = jnp.arange(l, dtype=jnp.int32)
    tp = t + l // 2
    split = 1 << (max(l.bit_length() - 1, 0) // 2)

    def tables(rows, cols):
        def table(r):
            ang = (2.0 * math.pi / n) * ((r[:, None] * cols[None, :]) % n).astype(F32)
            return jnp.cos(ang), jnp.sin(ang)
        return (*table(rows[::split]), *table(rows[:split] - rows[0]))

    alt = jnp.where(t % 2 == 0, 1.0, -1.0).astype(F32).reshape(1, l)
    wk = (jnp.where(k == 0, 1.0, 2.0).astype(F32) / n).reshape(1, l)
    out = pl.pallas_call(
        functools.partial(_dft_gen_kernel, split=split, l=l),
        grid=(l // split,),
        in_specs=[pl.BlockSpec((l // split, l), lambda i: (0, 0))] * 2 + [pl.BlockSpec((split, l), lambda i: (0, 0))] * 2
        + [pl.BlockSpec((l // split, l), lambda i: (0, 0))] * 2 + [pl.BlockSpec((split, l), lambda i: (0, 0))] * 2
        + [pl.BlockSpec((1, l), lambda i: (0, 0))] * 2,
        out_specs=[pl.BlockSpec((split, l), lambda i: (i, 0))] * 4,
        out_shape=[jax.ShapeDtypeStruct((l, l), BF16)] * 4,
        compiler_params=_cparams("parallel"),
        name="dft_gen",
    )(*tables(k, t), *tables(tp, k), alt, wk)
    return (out[0], out[1]), (out[2], out[3])


def _dft_gen_kernel(ch_ref, sh_ref, cl_ref, sl_ref, chi_ref, shi_ref, cli_ref, sli_ref, alt_ref, wk_ref,
                    fc_ref, fs_ref, ic_ref, is_ref, *, split, l):
    i = pl.program_id(0)

    def cos_sin(c_hi, s_hi, c_lo, s_lo):
        ch, sh = c_hi[pl.ds(i, 1), :], s_hi[pl.ds(i, 1), :]
        return ch * c_lo[...] - sh * s_lo[...], sh * c_lo[...] + ch * s_lo[...]

    row = lax.broadcasted_iota(jnp.int32, (split, l), 0) + i * split
    col = lax.broadcasted_iota(jnp.int32, (split, l), 1)
    c, s = cos_sin(ch_ref, sh_ref, cl_ref, sl_ref)
    fc_ref[...] = c.astype(BF16)
    fs_ref[...] = jnp.where(row == 0, alt_ref[...], -s).astype(BF16)
    c, s = cos_sin(chi_ref, shi_ref, cli_ref, sli_ref)
    wk = wk_ref[...]
    alt_i = jnp.where(row % 2 == 0, 1.0, -1.0) * (1.0 / (2 * l))
    ic_ref[...] = (c * wk).astype(BF16)
    is_ref[...] = jnp.where(col == 0, alt_i, -s * wk).astype(BF16)


def _seqs_per_step(b, l, rows=2048):
    bt = max(1, min(b, rows // l))
    while b % bt:
        bt -= 1
    return bt


def _dft_fwd_kernel(fc_ref, fs_ref, x_ref, *rest, with_filter, tm):
    for bb in range(x_ref.shape[0]):
        x = x_ref[bb]
        ur = jnp.dot(fc_ref[...], x, preferred_element_type=F32)
        ui = jnp.dot(fs_ref[...], x, preferred_element_type=F32)
        if not with_filter:
            zr_ref, zi_ref = rest
            zr_ref[bb] = ur
            zi_ref[bb] = ui
            continue
        hr_ref, hi_ref, zr_ref, zi_ref = rest
        hr, hi = hr_ref[0], hi_ref[0]
        row0 = (lax.broadcasted_iota(jnp.int32, ur.shape, 0) + pl.program_id(0) * tm) == 0
        zr_ref[bb] = (ur * hr - jnp.where(row0, 0.0, ui * hi)).astype(zr_ref.dtype)
        zi_ref[bb] = jnp.where(row0, ui * hi, ur * hi + ui * hr).astype(zi_ref.dtype)


def _dft_fwd(fwd, x, x_col0, c, spec_h=None, h_col0=0, tm=512, tn=512):
    b, l, _ = x.shape
    tm = min(tm, l)
    bt = _seqs_per_step(b, l)
    xo, ho = x_col0 // tn, h_col0 // tn
    out_dtype = F32 if spec_h is None else BF16
    fspec = pl.BlockSpec((tm, l), lambda i, bb, j: (i, 0))
    in_specs = [fspec, fspec, pl.BlockSpec((bt, l, tn), lambda i, bb, j: (bb, 0, xo + j))]
    args = [*fwd, x]
    if spec_h is not None:
        hspec = pl.BlockSpec((1, tm, tn), lambda i, bb, j: (0, i, ho + j))
        in_specs += [hspec, hspec]
        args += list(spec_h)
    ospec = pl.BlockSpec((bt, tm, tn), lambda i, bb, j: (bb, i, j))
    return pl.pallas_call(
        functools.partial(_dft_fwd_kernel, with_filter=spec_h is not None, tm=tm),
        grid=(l // tm, b // bt, c // tn),
        in_specs=in_specs,
        out_specs=[ospec, ospec],
        out_shape=[jax.ShapeDtypeStruct((b, l, c), out_dtype)] * 2,
        compiler_params=_cparams("parallel", "parallel", "parallel"),
        name="dft_fwd",
    )(*args)


def _dft_inv_kernel(ic_ref, is_ref, zr_ref, zi_ref, u_ref, m_ref, skip_ref, *rest, with_gate):
    for bb in range(zr_ref.shape[0]):
        y = (jnp.dot(ic_ref[...], zr_ref[bb], preferred_element_type=F32)
             + jnp.dot(is_ref[...], zi_ref[bb], preferred_element_type=F32))
        z = m_ref[bb] * (y + u_ref[bb] * skip_ref[...])
        if with_gate:
            g_ref, o_ref = rest
            o_ref[bb] = (z * _silu(g_ref[bb])).astype(o_ref.dtype)
        else:
            o_ref, ob_ref = rest
            o_ref[bb] = z
            ob_ref[bb] = z.astype(BF16)


def _dft_inv(inv, zr, zi, u, u_col0, mul, mul_col0, skip, gate=None, gate_col0=0, tm=512, tn=512):
    b, l, c = zr.shape
    tm = min(tm, l)
    bt = _seqs_per_step(b, l)
    win = lambda col0: pl.BlockSpec((bt, tm, tn), lambda i, bb, j, o=col0 // tn: (bb, i, o + j))
    zspec = pl.BlockSpec((bt, l, tn), lambda i, bb, j: (bb, 0, j))
    fspec = pl.BlockSpec((tm, l), lambda i, bb, j: (i, 0))
    in_specs = [fspec, fspec, zspec, zspec,
                win(u_col0), win(mul_col0), pl.BlockSpec((1, tn), lambda i, bb, j: (0, j))]
    args = [*inv, zr, zi, u, mul, skip]
    ospec = pl.BlockSpec((bt, tm, tn), lambda i, bb, j: (bb, i, j))
    if gate is not None:
        in_specs.append(win(gate_col0))
        args.append(gate)
        out_specs, out_shape = ospec, jax.ShapeDtypeStruct((b, l, c), BF16)
    else:
        out_specs = [ospec, ospec]
        out_shape = [jax.ShapeDtypeStruct((b, l, c), F32), jax.ShapeDtypeStruct((b, l, c), BF16)]
    return pl.pallas_call(
        functools.partial(_dft_inv_kernel, with_gate=gate is not None),
        grid=(l // tm, b // bt, c // tn),
        in_specs=in_specs,
        out_specs=out_specs,
        out_shape=out_shape,
        compiler_params=_cparams("parallel", "parallel", "parallel"),
        name="dft_inv",
    )(*args)


def _hyena(proj3, p, dft):
    l = proj3.shape[1]
    fwd, inv = dft
    filt = _hy_filter(l, p['hy_w1'], p['hy_b1'], p['hy_w2'], p['hy_b2'], p['hy_w3'], p['hy_b3'], p['hy_decay'])
    filt_b = filt[None]
    spec_h = _dft_fwd(fwd, filt_b, 0, 2 * BR_W)
    pre, pre_b = _hy_pre(proj3, p['hy_conv'])
    skip = p['hy_skip'].astype(F32)
    zr, zi = _dft_fwd(fwd, pre_b, 0, BR_W, spec_h, 0)
    z1, z1_b = _dft_inv(inv, zr, zi, pre, 0, pre, BR_W, skip[0:1])
    zr, zi = _dft_fwd(fwd, z1_b, 0, BR_W, spec_h, BR_W)
    return _dft_inv(inv, zr, zi, z1, 0, pre, 2 * BR_W, skip[1:2], gate=proj3, gate_col0=7 * BR_W)


def _softplus(x):
    return jnp.maximum(x, 0.0) + jnp.log1p(jnp.exp(-jnp.abs(x)))


def _split_bf16(x, parts):
    out = []
    for _ in range(parts - 1):
        piece = x.astype(BF16)
        out.append(piece)
        x = x - piece.astype(F32)
    out.append(x.astype(BF16))
    return out


def _bmm(a, b, hi=False):
    mm = lambda x, y: jnp.einsum('nij,njk->nik', x, y, preferred_element_type=F32)
    if not hi:
        return mm(a.astype(BF16), b.astype(BF16))
    (a1, a2), (b1, b2) = _split_bf16(a, 2), _split_bf16(b, 2)
    return mm(a1, b1) + (mm(a1, b2) + mm(a2, b1))


def _bmm_nt(a, b):
    return jnp.einsum('nid,njd->nij', a.astype(BF16), b.astype(BF16), preferred_element_type=F32)


TRI_BASE = 4


def _unit_tri_inverse(a, ri, ci):
    same = lambda w: (ri // w) == (ci // w)
    eye = (ri == ci).astype(F32)
    x = -jnp.where(same(TRI_BASE), a, 0.0)
    p = eye + x
    for _ in range(TRI_BASE.bit_length() - 2):
        x = _bmm(x, x, hi=True)
        p = p + _bmm(p, x, hi=True)
    w = TRI_BASE
    while w < a.shape[-1]:
        off = jnp.where(same(2 * w) & ~same(w), a, 0.0)
        p = p - _bmm(p, _bmm(off, p))
        w *= 2
    return p


def _gdn_prepare(q, k, v, ab, a_row, dt_row, head0, group):
    n, c, _ = q.shape
    two = lambda x: jnp.concatenate([x, x], axis=0)
    q, k, v, ab = two(q), two(k), two(v), two(ab)
    back3 = lambda shape: lax.broadcasted_iota(jnp.int32, shape, 0) >= n
    lane = lax.broadcasted_iota(jnp.int32, ab.shape, 2)
    bidx = lax.broadcasted_iota(jnp.int32, ab.shape, 0)
    head = head0 + jnp.where(bidx >= n, bidx - n, bidx) // group
    base = jnp.where(bidx >= n, 2 * N_HEAD, 0) + head
    g_all = -a_row * _softplus(ab + dt_row)
    g = jnp.sum(jnp.where(lane == base, g_all, 0.0), axis=2, keepdims=True)
    beta = jnp.sum(jnp.where(lane == base + N_HEAD, _sigmoid(ab), 0.0), axis=2, keepdims=True)

    sq = (2 * n, c, c)
    ri = lax.broadcasted_iota(jnp.int32, sq, 1)
    ci = lax.broadcasted_iota(jnp.int32, sq, 2)
    ahead = jnp.where(back3(sq), ci - ri, ri - ci)
    incl = ahead >= 0
    strict = ahead > 0
    tri = jnp.where(incl, 1.0, 0.0).astype(BF16)
    gc = sum(jnp.einsum('nij,njk->nik', tri, piece, preferred_element_type=F32)
             for piece in _split_bf16(jnp.broadcast_to(g, q.shape), 3))
    gc_row = jnp.swapaxes(gc, 1, 2)[:, :c, :]
    total = jnp.where(back3((2 * n, 1, D_HEAD)), gc[:, 0:1, :], gc[:, c - 1:c, :])
    decay = jnp.where(incl, jnp.exp(jnp.where(incl, gc[:, :, :c] - gc_row, 0.0)), 0.0)

    kb = k * beta
    a = jnp.where(strict, _bmm_nt(kb, k) * decay, 0.0)
    t = _unit_tri_inverse(a, ri, ci)
    e = jnp.exp(gc)
    u = _bmm(t, v * beta)
    w = _bmm(t, kb * e)
    a_intra = jnp.where(incl, _bmm_nt(q, k) * decay, 0.0)
    return (u, w.astype(BF16), (q * e).astype(BF16), (k * jnp.exp(total - gc)).astype(BF16),
            a_intra.astype(BF16), jnp.exp(total))


def _gdn_kernel(*refs, aliased, has_s0, group):
    if aliased:
        refs = refs[1:]
    if has_s0:
        (q_ref, k_ref, v_ref, z_ref, ab_ref, wq_ref, wk_ref, wv_ref, arow_ref, dt_ref, gn_ref, s0_ref,
         y_ref, sf_ref, qn, kn, vn, u_s, w_s, qd_s, kd_s, ai_s, gl_s) = refs
    else:
        (q_ref, k_ref, v_ref, z_ref, ab_ref, wq_ref, wk_ref, wv_ref, arow_ref, dt_ref, gn_ref,
         y_ref, sf_ref, qn, kn, vn, u_s, w_s, qd_s, kd_s, ai_s, gl_s) = refs
    l = q_ref.shape[1]
    heads = q_ref.shape[2] // D_HEAD
    head0 = pl.program_id(1) * heads
    n_chunks = l // CHUNK
    hcols = lambda hh: slice(hh * D_HEAD, (hh + 1) * D_HEAD)

    def l2n(x):
        return x * lax.rsqrt(jnp.sum(x * x, axis=-1, keepdims=True) + EPS)

    for hh in range(heads):
        cols = hcols(hh)
        qn[:, cols] = l2n(_silu(_dwconv3(q_ref[0, :, cols], wq_ref.at[:, cols]))) * (D_HEAD ** -0.5)
        kn[:, cols] = l2n(_silu(_dwconv3(k_ref[0, :, cols], wk_ref.at[:, cols])))
        vn[:, cols] = _silu(_dwconv3(v_ref[0, :, cols], wv_ref.at[:, cols]))

    a_row, dt_row = arow_ref[...], dt_ref[...]

    def prepare(gi, carry):
        span = group * CHUNK
        rows = pl.ds(pl.multiple_of(gi * span, span), span)
        chunks = lambda x: x.reshape(group, CHUNK, x.shape[-1])
        per_head = lambda ref: jnp.concatenate([chunks(ref[rows, hcols(hh)]) for hh in range(heads)], axis=0)
        ab = chunks(ab_ref[0, rows, :])
        u, w, qd, kd, ai, gl = _gdn_prepare(per_head(qn), per_head(kn), per_head(vn),
                                            jnp.concatenate([ab] * heads, axis=0), a_row, dt_row, head0, group)
        for d in range(2):
            for hh in range(heads):
                cols = hcols(hh)
                part = slice((d * heads + hh) * group, (d * heads + hh + 1) * group)
                u_s[d, rows, cols] = u[part].reshape(span, D_HEAD)
                w_s[d, rows, cols] = w[part].reshape(span, D_HEAD)
                qd_s[d, rows, cols] = qd[part].reshape(span, D_HEAD)
                kd_s[d, rows, cols] = kd[part].reshape(span, D_HEAD)
                ai_s[d, hh, rows, :] = ai[part].reshape(span, CHUNK)
                gl_s[d, hh, pl.ds(gi * group, group)] = jnp.broadcast_to(gl[part], (group,) + gl_s.shape[3:])
        return carry

    lax.fori_loop(0, n_chunks // group, prepare, 0)

    def scan(i, s):
        where = [(hh, d, pl.ds(pl.multiple_of(chunk * CHUNK, CHUNK), CHUNK), chunk)
                 for hh in range(heads) for d, chunk in ((0, i), (1, n_chunks - 1 - i))]
        gather = lambda ref: jnp.stack([ref[d, rows, hcols(hh)] for hh, d, rows, _ in where])
        a_intra = jnp.stack([ai_s[d, hh, rows, :] for hh, d, rows, _ in where])
        decay = jnp.stack([gl_s[d, hh, chunk][0:1, :] for hh, d, _, chunk in where])
        sb = s.astype(BF16)
        v_new = gather(u_s) - _bmm(gather(w_s), sb)
        vb = v_new.astype(BF16)
        o = _bmm(gather(qd_s), sb) + _bmm(a_intra, vb)
        for idx, (hh, d, rows, _) in enumerate(where):
            u_s[d, rows, hcols(hh)] = o[idx]
        return s * decay + jnp.einsum('nik,niv->nkv', gather(kd_s), vb, preferred_element_type=F32)

    if has_s0:
        init = jnp.stack([s0_ref[0, 0, d, hh] for hh in range(heads) for d in range(2)])
    else:
        init = jnp.zeros((2 * heads, D_HEAD, D_HEAD), F32)
    final = lax.fori_loop(0, n_chunks, scan, init)
    for hh in range(heads):
        cols = hcols(hh)
        sf_ref[0, 0, 0, hh] = final[2 * hh]
        sf_ref[0, 0, 1, hh] = final[2 * hh + 1]
        y_ref[0, :, cols] = (_rms(u_s[0, :, cols] + u_s[1, :, cols], gn_ref[...])
                             * _silu(z_ref[0, :, cols])).astype(y_ref.dtype)


def _gdn(proj3, ab3, conv_w, a_log, dt_bias, norm_g, layer, state=None, new_state=None):
    b, l, _ = proj3.shape
    depth_out, layer_out = (1, 0) if state is not None else (DEPTH, layer)
    aliased = new_state is not None
    lanes = jnp.zeros((2, 2 * N_HEAD), F32).at[:, :N_HEAD].set(1.0)
    a_row = jnp.pad((jnp.exp(a_log.astype(F32))[:, None, :] * lanes.reshape(2, 2, N_HEAD)).reshape(1, -1),
                    ((0, 0), (0, AB_PAD - 4 * N_HEAD)))
    dt_row = jnp.pad((dt_bias.astype(F32)[:, None, :] * lanes.reshape(2, 2, N_HEAD)).reshape(1, -1),
                     ((0, 0), (0, AB_PAD - 4 * N_HEAD)))
    hps = N_HEAD if l <= 512 else 1
    wid = hps * D_HEAD
    n_hb = N_HEAD // hps
    blk = lambda c: pl.BlockSpec((1, l, wid), lambda i, h, c=c: (i, 0, c * n_hb + h))
    wblk = lambda c: pl.BlockSpec((3, wid), lambda i, h, c=c: (0, c * n_hb + h))
    row = pl.BlockSpec((1, D_HEAD), lambda i, h: (0, 0))
    in_specs = [blk(0), blk(1), blk(2), blk(3),
                pl.BlockSpec((1, l, AB_PAD), lambda i, h: (i, 0, 0)),
                wblk(0), wblk(1), wblk(2), row, row, row]
    args = [proj3, proj3, proj3, proj3, ab3, conv_w, conv_w, conv_w, a_row, dt_row, norm_g]
    if aliased:
        in_specs.insert(0, pl.BlockSpec(memory_space=pl.ANY))
        args.insert(0, new_state)
    if state is not None:
        in_specs.append(pl.BlockSpec((1, 1, 2, hps, D_HEAD, D_HEAD), lambda i, h: (i, layer, 0, h, 0, 0)))
        args.append(state)
    return pl.pallas_call(
        functools.partial(_gdn_kernel, aliased=aliased, has_s0=state is not None, group=min(8, l // CHUNK)),
        grid=(b, n_hb),
        in_specs=in_specs,
        out_specs=[pl.BlockSpec((1, l, wid), lambda i, h: (i, 0, h)),
                   pl.BlockSpec((1, 1, 2, hps, D_HEAD, D_HEAD), lambda i, h: (i, layer_out, 0, h, 0, 0))],
        out_shape=[jax.ShapeDtypeStruct((b, l, BR_W), BF16),
                   jax.ShapeDtypeStruct((b, depth_out, 2, N_HEAD, D_HEAD, D_HEAD), F32)],
        input_output_aliases={0: 1} if aliased else {},
        scratch_shapes=[pltpu.VMEM((l, wid), F32)] * 3
        + [pltpu.VMEM((2, l, wid), F32)] + [pltpu.VMEM((2, l, wid), BF16)] * 3
        + [pltpu.VMEM((2, hps, l, CHUNK), BF16), pltpu.VMEM((2, hps, l // CHUNK, 8, D_HEAD), F32)],
        compiler_params=_cparams("parallel", "parallel"),
        name="gdn",
    )(*args)


def _mod_kernel(c_ref, w_ref, b_ref, o_ref):
    o_ref[...] = _dot_hi(_silu(c_ref[...]), w_ref[...]) + b_ref[...]


def _modulation(cond, w_mod, b_mod, layer, tn=512):
    n = cond.shape[0]
    rows = 8
    out = pl.pallas_call(
        _mod_kernel,
        grid=(3 * D_MODEL // tn,),
        in_specs=[pl.BlockSpec((rows, D_MODEL), lambda j: (0, 0)),
                  pl.BlockSpec((None, D_MODEL, tn), lambda j: (layer, 0, j)),
                  pl.BlockSpec((1, tn), lambda j: (0, j))],
        out_specs=pl.BlockSpec((rows, tn), lambda j: (0, j)),
        out_shape=jax.ShapeDtypeStruct((rows, 3 * D_MODEL), F32),
        compiler_params=_cparams("parallel"),
        name="modulation",
    )(jnp.pad(cond.astype(F32), ((0, rows - n), (0, 0))), w_mod, b_mod.reshape(1, -1))
    return out[:n].reshape(n, 3, D_MODEL)


def _split_w_in(w_in):
    n_a = 4 * BR_W + 4 * N_HEAD
    zeros = jnp.zeros(w_in.shape[:-1] + (AB_PAD - 4 * N_HEAD,), w_in.dtype)
    return jnp.concatenate([w_in[..., :4 * BR_W], w_in[..., n_a:], w_in[..., 4 * BR_W:n_a], zeros],
                           axis=-1).astype(BF16)


def _trunk_layer(x3, cond, p, big, layer, dft, latent, new_outputs=(None, None, None)):
    b, l, _ = x3.shape
    x2 = x3.reshape(b * l, D_MODEL)
    mod = _modulation(cond, big['w_mod'], p['b_mod'], layer)
    rows_per_mod = l if mod.shape[0] == b else b * l
    g_pre = p['g_pre'].reshape(1, D_MODEL)
    proj, ab = _inproj(x2, mod, g_pre, big['w_in'], layer, rows_per_mod)
    proj3 = proj.reshape(b, l, N_MAIN)
    ab3 = ab.reshape(b, l, AB_PAD)

    lam_init = 0.8 - 0.6 * math.exp(-0.3 * layer)
    lam_p = p['diff_lam'].astype(F32)
    lam = (jnp.exp(jnp.sum(lam_p[0] * lam_p[1])) - jnp.exp(jnp.sum(lam_p[2] * lam_p[3])) + lam_init).reshape(1, 1)
    diff_norm = p['diff_norm'].reshape(1, D_HEAD)
    gdn_args = (proj3, ab3, p['gdn_conv'], p['gdn_a_log'], p['gdn_dt_bias'], p['gdn_norm'].reshape(1, D_HEAD), layer)

    yb = _hyena(proj3, p, dft)
    if latent is None:
        new_state, nat_cache, diff_cache = new_outputs
        ya, new_state = _gdn(*gdn_args, new_state=new_state)
        yc, yd, nat_cache, diff_cache = _ctx_attention(proj3, lam, diff_norm, lam_init, layer, nat_cache, diff_cache)
        extras = (new_state, nat_cache, diff_cache)
    else:
        ya, _ = _gdn(*gdn_args, state=latent['state_gdn'])
        yc = _lat_nat(proj3, latent['cache_nat_kv'], layer, _nat_bias_table(p['nat_rpb']))
        yd = _lat_diff(proj3, latent['cache_diff_kv'], layer, lam, diff_norm, lam_init, latent['rope'])
        extras = None

    ys = [t.reshape(b * l, BR_W) for t in (ya, yb, yc, yd)]
    out = _merge(x2, mod, g_pre, p['g_post'].reshape(1, D_MODEL), ys, big['w_branch'], big['w_merge'],
                 p['b_merge'].reshape(1, -1).astype(F32), big['w_out'], layer, rows_per_mod)
    return out.reshape(b, l, D_MODEL), extras


def kernel(x_prompt, x_sample, state_gdn, cache_nat_kv, cache_diff_kv, c, c_ctx,
           w_mod, b_mod, g_pre, g_post, w_in, gdn_conv, gdn_a_log, gdn_dt_bias, gdn_norm,
           hy_conv, hy_w1, hy_b1, hy_w2, hy_b2, hy_w3, hy_b3, hy_decay, hy_skip,
           nat_rpb, diff_lam, diff_norm, w_branch, w_merge, b_merge, w_out):
    small = {
        'b_mod': b_mod, 'g_pre': g_pre, 'g_post': g_post,
        'gdn_conv': gdn_conv, 'gdn_a_log': gdn_a_log, 'gdn_dt_bias': gdn_dt_bias, 'gdn_norm': gdn_norm,
        'hy_conv': hy_conv, 'hy_w1': hy_w1, 'hy_b1': hy_b1, 'hy_w2': hy_w2, 'hy_b2': hy_b2,
        'hy_w3': hy_w3, 'hy_b3': hy_b3, 'hy_decay': hy_decay, 'hy_skip': hy_skip,
        'nat_rpb': nat_rpb, 'diff_lam': diff_lam, 'diff_norm': diff_norm, 'b_merge': b_merge,
    }
    layers = [{name: arr[i] for name, arr in small.items()} for i in range(DEPTH)]
    big = {'w_mod': w_mod.astype(F32), 'w_in': _split_w_in(w_in), 'w_branch': w_branch.astype(BF16),
           'w_merge': w_merge.astype(BF16), 'w_out': w_out.astype(BF16)}

    y_prompt = x_prompt
    dft_ctx = _dft_matrices(x_prompt.shape[1])
    outputs = (None, None, None)
    for i, p in enumerate(layers):
        y_prompt, outputs = _trunk_layer(y_prompt, c_ctx.reshape(1, D_MODEL), p, big, i, dft_ctx, None, outputs)
    new_state, nat_cache, diff_cache = outputs

    y_sample = x_sample
    dft_lat = _dft_matrices(x_sample.shape[1])
    latent = {'state_gdn': state_gdn, 'cache_nat_kv': cache_nat_kv, 'cache_diff_kv': cache_diff_kv,
              'rope': _rope_tables(x_sample.shape[1])}
    for i, p in enumerate(layers):
        y_sample, _ = _trunk_layer(y_sample, c, p, big, i, dft_lat, latent)

    return (y_prompt, y_sample, new_state, nat_cache, diff_cache)
```

```python
import functools
import math

import jax
import jax.numpy as jnp
import numpy as np
from jax import lax
from jax.experimental import pallas as pl
from jax.experimental.pallas import tpu as pltpu

F32 = jnp.float32
BF16 = jnp.bfloat16

D_MODEL = 1024
DEPTH = 2
GRID_W = 64
N_BRANCH = 4
BR_W = 512
N_HEAD = 4
D_HEAD = 128
SUBLANES = 8
CHUNK = 64
HY_BANDS = 16
WIN_R = 8
WIN_C = 16
DQK_D = 64
ROPE_BASE = 10000.0
EPS = 1e-6
N_MAIN = 4 * 4 * BR_W
AB_PAD = 128
NEG_INF = -1e30

VMEM_LIMIT = 48 * 1024 * 1024


def _cparams(*sem):
    return pltpu.CompilerParams(dimension_semantics=sem, vmem_limit_bytes=VMEM_LIMIT)


def _silu(x):
    return x * (1.0 / (1.0 + jnp.exp(-x)))


def _sigmoid(x):
    return 1.0 / (1.0 + jnp.exp(-x))


def _rms(x, g):
    return x * lax.rsqrt(jnp.mean(x * x, axis=-1, keepdims=True) + EPS) * g


def _dot(a, b):
    return jnp.dot(a.astype(BF16), b.astype(BF16), preferred_element_type=F32)


def _dot_nt(a, b):
    return lax.dot_general(a.astype(BF16), b.astype(BF16), (((1,), (1,)), ((), ())),
                           preferred_element_type=F32)


def _dot_tn(a, b):
    return lax.dot_general(a.astype(BF16), b.astype(BF16), (((0,), (0,)), ((), ())),
                           preferred_element_type=F32)


def _prenorm(x, g_pre, mod_ref):
    return _rms(x, g_pre) * (1.0 + mod_ref[0, 1:2, :]) + mod_ref[0, 0:1, :]


def _inproj_kernel(x_ref, mod_ref, gpre_ref, w_ref, wab_ref, proj_ref, ab_ref, h_scr):
    @pl.when(pl.program_id(1) == 0)
    def _():
        h = _prenorm(x_ref[...], gpre_ref[...], mod_ref).astype(BF16)
        h_scr[...] = h
        ab_ref[...] = jnp.dot(h, wab_ref[...], preferred_element_type=F32)

    acc = jnp.dot(h_scr[...], w_ref[...], preferred_element_type=F32)
    seqs, cblocks, rows, _ = proj_ref.shape
    for sq in range(seqs):
        for c in range(cblocks):
            proj_ref[sq, c] = acc[sq * rows:(sq + 1) * rows, c * D_HEAD:(c + 1) * D_HEAD]


def _inproj(x2, mod, g_pre, w_packed, layer, rows_per_mod, l, tm=1024, tn=1024):
    m = x2.shape[0]
    tm = math.gcd(tm, rows_per_mod)
    if tm >= l:
        proj_spec = pl.BlockSpec((tm // l, tn // D_HEAD, l, D_HEAD), lambda i, j: (i, j, 0, 0))
    else:
        per = l // tm
        proj_spec = pl.BlockSpec((1, tn // D_HEAD, tm, D_HEAD), lambda i, j: (i // per, j, i % per, 0))
    return pl.pallas_call(
        _inproj_kernel,
        grid=(m // tm, N_MAIN // tn),
        in_specs=[
            pl.BlockSpec((tm, D_MODEL), lambda i, j: (i, 0)),
            pl.BlockSpec((1, 3, D_MODEL), lambda i, j: ((i * tm) // rows_per_mod, 0, 0)),
            pl.BlockSpec((1, D_MODEL), lambda i, j: (0, 0)),
            pl.BlockSpec((None, D_MODEL, tn), lambda i, j: (layer, 0, j)),
            pl.BlockSpec((None, D_MODEL, AB_PAD), lambda i, j: (layer, 0, N_MAIN // AB_PAD)),
        ],
        out_specs=[
            proj_spec,
            pl.BlockSpec((tm, AB_PAD), lambda i, j: (i, 0)),
        ],
        out_shape=[jax.ShapeDtypeStruct((m // l, N_MAIN // D_HEAD, l, D_HEAD), F32),
                   jax.ShapeDtypeStruct((m, AB_PAD), F32)],
        scratch_shapes=[pltpu.VMEM((tm, D_MODEL), BF16)],
        compiler_params=_cparams("parallel", "arbitrary"),
        name="inproj",
    )(x2, mod, g_pre, w_packed, w_packed)


def _merge_kernel(x_ref, mod_ref, gpre_ref, gpost_ref, ya_ref, yb_ref, yc_ref, yd_ref,
                  wbr_ref, wmg_ref, bmg_ref, wout_ref, o_ref):
    x = x_ref[...]
    h = _prenorm(x, gpre_ref[...], mod_ref).astype(BF16)
    acc = None
    for k, y_ref in enumerate((ya_ref, yb_ref, yc_ref, yd_ref)):
        cols = slice(k * D_MODEL, (k + 1) * D_MODEL)
        gate = _sigmoid(jnp.dot(h, wmg_ref[:, cols], preferred_element_type=F32) + bmg_ref[:, cols])
        br = jnp.dot(y_ref[...], wbr_ref[k], preferred_element_type=F32)
        acc = gate * br if acc is None else acc + gate * br
    y = jnp.dot(acc.astype(BF16), wout_ref[...], preferred_element_type=F32)
    o_ref[...] = x + mod_ref[0, 2:3, :] * _rms(y, gpost_ref[...])


def _merge(x2, mod, g_pre, g_post, ys, w_branch, w_merge, b_merge, w_out, layer, rows_per_mod, tm=256):
    m = x2.shape[0]
    row = lambda i: (i, 0)
    fixed2 = lambda i: (0, 0)
    return pl.pallas_call(
        _merge_kernel,
        grid=(m // tm,),
        in_specs=[
            pl.BlockSpec((tm, D_MODEL), row),
            pl.BlockSpec((1, 3, D_MODEL), lambda i: ((i * tm) // rows_per_mod, 0, 0)),
            pl.BlockSpec((1, D_MODEL), fixed2),
            pl.BlockSpec((1, D_MODEL), fixed2),
            pl.BlockSpec((tm, BR_W), row),
            pl.BlockSpec((tm, BR_W), row),
            pl.BlockSpec((tm, BR_W), row),
            pl.BlockSpec((tm, BR_W), row),
            pl.BlockSpec((None, N_BRANCH, BR_W, D_MODEL), lambda i: (layer, 0, 0, 0)),
            pl.BlockSpec((None, D_MODEL, N_BRANCH * D_MODEL), lambda i: (layer, 0, 0)),
            pl.BlockSpec((1, N_BRANCH * D_MODEL), fixed2),
            pl.BlockSpec((None, D_MODEL, D_MODEL), lambda i: (layer, 0, 0)),
        ],
        out_specs=pl.BlockSpec((tm, D_MODEL), row),
        out_shape=jax.ShapeDtypeStruct((m, D_MODEL), F32),
        compiler_params=_cparams("parallel"),
        name="merge",
    )(x2, mod, g_pre, g_post, *ys, w_branch, w_merge, b_merge, w_out)


def _softmax_rows(s):
    p = jnp.exp(s - jnp.max(s, axis=-1, keepdims=True))
    return p, jnp.sum(p, axis=-1, keepdims=True)


def _head_cols(h):
    return slice(h * D_HEAD, (h + 1) * D_HEAD)


def _stack_heads(ref):
    return ref[0]


def _ctx_nat_kernel(*refs, aliased):
    q_ref, k_ref, v_ref, g_ref, y_ref, kv_ref = refs[1:] if aliased else refs
    scale = D_HEAD ** -0.5
    q, k, v = (_stack_heads(r) for r in (q_ref, k_ref, v_ref))
    p, l = _softmax_rows(_bmm_nt(q, k) * scale)
    o = _bmm(p, v) / l
    for h in range(N_HEAD):
        sl = _head_cols(h)
        y_ref[0, :, sl] = (o[h] * _silu(g_ref[0, h])).astype(y_ref.dtype)
        kv_ref[0, 0, 0, h] = k[h]
        kv_ref[0, 0, 1, h] = v[h]


def _map_masks():
    lane = lax.broadcasted_iota(jnp.int32, (1, D_HEAD), 1)
    first = (lane < DQK_D).astype(F32)
    return first, 1.0 - first


def _ctx_diff_kernel(*refs, aliased, out_scale):
    lam_ref, q_ref, k_ref, v_ref, g_ref, gn_ref, y_ref, kv_ref = refs[1:] if aliased else refs
    scale = DQK_D ** -0.5
    m1, m2 = _map_masks()
    q, k, v = (_stack_heads(r) for r in (q_ref, k_ref, v_ref))
    p, l = _softmax_rows(_bmm_nt(jnp.concatenate([q * m1, q * m2], axis=0), jnp.concatenate([k, k], axis=0)) * scale)
    pn = p / l
    a = pn[:N_HEAD] - lam_ref[...] * pn[N_HEAD:]
    o = _rms(_bmm(a, v), gn_ref[...]) * out_scale
    for h in range(N_HEAD):
        sl = _head_cols(h)
        y_ref[0, :, sl] = (o[h] * _silu(g_ref[0, h])).astype(y_ref.dtype)
        kv_ref[0, 0, 0, h] = k[h]
        kv_ref[0, 0, 1, h] = v[h]


def _ctx_attention(proj3, lam, diff_norm, lam_init, layer, nat_cache, diff_cache):
    b, _, l, _ = proj3.shape
    blk = lambda c: pl.BlockSpec((1, N_HEAD, l, D_HEAD), lambda i, c=c: (i, c, 0, 0))
    y_spec = pl.BlockSpec((1, l, BR_W), lambda i: (i, 0, 0))
    kv_spec = pl.BlockSpec((1, 1, 2, N_HEAD, l, D_HEAD), lambda i: (i, layer, 0, 0, 0, 0))
    out_shape = [jax.ShapeDtypeStruct((b, l, BR_W), BF16),
                 jax.ShapeDtypeStruct((b, DEPTH, 2, N_HEAD, l, D_HEAD), F32)]
    aliased = nat_cache is not None
    cache_specs = [pl.BlockSpec(memory_space=pl.ANY)] if aliased else []
    aliases = {0: 1} if aliased else {}
    yc, nat_cache = pl.pallas_call(
        functools.partial(_ctx_nat_kernel, aliased=aliased),
        grid=(b,),
        in_specs=cache_specs + [blk(8), blk(9), blk(10), blk(11)],
        out_specs=[y_spec, kv_spec],
        out_shape=out_shape,
        input_output_aliases=aliases,
        compiler_params=_cparams("parallel"),
        name="ctx_nat",
    )(*([nat_cache] if aliased else []), proj3, proj3, proj3, proj3)
    yd, diff_cache = pl.pallas_call(
        functools.partial(_ctx_diff_kernel, aliased=aliased, out_scale=1.0 - lam_init),
        grid=(b,),
        in_specs=cache_specs + [pl.BlockSpec((1, 1), lambda i: (0, 0)),
                                blk(12), blk(13), blk(14), blk(15),
                                pl.BlockSpec((1, D_HEAD), lambda i: (0, 0))],
        out_specs=[y_spec, kv_spec],
        out_shape=out_shape,
        input_output_aliases=aliases,
        compiler_params=_cparams("parallel"),
        name="ctx_diff",
    )(*([diff_cache] if aliased else []), lam, proj3, proj3, proj3, proj3, diff_norm)
    return yc, yd, nat_cache, diff_cache


def _nat_bias_table(rpb):
    cols = np.arange(GRID_W)
    start = np.clip(cols - WIN_C // 2, 0, GRID_W - WIN_C)
    inside = (cols[None, :] >= start[:, None]) & (cols[None, :] < start[:, None] + WIN_C)
    dc = cols[None, :] - cols[:, None] + (WIN_C - 1)
    onehot = ((dc[None] == np.arange(2 * WIN_C - 1)[:, None, None]) & inside[None]).astype(np.float32)
    t = jnp.einsum('hdx,xck->hdck', rpb.astype(F32), jnp.asarray(onehot), precision=lax.Precision.HIGHEST)
    t = jnp.where(jnp.asarray(inside)[None, None], t, NEG_INF)
    tab = jnp.stack([t[:, WIN_R - 1 - off:2 * WIN_R - 1 - off] for off in range(WIN_R)], axis=1)
    return tab.transpose(0, 1, 3, 2, 4).reshape(rpb.shape[0], WIN_R, GRID_W, WIN_R * GRID_W)


def _lat_nat_kernel(q_ref, k_ref, v_ref, g_ref, ckv_ref, bias_ref, y_ref, kb_scr, vb_scr, *, rb):
    scale = D_HEAD ** -0.5
    rows = q_ref.shape[1] // GRID_W
    win = WIN_R * GRID_W
    kb_scr[...] = k_ref[0].astype(BF16)
    vb_scr[...] = v_ref[0].astype(BF16)
    ck = ckv_ref[0, 0, 0, 0].astype(BF16)
    cv = ckv_ref[0, 0, 1, 0].astype(BF16)

    def row_block(i, carry):
        q0 = pl.multiple_of(i * (rb * GRID_W), rb * GRID_W)
        qrows = pl.ds(q0, rb * GRID_W)
        q = q_ref[0, qrows, :].astype(BF16)
        kw, vw, bias = [], [], []
        for j in range(rb):
            r = i * rb + j
            rs = jnp.clip(r - WIN_R // 2, 0, rows - WIN_R)
            wrows = pl.ds(pl.multiple_of(rs * GRID_W, GRID_W), win)
            kw.append(kb_scr[wrows, :])
            vw.append(vb_scr[wrows, :])
            bias.append(bias_ref[0, r - rs])
        q3 = q.reshape(rb, GRID_W, D_HEAD)
        s_lat = _bmm_nt(q3, jnp.stack(kw)) * scale + jnp.stack(bias)
        s_ctx = (_dot_nt(q, ck) * scale).reshape(rb, GRID_W, ck.shape[0])
        m = jnp.maximum(jnp.max(s_lat, axis=-1, keepdims=True), jnp.max(s_ctx, axis=-1, keepdims=True))
        p_lat = jnp.exp(s_lat - m)
        p_ctx = jnp.exp(s_ctx - m)
        l = jnp.sum(p_lat, axis=-1, keepdims=True) + jnp.sum(p_ctx, axis=-1, keepdims=True)
        o_ctx = _dot(p_ctx.reshape(rb * GRID_W, ck.shape[0]), cv).reshape(rb, GRID_W, D_HEAD)
        o = ((_bmm(p_lat, jnp.stack(vw)) + o_ctx) / l).reshape(rb * GRID_W, D_HEAD)
        y_ref[0, qrows, :] = (o * _silu(g_ref[0, qrows, :])).astype(y_ref.dtype)
        return carry

    lax.fori_loop(0, rows // rb, row_block, 0)


def _lat_nat(proj3, cache_nat_kv, layer, bias_tab):
    b, _, l, _ = proj3.shape
    past = cache_nat_kv.shape[4]
    blk = lambda c: pl.BlockSpec((1, None, l, D_HEAD), lambda i, h, c=c: (i, c + h, 0, 0))
    return pl.pallas_call(
        functools.partial(_lat_nat_kernel, rb=8),
        grid=(b, N_HEAD),
        in_specs=[blk(32), blk(36), blk(40), blk(44),
                  pl.BlockSpec((1, 1, 2, 1, past, D_HEAD), lambda i, h: (i, layer, 0, h, 0, 0)),
                  pl.BlockSpec((1, WIN_R, GRID_W, WIN_R * GRID_W), lambda i, h: (h, 0, 0, 0))],
        out_specs=pl.BlockSpec((1, l, D_HEAD), lambda i, h: (i, 0, h)),
        out_shape=jax.ShapeDtypeStruct((b, l, BR_W), BF16),
        scratch_shapes=[pltpu.VMEM((l, D_HEAD), BF16), pltpu.VMEM((l, D_HEAD), BF16)],
        compiler_params=_cparams("parallel", "parallel"),
        name="lat_nat",
    )(proj3, proj3, proj3, proj3, cache_nat_kv, bias_tab)


def _rope_tables(l):
    half = DQK_D // 2
    nf = half // 2
    t = jnp.arange(l)
    row = (t // GRID_W).astype(F32)
    col = (t % GRID_W).astype(F32)
    inv = ROPE_BASE ** (-jnp.arange(nf, dtype=F32) / nf)
    ang = jnp.concatenate([row[:, None] * inv, col[:, None] * inv], axis=-1)
    cos, sin = jnp.cos(ang), jnp.sin(ang)
    zero = jnp.zeros_like(sin)
    tile2 = lambda a, b: jnp.concatenate([a, b, a, b], axis=-1)
    return tile2(cos, cos), tile2(-sin, zero), tile2(zero, sin)


def _rope(x, cos, sin_a, sin_b):
    return x * cos + pltpu.roll(x, 96, 1) * sin_a + pltpu.roll(x, 32, 1) * sin_b


def _lat_diff_kernel(lam_ref, q_ref, k_ref, v_ref, g_ref, ckv_ref, gn_ref,
                     cq_ref, saq_ref, sbq_ref, ck_ref, sak_ref, sbk_ref,
                     y_ref, ks_scr, vt_scr, *, out_scale, prep_rows, key_block, ahead):
    scale = DQK_D ** -0.5
    l = k_ref.shape[1]

    @pl.when(pl.program_id(2) == 0)
    def _():
        def prep(i, carry):
            rows = pl.ds(pl.multiple_of(i * prep_rows, prep_rows), prep_rows)
            kr = _rope(k_ref[0, rows, :], ck_ref[rows, :], sak_ref[rows, :], sbk_ref[rows, :])
            ks_scr[rows, :] = kr.astype(BF16)
            vt_scr[:, rows] = v_ref[0, rows, :].T.astype(BF16)
            return carry

        lax.fori_loop(0, l // prep_rows, prep, 0)
        ks_scr[l:, :] = ckv_ref[0, 0, 0, 0].astype(BF16)
        vt_scr[:, l:] = ckv_ref[0, 0, 1, 0].T.astype(BF16)

    q = _rope(q_ref[0], cq_ref[...], saq_ref[...], sbq_ref[...]) * (scale * math.log2(math.e))
    m1, m2 = _map_masks()
    tq = q.shape[0]
    qm = jnp.concatenate([q * m1, q * m2], axis=0).astype(BF16)
    m = l_sum = acc = None
    n_blk = ks_scr.shape[0] // key_block
    block = lambda blk: slice(blk * key_block, (blk + 1) * key_block)
    scores = [_dot_nt(ks_scr[block(b), :], qm) for b in range(min(ahead, n_blk))]
    for blk in range(n_blk):
        rows = block(blk)
        s = scores.pop(0)
        if blk + ahead < n_blk:
            scores.append(_dot_nt(ks_scr[block(blk + ahead), :], qm))
        m_blk = jnp.max(s, axis=0, keepdims=True)
        if blk == 0:
            m = m_blk
            p = jnp.exp2(s - m)
            l_sum = jnp.sum(p, axis=0, keepdims=True)
            acc = _dot(vt_scr[:, rows], p)
        else:
            m_new = jnp.maximum(m, m_blk)
            alpha = jnp.exp2(m - m_new)
            p = jnp.exp2(s - m_new)
            l_sum = alpha * l_sum + jnp.sum(p, axis=0, keepdims=True)
            acc = alpha * acc + _dot(vt_scr[:, rows], p)
            m = m_new
    out = acc / l_sum
    d = out[:, :tq] - lam_ref[...] * out[:, tq:]
    d = d * lax.rsqrt(jnp.mean(d * d, axis=0, keepdims=True) + EPS)
    o = d.T * gn_ref[...] * out_scale
    y_ref[0] = (o * _silu(g_ref[0])).astype(y_ref.dtype)


def _lat_diff(proj3, cache_diff_kv, layer, lam, diff_norm, lam_init, rope_tabs, tq=256):
    b, _, l, _ = proj3.shape
    past = cache_diff_kv.shape[4]
    qblk = lambda c: pl.BlockSpec((1, None, tq, D_HEAD), lambda i, h, j, c=c: (i, c + h, j, 0))
    full = lambda c: pl.BlockSpec((1, None, l, D_HEAD), lambda i, h, j, c=c: (i, c + h, 0, 0))
    tq_tab = pl.BlockSpec((tq, D_HEAD), lambda i, h, j: (j, 0))
    full_tab = pl.BlockSpec((l, D_HEAD), lambda i, h, j: (0, 0))
    return pl.pallas_call(
        functools.partial(_lat_diff_kernel, out_scale=1.0 - lam_init, prep_rows=512, key_block=512, ahead=2),
        grid=(b, N_HEAD, l // tq),
        in_specs=[pl.BlockSpec((1, 1), lambda i, h, j: (0, 0)),
                  qblk(48), full(52), full(56), qblk(60),
                  pl.BlockSpec((1, 1, 2, 1, past, D_HEAD), lambda i, h, j: (i, layer, 0, h, 0, 0)),
                  pl.BlockSpec((1, D_HEAD), lambda i, h, j: (0, 0)),
                  tq_tab, tq_tab, tq_tab, full_tab, full_tab, full_tab],
        out_specs=pl.BlockSpec((1, tq, D_HEAD), lambda i, h, j: (i, j, h)),
        out_shape=jax.ShapeDtypeStruct((b, l, BR_W), BF16),
        scratch_shapes=[pltpu.VMEM((l + past, D_HEAD), BF16), pltpu.VMEM((D_HEAD, l + past), BF16)],
        compiler_params=_cparams("parallel", "parallel", "arbitrary"),
        name="lat_diff",
    )(lam, proj3, proj3, proj3, proj3, cache_diff_kv, diff_norm, *rope_tabs, *rope_tabs)


def _dwconv3(x, w_ref):
    l = x.shape[0]
    row = lax.broadcasted_iota(jnp.int32, x.shape, 0)
    prev = jnp.where(row == 0, 0.0, pltpu.roll(x, 1, 0))
    nxt = jnp.where(row == l - 1, 0.0, pltpu.roll(x, l - 1, 0))
    return prev * w_ref[0:1, :] + x * w_ref[1:2, :] + nxt * w_ref[2:3, :]


def _hy_pre_kernel(x_ref, above_ref, below_ref, w_ref, o_ref, ob_ref):
    t, n_t = pl.program_id(1), pl.num_programs(1)
    bt, cblocks, rows, _ = x_ref.shape
    row = lax.broadcasted_iota(jnp.int32, (rows, D_HEAD), 0)
    for bb in range(bt):
        for c in range(cblocks):
            cols = _head_cols(c)
            x = x_ref[bb, c]
            before = jnp.where(t == 0, 0.0, above_ref[bb, c, SUBLANES - 1:SUBLANES, :])
            after = jnp.where(t == n_t - 1, 0.0, below_ref[bb, c, 0:1, :])
            prev = jnp.where(row == 0, before, pltpu.roll(x, 1, 0))
            nxt = jnp.where(row == rows - 1, after, pltpu.roll(x, rows - 1, 0))
            y = prev * w_ref[0:1, cols] + x * w_ref[1:2, cols] + nxt * w_ref[2:3, cols]
            o_ref[bb, :, cols] = y
            ob_ref[bb, :, cols] = y.astype(BF16)


def _hy_pre(proj3, conv_w, tl=1024):
    b, _, l, _ = proj3.shape
    tl = min(tl, l)
    bt = _seqs_per_step(b, l)
    n = 3
    cb = BR_W // D_HEAD
    col0 = 4
    groups = tl // SUBLANES
    last_group = l // SUBLANES - 1
    spec = pl.BlockSpec((bt, tl, BR_W), lambda i, t, j: (i, t, j))
    return pl.pallas_call(
        _hy_pre_kernel,
        grid=(b // bt, l // tl, n),
        in_specs=[pl.BlockSpec((bt, cb, tl, D_HEAD), lambda i, t, j: (i, col0 + j, t, 0)),
                  pl.BlockSpec((bt, cb, SUBLANES, D_HEAD),
                               lambda i, t, j: (i, col0 + j, jnp.maximum(t * groups - 1, 0), 0)),
                  pl.BlockSpec((bt, cb, SUBLANES, D_HEAD),
                               lambda i, t, j: (i, col0 + j, jnp.minimum((t + 1) * groups, last_group), 0)),
                  pl.BlockSpec((3, BR_W), lambda i, t, j: (0, j))],
        out_specs=[spec, spec],
        out_shape=[jax.ShapeDtypeStruct((b, l, 3 * BR_W), F32),
                   jax.ShapeDtypeStruct((b, l, 3 * BR_W), BF16)],
        compiler_params=_cparams("parallel", "parallel", "parallel"),
        name="hy_pre",
    )(proj3, proj3, proj3, conv_w)


def _dot_hi(a, b):
    return jnp.dot(a, b, preferred_element_type=F32, precision=lax.Precision.HIGHEST)


def _hy_filter_kernel(feat_ref, dist_ref, w1_ref, b1_ref, w2_ref, b2_ref, w3_ref, b3_ref, dec_ref, o_ref):
    hid = jnp.sin(_dot_hi(feat_ref[...], w1_ref[...]) + b1_ref[...])
    hid = jnp.sin(_dot_hi(hid, w2_ref[...]) + b2_ref[...])
    dist = dist_ref[...]
    for j in range(o_ref.shape[1] // D_HEAD):
        cols = slice(j * D_HEAD, (j + 1) * D_HEAD)
        filt = _dot_hi(hid, w3_ref[:, cols]) + b3_ref[:, cols]
        o_ref[:, cols] = (filt * jnp.exp(-dist * jnp.abs(dec_ref[:, cols]))).astype(o_ref.dtype)


def _hy_filter(l, w1, b1, w2, b2, w3, b3, decay):
    pos = jnp.arange(l, dtype=F32)
    t = pos / l
    ang = (2.0 * math.pi) * t[:, None] * jnp.arange(1, HY_BANDS + 1, dtype=F32)
    feat = jnp.concatenate([t[:, None], jnp.cos(ang), jnp.sin(ang)], axis=-1)
    dist = jnp.broadcast_to((jnp.abs(pos - l // 2) / l)[:, None], (l, D_HEAD))
    pad = D_HEAD
    emb, ff = w1.shape
    feat = jnp.pad(feat, ((0, 0), (0, pad - emb)))
    w1p = jnp.pad(w1, ((0, pad - emb), (0, pad - ff)))
    w2p = jnp.pad(w2, ((0, pad - ff), (0, pad - ff)))
    w3p = jnp.pad(w3, ((0, pad - ff), (0, 0)))
    b1p = jnp.pad(b1, (0, pad - ff)).reshape(1, pad)
    b2p = jnp.pad(b2, (0, pad - ff)).reshape(1, pad)
    tl = min(l, 256)
    n = 2 * BR_W
    fixed = lambda shape: pl.BlockSpec(shape, lambda i: (0, 0))
    return pl.pallas_call(
        _hy_filter_kernel,
        grid=(l // tl,),
        in_specs=[pl.BlockSpec((tl, pad), lambda i: (i, 0)),
                  pl.BlockSpec((tl, D_HEAD), lambda i: (i, 0)),
                  fixed((pad, pad)), fixed((1, pad)), fixed((pad, pad)), fixed((1, pad)),
                  fixed((pad, n)), fixed((1, n)), fixed((1, n))],
        out_specs=pl.BlockSpec((tl, n), lambda i: (i, 0)),
        out_shape=jax.ShapeDtypeStruct((l, n), BF16),
        compiler_params=_cparams("parallel"),
        name="hy_filter",
    )(feat, dist, w1p, b1p, w2p, b2p, w3p, b3.reshape(1, n), decay.reshape(1, n))


def _dft_matrices(l):
    n = 2 * l
    k = jnp.arange(l, dtype=jnp.int32)
    t = jnp.arange(l, dtype=jnp.int32)
    tp = t + l // 2
    split = 1 << (max(l.bit_length() - 1, 0) // 2)

    def tables(rows, cols):
        def table(r):
            ang = (2.0 * math.pi / n) * ((r[:, None] * cols[None, :]) % n).astype(F32)
            return jnp.cos(ang), jnp.sin(ang)
        return (*table(rows[::split]), *table(rows[:split] - rows[0]))

    alt = jnp.where(t % 2 == 0, 1.0, -1.0).astype(F32).reshape(1, l)
    wk = (jnp.where(k == 0, 1.0, 2.0).astype(F32) / n).reshape(1, l)
    out = pl.pallas_call(
        functools.partial(_dft_gen_kernel, split=split, l=l),
        grid=(l // split,),
        in_specs=[pl.BlockSpec((l // split, l), lambda i: (0, 0))] * 2 + [pl.BlockSpec((split, l), lambda i: (0, 0))] * 2
        + [pl.BlockSpec((l // split, l), lambda i: (0, 0))] * 2 + [pl.BlockSpec((split, l), lambda i: (0, 0))] * 2
        + [pl.BlockSpec((1, l), lambda i: (0, 0))] * 2,
        out_specs=[pl.BlockSpec((split, l), lambda i: (i, 0))] * 4,
        out_shape=[jax.ShapeDtypeStruct((l, l), BF16)] * 4,
        compiler_params=_cparams("parallel"),
        name="dft_gen",
    )(*tables(k, t), *tables(tp, k), alt, wk)
    return (out[0], out[1]), (out[2], out[3])


def _dft_gen_kernel(ch_ref, sh_ref, cl_ref, sl_ref, chi_ref, shi_ref, cli_ref, sli_ref, alt_ref, wk_ref,
                    fc_ref, fs_ref, ic_ref, is_ref, *, split, l):
    i = pl.program_id(0)

    def cos_sin(c_hi, s_hi, c_lo, s_lo):
        ch, sh = c_hi[pl.ds(i, 1), :], s_hi[pl.ds(i, 1), :]
        return ch * c_lo[...] - sh * s_lo[...], sh * c_lo[...] + ch * s_lo[...]

    row = lax.broadcasted_iota(jnp.int32, (split, l), 0) + i * split
    col = lax.broadcasted_iota(jnp.int32, (split, l), 1)
    c, s = cos_sin(ch_ref, sh_ref, cl_ref, sl_ref)
    fc_ref[...] = c.astype(BF16)
    fs_ref[...] = jnp.where(row == 0, alt_ref[...], -s).astype(BF16)
    c, s = cos_sin(chi_ref, shi_ref, cli_ref, sli_ref)
    wk = wk_ref[...]
    alt_i = jnp.where(row % 2 == 0, 1.0, -1.0) * (1.0 / (2 * l))
    ic_ref[...] = (c * wk).astype(BF16)
    is_ref[...] = jnp.where(col == 0, alt_i, -s * wk).astype(BF16)


def _seqs_per_step(b, l, rows=2048):
    bt = max(1, min(b, rows // l))
    while b % bt:
        bt -= 1
    return bt


def _dft_fwd_kernel(fc_ref, fs_ref, x_ref, *rest, with_filter, tm):
    for bb in range(x_ref.shape[0]):
        x = x_ref[bb]
        ur = jnp.dot(fc_ref[...], x, preferred_element_type=F32)
        ui = jnp.dot(fs_ref[...], x, preferred_element_type=F32)
        if not with_filter:
            zr_ref, zi_ref = rest
            zr_ref[bb] = ur
            zi_ref[bb] = ui
            continue
        hr_ref, hi_ref, zr_ref, zi_ref = rest
        hr, hi = hr_ref[0], hi_ref[0]
        row0 = (lax.broadcasted_iota(jnp.int32, ur.shape, 0) + pl.program_id(0) * tm) == 0
        zr_ref[bb] = (ur * hr - jnp.where(row0, 0.0, ui * hi)).astype(zr_ref.dtype)
        zi_ref[bb] = jnp.where(row0, ui * hi, ur * hi + ui * hr).astype(zi_ref.dtype)


def _dft_fwd(fwd, x, x_col0, c, spec_h=None, h_col0=0, tm=512, tn=512):
    b, l, _ = x.shape
    tm = min(tm, l)
    bt = _seqs_per_step(b, l)
    xo, ho = x_col0 // tn, h_col0 // tn
    out_dtype = F32 if spec_h is None else BF16
    fspec = pl.BlockSpec((tm, l), lambda i, bb, j: (i, 0))
    in_specs = [fspec, fspec, pl.BlockSpec((bt, l, tn), lambda i, bb, j: (bb, 0, xo + j))]
    args = [*fwd, x]
    if spec_h is not None:
        hspec = pl.BlockSpec((1, tm, tn), lambda i, bb, j: (0, i, ho + j))
        in_specs += [hspec, hspec]
        args += list(spec_h)
    ospec = pl.BlockSpec((bt, tm, tn), lambda i, bb, j: (bb, i, j))
    return pl.pallas_call(
        functools.partial(_dft_fwd_kernel, with_filter=spec_h is not None, tm=tm),
        grid=(l // tm, b // bt, c // tn),
        in_specs=in_specs,
        out_specs=[ospec, ospec],
        out_shape=[jax.ShapeDtypeStruct((b, l, c), out_dtype)] * 2,
        compiler_params=_cparams("parallel", "parallel", "parallel"),
        name="dft_fwd",
    )(*args)


def _dft_inv_kernel(ic_ref, is_ref, zr_ref, zi_ref, u_ref, m_ref, skip_ref, *rest, with_gate):
    for bb in range(zr_ref.shape[0]):
        y = (jnp.dot(ic_ref[...], zr_ref[bb], preferred_element_type=F32)
             + jnp.dot(is_ref[...], zi_ref[bb], preferred_element_type=F32))
        z = m_ref[bb] * (y + u_ref[bb] * skip_ref[...])
        if with_gate:
            g_ref, o_ref = rest
            gate = jnp.concatenate([g_ref[bb, c] for c in range(g_ref.shape[1])], axis=-1)
            o_ref[bb] = (z * _silu(gate)).astype(o_ref.dtype)
        else:
            o_ref, ob_ref = rest
            o_ref[bb] = z
            ob_ref[bb] = z.astype(BF16)


def _dft_inv(inv, zr, zi, u, u_col0, mul, mul_col0, skip, gate=None, gate_col0=0, tm=512, tn=512):
    b, l, c = zr.shape
    tm = min(tm, l)
    bt = _seqs_per_step(b, l)
    win = lambda col0: pl.BlockSpec((bt, tm, tn), lambda i, bb, j, o=col0 // tn: (bb, i, o + j))
    zspec = pl.BlockSpec((bt, l, tn), lambda i, bb, j: (bb, 0, j))
    fspec = pl.BlockSpec((tm, l), lambda i, bb, j: (i, 0))
    in_specs = [fspec, fspec, zspec, zspec,
                win(u_col0), win(mul_col0), pl.BlockSpec((1, tn), lambda i, bb, j: (0, j))]
    args = [*inv, zr, zi, u, mul, skip]
    ospec = pl.BlockSpec((bt, tm, tn), lambda i, bb, j: (bb, i, j))
    if gate is not None:
        in_specs.append(pl.BlockSpec((bt, tn // D_HEAD, tm, D_HEAD),
                                     lambda i, bb, j, o=gate_col0 // tn: (bb, o + j, i, 0)))
        args.append(gate)
        out_specs, out_shape = ospec, jax.ShapeDtypeStruct((b, l, c), BF16)
    else:
        out_specs = [ospec, ospec]
        out_shape = [jax.ShapeDtypeStruct((b, l, c), F32), jax.ShapeDtypeStruct((b, l, c), BF16)]
    return pl.pallas_call(
        functools.partial(_dft_inv_kernel, with_gate=gate is not None),
        grid=(l // tm, b // bt, c // tn),
        in_specs=in_specs,
        out_specs=out_specs,
        out_shape=out_shape,
        compiler_params=_cparams("parallel", "parallel", "parallel"),
        name="dft_inv",
    )(*args)


def _hyena(proj3, p, dft):
    l = proj3.shape[2]
    fwd, inv = dft
    filt = _hy_filter(l, p['hy_w1'], p['hy_b1'], p['hy_w2'], p['hy_b2'], p['hy_w3'], p['hy_b3'], p['hy_decay'])
    filt_b = filt[None]
    spec_h = _dft_fwd(fwd, filt_b, 0, 2 * BR_W)
    pre, pre_b = _hy_pre(proj3, p['hy_conv'])
    skip = p['hy_skip'].astype(F32)
    zr, zi = _dft_fwd(fwd, pre_b, 0, BR_W, spec_h, 0)
    z1, z1_b = _dft_inv(inv, zr, zi, pre, 0, pre, BR_W, skip[0:1])
    zr, zi = _dft_fwd(fwd, z1_b, 0, BR_W, spec_h, BR_W)
    return _dft_inv(inv, zr, zi, z1, 0, pre, 2 * BR_W, skip[1:2], gate=proj3, gate_col0=7 * BR_W)


def _softplus(x):
    return jnp.maximum(x, 0.0) + jnp.log1p(jnp.exp(-jnp.abs(x)))


def _split_bf16(x, parts):
    out = []
    for _ in range(parts - 1):
        piece = x.astype(BF16)
        out.append(piece)
        x = x - piece.astype(F32)
    out.append(x.astype(BF16))
    return out


def _bmm(a, b, hi=False):
    mm = lambda x, y: jnp.einsum('nij,njk->nik', x, y, preferred_element_type=F32)
    if not hi:
        return mm(a.astype(BF16), b.astype(BF16))
    (a1, a2), (b1, b2) = _split_bf16(a, 2), _split_bf16(b, 2)
    return mm(a1, b1) + (mm(a1, b2) + mm(a2, b1))


def _bmm_nt(a, b):
    return jnp.einsum('nid,njd->nij', a.astype(BF16), b.astype(BF16), preferred_element_type=F32)


TRI_BASE = 4


def _unit_tri_inverse(a, ri, ci):
    same = lambda w: (ri // w) == (ci // w)
    eye = (ri == ci).astype(F32)
    x = -jnp.where(same(TRI_BASE), a, 0.0)
    p = eye + x
    for _ in range(TRI_BASE.bit_length() - 2):
        x = _bmm(x, x, hi=True)
        p = p + _bmm(p, x, hi=True)
    w = TRI_BASE
    while w < a.shape[-1]:
        off = jnp.where(same(2 * w) & ~same(w), a, 0.0)
        p = p - _bmm(p, _bmm(off, p))
        w *= 2
    return p


def _gdn_prepare(q, k, v, ab, a_row, dt_row, head0, group):
    n, c, _ = q.shape
    two = lambda x: jnp.concatenate([x, x], axis=0)
    q, k, v, ab = two(q), two(k), two(v), two(ab)
    back3 = lambda shape: lax.broadcasted_iota(jnp.int32, shape, 0) >= n
    lane = lax.broadcasted_iota(jnp.int32, ab.shape, 2)
    bidx = lax.broadcasted_iota(jnp.int32, ab.shape, 0)
    head = head0 + jnp.where(bidx >= n, bidx - n, bidx) // group
    base = jnp.where(bidx >= n, 2 * N_HEAD, 0) + head
    g_all = -a_row * _softplus(ab + dt_row)
    g = jnp.sum(jnp.where(lane == base, g_all, 0.0), axis=2, keepdims=True)
    beta = jnp.sum(jnp.where(lane == base + N_HEAD, _sigmoid(ab), 0.0), axis=2, keepdims=True)

    sq = (2 * n, c, c)
    ri = lax.broadcasted_iota(jnp.int32, sq, 1)
    ci = lax.broadcasted_iota(jnp.int32, sq, 2)
    ahead = jnp.where(back3(sq), ci - ri, ri - ci)
    incl = ahead >= 0
    strict = ahead > 0
    tri = jnp.where(incl, 1.0, 0.0).astype(BF16)
    gc = sum(jnp.einsum('nij,njk->nik', tri, piece, preferred_element_type=F32)
             for piece in _split_bf16(jnp.broadcast_to(g, q.shape), 3))
    gc_row = jnp.swapaxes(gc, 1, 2)[:, :c, :]
    total = jnp.where(back3((2 * n, 1, D_HEAD)), gc[:, 0:1, :], gc[:, c - 1:c, :])
    decay = jnp.where(incl, jnp.exp(jnp.where(incl, gc[:, :, :c] - gc_row, 0.0)), 0.0)

    kb = k * beta
    a = jnp.where(strict, _bmm_nt(kb, k) * decay, 0.0)
    t = _unit_tri_inverse(a, ri, ci)
    e = jnp.exp(gc)
    u = _bmm(t, v * beta)
    w = _bmm(t, kb * e)
    a_intra = jnp.where(incl, _bmm_nt(q, k) * decay, 0.0)
    return (u, w.astype(BF16), (q * e).astype(BF16), (k * jnp.exp(total - gc)).astype(BF16),
            a_intra.astype(BF16), jnp.exp(total))


def _gdn_kernel(*refs, aliased, has_s0, group):
    if aliased:
        refs = refs[1:]
    if has_s0:
        (q_ref, k_ref, v_ref, z_ref, ab_ref, wq_ref, wk_ref, wv_ref, arow_ref, dt_ref, gn_ref, s0_ref,
         y_ref, sf_ref, qn, kn, vn, u_s, w_s, qd_s, kd_s, ai_s, gl_s) = refs
    else:
        (q_ref, k_ref, v_ref, z_ref, ab_ref, wq_ref, wk_ref, wv_ref, arow_ref, dt_ref, gn_ref,
         y_ref, sf_ref, qn, kn, vn, u_s, w_s, qd_s, kd_s, ai_s, gl_s) = refs
    _, heads, l, _ = q_ref.shape
    head0 = pl.program_id(1) * heads
    n_chunks = l // CHUNK
    hcols = lambda hh: slice(hh * D_HEAD, (hh + 1) * D_HEAD)

    def l2n(x):
        return x * lax.rsqrt(jnp.sum(x * x, axis=-1, keepdims=True) + EPS)

    for hh in range(heads):
        cols = hcols(hh)
        qn[:, cols] = l2n(_silu(_dwconv3(q_ref[0, hh], wq_ref.at[:, cols]))) * (D_HEAD ** -0.5)
        kn[:, cols] = l2n(_silu(_dwconv3(k_ref[0, hh], wk_ref.at[:, cols])))
        vn[:, cols] = _silu(_dwconv3(v_ref[0, hh], wv_ref.at[:, cols]))

    a_row, dt_row = arow_ref[...], dt_ref[...]

    def prepare(gi, carry):
        span = group * CHUNK
        rows = pl.ds(pl.multiple_of(gi * span, span), span)
        chunks = lambda x: x.reshape(group, CHUNK, x.shape[-1])
        per_head = lambda ref: jnp.concatenate([chunks(ref[rows, hcols(hh)]) for hh in range(heads)], axis=0)
        ab = chunks(ab_ref[0, rows, :])
        u, w, qd, kd, ai, gl = _gdn_prepare(per_head(qn), per_head(kn), per_head(vn),
                                            jnp.concatenate([ab] * heads, axis=0), a_row, dt_row, head0, group)
        for d in range(2):
            for hh in range(heads):
                cols = hcols(hh)
                part = slice((d * heads + hh) * group, (d * heads + hh + 1) * group)
                u_s[d, rows, cols] = u[part].reshape(span, D_HEAD)
                w_s[d, rows, cols] = w[part].reshape(span, D_HEAD)
                qd_s[d, rows, cols] = qd[part].reshape(span, D_HEAD)
                kd_s[d, rows, cols] = kd[part].reshape(span, D_HEAD)
                ai_s[d, hh, rows, :] = ai[part].reshape(span, CHUNK)
                gl_s[d, hh, pl.ds(gi * group, group)] = jnp.broadcast_to(gl[part], (group,) + gl_s.shape[3:])
        return carry

    lax.fori_loop(0, n_chunks // group, prepare, 0)

    def scan(i, s):
        where = [(hh, d, pl.ds(pl.multiple_of(chunk * CHUNK, CHUNK), CHUNK), chunk)
                 for hh in range(heads) for d, chunk in ((0, i), (1, n_chunks - 1 - i))]
        gather = lambda ref: jnp.stack([ref[d, rows, hcols(hh)] for hh, d, rows, _ in where])
        a_intra = jnp.stack([ai_s[d, hh, rows, :] for hh, d, rows, _ in where])
        decay = jnp.stack([gl_s[d, hh, chunk][0:1, :] for hh, d, _, chunk in where])
        sb = s.astype(BF16)
        v_new = gather(u_s) - _bmm(gather(w_s), sb)
        vb = v_new.astype(BF16)
        o = _bmm(gather(qd_s), sb) + _bmm(a_intra, vb)
        for idx, (hh, d, rows, _) in enumerate(where):
            u_s[d, rows, hcols(hh)] = o[idx]
        return s * decay + jnp.einsum('nik,niv->nkv', gather(kd_s), vb, preferred_element_type=F32)

    if has_s0:
        init = jnp.stack([s0_ref[0, 0, d, hh] for hh in range(heads) for d in range(2)])
    else:
        init = jnp.zeros((2 * heads, D_HEAD, D_HEAD), F32)
    final = lax.fori_loop(0, n_chunks, scan, init)
    for hh in range(heads):
        cols = hcols(hh)
        sf_ref[0, 0, 0, hh] = final[2 * hh]
        sf_ref[0, 0, 1, hh] = final[2 * hh + 1]
        y_ref[0, :, cols] = (_rms(u_s[0, :, cols] + u_s[1, :, cols], gn_ref[...])
                             * _silu(z_ref[0, hh])).astype(y_ref.dtype)


def _gdn(proj3, ab3, conv_w, a_log, dt_bias, norm_g, layer, state=None, new_state=None):
    b, _, l, _ = proj3.shape
    depth_out, layer_out = (1, 0) if state is not None else (DEPTH, layer)
    aliased = new_state is not None
    lanes = jnp.zeros((2, 2 * N_HEAD), F32).at[:, :N_HEAD].set(1.0)
    a_row = jnp.pad((jnp.exp(a_log.astype(F32))[:, None, :] * lanes.reshape(2, 2, N_HEAD)).reshape(1, -1),
                    ((0, 0), (0, AB_PAD - 4 * N_HEAD)))
    dt_row = jnp.pad((dt_bias.astype(F32)[:, None, :] * lanes.reshape(2, 2, N_HEAD)).reshape(1, -1),
                     ((0, 0), (0, AB_PAD - 4 * N_HEAD)))
    hps = N_HEAD if l <= 512 else 1
    wid = hps * D_HEAD
    n_hb = N_HEAD // hps
    blk = lambda c: pl.BlockSpec((1, hps, l, D_HEAD), lambda i, h, c=c: (i, c * n_hb + h, 0, 0))
    wblk = lambda c: pl.BlockSpec((3, wid), lambda i, h, c=c: (0, c * n_hb + h))
    row = pl.BlockSpec((1, D_HEAD), lambda i, h: (0, 0))
    in_specs = [blk(0), blk(1), blk(2), blk(3),
                pl.BlockSpec((1, l, AB_PAD), lambda i, h: (i, 0, 0)),
                wblk(0), wblk(1), wblk(2), row, row, row]
    args = [proj3, proj3, proj3, proj3, ab3, conv_w, conv_w, conv_w, a_row, dt_row, norm_g]
    if aliased:
        in_specs.insert(0, pl.BlockSpec(memory_space=pl.ANY))
        args.insert(0, new_state)
    if state is not None:
        in_specs.append(pl.BlockSpec((1, 1, 2, hps, D_HEAD, D_HEAD), lambda i, h: (i, layer, 0, h, 0, 0)))
        args.append(state)
    return pl.pallas_call(
        functools.partial(_gdn_kernel, aliased=aliased, has_s0=state is not None, group=min(8, l // CHUNK)),
        grid=(b, n_hb),
        in_specs=in_specs,
        out_specs=[pl.BlockSpec((1, l, wid), lambda i, h: (i, 0, h)),
                   pl.BlockSpec((1, 1, 2, hps, D_HEAD, D_HEAD), lambda i, h: (i, layer_out, 0, h, 0, 0))],
        out_shape=[jax.ShapeDtypeStruct((b, l, BR_W), BF16),
                   jax.ShapeDtypeStruct((b, depth_out, 2, N_HEAD, D_HEAD, D_HEAD), F32)],
        input_output_aliases={0: 1} if aliased else {},
        scratch_shapes=[pltpu.VMEM((l, wid), F32)] * 3
        + [pltpu.VMEM((2, l, wid), F32)] + [pltpu.VMEM((2, l, wid), BF16)] * 3
        + [pltpu.VMEM((2, hps, l, CHUNK), BF16), pltpu.VMEM((2, hps, l // CHUNK, 8, D_HEAD), F32)],
        compiler_params=_cparams("parallel", "parallel"),
        name="gdn",
    )(*args)


def _mod_kernel(c_ref, w_ref, b_ref, o_ref):
    o_ref[...] = _dot_hi(_silu(c_ref[...]), w_ref[...]) + b_ref[...]


def _modulation(cond, w_mod, b_mod, layer, tn=512):
    n = cond.shape[0]
    rows = 8
    out = pl.pallas_call(
        _mod_kernel,
        grid=(3 * D_MODEL // tn,),
        in_specs=[pl.BlockSpec((rows, D_MODEL), lambda j: (0, 0)),
                  pl.BlockSpec((None, D_MODEL, tn), lambda j: (layer, 0, j)),
                  pl.BlockSpec((1, tn), lambda j: (0, j))],
        out_specs=pl.BlockSpec((rows, tn), lambda j: (0, j)),
        out_shape=jax.ShapeDtypeStruct((rows, 3 * D_MODEL), F32),
        compiler_params=_cparams("parallel"),
        name="modulation",
    )(jnp.pad(cond.astype(F32), ((0, rows - n), (0, 0))), w_mod, b_mod.reshape(1, -1))
    return out[:n].reshape(n, 3, D_MODEL)


def _split_w_in(w_in):
    n_a = 4 * BR_W + 4 * N_HEAD
    zeros = jnp.zeros(w_in.shape[:-1] + (AB_PAD - 4 * N_HEAD,), w_in.dtype)
    return jnp.concatenate([w_in[..., :4 * BR_W], w_in[..., n_a:], w_in[..., 4 * BR_W:n_a], zeros],
                           axis=-1).astype(BF16)


def _trunk_layer(x3, cond, p, big, layer, dft, latent, new_outputs=(None, None, None)):
    b, l, _ = x3.shape
    x2 = x3.reshape(b * l, D_MODEL)
    mod = _modulation(cond, big['w_mod'], p['b_mod'], layer)
    rows_per_mod = l if mod.shape[0] == b else b * l
    g_pre = p['g_pre'].reshape(1, D_MODEL)
    proj3, ab = _inproj(x2, mod, g_pre, big['w_in'], layer, rows_per_mod, l)
    ab3 = ab.reshape(b, l, AB_PAD)

    lam_init = 0.8 - 0.6 * math.exp(-0.3 * layer)
    lam_p = p['diff_lam'].astype(F32)
    lam = (jnp.exp(jnp.sum(lam_p[0] * lam_p[1])) - jnp.exp(jnp.sum(lam_p[2] * lam_p[3])) + lam_init).reshape(1, 1)
    diff_norm = p['diff_norm'].reshape(1, D_HEAD)
    gdn_args = (proj3, ab3, p['gdn_conv'], p['gdn_a_log'], p['gdn_dt_bias'], p['gdn_norm'].reshape(1, D_HEAD), layer)

    yb = _hyena(proj3, p, dft)
    if latent is None:
        new_state, nat_cache, diff_cache = new_outputs
        ya, new_state = _gdn(*gdn_args, new_state=new_state)
        yc, yd, nat_cache, diff_cache = _ctx_attention(proj3, lam, diff_norm, lam_init, layer, nat_cache, diff_cache)
        extras = (new_state, nat_cache, diff_cache)
    else:
        ya, _ = _gdn(*gdn_args, state=latent['state_gdn'])
        yc = _lat_nat(proj3, latent['cache_nat_kv'], layer, _nat_bias_table(p['nat_rpb']))
        yd = _lat_diff(proj3, latent['cache_diff_kv'], layer, lam, diff_norm, lam_init, latent['rope'])
        extras = None

    ys = [t.reshape(b * l, BR_W) for t in (ya, yb, yc, yd)]
    out = _merge(x2, mod, g_pre, p['g_post'].reshape(1, D_MODEL), ys, big['w_branch'], big['w_merge'],
                 p['b_merge'].reshape(1, -1).astype(F32), big['w_out'], layer, rows_per_mod)
    return out.reshape(b, l, D_MODEL), extras


def kernel(x_prompt, x_sample, state_gdn, cache_nat_kv, cache_diff_kv, c, c_ctx,
           w_mod, b_mod, g_pre, g_post, w_in, gdn_conv, gdn_a_log, gdn_dt_bias, gdn_norm,
           hy_conv, hy_w1, hy_b1, hy_w2, hy_b2, hy_w3, hy_b3, hy_decay, hy_skip,
           nat_rpb, diff_lam, diff_norm, w_branch, w_merge, b_merge, w_out):
    small = {
        'b_mod': b_mod, 'g_pre': g_pre, 'g_post': g_post,
        'gdn_conv': gdn_conv, 'gdn_a_log': gdn_a_log, 'gdn_dt_bias': gdn_dt_bias, 'gdn_norm': gdn_norm,
        'hy_conv': hy_conv, 'hy_w1': hy_w1, 'hy_b1': hy_b1, 'hy_w2': hy_w2, 'hy_b2': hy_b2,
        'hy_w3': hy_w3, 'hy_b3': hy_b3, 'hy_decay': hy_decay, 'hy_skip': hy_skip,
        'nat_rpb': nat_rpb, 'diff_lam': diff_lam, 'diff_norm': diff_norm, 'b_merge': b_merge,
    }
    layers = [{name: arr[i] for name, arr in small.items()} for i in range(DEPTH)]
    big = {'w_mod': w_mod.astype(F32), 'w_in': _split_w_in(w_in), 'w_branch': w_branch.astype(BF16),
           'w_merge': w_merge.astype(BF16), 'w_out': w_out.astype(BF16)}

    y_prompt = x_prompt
    dft_ctx = _dft_matrices(x_prompt.shape[1])
    outputs = (None, None, None)
    for i, p in enumerate(layers):
        y_prompt, outputs = _trunk_layer(y_prompt, c_ctx.reshape(1, D_MODEL), p, big, i, dft_ctx, None, outputs)
    new_state, nat_cache, diff_cache = outputs

    y_sample = x_sample
    dft_lat = _dft_matrices(x_sample.shape[1])
    latent = {'state_gdn': state_gdn, 'cache_nat_kv': cache_nat_kv, 'cache_diff_kv': cache_diff_kv,
              'rope': _rope_tables(x_sample.shape[1])}
    for i, p in enumerate(layers):
        y_sample, _ = _trunk_layer(y_sample, c, p, big, i, dft_lat, latent)

    return (y_prompt, y_sample, new_state, nat_cache, diff_cache)
```

```python
import functools
import math

import jax
import jax.numpy as jnp
import numpy as np
from jax import lax
from jax.experimental import pallas as pl
from jax.experimental.pallas import tpu as pltpu

F32 = jnp.float32
BF16 = jnp.bfloat16

D_MODEL = 1024
DEPTH = 2
GRID_W = 64
N_BRANCH = 4
BR_W = 512
N_HEAD = 4
D_HEAD = 128
SUBLANES = 8
CHUNK = 64
HY_BANDS = 16
WIN_R = 8
WIN_C = 16
DQK_D = 64
ROPE_BASE = 10000.0
EPS = 1e-6
N_MAIN = 4 * 4 * BR_W
AB_PAD = 128
NEG_INF = -1e30

VMEM_LIMIT = 48 * 1024 * 1024


def _cparams(*sem):
    return pltpu.CompilerParams(dimension_semantics=sem, vmem_limit_bytes=VMEM_LIMIT)


def _silu(x):
    return x * (1.0 / (1.0 + jnp.exp(-x)))


def _sigmoid(x):
    return 1.0 / (1.0 + jnp.exp(-x))


def _rms(x, g):
    return x * lax.rsqrt(jnp.mean(x * x, axis=-1, keepdims=True) + EPS) * g


def _dot(a, b):
    return jnp.dot(a.astype(BF16), b.astype(BF16), preferred_element_type=F32)


def _dot_nt(a, b):
    return lax.dot_general(a.astype(BF16), b.astype(BF16), (((1,), (1,)), ((), ())),
                           preferred_element_type=F32)


def _dot_tn(a, b):
    return lax.dot_general(a.astype(BF16), b.astype(BF16), (((0,), (0,)), ((), ())),
                           preferred_element_type=F32)


def _prenorm(x, g_pre, mod_ref):
    return _rms(x, g_pre) * (1.0 + mod_ref[0, 1:2, :]) + mod_ref[0, 0:1, :]


def _inproj_kernel(x_ref, mod_ref, gpre_ref, w_ref, wab_ref, proj_ref, ab_ref, h_scr):
    @pl.when(pl.program_id(1) == 0)
    def _():
        h = _prenorm(x_ref[...], gpre_ref[...], mod_ref).astype(BF16)
        h_scr[...] = h
        ab_ref[...] = jnp.dot(h, wab_ref[...], preferred_element_type=F32)

    acc = jnp.dot(h_scr[...], w_ref[...], preferred_element_type=F32)
    seqs, cblocks, rows, _ = proj_ref.shape
    for sq in range(seqs):
        for c in range(cblocks):
            proj_ref[sq, c] = acc[sq * rows:(sq + 1) * rows, c * D_HEAD:(c + 1) * D_HEAD]


def _inproj(x2, mod, g_pre, w_main, w_ab, layer, rows_per_mod, l, tm=1024, tn=1024):
    m = x2.shape[0]
    tm = math.gcd(tm, rows_per_mod)
    if tm >= l:
        proj_spec = pl.BlockSpec((tm // l, tn // D_HEAD, l, D_HEAD), lambda i, j: (i, j, 0, 0))
    else:
        per = l // tm
        proj_spec = pl.BlockSpec((1, tn // D_HEAD, tm, D_HEAD), lambda i, j: (i // per, j, i % per, 0))
    return pl.pallas_call(
        _inproj_kernel,
        grid=(m // tm, N_MAIN // tn),
        in_specs=[
            pl.BlockSpec((tm, D_MODEL), lambda i, j: (i, 0)),
            pl.BlockSpec((1, 3, D_MODEL), lambda i, j: ((i * tm) // rows_per_mod, 0, 0)),
            pl.BlockSpec((1, D_MODEL), lambda i, j: (0, 0)),
            pl.BlockSpec((None, D_MODEL, tn), lambda i, j: (layer, 0, j)),
            pl.BlockSpec((None, D_MODEL, AB_PAD), lambda i, j: (layer, 0, 0)),
        ],
        out_specs=[
            proj_spec,
            pl.BlockSpec((tm, AB_PAD), lambda i, j: (i, 0)),
        ],
        out_shape=[jax.ShapeDtypeStruct((m // l, N_MAIN // D_HEAD, l, D_HEAD), F32),
                   jax.ShapeDtypeStruct((m, AB_PAD), F32)],
        scratch_shapes=[pltpu.VMEM((tm, D_MODEL), BF16)],
        compiler_params=_cparams("parallel", "arbitrary"),
        name="inproj",
    )(x2, mod, g_pre, w_main, w_ab)


def _merge_kernel(x_ref, mod_ref, gpre_ref, gpost_ref, ya_ref, yb_ref, yc_ref, yd_ref,
                  wbr_ref, wmg_ref, bmg_ref, wout_ref, o_ref):
    x = x_ref[...]
    h = _prenorm(x, gpre_ref[...], mod_ref).astype(BF16)
    acc = None
    for k, y_ref in enumerate((ya_ref, yb_ref, yc_ref, yd_ref)):
        cols = slice(k * D_MODEL, (k + 1) * D_MODEL)
        gate = _sigmoid(jnp.dot(h, wmg_ref[:, cols], preferred_element_type=F32) + bmg_ref[:, cols])
        br = jnp.dot(y_ref[...], wbr_ref[k], preferred_element_type=F32)
        acc = gate * br if acc is None else acc + gate * br
    y = jnp.dot(acc.astype(BF16), wout_ref[...], preferred_element_type=F32)
    o_ref[...] = x + mod_ref[0, 2:3, :] * _rms(y, gpost_ref[...])


def _merge(x2, mod, g_pre, g_post, ys, w_branch, w_merge, b_merge, w_out, layer, rows_per_mod, tm=256):
    m = x2.shape[0]
    row = lambda i: (i, 0)
    fixed2 = lambda i: (0, 0)
    return pl.pallas_call(
        _merge_kernel,
        grid=(m // tm,),
        in_specs=[
            pl.BlockSpec((tm, D_MODEL), row),
            pl.BlockSpec((1, 3, D_MODEL), lambda i: ((i * tm) // rows_per_mod, 0, 0)),
            pl.BlockSpec((1, D_MODEL), fixed2),
            pl.BlockSpec((1, D_MODEL), fixed2),
            pl.BlockSpec((tm, BR_W), row),
            pl.BlockSpec((tm, BR_W), row),
            pl.BlockSpec((tm, BR_W), row),
            pl.BlockSpec((tm, BR_W), row),
            pl.BlockSpec((None, N_BRANCH, BR_W, D_MODEL), lambda i: (layer, 0, 0, 0)),
            pl.BlockSpec((None, D_MODEL, N_BRANCH * D_MODEL), lambda i: (layer, 0, 0)),
            pl.BlockSpec((1, N_BRANCH * D_MODEL), fixed2),
            pl.BlockSpec((None, D_MODEL, D_MODEL), lambda i: (layer, 0, 0)),
        ],
        out_specs=pl.BlockSpec((tm, D_MODEL), row),
        out_shape=jax.ShapeDtypeStruct((m, D_MODEL), F32),
        compiler_params=_cparams("parallel"),
        name="merge",
    )(x2, mod, g_pre, g_post, *ys, w_branch, w_merge, b_merge, w_out)


def _softmax_rows(s):
    p = jnp.exp(s - jnp.max(s, axis=-1, keepdims=True))
    return p, jnp.sum(p, axis=-1, keepdims=True)


def _head_cols(h):
    return slice(h * D_HEAD, (h + 1) * D_HEAD)


def _stack_heads(ref):
    return ref[0]


def _ctx_nat_kernel(*refs, aliased):
    q_ref, k_ref, v_ref, g_ref, y_ref, kv_ref = refs[1:] if aliased else refs
    scale = D_HEAD ** -0.5
    q, k, v = (_stack_heads(r) for r in (q_ref, k_ref, v_ref))
    p, l = _softmax_rows(_bmm_nt(q, k) * scale)
    o = _bmm(p, v) / l
    for h in range(N_HEAD):
        sl = _head_cols(h)
        y_ref[0, :, sl] = (o[h] * _silu(g_ref[0, h])).astype(y_ref.dtype)
        kv_ref[0, 0, 0, h] = k[h]
        kv_ref[0, 0, 1, h] = v[h]


def _map_masks():
    lane = lax.broadcasted_iota(jnp.int32, (1, D_HEAD), 1)
    first = (lane < DQK_D).astype(F32)
    return first, 1.0 - first


def _ctx_diff_kernel(*refs, aliased, out_scale):
    lam_ref, q_ref, k_ref, v_ref, g_ref, gn_ref, y_ref, kv_ref = refs[1:] if aliased else refs
    scale = DQK_D ** -0.5
    m1, m2 = _map_masks()
    q, k, v = (_stack_heads(r) for r in (q_ref, k_ref, v_ref))
    p, l = _softmax_rows(_bmm_nt(jnp.concatenate([q * m1, q * m2], axis=0), jnp.concatenate([k, k], axis=0)) * scale)
    pn = p / l
    a = pn[:N_HEAD] - lam_ref[...] * pn[N_HEAD:]
    o = _rms(_bmm(a, v), gn_ref[...]) * out_scale
    for h in range(N_HEAD):
        sl = _head_cols(h)
        y_ref[0, :, sl] = (o[h] * _silu(g_ref[0, h])).astype(y_ref.dtype)
        kv_ref[0, 0, 0, h] = k[h]
        kv_ref[0, 0, 1, h] = v[h]


def _ctx_attention(proj3, lam, diff_norm, lam_init, layer, nat_cache, diff_cache):
    b, _, l, _ = proj3.shape
    blk = lambda c: pl.BlockSpec((1, N_HEAD, l, D_HEAD), lambda i, c=c: (i, c, 0, 0))
    y_spec = pl.BlockSpec((1, l, BR_W), lambda i: (i, 0, 0))
    kv_spec = pl.BlockSpec((1, 1, 2, N_HEAD, l, D_HEAD), lambda i: (i, layer, 0, 0, 0, 0))
    out_shape = [jax.ShapeDtypeStruct((b, l, BR_W), BF16),
                 jax.ShapeDtypeStruct((b, DEPTH, 2, N_HEAD, l, D_HEAD), F32)]
    aliased = nat_cache is not None
    cache_specs = [pl.BlockSpec(memory_space=pl.ANY)] if aliased else []
    aliases = {0: 1} if aliased else {}
    yc, nat_cache = pl.pallas_call(
        functools.partial(_ctx_nat_kernel, aliased=aliased),
        grid=(b,),
        in_specs=cache_specs + [blk(8), blk(9), blk(10), blk(11)],
        out_specs=[y_spec, kv_spec],
        out_shape=out_shape,
        input_output_aliases=aliases,
        compiler_params=_cparams("parallel"),
        name="ctx_nat",
    )(*([nat_cache] if aliased else []), proj3, proj3, proj3, proj3)
    yd, diff_cache = pl.pallas_call(
        functools.partial(_ctx_diff_kernel, aliased=aliased, out_scale=1.0 - lam_init),
        grid=(b,),
        in_specs=cache_specs + [pl.BlockSpec((1, 1), lambda i: (0, 0)),
                                blk(12), blk(13), blk(14), blk(15),
                                pl.BlockSpec((1, D_HEAD), lambda i: (0, 0))],
        out_specs=[y_spec, kv_spec],
        out_shape=out_shape,
        input_output_aliases=aliases,
        compiler_params=_cparams("parallel"),
        name="ctx_diff",
    )(*([diff_cache] if aliased else []), lam, proj3, proj3, proj3, proj3, diff_norm)
    return yc, yd, nat_cache, diff_cache


def _nat_bias_table(rpb):
    cols = np.arange(GRID_W)
    start = np.clip(cols - WIN_C // 2, 0, GRID_W - WIN_C)
    inside = (cols[None, :] >= start[:, None]) & (cols[None, :] < start[:, None] + WIN_C)
    dc = cols[None, :] - cols[:, None] + (WIN_C - 1)
    onehot = ((dc[None] == np.arange(2 * WIN_C - 1)[:, None, None]) & inside[None]).astype(np.float32)
    t = jnp.einsum('hdx,xck->hdck', rpb.astype(F32), jnp.asarray(onehot), precision=lax.Precision.HIGHEST)
    t = jnp.where(jnp.asarray(inside)[None, None], t, NEG_INF)
    tab = jnp.stack([t[:, WIN_R - 1 - off:2 * WIN_R - 1 - off] for off in range(WIN_R)], axis=1)
    return tab.transpose(0, 1, 3, 2, 4).reshape(rpb.shape[0], WIN_R, GRID_W, WIN_R * GRID_W)


def _lat_nat_kernel(q_ref, k_ref, v_ref, g_ref, ckv_ref, bias_ref, y_ref, kb_scr, vb_scr, *, rb):
    scale = D_HEAD ** -0.5
    rows = q_ref.shape[1] // GRID_W
    win = WIN_R * GRID_W
    kb_scr[...] = k_ref[0].astype(BF16)
    vb_scr[...] = v_ref[0].astype(BF16)
    ck = ckv_ref[0, 0, 0, 0].astype(BF16)
    cv = ckv_ref[0, 0, 1, 0].astype(BF16)

    def row_block(i, carry):
        q0 = pl.multiple_of(i * (rb * GRID_W), rb * GRID_W)
        qrows = pl.ds(q0, rb * GRID_W)
        q = q_ref[0, qrows, :].astype(BF16)
        kw, vw, bias = [], [], []
        for j in range(rb):
            r = i * rb + j
            rs = jnp.clip(r - WIN_R // 2, 0, rows - WIN_R)
            wrows = pl.ds(pl.multiple_of(rs * GRID_W, GRID_W), win)
            kw.append(kb_scr[wrows, :])
            vw.append(vb_scr[wrows, :])
            bias.append(bias_ref[0, r - rs])
        q3 = q.reshape(rb, GRID_W, D_HEAD)
        s_lat = _bmm_nt(q3, jnp.stack(kw)) * scale + jnp.stack(bias)
        s_ctx = (_dot_nt(q, ck) * scale).reshape(rb, GRID_W, ck.shape[0])
        m = jnp.maximum(jnp.max(s_lat, axis=-1, keepdims=True), jnp.max(s_ctx, axis=-1, keepdims=True))
        p_lat = jnp.exp(s_lat - m)
        p_ctx = jnp.exp(s_ctx - m)
        l = jnp.sum(p_lat, axis=-1, keepdims=True) + jnp.sum(p_ctx, axis=-1, keepdims=True)
        o_ctx = _dot(p_ctx.reshape(rb * GRID_W, ck.shape[0]), cv).reshape(rb, GRID_W, D_HEAD)
        o = ((_bmm(p_lat, jnp.stack(vw)) + o_ctx) / l).reshape(rb * GRID_W, D_HEAD)
        y_ref[0, qrows, :] = (o * _silu(g_ref[0, qrows, :])).astype(y_ref.dtype)
        return carry

    lax.fori_loop(0, rows // rb, row_block, 0)


def _lat_nat(proj3, cache_nat_kv, layer, bias_tab):
    b, _, l, _ = proj3.shape
    past = cache_nat_kv.shape[4]
    blk = lambda c: pl.BlockSpec((1, None, l, D_HEAD), lambda i, h, c=c: (i, c + h, 0, 0))
    return pl.pallas_call(
        functools.partial(_lat_nat_kernel, rb=8),
        grid=(b, N_HEAD),
        in_specs=[blk(32), blk(36), blk(40), blk(44),
                  pl.BlockSpec((1, 1, 2, 1, past, D_HEAD), lambda i, h: (i, layer, 0, h, 0, 0)),
                  pl.BlockSpec((1, WIN_R, GRID_W, WIN_R * GRID_W), lambda i, h: (h, 0, 0, 0))],
        out_specs=pl.BlockSpec((1, l, D_HEAD), lambda i, h: (i, 0, h)),
        out_shape=jax.ShapeDtypeStruct((b, l, BR_W), BF16),
        scratch_shapes=[pltpu.VMEM((l, D_HEAD), BF16), pltpu.VMEM((l, D_HEAD), BF16)],
        compiler_params=_cparams("parallel", "parallel"),
        name="lat_nat",
    )(proj3, proj3, proj3, proj3, cache_nat_kv, bias_tab)


def _rope_tables(l):
    half = DQK_D // 2
    nf = half // 2
    t = jnp.arange(l)
    row = (t // GRID_W).astype(F32)
    col = (t % GRID_W).astype(F32)
    inv = ROPE_BASE ** (-jnp.arange(nf, dtype=F32) / nf)
    ang = jnp.concatenate([row[:, None] * inv, col[:, None] * inv], axis=-1)
    cos, sin = jnp.cos(ang), jnp.sin(ang)
    zero = jnp.zeros_like(sin)
    tile2 = lambda a, b: jnp.concatenate([a, b, a, b], axis=-1)
    return tile2(cos, cos), tile2(-sin, zero), tile2(zero, sin)


def _rope(x, cos, sin_a, sin_b):
    return x * cos + pltpu.roll(x, 96, 1) * sin_a + pltpu.roll(x, 32, 1) * sin_b


def _lat_diff_kernel(lam_ref, q_ref, k_ref, v_ref, g_ref, ckv_ref, gn_ref,
                     cq_ref, saq_ref, sbq_ref, ck_ref, sak_ref, sbk_ref,
                     y_ref, ks_scr, vt_scr, *, out_scale, prep_rows, key_block, ahead):
    scale = DQK_D ** -0.5
    l = k_ref.shape[1]

    @pl.when(pl.program_id(2) == 0)
    def _():
        def prep(i, carry):
            rows = pl.ds(pl.multiple_of(i * prep_rows, prep_rows), prep_rows)
            kr = _rope(k_ref[0, rows, :], ck_ref[rows, :], sak_ref[rows, :], sbk_ref[rows, :])
            ks_scr[rows, :] = kr.astype(BF16)
            vt_scr[:, rows] = v_ref[0, rows, :].T.astype(BF16)
            return carry

        lax.fori_loop(0, l // prep_rows, prep, 0)
        ks_scr[l:, :] = ckv_ref[0, 0, 0, 0].astype(BF16)
        vt_scr[:, l:] = ckv_ref[0, 0, 1, 0].T.astype(BF16)

    q = _rope(q_ref[0], cq_ref[...], saq_ref[...], sbq_ref[...]) * (scale * math.log2(math.e))
    m1, m2 = _map_masks()
    tq = q.shape[0]
    qm = jnp.concatenate([q * m1, q * m2], axis=0).astype(BF16)
    m = l_sum = acc = None
    n_blk = ks_scr.shape[0] // key_block
    block = lambda blk: slice(blk * key_block, (blk + 1) * key_block)
    scores = [_dot_nt(ks_scr[block(b), :], qm) for b in range(min(ahead, n_blk))]
    for blk in range(n_blk):
        rows = block(blk)
        s = scores.pop(0)
        if blk + ahead < n_blk:
            scores.append(_dot_nt(ks_scr[block(blk + ahead), :], qm))
        m_blk = jnp.max(s, axis=0, keepdims=True)
        if blk == 0:
            m = m_blk
            p = jnp.exp2(s - m)
            l_sum = jnp.sum(p, axis=0, keepdims=True)
            acc = _dot(vt_scr[:, rows], p)
        else:
            m_new = jnp.maximum(m, m_blk)
            alpha = jnp.exp2(m - m_new)
            p = jnp.exp2(s - m_new)
            l_sum = alpha * l_sum + jnp.sum(p, axis=0, keepdims=True)
            acc = alpha * acc + _dot(vt_scr[:, rows], p)
            m = m_new
    out = acc / l_sum
    d = out[:, :tq] - lam_ref[...] * out[:, tq:]
    d = d * lax.rsqrt(jnp.mean(d * d, axis=0, keepdims=True) + EPS)
    o = d.T * gn_ref[...] * out_scale
    y_ref[0] = (o * _silu(g_ref[0])).astype(y_ref.dtype)


def _lat_diff(proj3, cache_diff_kv, layer, lam, diff_norm, lam_init, rope_tabs, tq=512):
    b, _, l, _ = proj3.shape
    past = cache_diff_kv.shape[4]
    qblk = lambda c: pl.BlockSpec((1, None, tq, D_HEAD), lambda i, h, j, c=c: (i, c + h, j, 0))
    full = lambda c: pl.BlockSpec((1, None, l, D_HEAD), lambda i, h, j, c=c: (i, c + h, 0, 0))
    tq_tab = pl.BlockSpec((tq, D_HEAD), lambda i, h, j: (j, 0))
    full_tab = pl.BlockSpec((l, D_HEAD), lambda i, h, j: (0, 0))
    return pl.pallas_call(
        functools.partial(_lat_diff_kernel, out_scale=1.0 - lam_init, prep_rows=512, key_block=512, ahead=2),
        grid=(b, N_HEAD, l // tq),
        in_specs=[pl.BlockSpec((1, 1), lambda i, h, j: (0, 0)),
                  qblk(48), full(52), full(56), qblk(60),
                  pl.BlockSpec((1, 1, 2, 1, past, D_HEAD), lambda i, h, j: (i, layer, 0, h, 0, 0)),
                  pl.BlockSpec((1, D_HEAD), lambda i, h, j: (0, 0)),
                  tq_tab, tq_tab, tq_tab, full_tab, full_tab, full_tab],
        out_specs=pl.BlockSpec((1, tq, D_HEAD), lambda i, h, j: (i, j, h)),
        out_shape=jax.ShapeDtypeStruct((b, l, BR_W), BF16),
        scratch_shapes=[pltpu.VMEM((l + past, D_HEAD), BF16), pltpu.VMEM((D_HEAD, l + past), BF16)],
        compiler_params=_cparams("parallel", "parallel", "arbitrary"),
        name="lat_diff",
    )(lam, proj3, proj3, proj3, proj3, cache_diff_kv, diff_norm, *rope_tabs, *rope_tabs)


def _dwconv3(x, w_ref):
    l = x.shape[0]
    row = lax.broadcasted_iota(jnp.int32, x.shape, 0)
    prev = jnp.where(row == 0, 0.0, pltpu.roll(x, 1, 0))
    nxt = jnp.where(row == l - 1, 0.0, pltpu.roll(x, l - 1, 0))
    return prev * w_ref[0:1, :] + x * w_ref[1:2, :] + nxt * w_ref[2:3, :]


def _hy_pre_kernel(x_ref, above_ref, below_ref, w_ref, o_ref, ob_ref):
    t, n_t = pl.program_id(1), pl.num_programs(1)
    bt, cblocks, rows, _ = x_ref.shape
    row = lax.broadcasted_iota(jnp.int32, (rows, D_HEAD), 0)
    for bb in range(bt):
        for c in range(cblocks):
            cols = _head_cols(c)
            x = x_ref[bb, c]
            before = jnp.where(t == 0, 0.0, above_ref[bb, c, SUBLANES - 1:SUBLANES, :])
            after = jnp.where(t == n_t - 1, 0.0, below_ref[bb, c, 0:1, :])
            prev = jnp.where(row == 0, before, pltpu.roll(x, 1, 0))
            nxt = jnp.where(row == rows - 1, after, pltpu.roll(x, rows - 1, 0))
            y = prev * w_ref[0:1, cols] + x * w_ref[1:2, cols] + nxt * w_ref[2:3, cols]
            o_ref[bb, :, cols] = y
            ob_ref[bb, :, cols] = y.astype(BF16)


def _hy_pre(proj3, conv_w, tl=1024):
    b, _, l, _ = proj3.shape
    tl = min(tl, l)
    bt = _seqs_per_step(b, l)
    n = 3
    cb = BR_W // D_HEAD
    col0 = 4
    groups = tl // SUBLANES
    last_group = l // SUBLANES - 1
    spec = pl.BlockSpec((bt, tl, BR_W), lambda i, t, j: (i, t, j))
    return pl.pallas_call(
        _hy_pre_kernel,
        grid=(b // bt, l // tl, n),
        in_specs=[pl.BlockSpec((bt, cb, tl, D_HEAD), lambda i, t, j: (i, col0 + j, t, 0)),
                  pl.BlockSpec((bt, cb, SUBLANES, D_HEAD),
                               lambda i, t, j: (i, col0 + j, jnp.maximum(t * groups - 1, 0), 0)),
                  pl.BlockSpec((bt, cb, SUBLANES, D_HEAD),
                               lambda i, t, j: (i, col0 + j, jnp.minimum((t + 1) * groups, last_group), 0)),
                  pl.BlockSpec((3, BR_W), lambda i, t, j: (0, j))],
        out_specs=[spec, spec],
        out_shape=[jax.ShapeDtypeStruct((b, l, 3 * BR_W), F32),
                   jax.ShapeDtypeStruct((b, l, 3 * BR_W), BF16)],
        compiler_params=_cparams("parallel", "parallel", "parallel"),
        name="hy_pre",
    )(proj3, proj3, proj3, conv_w)


def _dot_hi(a, b):
    return jnp.dot(a, b, preferred_element_type=F32, precision=lax.Precision.HIGHEST)


def _hy_filter_kernel(feat_ref, dist_ref, w1_ref, b1_ref, w2_ref, b2_ref, w3_ref, b3_ref, dec_ref, o_ref):
    hid = jnp.sin(_dot_hi(feat_ref[...], w1_ref[...]) + b1_ref[...])
    hid = jnp.sin(_dot_hi(hid, w2_ref[...]) + b2_ref[...])
    dist = dist_ref[...]
    for j in range(o_ref.shape[1] // D_HEAD):
        cols = slice(j * D_HEAD, (j + 1) * D_HEAD)
        filt = _dot_hi(hid, w3_ref[:, cols]) + b3_ref[:, cols]
        o_ref[:, cols] = (filt * jnp.exp(-dist * jnp.abs(dec_ref[:, cols]))).astype(o_ref.dtype)


def _hy_filter(l, w1, b1, w2, b2, w3, b3, decay):
    pos = jnp.arange(l, dtype=F32)
    t = pos / l
    ang = (2.0 * math.pi) * t[:, None] * jnp.arange(1, HY_BANDS + 1, dtype=F32)
    feat = jnp.concatenate([t[:, None], jnp.cos(ang), jnp.sin(ang)], axis=-1)
    dist = jnp.broadcast_to((jnp.abs(pos - l // 2) / l)[:, None], (l, D_HEAD))
    pad = D_HEAD
    emb, ff = w1.shape
    feat = jnp.pad(feat, ((0, 0), (0, pad - emb)))
    w1p = jnp.pad(w1, ((0, pad - emb), (0, pad - ff)))
    w2p = jnp.pad(w2, ((0, pad - ff), (0, pad - ff)))
    w3p = jnp.pad(w3, ((0, pad - ff), (0, 0)))
    b1p = jnp.pad(b1, (0, pad - ff)).reshape(1, pad)
    b2p = jnp.pad(b2, (0, pad - ff)).reshape(1, pad)
    tl = min(l, 256)
    n = 2 * BR_W
    fixed = lambda shape: pl.BlockSpec(shape, lambda i: (0, 0))
    return pl.pallas_call(
        _hy_filter_kernel,
        grid=(l // tl,),
        in_specs=[pl.BlockSpec((tl, pad), lambda i: (i, 0)),
                  pl.BlockSpec((tl, D_HEAD), lambda i: (i, 0)),
                  fixed((pad, pad)), fixed((1, pad)), fixed((pad, pad)), fixed((1, pad)),
                  fixed((pad, n)), fixed((1, n)), fixed((1, n))],
        out_specs=pl.BlockSpec((tl, n), lambda i: (i, 0)),
        out_shape=jax.ShapeDtypeStruct((l, n), BF16),
        compiler_params=_cparams("parallel"),
        name="hy_filter",
    )(feat, dist, w1p, b1p, w2p, b2p, w3p, b3.reshape(1, n), decay.reshape(1, n))


def _dft_matrices(l):
    n = 2 * l
    k = jnp.arange(l, dtype=jnp.int32)
    t = jnp.arange(l, dtype=jnp.int32)
    tp = t + l // 2
    split = 1 << (max(l.bit_length() - 1, 0) // 2)

    def tables(rows, cols):
        def table(r):
            ang = (2.0 * math.pi / n) * ((r[:, None] * cols[None, :]) % n).astype(F32)
            return jnp.cos(ang), jnp.sin(ang)
        return (*table(rows[::split]), *table(rows[:split] - rows[0]))

    alt = jnp.where(t % 2 == 0, 1.0, -1.0).astype(F32).reshape(1, l)
    wk = (jnp.where(k == 0, 1.0, 2.0).astype(F32) / n).reshape(1, l)
    out = pl.pallas_call(
        functools.partial(_dft_gen_kernel, split=split, l=l),
        grid=(l // split,),
        in_specs=[pl.BlockSpec((l // split, l), lambda i: (0, 0))] * 2 + [pl.BlockSpec((split, l), lambda i: (0, 0))] * 2
        + [pl.BlockSpec((l // split, l), lambda i: (0, 0))] * 2 + [pl.BlockSpec((split, l), lambda i: (0, 0))] * 2
        + [pl.BlockSpec((1, l), lambda i: (0, 0))] * 2,
        out_specs=[pl.BlockSpec((split, l), lambda i: (i, 0))] * 4,
        out_shape=[jax.ShapeDtypeStruct((l, l), BF16)] * 4,
        compiler_params=_cparams("parallel"),
        name="dft_gen",
    )(*tables(k, t), *tables(tp, k), alt, wk)
    return (out[0], out[1]), (out[2], out[3])


def _dft_gen_kernel(ch_ref, sh_ref, cl_ref, sl_ref, chi_ref, shi_ref, cli_ref, sli_ref, alt_ref, wk_ref,
                    fc_ref, fs_ref, ic_ref, is_ref, *, split, l):
    i = pl.program_id(0)

    def cos_sin(c_hi, s_hi, c_lo, s_lo):
        ch, sh = c_hi[pl.ds(i, 1), :], s_hi[pl.ds(i, 1), :]
        return ch * c_lo[...] - sh * s_lo[...], sh * c_lo[...] + ch * s_lo[...]

    row = lax.broadcasted_iota(jnp.int32, (split, l), 0) + i * split
    col = lax.broadcasted_iota(jnp.int32, (split, l), 1)
    c, s = cos_sin(ch_ref, sh_ref, cl_ref, sl_ref)
    fc_ref[...] = c.astype(BF16)
    fs_ref[...] = jnp.where(row == 0, alt_ref[...], -s).astype(BF16)
    c, s = cos_sin(chi_ref, shi_ref, cli_ref, sli_ref)
    wk = wk_ref[...]
    alt_i = jnp.where(row % 2 == 0, 1.0, -1.0) * (1.0 / (2 * l))
    ic_ref[...] = (c * wk).astype(BF16)
    is_ref[...] = jnp.where(col == 0, alt_i, -s * wk).astype(BF16)


def _seqs_per_step(b, l, rows=2048):
    bt = max(1, min(b, rows // l))
    while b % bt:
        bt -= 1
    return bt


def _dft_fwd_kernel(fc_ref, fs_ref, x_ref, *rest, with_filter, tm):
    for bb in range(x_ref.shape[0]):
        x = x_ref[bb]
        ur = jnp.dot(fc_ref[...], x, preferred_element_type=F32)
        ui = jnp.dot(fs_ref[...], x, preferred_element_type=F32)
        if not with_filter:
            zr_ref, zi_ref = rest
            zr_ref[bb] = ur
            zi_ref[bb] = ui
            continue
        hr_ref, hi_ref, zr_ref, zi_ref = rest
        hr, hi = hr_ref[0], hi_ref[0]
        row0 = (lax.broadcasted_iota(jnp.int32, ur.shape, 0) + pl.program_id(0) * tm) == 0
        zr_ref[bb] = (ur * hr - jnp.where(row0, 0.0, ui * hi)).astype(zr_ref.dtype)
        zi_ref[bb] = jnp.where(row0, ui * hi, ur * hi + ui * hr).astype(zi_ref.dtype)


def _dft_fwd(fwd, x, x_col0, c, spec_h=None, h_col0=0, tm=512, tn=512):
    b, l, _ = x.shape
    tm = min(tm, l)
    bt = _seqs_per_step(b, l)
    xo, ho = x_col0 // tn, h_col0 // tn
    out_dtype = F32 if spec_h is None else BF16
    fspec = pl.BlockSpec((tm, l), lambda i, bb, j: (i, 0))
    in_specs = [fspec, fspec, pl.BlockSpec((bt, l, tn), lambda i, bb, j: (bb, 0, xo + j))]
    args = [*fwd, x]
    if spec_h is not None:
        hspec = pl.BlockSpec((1, tm, tn), lambda i, bb, j: (0, i, ho + j))
        in_specs += [hspec, hspec]
        args += list(spec_h)
    ospec = pl.BlockSpec((bt, tm, tn), lambda i, bb, j: (bb, i, j))
    return pl.pallas_call(
        functools.partial(_dft_fwd_kernel, with_filter=spec_h is not None, tm=tm),
        grid=(l // tm, b // bt, c // tn),
        in_specs=in_specs,
        out_specs=[ospec, ospec],
        out_shape=[jax.ShapeDtypeStruct((b, l, c), out_dtype)] * 2,
        compiler_params=_cparams("parallel", "parallel", "parallel"),
        name="dft_fwd",
    )(*args)


def _dft_inv_kernel(ic_ref, is_ref, zr_ref, zi_ref, u_ref, m_ref, skip_ref, *rest, with_gate):
    for bb in range(zr_ref.shape[0]):
        y = (jnp.dot(ic_ref[...], zr_ref[bb], preferred_element_type=F32)
             + jnp.dot(is_ref[...], zi_ref[bb], preferred_element_type=F32))
        z = m_ref[bb] * (y + u_ref[bb] * skip_ref[...])
        if with_gate:
            g_ref, o_ref = rest
            gate = jnp.concatenate([g_ref[bb, c] for c in range(g_ref.shape[1])], axis=-1)
            o_ref[bb] = (z * _silu(gate)).astype(o_ref.dtype)
        else:
            o_ref, ob_ref = rest
            o_ref[bb] = z
            ob_ref[bb] = z.astype(BF16)


def _dft_inv(inv, zr, zi, u, u_col0, mul, mul_col0, skip, gate=None, gate_col0=0, tm=512, tn=512):
    b, l, c = zr.shape
    tm = min(tm, l)
    bt = _seqs_per_step(b, l)
    win = lambda col0: pl.BlockSpec((bt, tm, tn), lambda i, bb, j, o=col0 // tn: (bb, i, o + j))
    zspec = pl.BlockSpec((bt, l, tn), lambda i, bb, j: (bb, 0, j))
    fspec = pl.BlockSpec((tm, l), lambda i, bb, j: (i, 0))
    in_specs = [fspec, fspec, zspec, zspec,
                win(u_col0), win(mul_col0), pl.BlockSpec((1, tn), lambda i, bb, j: (0, j))]
    args = [*inv, zr, zi, u, mul, skip]
    ospec = pl.BlockSpec((bt, tm, tn), lambda i, bb, j: (bb, i, j))
    if gate is not None:
        in_specs.append(pl.BlockSpec((bt, tn // D_HEAD, tm, D_HEAD),
                                     lambda i, bb, j, o=gate_col0 // tn: (bb, o + j, i, 0)))
        args.append(gate)
        out_specs, out_shape = ospec, jax.ShapeDtypeStruct((b, l, c), BF16)
    else:
        out_specs = [ospec, ospec]
        out_shape = [jax.ShapeDtypeStruct((b, l, c), F32), jax.ShapeDtypeStruct((b, l, c), BF16)]
    return pl.pallas_call(
        functools.partial(_dft_inv_kernel, with_gate=gate is not None),
        grid=(l // tm, b // bt, c // tn),
        in_specs=in_specs,
        out_specs=out_specs,
        out_shape=out_shape,
        compiler_params=_cparams("parallel", "parallel", "parallel"),
        name="dft_inv",
    )(*args)


def _hyena(proj3, p, dft):
    l = proj3.shape[2]
    fwd, inv = dft
    filt = _hy_filter(l, p['hy_w1'], p['hy_b1'], p['hy_w2'], p['hy_b2'], p['hy_w3'], p['hy_b3'], p['hy_decay'])
    filt_b = filt[None]
    spec_h = _dft_fwd(fwd, filt_b, 0, 2 * BR_W)
    pre, pre_b = _hy_pre(proj3, p['hy_conv'])
    skip = p['hy_skip'].astype(F32)
    zr, zi = _dft_fwd(fwd, pre_b, 0, BR_W, spec_h, 0)
    z1, z1_b = _dft_inv(inv, zr, zi, pre, 0, pre, BR_W, skip[0:1])
    zr, zi = _dft_fwd(fwd, z1_b, 0, BR_W, spec_h, BR_W)
    return _dft_inv(inv, zr, zi, z1, 0, pre, 2 * BR_W, skip[1:2], gate=proj3, gate_col0=7 * BR_W)


def _softplus(x):
    return jnp.maximum(x, 0.0) + jnp.log1p(jnp.exp(-jnp.abs(x)))


def _split_bf16(x, parts):
    out = []
    for _ in range(parts - 1):
        piece = x.astype(BF16)
        out.append(piece)
        x = x - piece.astype(F32)
    out.append(x.astype(BF16))
    return out


def _bmm(a, b, hi=False):
    mm = lambda x, y: jnp.einsum('nij,njk->nik', x, y, preferred_element_type=F32)
    if not hi:
        return mm(a.astype(BF16), b.astype(BF16))
    (a1, a2), (b1, b2) = _split_bf16(a, 2), _split_bf16(b, 2)
    return mm(a1, b1) + (mm(a1, b2) + mm(a2, b1))


def _bmm_nt(a, b):
    return jnp.einsum('nid,njd->nij', a.astype(BF16), b.astype(BF16), preferred_element_type=F32)


TRI_BASE = 4


def _unit_tri_inverse(a, ri, ci):
    same = lambda w: (ri // w) == (ci // w)
    eye = (ri == ci).astype(F32)
    x = -jnp.where(same(TRI_BASE), a, 0.0)
    p = eye + x
    for _ in range(TRI_BASE.bit_length() - 2):
        x = _bmm(x, x, hi=True)
        p = p + _bmm(p, x, hi=True)
    w = TRI_BASE
    while w < a.shape[-1]:
        off = jnp.where(same(2 * w) & ~same(w), a, 0.0)
        p = p - _bmm(p, _bmm(off, p))
        w *= 2
    return p


def _gdn_prepare(q, k, v, ab, a_row, dt_row, head0, group):
    n, c, _ = q.shape
    two = lambda x: jnp.concatenate([x, x], axis=0)
    q, k, v, ab = two(q), two(k), two(v), two(ab)
    back3 = lambda shape: lax.broadcasted_iota(jnp.int32, shape, 0) >= n
    lane = lax.broadcasted_iota(jnp.int32, ab.shape, 2)
    bidx = lax.broadcasted_iota(jnp.int32, ab.shape, 0)
    head = head0 + jnp.where(bidx >= n, bidx - n, bidx) // group
    base = jnp.where(bidx >= n, 2 * N_HEAD, 0) + head
    g_all = -a_row * _softplus(ab + dt_row)
    g = jnp.sum(jnp.where(lane == base, g_all, 0.0), axis=2, keepdims=True)
    beta = jnp.sum(jnp.where(lane == base + N_HEAD, _sigmoid(ab), 0.0), axis=2, keepdims=True)

    sq = (2 * n, c, c)
    ri = lax.broadcasted_iota(jnp.int32, sq, 1)
    ci = lax.broadcasted_iota(jnp.int32, sq, 2)
    ahead = jnp.where(back3(sq), ci - ri, ri - ci)
    incl = ahead >= 0
    strict = ahead > 0
    tri = jnp.where(incl, 1.0, 0.0).astype(BF16)
    gc = sum(jnp.einsum('nij,njk->nik', tri, piece, preferred_element_type=F32)
             for piece in _split_bf16(jnp.broadcast_to(g, q.shape), 3))
    gc_row = jnp.swapaxes(gc, 1, 2)[:, :c, :]
    total = jnp.where(back3((2 * n, 1, D_HEAD)), gc[:, 0:1, :], gc[:, c - 1:c, :])
    decay = jnp.where(incl, jnp.exp(jnp.where(incl, gc[:, :, :c] - gc_row, 0.0)), 0.0)

    kb = k * beta
    a = jnp.where(strict, _bmm_nt(kb, k) * decay, 0.0)
    t = _unit_tri_inverse(a, ri, ci)
    e = jnp.exp(gc)
    u = _bmm(t, v * beta)
    w = _bmm(t, kb * e)
    a_intra = jnp.where(incl, _bmm_nt(q, k) * decay, 0.0)
    return (u, w.astype(BF16), (q * e).astype(BF16), (k * jnp.exp(total - gc)).astype(BF16),
            a_intra.astype(BF16), jnp.exp(total))


def _gdn_kernel(*refs, aliased, has_s0, group):
    if aliased:
        refs = refs[1:]
    if has_s0:
        (q_ref, k_ref, v_ref, z_ref, ab_ref, wq_ref, wk_ref, wv_ref, arow_ref, dt_ref, gn_ref, s0_ref,
         y_ref, sf_ref, qn, kn, vn, u_s, w_s, qd_s, kd_s, ai_s, gl_s) = refs
    else:
        (q_ref, k_ref, v_ref, z_ref, ab_ref, wq_ref, wk_ref, wv_ref, arow_ref, dt_ref, gn_ref,
         y_ref, sf_ref, qn, kn, vn, u_s, w_s, qd_s, kd_s, ai_s, gl_s) = refs
    _, heads, l, _ = q_ref.shape
    head0 = pl.program_id(1) * heads
    n_chunks = l // CHUNK
    hcols = lambda hh: slice(hh * D_HEAD, (hh + 1) * D_HEAD)

    def l2n(x):
        return x * lax.rsqrt(jnp.sum(x * x, axis=-1, keepdims=True) + EPS)

    for hh in range(heads):
        cols = hcols(hh)
        qn[:, cols] = l2n(_silu(_dwconv3(q_ref[0, hh], wq_ref.at[:, cols]))) * (D_HEAD ** -0.5)
        kn[:, cols] = l2n(_silu(_dwconv3(k_ref[0, hh], wk_ref.at[:, cols])))
        vn[:, cols] = _silu(_dwconv3(v_ref[0, hh], wv_ref.at[:, cols]))

    a_row, dt_row = arow_ref[...], dt_ref[...]

    def prepare(gi, carry):
        span = group * CHUNK
        rows = pl.ds(pl.multiple_of(gi * span, span), span)
        chunks = lambda x: x.reshape(group, CHUNK, x.shape[-1])
        per_head = lambda ref: jnp.concatenate([chunks(ref[rows, hcols(hh)]) for hh in range(heads)], axis=0)
        ab = chunks(ab_ref[0, rows, :])
        u, w, qd, kd, ai, gl = _gdn_prepare(per_head(qn), per_head(kn), per_head(vn),
                                            jnp.concatenate([ab] * heads, axis=0), a_row, dt_row, head0, group)
        for d in range(2):
            for hh in range(heads):
                cols = hcols(hh)
                part = slice((d * heads + hh) * group, (d * heads + hh + 1) * group)
                u_s[d, rows, cols] = u[part].reshape(span, D_HEAD)
                w_s[d, rows, cols] = w[part].reshape(span, D_HEAD)
                qd_s[d, rows, cols] = qd[part].reshape(span, D_HEAD)
                kd_s[d, rows, cols] = kd[part].reshape(span, D_HEAD)
                ai_s[d, hh, rows, :] = ai[part].reshape(span, CHUNK)
                gl_s[d, hh, pl.ds(gi * group, group)] = jnp.broadcast_to(gl[part], (group,) + gl_s.shape[3:])
        return carry

    lax.fori_loop(0, n_chunks // group, prepare, 0)

    def scan(i, s):
        where = [(hh, d, pl.ds(pl.multiple_of(chunk * CHUNK, CHUNK), CHUNK), chunk)
                 for hh in range(heads) for d, chunk in ((0, i), (1, n_chunks - 1 - i))]
        gather = lambda ref: jnp.stack([ref[d, rows, hcols(hh)] for hh, d, rows, _ in where])
        a_intra = jnp.stack([ai_s[d, hh, rows, :] for hh, d, rows, _ in where])
        decay = jnp.stack([gl_s[d, hh, chunk][0:1, :] for hh, d, _, chunk in where])
        sb = s.astype(BF16)
        v_new = gather(u_s) - _bmm(gather(w_s), sb)
        vb = v_new.astype(BF16)
        o = _bmm(gather(qd_s), sb) + _bmm(a_intra, vb)
        for idx, (hh, d, rows, _) in enumerate(where):
            u_s[d, rows, hcols(hh)] = o[idx]
        return s * decay + jnp.einsum('nik,niv->nkv', gather(kd_s), vb, preferred_element_type=F32)

    if has_s0:
        init = jnp.stack([s0_ref[0, 0, d, hh] for hh in range(heads) for d in range(2)])
    else:
        init = jnp.zeros((2 * heads, D_HEAD, D_HEAD), F32)
    final = lax.fori_loop(0, n_chunks, scan, init)
    for hh in range(heads):
        cols = hcols(hh)
        sf_ref[0, 0, 0, hh] = final[2 * hh]
        sf_ref[0, 0, 1, hh] = final[2 * hh + 1]
        y_ref[0, :, cols] = (_rms(u_s[0, :, cols] + u_s[1, :, cols], gn_ref[...])
                             * _silu(z_ref[0, hh])).astype(y_ref.dtype)


def _gdn(proj3, ab3, conv_w, a_log, dt_bias, norm_g, layer, state=None, new_state=None):
    b, _, l, _ = proj3.shape
    depth_out, layer_out = (1, 0) if state is not None else (DEPTH, layer)
    aliased = new_state is not None
    lanes = jnp.zeros((2, 2 * N_HEAD), F32).at[:, :N_HEAD].set(1.0)
    a_row = jnp.pad((jnp.exp(a_log.astype(F32))[:, None, :] * lanes.reshape(2, 2, N_HEAD)).reshape(1, -1),
                    ((0, 0), (0, AB_PAD - 4 * N_HEAD)))
    dt_row = jnp.pad((dt_bias.astype(F32)[:, None, :] * lanes.reshape(2, 2, N_HEAD)).reshape(1, -1),
                     ((0, 0), (0, AB_PAD - 4 * N_HEAD)))
    hps = N_HEAD if l <= 512 else 1
    wid = hps * D_HEAD
    n_hb = N_HEAD // hps
    blk = lambda c: pl.BlockSpec((1, hps, l, D_HEAD), lambda i, h, c=c: (i, c * n_hb + h, 0, 0))
    wblk = lambda c: pl.BlockSpec((3, wid), lambda i, h, c=c: (0, c * n_hb + h))
    row = pl.BlockSpec((1, D_HEAD), lambda i, h: (0, 0))
    in_specs = [blk(0), blk(1), blk(2), blk(3),
                pl.BlockSpec((1, l, AB_PAD), lambda i, h: (i, 0, 0)),
                wblk(0), wblk(1), wblk(2), row, row, row]
    args = [proj3, proj3, proj3, proj3, ab3, conv_w, conv_w, conv_w, a_row, dt_row, norm_g]
    if aliased:
        in_specs.insert(0, pl.BlockSpec(memory_space=pl.ANY))
        args.insert(0, new_state)
    if state is not None:
        in_specs.append(pl.BlockSpec((1, 1, 2, hps, D_HEAD, D_HEAD), lambda i, h: (i, layer, 0, h, 0, 0)))
        args.append(state)
    return pl.pallas_call(
        functools.partial(_gdn_kernel, aliased=aliased, has_s0=state is not None, group=min(8, l // CHUNK)),
        grid=(b, n_hb),
        in_specs=in_specs,
        out_specs=[pl.BlockSpec((1, l, wid), lambda i, h: (i, 0, h)),
                   pl.BlockSpec((1, 1, 2, hps, D_HEAD, D_HEAD), lambda i, h: (i, layer_out, 0, h, 0, 0))],
        out_shape=[jax.ShapeDtypeStruct((b, l, BR_W), BF16),
                   jax.ShapeDtypeStruct((b, depth_out, 2, N_HEAD, D_HEAD, D_HEAD), F32)],
        input_output_aliases={0: 1} if aliased else {},
        scratch_shapes=[pltpu.VMEM((l, wid), F32)] * 3
        + [pltpu.VMEM((2, l, wid), F32)] + [pltpu.VMEM((2, l, wid), BF16)] * 3
        + [pltpu.VMEM((2, hps, l, CHUNK), BF16), pltpu.VMEM((2, hps, l // CHUNK, 8, D_HEAD), F32)],
        compiler_params=_cparams("parallel", "parallel"),
        name="gdn",
    )(*args)


def _mod_kernel(c_ref, w_ref, b_ref, o_ref):
    o_ref[...] = _dot_hi(_silu(c_ref[...]), w_ref[...]) + b_ref[...]


def _modulation(cond, w_mod, b_mod, layer, tn=512):
    n = cond.shape[0]
    rows = 8
    out = pl.pallas_call(
        _mod_kernel,
        grid=(3 * D_MODEL // tn,),
        in_specs=[pl.BlockSpec((rows, D_MODEL), lambda j: (0, 0)),
                  pl.BlockSpec((None, D_MODEL, tn), lambda j: (layer, 0, j)),
                  pl.BlockSpec((1, tn), lambda j: (0, j))],
        out_specs=pl.BlockSpec((rows, tn), lambda j: (0, j)),
        out_shape=jax.ShapeDtypeStruct((rows, 3 * D_MODEL), F32),
        compiler_params=_cparams("parallel"),
        name="modulation",
    )(jnp.pad(cond.astype(F32), ((0, rows - n), (0, 0))), w_mod, b_mod.reshape(1, -1))
    return out[:n].reshape(n, 3, D_MODEL)


def _split_w_in(w_in):
    n_a = 4 * BR_W + 4 * N_HEAD
    w_in = w_in.astype(BF16)
    main = jnp.concatenate([w_in[..., :4 * BR_W], w_in[..., n_a:]], axis=-1)
    ab = jnp.pad(w_in[..., 4 * BR_W:n_a], ((0, 0),) * (w_in.ndim - 1) + ((0, AB_PAD - 4 * N_HEAD),))
    return main, ab


def _trunk_layer(x3, cond, p, big, layer, dft, latent, new_outputs=(None, None, None)):
    b, l, _ = x3.shape
    x2 = x3.reshape(b * l, D_MODEL)
    mod = _modulation(cond, big['w_mod'], p['b_mod'], layer)
    rows_per_mod = l if mod.shape[0] == b else b * l
    g_pre = p['g_pre'].reshape(1, D_MODEL)
    proj3, ab = _inproj(x2, mod, g_pre, big['w_main'], big['w_ab'], layer, rows_per_mod, l)
    ab3 = ab.reshape(b, l, AB_PAD)

    lam_init = 0.8 - 0.6 * math.exp(-0.3 * layer)
    lam_p = p['diff_lam'].astype(F32)
    lam = (jnp.exp(jnp.sum(lam_p[0] * lam_p[1])) - jnp.exp(jnp.sum(lam_p[2] * lam_p[3])) + lam_init).reshape(1, 1)
    diff_norm = p['diff_norm'].reshape(1, D_HEAD)
    gdn_args = (proj3, ab3, p['gdn_conv'], p['gdn_a_log'], p['gdn_dt_bias'], p['gdn_norm'].reshape(1, D_HEAD), layer)

    yb = _hyena(proj3, p, dft)
    if latent is None:
        new_state, nat_cache, diff_cache = new_outputs
        ya, new_state = _gdn(*gdn_args, new_state=new_state)
        yc, yd, nat_cache, diff_cache = _ctx_attention(proj3, lam, diff_norm, lam_init, layer, nat_cache, diff_cache)
        extras = (new_state, nat_cache, diff_cache)
    else:
        ya, _ = _gdn(*gdn_args, state=latent['state_gdn'])
        yc = _lat_nat(proj3, latent['cache_nat_kv'], layer, _nat_bias_table(p['nat_rpb']))
        yd = _lat_diff(proj3, latent['cache_diff_kv'], layer, lam, diff_norm, lam_init, latent['rope'])
        extras = None

    ys = [t.reshape(b * l, BR_W) for t in (ya, yb, yc, yd)]
    out = _merge(x2, mod, g_pre, p['g_post'].reshape(1, D_MODEL), ys, big['w_branch'], big['w_merge'],
                 p['b_merge'].reshape(1, -1).astype(F32), big['w_out'], layer, rows_per_mod)
    return out.reshape(b, l, D_MODEL), extras


def kernel(x_prompt, x_sample, state_gdn, cache_nat_kv, cache_diff_kv, c, c_ctx,
           w_mod, b_mod, g_pre, g_post, w_in, gdn_conv, gdn_a_log, gdn_dt_bias, gdn_norm,
           hy_conv, hy_w1, hy_b1, hy_w2, hy_b2, hy_w3, hy_b3, hy_decay, hy_skip,
           nat_rpb, diff_lam, diff_norm, w_branch, w_merge, b_merge, w_out):
    small = {
        'b_mod': b_mod, 'g_pre': g_pre, 'g_post': g_post,
        'gdn_conv': gdn_conv, 'gdn_a_log': gdn_a_log, 'gdn_dt_bias': gdn_dt_bias, 'gdn_norm': gdn_norm,
        'hy_conv': hy_conv, 'hy_w1': hy_w1, 'hy_b1': hy_b1, 'hy_w2': hy_w2, 'hy_b2': hy_b2,
        'hy_w3': hy_w3, 'hy_b3': hy_b3, 'hy_decay': hy_decay, 'hy_skip': hy_skip,
        'nat_rpb': nat_rpb, 'diff_lam': diff_lam, 'diff_norm': diff_norm, 'b_merge': b_merge,
    }
    layers = [{name: arr[i] for name, arr in small.items()} for i in range(DEPTH)]
    w_main, w_ab = _split_w_in(w_in)
    big = {'w_mod': w_mod.astype(F32), 'w_main': w_main, 'w_ab': w_ab, 'w_branch': w_branch.astype(BF16),
           'w_merge': w_merge.astype(BF16), 'w_out': w_out.astype(BF16)}

    y_prompt = x_prompt
    dft_ctx = _dft_matrices(x_prompt.shape[1])
    outputs = (None, None, None)
    for i, p in enumerate(layers):
        y_prompt, outputs = _trunk_layer(y_prompt, c_ctx.reshape(1, D_MODEL), p, big, i, dft_ctx, None, outputs)
    new_state, nat_cache, diff_cache = outputs

    y_sample = x_sample
    dft_lat = _dft_matrices(x_sample.shape[1])
    latent = {'state_gdn': state_gdn, 'cache_nat_kv': cache_nat_kv, 'cache_diff_kv': cache_diff_kv,
              'rope': _rope_tables(x_sample.shape[1])}
    for i, p in enumerate(layers):
        y_sample, _ = _trunk_layer(y_sample, c, p, big, i, dft_lat, latent)

    return (y_prompt, y_sample, new_state, nat_cache, diff_cache)
```

```python
import functools
import math

import jax
import jax.numpy as jnp
import numpy as np
from jax import lax
from jax.experimental import pallas as pl
from jax.experimental.pallas import tpu as pltpu

F32 = jnp.float32
BF16 = jnp.bfloat16

D_MODEL = 1024
DEPTH = 2
GRID_W = 64
N_BRANCH = 4
BR_W = 512
N_HEAD = 4
D_HEAD = 128
SUBLANES = 8
CHUNK = 64
HY_BANDS = 16
WIN_R = 8
WIN_C = 16
DQK_D = 64
ROPE_BASE = 10000.0
EPS = 1e-6
N_MAIN = 4 * 4 * BR_W
AB_PAD = 128
NEG_INF = -1e30

VMEM_LIMIT = 48 * 1024 * 1024


def _cparams(*sem):
    return pltpu.CompilerParams(dimension_semantics=sem, vmem_limit_bytes=VMEM_LIMIT)


def _silu(x):
    return x * (1.0 / (1.0 + jnp.exp(-x)))


def _sigmoid(x):
    return 1.0 / (1.0 + jnp.exp(-x))


def _rms(x, g):
    return x * lax.rsqrt(jnp.mean(x * x, axis=-1, keepdims=True) + EPS) * g


def _dot(a, b):
    return jnp.dot(a.astype(BF16), b.astype(BF16), preferred_element_type=F32)


def _dot_nt(a, b):
    return lax.dot_general(a.astype(BF16), b.astype(BF16), (((1,), (1,)), ((), ())),
                           preferred_element_type=F32)


def _dot_tn(a, b):
    return lax.dot_general(a.astype(BF16), b.astype(BF16), (((0,), (0,)), ((), ())),
                           preferred_element_type=F32)


def _prenorm(x, g_pre, mod_ref):
    return _rms(x, g_pre) * (1.0 + mod_ref[0, 1:2, :]) + mod_ref[0, 0:1, :]


def _inproj_kernel(x_ref, mod_ref, gpre_ref, w_ref, wab_ref, proj_ref, ab_ref, h_scr):
    @pl.when(pl.program_id(1) == 0)
    def _():
        h = _prenorm(x_ref[...], gpre_ref[...], mod_ref).astype(BF16)
        h_scr[...] = h
        ab_ref[...] = jnp.dot(h, wab_ref[...], preferred_element_type=F32)

    acc = jnp.dot(h_scr[...], w_ref[...], preferred_element_type=F32)
    seqs, cblocks, rows, _ = proj_ref.shape
    for sq in range(seqs):
        for c in range(cblocks):
            proj_ref[sq, c] = acc[sq * rows:(sq + 1) * rows, c * D_HEAD:(c + 1) * D_HEAD]


def _inproj(x2, mod, g_pre, w_main, w_ab, layer, rows_per_mod, l, tm=1024, tn=2048):
    m = x2.shape[0]
    tm = math.gcd(tm, rows_per_mod)
    if tm >= l:
        proj_spec = pl.BlockSpec((tm // l, tn // D_HEAD, l, D_HEAD), lambda i, j: (i, j, 0, 0))
    else:
        per = l // tm
        proj_spec = pl.BlockSpec((1, tn // D_HEAD, tm, D_HEAD), lambda i, j: (i // per, j, i % per, 0))
    return pl.pallas_call(
        _inproj_kernel,
        grid=(m // tm, N_MAIN // tn),
        in_specs=[
            pl.BlockSpec((tm, D_MODEL), lambda i, j: (i, 0)),
            pl.BlockSpec((1, 3, D_MODEL), lambda i, j: ((i * tm) // rows_per_mod, 0, 0)),
            pl.BlockSpec((1, D_MODEL), lambda i, j: (0, 0)),
            pl.BlockSpec((None, D_MODEL, tn), lambda i, j: (layer, 0, j)),
            pl.BlockSpec((None, D_MODEL, AB_PAD), lambda i, j: (layer, 0, 0)),
        ],
        out_specs=[
            proj_spec,
            pl.BlockSpec((tm, AB_PAD), lambda i, j: (i, 0)),
        ],
        out_shape=[jax.ShapeDtypeStruct((m // l, N_MAIN // D_HEAD, l, D_HEAD), F32),
                   jax.ShapeDtypeStruct((m, AB_PAD), F32)],
        scratch_shapes=[pltpu.VMEM((tm, D_MODEL), BF16)],
        compiler_params=_cparams("parallel", "arbitrary"),
        name="inproj",
    )(x2, mod, g_pre, w_main, w_ab)


def _merge_kernel(x_ref, mod_ref, gpre_ref, gpost_ref, ya_ref, yb_ref, yc_ref, yd_ref,
                  wbr_ref, wmg_ref, bmg_ref, wout_ref, o_ref):
    x = x_ref[...]
    h = _prenorm(x, gpre_ref[...], mod_ref).astype(BF16)
    acc = None
    for k, y_ref in enumerate((ya_ref, yb_ref, yc_ref, yd_ref)):
        cols = slice(k * D_MODEL, (k + 1) * D_MODEL)
        gate = _sigmoid(jnp.dot(h, wmg_ref[:, cols], preferred_element_type=F32) + bmg_ref[:, cols])
        br = jnp.dot(y_ref[...], wbr_ref[k], preferred_element_type=F32)
        acc = gate * br if acc is None else acc + gate * br
    y = jnp.dot(acc.astype(BF16), wout_ref[...], preferred_element_type=F32)
    o_ref[...] = x + mod_ref[0, 2:3, :] * _rms(y, gpost_ref[...])


def _merge(x2, mod, g_pre, g_post, ys, w_branch, w_merge, b_merge, w_out, layer, rows_per_mod, tm=256):
    m = x2.shape[0]
    row = lambda i: (i, 0)
    fixed2 = lambda i: (0, 0)
    return pl.pallas_call(
        _merge_kernel,
        grid=(m // tm,),
        in_specs=[
            pl.BlockSpec((tm, D_MODEL), row),
            pl.BlockSpec((1, 3, D_MODEL), lambda i: ((i * tm) // rows_per_mod, 0, 0)),
            pl.BlockSpec((1, D_MODEL), fixed2),
            pl.BlockSpec((1, D_MODEL), fixed2),
            pl.BlockSpec((tm, BR_W), row),
            pl.BlockSpec((tm, BR_W), row),
            pl.BlockSpec((tm, BR_W), row),
            pl.BlockSpec((tm, BR_W), row),
            pl.BlockSpec((None, N_BRANCH, BR_W, D_MODEL), lambda i: (layer, 0, 0, 0)),
            pl.BlockSpec((None, D_MODEL, N_BRANCH * D_MODEL), lambda i: (layer, 0, 0)),
            pl.BlockSpec((1, N_BRANCH * D_MODEL), fixed2),
            pl.BlockSpec((None, D_MODEL, D_MODEL), lambda i: (layer, 0, 0)),
        ],
        out_specs=pl.BlockSpec((tm, D_MODEL), row),
        out_shape=jax.ShapeDtypeStruct((m, D_MODEL), F32),
        compiler_params=_cparams("parallel"),
        name="merge",
    )(x2, mod, g_pre, g_post, *ys, w_branch, w_merge, b_merge, w_out)


def _softmax_rows(s):
    p = jnp.exp(s - jnp.max(s, axis=-1, keepdims=True))
    return p, jnp.sum(p, axis=-1, keepdims=True)


def _head_cols(h):
    return slice(h * D_HEAD, (h + 1) * D_HEAD)


def _stack_heads(ref):
    return ref[0]


def _ctx_nat_kernel(*refs, aliased):
    q_ref, k_ref, v_ref, g_ref, y_ref, kv_ref = refs[1:] if aliased else refs
    scale = D_HEAD ** -0.5
    q, k, v = (_stack_heads(r) for r in (q_ref, k_ref, v_ref))
    p, l = _softmax_rows(_bmm_nt(q, k) * scale)
    o = _bmm(p, v) / l
    for h in range(N_HEAD):
        sl = _head_cols(h)
        y_ref[0, :, sl] = (o[h] * _silu(g_ref[0, h])).astype(y_ref.dtype)
        kv_ref[0, 0, 0, h] = k[h]
        kv_ref[0, 0, 1, h] = v[h]


def _map_masks():
    lane = lax.broadcasted_iota(jnp.int32, (1, D_HEAD), 1)
    first = (lane < DQK_D).astype(F32)
    return first, 1.0 - first


def _ctx_diff_kernel(*refs, aliased, out_scale):
    lam_ref, q_ref, k_ref, v_ref, g_ref, gn_ref, y_ref, kv_ref = refs[1:] if aliased else refs
    scale = DQK_D ** -0.5
    m1, m2 = _map_masks()
    q, k, v = (_stack_heads(r) for r in (q_ref, k_ref, v_ref))
    p, l = _softmax_rows(_bmm_nt(jnp.concatenate([q * m1, q * m2], axis=0), jnp.concatenate([k, k], axis=0)) * scale)
    pn = p / l
    a = pn[:N_HEAD] - lam_ref[...] * pn[N_HEAD:]
    o = _rms(_bmm(a, v), gn_ref[...]) * out_scale
    for h in range(N_HEAD):
        sl = _head_cols(h)
        y_ref[0, :, sl] = (o[h] * _silu(g_ref[0, h])).astype(y_ref.dtype)
        kv_ref[0, 0, 0, h] = k[h]
        kv_ref[0, 0, 1, h] = v[h]


def _ctx_attention(proj3, lam, diff_norm, lam_init, layer, nat_cache, diff_cache):
    b, _, l, _ = proj3.shape
    blk = lambda c: pl.BlockSpec((1, N_HEAD, l, D_HEAD), lambda i, c=c: (i, c, 0, 0))
    y_spec = pl.BlockSpec((1, l, BR_W), lambda i: (i, 0, 0))
    kv_spec = pl.BlockSpec((1, 1, 2, N_HEAD, l, D_HEAD), lambda i: (i, layer, 0, 0, 0, 0))
    out_shape = [jax.ShapeDtypeStruct((b, l, BR_W), BF16),
                 jax.ShapeDtypeStruct((b, DEPTH, 2, N_HEAD, l, D_HEAD), F32)]
    aliased = nat_cache is not None
    cache_specs = [pl.BlockSpec(memory_space=pl.ANY)] if aliased else []
    aliases = {0: 1} if aliased else {}
    yc, nat_cache = pl.pallas_call(
        functools.partial(_ctx_nat_kernel, aliased=aliased),
        grid=(b,),
        in_specs=cache_specs + [blk(8), blk(9), blk(10), blk(11)],
        out_specs=[y_spec, kv_spec],
        out_shape=out_shape,
        input_output_aliases=aliases,
        compiler_params=_cparams("parallel"),
        name="ctx_nat",
    )(*([nat_cache] if aliased else []), proj3, proj3, proj3, proj3)
    yd, diff_cache = pl.pallas_call(
        functools.partial(_ctx_diff_kernel, aliased=aliased, out_scale=1.0 - lam_init),
        grid=(b,),
        in_specs=cache_specs + [pl.BlockSpec((1, 1), lambda i: (0, 0)),
                                blk(12), blk(13), blk(14), blk(15),
                                pl.BlockSpec((1, D_HEAD), lambda i: (0, 0))],
        out_specs=[y_spec, kv_spec],
        out_shape=out_shape,
        input_output_aliases=aliases,
        compiler_params=_cparams("parallel"),
        name="ctx_diff",
    )(*([diff_cache] if aliased else []), lam, proj3, proj3, proj3, proj3, diff_norm)
    return yc, yd, nat_cache, diff_cache


def _nat_bias_table(rpb):
    cols = np.arange(GRID_W)
    start = np.clip(cols - WIN_C // 2, 0, GRID_W - WIN_C)
    inside = (cols[None, :] >= start[:, None]) & (cols[None, :] < start[:, None] + WIN_C)
    dc = cols[None, :] - cols[:, None] + (WIN_C - 1)
    onehot = ((dc[None] == np.arange(2 * WIN_C - 1)[:, None, None]) & inside[None]).astype(np.float32)
    t = jnp.einsum('hdx,xck->hdck', rpb.astype(F32), jnp.asarray(onehot), precision=lax.Precision.HIGHEST)
    t = jnp.where(jnp.asarray(inside)[None, None], t, NEG_INF)
    tab = jnp.stack([t[:, WIN_R - 1 - off:2 * WIN_R - 1 - off] for off in range(WIN_R)], axis=1)
    return tab.transpose(0, 1, 3, 2, 4).reshape(rpb.shape[0], WIN_R, GRID_W, WIN_R * GRID_W)


def _lat_nat_kernel(q_ref, k_ref, v_ref, g_ref, ckv_ref, bias_ref, y_ref, kb_scr, vb_scr, *, rb):
    scale = D_HEAD ** -0.5
    rows = q_ref.shape[1] // GRID_W
    win = WIN_R * GRID_W
    kb_scr[...] = k_ref[0].astype(BF16)
    vb_scr[...] = v_ref[0].astype(BF16)
    ck = ckv_ref[0, 0, 0, 0].astype(BF16)
    cv = ckv_ref[0, 0, 1, 0].astype(BF16)

    def row_block(i, carry):
        q0 = pl.multiple_of(i * (rb * GRID_W), rb * GRID_W)
        qrows = pl.ds(q0, rb * GRID_W)
        q = q_ref[0, qrows, :].astype(BF16)
        kw, vw, bias = [], [], []
        for j in range(rb):
            r = i * rb + j
            rs = jnp.clip(r - WIN_R // 2, 0, rows - WIN_R)
            wrows = pl.ds(pl.multiple_of(rs * GRID_W, GRID_W), win)
            kw.append(kb_scr[wrows, :])
            vw.append(vb_scr[wrows, :])
            bias.append(bias_ref[0, r - rs])
        q3 = q.reshape(rb, GRID_W, D_HEAD)
        s_lat = _bmm_nt(q3, jnp.stack(kw)) * scale + jnp.stack(bias)
        s_ctx = (_dot_nt(q, ck) * scale).reshape(rb, GRID_W, ck.shape[0])
        m = jnp.maximum(jnp.max(s_lat, axis=-1, keepdims=True), jnp.max(s_ctx, axis=-1, keepdims=True))
        p_lat = jnp.exp(s_lat - m)
        p_ctx = jnp.exp(s_ctx - m)
        l = jnp.sum(p_lat, axis=-1, keepdims=True) + jnp.sum(p_ctx, axis=-1, keepdims=True)
        o_ctx = _dot(p_ctx.reshape(rb * GRID_W, ck.shape[0]), cv).reshape(rb, GRID_W, D_HEAD)
        o = ((_bmm(p_lat, jnp.stack(vw)) + o_ctx) / l).reshape(rb * GRID_W, D_HEAD)
        y_ref[0, qrows, :] = (o * _silu(g_ref[0, qrows, :])).astype(y_ref.dtype)
        return carry

    lax.fori_loop(0, rows // rb, row_block, 0)


def _lat_nat(proj3, cache_nat_kv, layer, bias_tab):
    b, _, l, _ = proj3.shape
    past = cache_nat_kv.shape[4]
    blk = lambda c: pl.BlockSpec((1, None, l, D_HEAD), lambda i, h, c=c: (i, c + h, 0, 0))
    return pl.pallas_call(
        functools.partial(_lat_nat_kernel, rb=8),
        grid=(b, N_HEAD),
        in_specs=[blk(32), blk(36), blk(40), blk(44),
                  pl.BlockSpec((1, 1, 2, 1, past, D_HEAD), lambda i, h: (i, layer, 0, h, 0, 0)),
                  pl.BlockSpec((1, WIN_R, GRID_W, WIN_R * GRID_W), lambda i, h: (h, 0, 0, 0))],
        out_specs=pl.BlockSpec((1, l, D_HEAD), lambda i, h: (i, 0, h)),
        out_shape=jax.ShapeDtypeStruct((b, l, BR_W), BF16),
        scratch_shapes=[pltpu.VMEM((l, D_HEAD), BF16), pltpu.VMEM((l, D_HEAD), BF16)],
        compiler_params=_cparams("parallel", "parallel"),
        name="lat_nat",
    )(proj3, proj3, proj3, proj3, cache_nat_kv, bias_tab)


def _rope_tables(l):
    half = DQK_D // 2
    nf = half // 2
    t = jnp.arange(l)
    row = (t // GRID_W).astype(F32)
    col = (t % GRID_W).astype(F32)
    inv = ROPE_BASE ** (-jnp.arange(nf, dtype=F32) / nf)
    ang = jnp.concatenate([row[:, None] * inv, col[:, None] * inv], axis=-1)
    cos, sin = jnp.cos(ang), jnp.sin(ang)
    zero = jnp.zeros_like(sin)
    tile2 = lambda a, b: jnp.concatenate([a, b, a, b], axis=-1)
    return tile2(cos, cos), tile2(-sin, zero), tile2(zero, sin)


def _rope(x, cos, sin_a, sin_b):
    return x * cos + pltpu.roll(x, 96, 1) * sin_a + pltpu.roll(x, 32, 1) * sin_b


def _lat_diff_kernel(lam_ref, q_ref, k_ref, v_ref, g_ref, ckv_ref, gn_ref,
                     cq_ref, saq_ref, sbq_ref, ck_ref, sak_ref, sbk_ref,
                     y_ref, ks_scr, vt_scr, *, out_scale, prep_rows, key_block, ahead):
    scale = DQK_D ** -0.5
    l = k_ref.shape[1]

    @pl.when(pl.program_id(2) == 0)
    def _():
        def prep(i, carry):
            rows = pl.ds(pl.multiple_of(i * prep_rows, prep_rows), prep_rows)
            kr = _rope(k_ref[0, rows, :], ck_ref[rows, :], sak_ref[rows, :], sbk_ref[rows, :])
            ks_scr[rows, :] = kr.astype(BF16)
            vt_scr[:, rows] = v_ref[0, rows, :].T.astype(BF16)
            return carry

        lax.fori_loop(0, l // prep_rows, prep, 0)
        ks_scr[l:, :] = ckv_ref[0, 0, 0, 0].astype(BF16)
        vt_scr[:, l:] = ckv_ref[0, 0, 1, 0].T.astype(BF16)

    q = _rope(q_ref[0], cq_ref[...], saq_ref[...], sbq_ref[...]) * (scale * math.log2(math.e))
    m1, m2 = _map_masks()
    tq = q.shape[0]
    qm = jnp.concatenate([q * m1, q * m2], axis=0).astype(BF16)
    m = l_sum = acc = None
    n_blk = ks_scr.shape[0] // key_block
    block = lambda blk: slice(blk * key_block, (blk + 1) * key_block)
    scores = [_dot_nt(ks_scr[block(b), :], qm) for b in range(min(ahead, n_blk))]
    for blk in range(n_blk):
        rows = block(blk)
        s = scores.pop(0)
        if blk + ahead < n_blk:
            scores.append(_dot_nt(ks_scr[block(blk + ahead), :], qm))
        m_blk = jnp.max(s, axis=0, keepdims=True)
        if blk == 0:
            m = m_blk
            p = jnp.exp2(s - m)
            l_sum = jnp.sum(p, axis=0, keepdims=True)
            acc = _dot(vt_scr[:, rows], p)
        else:
            m_new = jnp.maximum(m, m_blk)
            alpha = jnp.exp2(m - m_new)
            p = jnp.exp2(s - m_new)
            l_sum = alpha * l_sum + jnp.sum(p, axis=0, keepdims=True)
            acc = alpha * acc + _dot(vt_scr[:, rows], p)
            m = m_new
    out = acc / l_sum
    d = out[:, :tq] - lam_ref[...] * out[:, tq:]
    d = d * lax.rsqrt(jnp.mean(d * d, axis=0, keepdims=True) + EPS)
    o = d.T * gn_ref[...] * out_scale
    y_ref[0] = (o * _silu(g_ref[0])).astype(y_ref.dtype)


def _lat_diff(proj3, cache_diff_kv, layer, lam, diff_norm, lam_init, rope_tabs, tq=1024):
    b, _, l, _ = proj3.shape
    past = cache_diff_kv.shape[4]
    qblk = lambda c: pl.BlockSpec((1, None, tq, D_HEAD), lambda i, h, j, c=c: (i, c + h, j, 0))
    full = lambda c: pl.BlockSpec((1, None, l, D_HEAD), lambda i, h, j, c=c: (i, c + h, 0, 0))
    tq_tab = pl.BlockSpec((tq, D_HEAD), lambda i, h, j: (j, 0))
    full_tab = pl.BlockSpec((l, D_HEAD), lambda i, h, j: (0, 0))
    return pl.pallas_call(
        functools.partial(_lat_diff_kernel, out_scale=1.0 - lam_init, prep_rows=512, key_block=512, ahead=2),
        grid=(b, N_HEAD, l // tq),
        in_specs=[pl.BlockSpec((1, 1), lambda i, h, j: (0, 0)),
                  qblk(48), full(52), full(56), qblk(60),
                  pl.BlockSpec((1, 1, 2, 1, past, D_HEAD), lambda i, h, j: (i, layer, 0, h, 0, 0)),
                  pl.BlockSpec((1, D_HEAD), lambda i, h, j: (0, 0)),
                  tq_tab, tq_tab, tq_tab, full_tab, full_tab, full_tab],
        out_specs=pl.BlockSpec((1, tq, D_HEAD), lambda i, h, j: (i, j, h)),
        out_shape=jax.ShapeDtypeStruct((b, l, BR_W), BF16),
        scratch_shapes=[pltpu.VMEM((l + past, D_HEAD), BF16), pltpu.VMEM((D_HEAD, l + past), BF16)],
        compiler_params=_cparams("parallel", "parallel", "arbitrary"),
        name="lat_diff",
    )(lam, proj3, proj3, proj3, proj3, cache_diff_kv, diff_norm, *rope_tabs, *rope_tabs)


def _dwconv3(x, w_ref):
    l = x.shape[0]
    row = lax.broadcasted_iota(jnp.int32, x.shape, 0)
    prev = jnp.where(row == 0, 0.0, pltpu.roll(x, 1, 0))
    nxt = jnp.where(row == l - 1, 0.0, pltpu.roll(x, l - 1, 0))
    return prev * w_ref[0:1, :] + x * w_ref[1:2, :] + nxt * w_ref[2:3, :]


def _hy_pre_kernel(x_ref, above_ref, below_ref, w_ref, o_ref, ob_ref):
    t, n_t = pl.program_id(1), pl.num_programs(1)
    bt, cblocks, rows, _ = x_ref.shape
    row = lax.broadcasted_iota(jnp.int32, (rows, D_HEAD), 0)
    for bb in range(bt):
        for c in range(cblocks):
            cols = _head_cols(c)
            x = x_ref[bb, c]
            before = jnp.where(t == 0, 0.0, above_ref[bb, c, SUBLANES - 1:SUBLANES, :])
            after = jnp.where(t == n_t - 1, 0.0, below_ref[bb, c, 0:1, :])
            prev = jnp.where(row == 0, before, pltpu.roll(x, 1, 0))
            nxt = jnp.where(row == rows - 1, after, pltpu.roll(x, rows - 1, 0))
            y = prev * w_ref[0:1, cols] + x * w_ref[1:2, cols] + nxt * w_ref[2:3, cols]
            o_ref[bb, :, cols] = y
            ob_ref[bb, :, cols] = y.astype(BF16)


def _hy_pre(proj3, conv_w, tl=1024):
    b, _, l, _ = proj3.shape
    tl = min(tl, l)
    bt = _seqs_per_step(b, l)
    n = 3
    cb = BR_W // D_HEAD
    col0 = 4
    groups = tl // SUBLANES
    last_group = l // SUBLANES - 1
    spec = pl.BlockSpec((bt, tl, BR_W), lambda i, t, j: (i, t, j))
    return pl.pallas_call(
        _hy_pre_kernel,
        grid=(b // bt, l // tl, n),
        in_specs=[pl.BlockSpec((bt, cb, tl, D_HEAD), lambda i, t, j: (i, col0 + j, t, 0)),
                  pl.BlockSpec((bt, cb, SUBLANES, D_HEAD),
                               lambda i, t, j: (i, col0 + j, jnp.maximum(t * groups - 1, 0), 0)),
                  pl.BlockSpec((bt, cb, SUBLANES, D_HEAD),
                               lambda i, t, j: (i, col0 + j, jnp.minimum((t + 1) * groups, last_group), 0)),
                  pl.BlockSpec((3, BR_W), lambda i, t, j: (0, j))],
        out_specs=[spec, spec],
        out_shape=[jax.ShapeDtypeStruct((b, l, 3 * BR_W), F32),
                   jax.ShapeDtypeStruct((b, l, 3 * BR_W), BF16)],
        compiler_params=_cparams("parallel", "parallel", "parallel"),
        name="hy_pre",
    )(proj3, proj3, proj3, conv_w)


def _dot_hi(a, b):
    return jnp.dot(a, b, preferred_element_type=F32, precision=lax.Precision.HIGHEST)


def _hy_filter_kernel(feat_ref, dist_ref, w1_ref, b1_ref, w2_ref, b2_ref, w3_ref, b3_ref, dec_ref, o_ref):
    hid = jnp.sin(_dot_hi(feat_ref[...], w1_ref[...]) + b1_ref[...])
    hid = jnp.sin(_dot_hi(hid, w2_ref[...]) + b2_ref[...])
    dist = dist_ref[...]
    for j in range(o_ref.shape[1] // D_HEAD):
        cols = slice(j * D_HEAD, (j + 1) * D_HEAD)
        filt = _dot_hi(hid, w3_ref[:, cols]) + b3_ref[:, cols]
        o_ref[:, cols] = (filt * jnp.exp(-dist * jnp.abs(dec_ref[:, cols]))).astype(o_ref.dtype)


def _hy_filter(l, w1, b1, w2, b2, w3, b3, decay):
    pos = jnp.arange(l, dtype=F32)
    t = pos / l
    ang = (2.0 * math.pi) * t[:, None] * jnp.arange(1, HY_BANDS + 1, dtype=F32)
    feat = jnp.concatenate([t[:, None], jnp.cos(ang), jnp.sin(ang)], axis=-1)
    dist = jnp.broadcast_to((jnp.abs(pos - l // 2) / l)[:, None], (l, D_HEAD))
    pad = D_HEAD
    emb, ff = w1.shape
    feat = jnp.pad(feat, ((0, 0), (0, pad - emb)))
    w1p = jnp.pad(w1, ((0, pad - emb), (0, pad - ff)))
    w2p = jnp.pad(w2, ((0, pad - ff), (0, pad - ff)))
    w3p = jnp.pad(w3, ((0, pad - ff), (0, 0)))
    b1p = jnp.pad(b1, (0, pad - ff)).reshape(1, pad)
    b2p = jnp.pad(b2, (0, pad - ff)).reshape(1, pad)
    tl = min(l, 256)
    n = 2 * BR_W
    fixed = lambda shape: pl.BlockSpec(shape, lambda i: (0, 0))
    return pl.pallas_call(
        _hy_filter_kernel,
        grid=(l // tl,),
        in_specs=[pl.BlockSpec((tl, pad), lambda i: (i, 0)),
                  pl.BlockSpec((tl, D_HEAD), lambda i: (i, 0)),
                  fixed((pad, pad)), fixed((1, pad)), fixed((pad, pad)), fixed((1, pad)),
                  fixed((pad, n)), fixed((1, n)), fixed((1, n))],
        out_specs=pl.BlockSpec((tl, n), lambda i: (i, 0)),
        out_shape=jax.ShapeDtypeStruct((l, n), BF16),
        compiler_params=_cparams("parallel"),
        name="hy_filter",
    )(feat, dist, w1p, b1p, w2p, b2p, w3p, b3.reshape(1, n), decay.reshape(1, n))


def _dft_matrices(l):
    n = 2 * l
    k = jnp.arange(l, dtype=jnp.int32)
    t = jnp.arange(l, dtype=jnp.int32)
    tp = t + l // 2
    split = 1 << (max(l.bit_length() - 1, 0) // 2)

    def tables(rows, cols):
        def table(r):
            ang = (2.0 * math.pi / n) * ((r[:, None] * cols[None, :]) % n).astype(F32)
            return jnp.cos(ang), jnp.sin(ang)
        return (*table(rows[::split]), *table(rows[:split] - rows[0]))

    alt = jnp.where(t % 2 == 0, 1.0, -1.0).astype(F32).reshape(1, l)
    wk = (jnp.where(k == 0, 1.0, 2.0).astype(F32) / n).reshape(1, l)
    out = pl.pallas_call(
        functools.partial(_dft_gen_kernel, split=split, l=l),
        grid=(l // split,),
        in_specs=[pl.BlockSpec((l // split, l), lambda i: (0, 0))] * 2 + [pl.BlockSpec((split, l), lambda i: (0, 0))] * 2
        + [pl.BlockSpec((l // split, l), lambda i: (0, 0))] * 2 + [pl.BlockSpec((split, l), lambda i: (0, 0))] * 2
        + [pl.BlockSpec((1, l), lambda i: (0, 0))] * 2,
        out_specs=[pl.BlockSpec((split, l), lambda i: (i, 0))] * 4,
        out_shape=[jax.ShapeDtypeStruct((l, l), BF16)] * 4,
        compiler_params=_cparams("parallel"),
        name="dft_gen",
    )(*tables(k, t), *tables(tp, k), alt, wk)
    return (out[0], out[1]), (out[2], out[3])


def _dft_gen_kernel(ch_ref, sh_ref, cl_ref, sl_ref, chi_ref, shi_ref, cli_ref, sli_ref, alt_ref, wk_ref,
                    fc_ref, fs_ref, ic_ref, is_ref, *, split, l):
    i = pl.program_id(0)

    def cos_sin(c_hi, s_hi, c_lo, s_lo):
        ch, sh = c_hi[pl.ds(i, 1), :], s_hi[pl.ds(i, 1), :]
        return ch * c_lo[...] - sh * s_lo[...], sh * c_lo[...] + ch * s_lo[...]

    row = lax.broadcasted_iota(jnp.int32, (split, l), 0) + i * split
    col = lax.broadcasted_iota(jnp.int32, (split, l), 1)
    c, s = cos_sin(ch_ref, sh_ref, cl_ref, sl_ref)
    fc_ref[...] = c.astype(BF16)
    fs_ref[...] = jnp.where(row == 0, alt_ref[...], -s).astype(BF16)
    c, s = cos_sin(chi_ref, shi_ref, cli_ref, sli_ref)
    wk = wk_ref[...]
    alt_i = jnp.where(row % 2 == 0, 1.0, -1.0) * (1.0 / (2 * l))
    ic_ref[...] = (c * wk).astype(BF16)
    is_ref[...] = jnp.where(col == 0, alt_i, -s * wk).astype(BF16)


def _seqs_per_step(b, l, rows=2048):
    bt = max(1, min(b, rows // l))
    while b % bt:
        bt -= 1
    return bt


def _dft_fwd_kernel(fc_ref, fs_ref, x_ref, *rest, with_filter, tm):
    for bb in range(x_ref.shape[0]):
        x = x_ref[bb]
        ur = jnp.dot(fc_ref[...], x, preferred_element_type=F32)
        ui = jnp.dot(fs_ref[...], x, preferred_element_type=F32)
        if not with_filter:
            zr_ref, zi_ref = rest
            zr_ref[bb] = ur
            zi_ref[bb] = ui
            continue
        hr_ref, hi_ref, zr_ref, zi_ref = rest
        hr, hi = hr_ref[0], hi_ref[0]
        row0 = (lax.broadcasted_iota(jnp.int32, ur.shape, 0) + pl.program_id(0) * tm) == 0
        zr_ref[bb] = (ur * hr - jnp.where(row0, 0.0, ui * hi)).astype(zr_ref.dtype)
        zi_ref[bb] = jnp.where(row0, ui * hi, ur * hi + ui * hr).astype(zi_ref.dtype)


def _dft_fwd(fwd, x, x_col0, c, spec_h=None, h_col0=0, tm=512, tn=512):
    b, l, _ = x.shape
    tm = min(tm, l)
    bt = _seqs_per_step(b, l)
    xo, ho = x_col0 // tn, h_col0 // tn
    out_dtype = F32 if spec_h is None else BF16
    fspec = pl.BlockSpec((tm, l), lambda i, bb, j: (i, 0))
    in_specs = [fspec, fspec, pl.BlockSpec((bt, l, tn), lambda i, bb, j: (bb, 0, xo + j))]
    args = [*fwd, x]
    if spec_h is not None:
        hspec = pl.BlockSpec((1, tm, tn), lambda i, bb, j: (0, i, ho + j))
        in_specs += [hspec, hspec]
        args += list(spec_h)
    ospec = pl.BlockSpec((bt, tm, tn), lambda i, bb, j: (bb, i, j))
    return pl.pallas_call(
        functools.partial(_dft_fwd_kernel, with_filter=spec_h is not None, tm=tm),
        grid=(l // tm, b // bt, c // tn),
        in_specs=in_specs,
        out_specs=[ospec, ospec],
        out_shape=[jax.ShapeDtypeStruct((b, l, c), out_dtype)] * 2,
        compiler_params=_cparams("parallel", "parallel", "parallel"),
        name="dft_fwd",
    )(*args)


def _dft_inv_kernel(ic_ref, is_ref, zr_ref, zi_ref, u_ref, m_ref, skip_ref, *rest, with_gate):
    for bb in range(zr_ref.shape[0]):
        y = (jnp.dot(ic_ref[...], zr_ref[bb], preferred_element_type=F32)
             + jnp.dot(is_ref[...], zi_ref[bb], preferred_element_type=F32))
        z = m_ref[bb] * (y + u_ref[bb] * skip_ref[...])
        if with_gate:
            g_ref, o_ref = rest
            gate = jnp.concatenate([g_ref[bb, c] for c in range(g_ref.shape[1])], axis=-1)
            o_ref[bb] = (z * _silu(gate)).astype(o_ref.dtype)
        else:
            o_ref, ob_ref = rest
            o_ref[bb] = z
            ob_ref[bb] = z.astype(BF16)


def _dft_inv(inv, zr, zi, u, u_col0, mul, mul_col0, skip, gate=None, gate_col0=0, tm=512, tn=512):
    b, l, c = zr.shape
    tm = min(tm, l)
    bt = _seqs_per_step(b, l)
    win = lambda col0: pl.BlockSpec((bt, tm, tn), lambda i, bb, j, o=col0 // tn: (bb, i, o + j))
    zspec = pl.BlockSpec((bt, l, tn), lambda i, bb, j: (bb, 0, j))
    fspec = pl.BlockSpec((tm, l), lambda i, bb, j: (i, 0))
    in_specs = [fspec, fspec, zspec, zspec,
                win(u_col0), win(mul_col0), pl.BlockSpec((1, tn), lambda i, bb, j: (0, j))]
    args = [*inv, zr, zi, u, mul, skip]
    ospec = pl.BlockSpec((bt, tm, tn), lambda i, bb, j: (bb, i, j))
    if gate is not None:
        in_specs.append(pl.BlockSpec((bt, tn // D_HEAD, tm, D_HEAD),
                                     lambda i, bb, j, o=gate_col0 // tn: (bb, o + j, i, 0)))
        args.append(gate)
        out_specs, out_shape = ospec, jax.ShapeDtypeStruct((b, l, c), BF16)
    else:
        out_specs = [ospec, ospec]
        out_shape = [jax.ShapeDtypeStruct((b, l, c), F32), jax.ShapeDtypeStruct((b, l, c), BF16)]
    return pl.pallas_call(
        functools.partial(_dft_inv_kernel, with_gate=gate is not None),
        grid=(l // tm, b // bt, c // tn),
        in_specs=in_specs,
        out_specs=out_specs,
        out_shape=out_shape,
        compiler_params=_cparams("parallel", "parallel", "parallel"),
        name="dft_inv",
    )(*args)


def _hyena(proj3, p, dft):
    l = proj3.shape[2]
    fwd, inv = dft
    filt = _hy_filter(l, p['hy_w1'], p['hy_b1'], p['hy_w2'], p['hy_b2'], p['hy_w3'], p['hy_b3'], p['hy_decay'])
    filt_b = filt[None]
    spec_h = _dft_fwd(fwd, filt_b, 0, 2 * BR_W)
    pre, pre_b = _hy_pre(proj3, p['hy_conv'])
    skip = p['hy_skip'].astype(F32)
    zr, zi = _dft_fwd(fwd, pre_b, 0, BR_W, spec_h, 0)
    z1, z1_b = _dft_inv(inv, zr, zi, pre, 0, pre, BR_W, skip[0:1])
    zr, zi = _dft_fwd(fwd, z1_b, 0, BR_W, spec_h, BR_W)
    return _dft_inv(inv, zr, zi, z1, 0, pre, 2 * BR_W, skip[1:2], gate=proj3, gate_col0=7 * BR_W)


def _softplus(x):
    return jnp.maximum(x, 0.0) + jnp.log1p(jnp.exp(-jnp.abs(x)))


def _split_bf16(x, parts):
    out = []
    for _ in range(parts - 1):
        piece = x.astype(BF16)
        out.append(piece)
        x = x - piece.astype(F32)
    out.append(x.astype(BF16))
    return out


def _bmm(a, b, hi=False):
    mm = lambda x, y: jnp.einsum('nij,njk->nik', x, y, preferred_element_type=F32)
    if not hi:
        return mm(a.astype(BF16), b.astype(BF16))
    (a1, a2), (b1, b2) = _split_bf16(a, 2), _split_bf16(b, 2)
    return mm(a1, b1) + (mm(a1, b2) + mm(a2, b1))


def _bmm_nt(a, b):
    return jnp.einsum('nid,njd->nij', a.astype(BF16), b.astype(BF16), preferred_element_type=F32)


TRI_BASE = 4


def _unit_tri_inverse(a, ri, ci):
    same = lambda w: (ri // w) == (ci // w)
    eye = (ri == ci).astype(F32)
    x = -jnp.where(same(TRI_BASE), a, 0.0)
    p = eye + x
    for _ in range(TRI_BASE.bit_length() - 2):
        x = _bmm(x, x, hi=True)
        p = p + _bmm(p, x, hi=True)
    w = TRI_BASE
    while w < a.shape[-1]:
        off = jnp.where(same(2 * w) & ~same(w), a, 0.0)
        p = p - _bmm(p, _bmm(off, p))
        w *= 2
    return p


def _gdn_prepare(q, k, v, ab, a_row, dt_row, head0, group):
    n, c, _ = q.shape
    two = lambda x: jnp.concatenate([x, x], axis=0)
    q, k, v, ab = two(q), two(k), two(v), two(ab)
    back3 = lambda shape: lax.broadcasted_iota(jnp.int32, shape, 0) >= n
    lane = lax.broadcasted_iota(jnp.int32, ab.shape, 2)
    bidx = lax.broadcasted_iota(jnp.int32, ab.shape, 0)
    head = head0 + jnp.where(bidx >= n, bidx - n, bidx) // group
    base = jnp.where(bidx >= n, 2 * N_HEAD, 0) + head
    g_all = -a_row * _softplus(ab + dt_row)
    g = jnp.sum(jnp.where(lane == base, g_all, 0.0), axis=2, keepdims=True)
    beta = jnp.sum(jnp.where(lane == base + N_HEAD, _sigmoid(ab), 0.0), axis=2, keepdims=True)

    sq = (2 * n, c, c)
    ri = lax.broadcasted_iota(jnp.int32, sq, 1)
    ci = lax.broadcasted_iota(jnp.int32, sq, 2)
    ahead = jnp.where(back3(sq), ci - ri, ri - ci)
    incl = ahead >= 0
    strict = ahead > 0
    tri = jnp.where(incl, 1.0, 0.0).astype(BF16)
    gc = sum(jnp.einsum('nij,njk->nik', tri, piece, preferred_element_type=F32)
             for piece in _split_bf16(jnp.broadcast_to(g, q.shape), 3))
    gc_row = jnp.swapaxes(gc, 1, 2)[:, :c, :]
    total = jnp.where(back3((2 * n, 1, D_HEAD)), gc[:, 0:1, :], gc[:, c - 1:c, :])
    decay = jnp.where(incl, jnp.exp(jnp.where(incl, gc[:, :, :c] - gc_row, 0.0)), 0.0)

    kb = k * beta
    a = jnp.where(strict, _bmm_nt(kb, k) * decay, 0.0)
    t = _unit_tri_inverse(a, ri, ci)
    e = jnp.exp(gc)
    u = _bmm(t, v * beta)
    w = _bmm(t, kb * e)
    a_intra = jnp.where(incl, _bmm_nt(q, k) * decay, 0.0)
    return (u, w.astype(BF16), (q * e).astype(BF16), (k * jnp.exp(total - gc)).astype(BF16),
            a_intra.astype(BF16), jnp.exp(total))


def _gdn_kernel(*refs, aliased, has_s0, group):
    if aliased:
        refs = refs[1:]
    if has_s0:
        (q_ref, k_ref, v_ref, z_ref, ab_ref, wq_ref, wk_ref, wv_ref, arow_ref, dt_ref, gn_ref, s0_ref,
         y_ref, sf_ref, qn, kn, vn, u_s, w_s, qd_s, kd_s, ai_s, gl_s) = refs
    else:
        (q_ref, k_ref, v_ref, z_ref, ab_ref, wq_ref, wk_ref, wv_ref, arow_ref, dt_ref, gn_ref,
         y_ref, sf_ref, qn, kn, vn, u_s, w_s, qd_s, kd_s, ai_s, gl_s) = refs
    _, heads, l, _ = q_ref.shape
    head0 = pl.program_id(1) * heads
    n_chunks = l // CHUNK
    hcols = lambda hh: slice(hh * D_HEAD, (hh + 1) * D_HEAD)

    def l2n(x):
        return x * lax.rsqrt(jnp.sum(x * x, axis=-1, keepdims=True) + EPS)

    for hh in range(heads):
        cols = hcols(hh)
        qn[:, cols] = l2n(_silu(_dwconv3(q_ref[0, hh], wq_ref.at[:, cols]))) * (D_HEAD ** -0.5)
        kn[:, cols] = l2n(_silu(_dwconv3(k_ref[0, hh], wk_ref.at[:, cols])))
        vn[:, cols] = _silu(_dwconv3(v_ref[0, hh], wv_ref.at[:, cols]))

    a_row, dt_row = arow_ref[...], dt_ref[...]

    def prepare(gi, carry):
        span = group * CHUNK
        rows = pl.ds(pl.multiple_of(gi * span, span), span)
        chunks = lambda x: x.reshape(group, CHUNK, x.shape[-1])
        per_head = lambda ref: jnp.concatenate([chunks(ref[rows, hcols(hh)]) for hh in range(heads)], axis=0)
        ab = chunks(ab_ref[0, rows, :])
        u, w, qd, kd, ai, gl = _gdn_prepare(per_head(qn), per_head(kn), per_head(vn),
                                            jnp.concatenate([ab] * heads, axis=0), a_row, dt_row, head0, group)
        for d in range(2):
            for hh in range(heads):
                cols = hcols(hh)
                part = slice((d * heads + hh) * group, (d * heads + hh + 1) * group)
                u_s[d, rows, cols] = u[part].reshape(span, D_HEAD)
                w_s[d, rows, cols] = w[part].reshape(span, D_HEAD)
                qd_s[d, rows, cols] = qd[part].reshape(span, D_HEAD)
                kd_s[d, rows, cols] = kd[part].reshape(span, D_HEAD)
                ai_s[d, hh, rows, :] = ai[part].reshape(span, CHUNK)
                gl_s[d, hh, pl.ds(gi * group, group)] = jnp.broadcast_to(gl[part], (group,) + gl_s.shape[3:])
        return carry

    lax.fori_loop(0, n_chunks // group, prepare, 0)

    def scan(i, s):
        where = [(hh, d, pl.ds(pl.multiple_of(chunk * CHUNK, CHUNK), CHUNK), chunk)
                 for hh in range(heads) for d, chunk in ((0, i), (1, n_chunks - 1 - i))]
        gather = lambda ref: jnp.stack([ref[d, rows, hcols(hh)] for hh, d, rows, _ in where])
        a_intra = jnp.stack([ai_s[d, hh, rows, :] for hh, d, rows, _ in where])
        decay = jnp.stack([gl_s[d, hh, chunk][0:1, :] for hh, d, _, chunk in where])
        sb = s.astype(BF16)
        v_new = gather(u_s) - _bmm(gather(w_s), sb)
        vb = v_new.astype(BF16)
        o = _bmm(gather(qd_s), sb) + _bmm(a_intra, vb)
        for idx, (hh, d, rows, _) in enumerate(where):
            u_s[d, rows, hcols(hh)] = o[idx]
        return s * decay + jnp.einsum('nik,niv->nkv', gather(kd_s), vb, preferred_element_type=F32)

    if has_s0:
        init = jnp.stack([s0_ref[0, 0, d, hh] for hh in range(heads) for d in range(2)])
    else:
        init = jnp.zeros((2 * heads, D_HEAD, D_HEAD), F32)
    final = lax.fori_loop(0, n_chunks, scan, init)
    for hh in range(heads):
        cols = hcols(hh)
        sf_ref[0, 0, 0, hh] = final[2 * hh]
        sf_ref[0, 0, 1, hh] = final[2 * hh + 1]
        y_ref[0, :, cols] = (_rms(u_s[0, :, cols] + u_s[1, :, cols], gn_ref[...])
                             * _silu(z_ref[0, hh])).astype(y_ref.dtype)


def _gdn(proj3, ab3, conv_w, a_log, dt_bias, norm_g, layer, state=None, new_state=None):
    b, _, l, _ = proj3.shape
    depth_out, layer_out = (1, 0) if state is not None else (DEPTH, layer)
    aliased = new_state is not None
    lanes = jnp.zeros((2, 2 * N_HEAD), F32).at[:, :N_HEAD].set(1.0)
    a_row = jnp.pad((jnp.exp(a_log.astype(F32))[:, None, :] * lanes.reshape(2, 2, N_HEAD)).reshape(1, -1),
                    ((0, 0), (0, AB_PAD - 4 * N_HEAD)))
    dt_row = jnp.pad((dt_bias.astype(F32)[:, None, :] * lanes.reshape(2, 2, N_HEAD)).reshape(1, -1),
                     ((0, 0), (0, AB_PAD - 4 * N_HEAD)))
    hps = N_HEAD if l <= 512 else 1
    wid = hps * D_HEAD
    n_hb = N_HEAD // hps
    blk = lambda c: pl.BlockSpec((1, hps, l, D_HEAD), lambda i, h, c=c: (i, c * n_hb + h, 0, 0))
    wblk = lambda c: pl.BlockSpec((3, wid), lambda i, h, c=c: (0, c * n_hb + h))
    row = pl.BlockSpec((1, D_HEAD), lambda i, h: (0, 0))
    in_specs = [blk(0), blk(1), blk(2), blk(3),
                pl.BlockSpec((1, l, AB_PAD), lambda i, h: (i, 0, 0)),
                wblk(0), wblk(1), wblk(2), row, row, row]
    args = [proj3, proj3, proj3, proj3, ab3, conv_w, conv_w, conv_w, a_row, dt_row, norm_g]
    if aliased:
        in_specs.insert(0, pl.BlockSpec(memory_space=pl.ANY))
        args.insert(0, new_state)
    if state is not None:
        in_specs.append(pl.BlockSpec((1, 1, 2, hps, D_HEAD, D_HEAD), lambda i, h: (i, layer, 0, h, 0, 0)))
        args.append(state)
    return pl.pallas_call(
        functools.partial(_gdn_kernel, aliased=aliased, has_s0=state is not None, group=min(8, l // CHUNK)),
        grid=(b, n_hb),
        in_specs=in_specs,
        out_specs=[pl.BlockSpec((1, l, wid), lambda i, h: (i, 0, h)),
                   pl.BlockSpec((1, 1, 2, hps, D_HEAD, D_HEAD), lambda i, h: (i, layer_out, 0, h, 0, 0))],
        out_shape=[jax.ShapeDtypeStruct((b, l, BR_W), BF16),
                   jax.ShapeDtypeStruct((b, depth_out, 2, N_HEAD, D_HEAD, D_HEAD), F32)],
        input_output_aliases={0: 1} if aliased else {},
        scratch_shapes=[pltpu.VMEM((l, wid), F32)] * 3
        + [pltpu.VMEM((2, l, wid), F32)] + [pltpu.VMEM((2, l, wid), BF16)] * 3
        + [pltpu.VMEM((2, hps, l, CHUNK), BF16), pltpu.VMEM((2, hps, l // CHUNK, 8, D_HEAD), F32)],
        compiler_params=_cparams("parallel", "parallel"),
        name="gdn",
    )(*args)


def _mod_kernel(c_ref, w_ref, b_ref, o_ref):
    o_ref[...] = _dot_hi(_silu(c_ref[...]), w_ref[...]) + b_ref[...]


def _modulation(cond, w_mod, b_mod, layer, tn=512):
    n = cond.shape[0]
    rows = 8
    out = pl.pallas_call(
        _mod_kernel,
        grid=(3 * D_MODEL // tn,),
        in_specs=[pl.BlockSpec((rows, D_MODEL), lambda j: (0, 0)),
                  pl.BlockSpec((None, D_MODEL, tn), lambda j: (layer, 0, j)),
                  pl.BlockSpec((1, tn), lambda j: (0, j))],
        out_specs=pl.BlockSpec((rows, tn), lambda j: (0, j)),
        out_shape=jax.ShapeDtypeStruct((rows, 3 * D_MODEL), F32),
        compiler_params=_cparams("parallel"),
        name="modulation",
    )(jnp.pad(cond.astype(F32), ((0, rows - n), (0, 0))), w_mod, b_mod.reshape(1, -1))
    return out[:n].reshape(n, 3, D_MODEL)


def _split_w_in(w_in):
    n_a = 4 * BR_W + 4 * N_HEAD
    w_in = w_in.astype(BF16)
    main = jnp.concatenate([w_in[..., :4 * BR_W], w_in[..., n_a:]], axis=-1)
    ab = jnp.pad(w_in[..., 4 * BR_W:n_a], ((0, 0),) * (w_in.ndim - 1) + ((0, AB_PAD - 4 * N_HEAD),))
    return main, ab


def _trunk_layer(x3, cond, p, big, layer, dft, latent, new_outputs=(None, None, None)):
    b, l, _ = x3.shape
    x2 = x3.reshape(b * l, D_MODEL)
    mod = _modulation(cond, big['w_mod'], p['b_mod'], layer)
    rows_per_mod = l if mod.shape[0] == b else b * l
    g_pre = p['g_pre'].reshape(1, D_MODEL)
    proj3, ab = _inproj(x2, mod, g_pre, big['w_main'], big['w_ab'], layer, rows_per_mod, l)
    ab3 = ab.reshape(b, l, AB_PAD)

    lam_init = 0.8 - 0.6 * math.exp(-0.3 * layer)
    lam_p = p['diff_lam'].astype(F32)
    lam = (jnp.exp(jnp.sum(lam_p[0] * lam_p[1])) - jnp.exp(jnp.sum(lam_p[2] * lam_p[3])) + lam_init).reshape(1, 1)
    diff_norm = p['diff_norm'].reshape(1, D_HEAD)
    gdn_args = (proj3, ab3, p['gdn_conv'], p['gdn_a_log'], p['gdn_dt_bias'], p['gdn_norm'].reshape(1, D_HEAD), layer)

    yb = _hyena(proj3, p, dft)
    if latent is None:
        new_state, nat_cache, diff_cache = new_outputs
        ya, new_state = _gdn(*gdn_args, new_state=new_state)
        yc, yd, nat_cache, diff_cache = _ctx_attention(proj3, lam, diff_norm, lam_init, layer, nat_cache, diff_cache)
        extras = (new_state, nat_cache, diff_cache)
    else:
        ya, _ = _gdn(*gdn_args, state=latent['state_gdn'])
        yc = _lat_nat(proj3, latent['cache_nat_kv'], layer, _nat_bias_table(p['nat_rpb']))
        yd = _lat_diff(proj3, latent['cache_diff_kv'], layer, lam, diff_norm, lam_init, latent['rope'])
        extras = None

    ys = [t.reshape(b * l, BR_W) for t in (ya, yb, yc, yd)]
    out = _merge(x2, mod, g_pre, p['g_post'].reshape(1, D_MODEL), ys, big['w_branch'], big['w_merge'],
                 p['b_merge'].reshape(1, -1).astype(F32), big['w_out'], layer, rows_per_mod)
    return out.reshape(b, l, D_MODEL), extras


def kernel(x_prompt, x_sample, state_gdn, cache_nat_kv, cache_diff_kv, c, c_ctx,
           w_mod, b_mod, g_pre, g_post, w_in, gdn_conv, gdn_a_log, gdn_dt_bias, gdn_norm,
           hy_conv, hy_w1, hy_b1, hy_w2, hy_b2, hy_w3, hy_b3, hy_decay, hy_skip,
           nat_rpb, diff_lam, diff_norm, w_branch, w_merge, b_merge, w_out):
    small = {
        'b_mod': b_mod, 'g_pre': g_pre, 'g_post': g_post,
        'gdn_conv': gdn_conv, 'gdn_a_log': gdn_a_log, 'gdn_dt_bias': gdn_dt_bias, 'gdn_norm': gdn_norm,
        'hy_conv': hy_conv, 'hy_w1': hy_w1, 'hy_b1': hy_b1, 'hy_w2': hy_w2, 'hy_b2': hy_b2,
        'hy_w3': hy_w3, 'hy_b3': hy_b3, 'hy_decay': hy_decay, 'hy_skip': hy_skip,
        'nat_rpb': nat_rpb, 'diff_lam': diff_lam, 'diff_norm': diff_norm, 'b_merge': b_merge,
    }
    layers = [{name: arr[i] for name, arr in small.items()} for i in range(DEPTH)]
    w_main, w_ab = _split_w_in(w_in)
    big = {'w_mod': w_mod.astype(F32), 'w_main': w_main, 'w_ab': w_ab, 'w_branch': w_branch.astype(BF16),
           'w_merge': w_merge.astype(BF16), 'w_out': w_out.astype(BF16)}

    y_prompt = x_prompt
    dft_ctx = _dft_matrices(x_prompt.shape[1])
    outputs = (None, None, None)
    for i, p in enumerate(layers):
        y_prompt, outputs = _trunk_layer(y_prompt, c_ctx.reshape(1, D_MODEL), p, big, i, dft_ctx, None, outputs)
    new_state, nat_cache, diff_cache = outputs

    y_sample = x_sample
    dft_lat = _dft_matrices(x_sample.shape[1])
    latent = {'state_gdn': state_gdn, 'cache_nat_kv': cache_nat_kv, 'cache_diff_kv': cache_diff_kv,
              'rope': _rope_tables(x_sample.shape[1])}
    for i, p in enumerate(layers):
        y_sample, _ = _trunk_layer(y_sample, c, p, big, i, dft_lat, latent)

    return (y_prompt, y_sample, new_state, nat_cache, diff_cache)
```

```python
import functools
import math

import jax
import jax.numpy as jnp
import numpy as np
from jax import lax
from jax.experimental import pallas as pl
from jax.experimental.pallas import tpu as pltpu

F32 = jnp.float32
BF16 = jnp.bfloat16

D_MODEL = 1024
DEPTH = 2
GRID_W = 64
N_BRANCH = 4
BR_W = 512
N_HEAD = 4
D_HEAD = 128
SUBLANES = 8
CHUNK = 128
HY_BANDS = 16
WIN_R = 8
WIN_C = 16
DQK_D = 64
ROPE_BASE = 10000.0
EPS = 1e-6
N_MAIN = 4 * 4 * BR_W
AB_PAD = 128
NEG_INF = -1e30

VMEM_LIMIT = 48 * 1024 * 1024


def _cparams(*sem):
    return pltpu.CompilerParams(dimension_semantics=sem, vmem_limit_bytes=VMEM_LIMIT)


def _silu(x):
    return x * (1.0 / (1.0 + jnp.exp(-x)))


def _sigmoid(x):
    return 1.0 / (1.0 + jnp.exp(-x))


def _rms(x, g):
    return x * lax.rsqrt(jnp.mean(x * x, axis=-1, keepdims=True) + EPS) * g


def _dot(a, b):
    return jnp.dot(a.astype(BF16), b.astype(BF16), preferred_element_type=F32)


def _dot_nt(a, b):
    return lax.dot_general(a.astype(BF16), b.astype(BF16), (((1,), (1,)), ((), ())),
                           preferred_element_type=F32)


def _dot_tn(a, b):
    return lax.dot_general(a.astype(BF16), b.astype(BF16), (((0,), (0,)), ((), ())),
                           preferred_element_type=F32)


def _prenorm(x, g_pre, mod_ref):
    return _rms(x, g_pre) * (1.0 + mod_ref[0, 1:2, :]) + mod_ref[0, 0:1, :]


def _inproj_kernel(x_ref, mod_ref, gpre_ref, w_ref, wab_ref, proj_ref, ab_ref, h_scr):
    @pl.when(pl.program_id(1) == 0)
    def _():
        h = _prenorm(x_ref[...], gpre_ref[...], mod_ref).astype(BF16)
        h_scr[...] = h
        ab_ref[...] = jnp.dot(h, wab_ref[...], preferred_element_type=F32)

    acc = jnp.dot(h_scr[...], w_ref[...], preferred_element_type=F32)
    seqs, cblocks, rows, _ = proj_ref.shape
    for sq in range(seqs):
        for c in range(cblocks):
            proj_ref[sq, c] = acc[sq * rows:(sq + 1) * rows, c * D_HEAD:(c + 1) * D_HEAD]


def _inproj(x2, mod, g_pre, w_main, w_ab, layer, rows_per_mod, l, tm=1024, tn=2048):
    m = x2.shape[0]
    tm = math.gcd(tm, rows_per_mod)
    if tm >= l:
        proj_spec = pl.BlockSpec((tm // l, tn // D_HEAD, l, D_HEAD), lambda i, j: (i, j, 0, 0))
    else:
        per = l // tm
        proj_spec = pl.BlockSpec((1, tn // D_HEAD, tm, D_HEAD), lambda i, j: (i // per, j, i % per, 0))
    return pl.pallas_call(
        _inproj_kernel,
        grid=(m // tm, N_MAIN // tn),
        in_specs=[
            pl.BlockSpec((tm, D_MODEL), lambda i, j: (i, 0)),
            pl.BlockSpec((1, 3, D_MODEL), lambda i, j: ((i * tm) // rows_per_mod, 0, 0)),
            pl.BlockSpec((1, D_MODEL), lambda i, j: (0, 0)),
            pl.BlockSpec((None, D_MODEL, tn), lambda i, j: (layer, 0, j)),
            pl.BlockSpec((None, D_MODEL, AB_PAD), lambda i, j: (layer, 0, 0)),
        ],
        out_specs=[
            proj_spec,
            pl.BlockSpec((tm, AB_PAD), lambda i, j: (i, 0)),
        ],
        out_shape=[jax.ShapeDtypeStruct((m // l, N_MAIN // D_HEAD, l, D_HEAD), F32),
                   jax.ShapeDtypeStruct((m, AB_PAD), F32)],
        scratch_shapes=[pltpu.VMEM((tm, D_MODEL), BF16)],
        compiler_params=_cparams("parallel", "arbitrary"),
        name="inproj",
    )(x2, mod, g_pre, w_main, w_ab)


def _merge_kernel(x_ref, mod_ref, gpre_ref, gpost_ref, ya_ref, yb_ref, yc_ref, yd_ref,
                  wbr_ref, wmg_ref, bmg_ref, wout_ref, o_ref):
    x = x_ref[...]
    h = _prenorm(x, gpre_ref[...], mod_ref).astype(BF16)
    acc = None
    for k, y_ref in enumerate((ya_ref, yb_ref, yc_ref, yd_ref)):
        cols = slice(k * D_MODEL, (k + 1) * D_MODEL)
        gate = _sigmoid(jnp.dot(h, wmg_ref[:, cols], preferred_element_type=F32) + bmg_ref[:, cols])
        br = jnp.dot(y_ref[...], wbr_ref[k], preferred_element_type=F32)
        acc = gate * br if acc is None else acc + gate * br
    y = jnp.dot(acc.astype(BF16), wout_ref[...], preferred_element_type=F32)
    o_ref[...] = x + mod_ref[0, 2:3, :] * _rms(y, gpost_ref[...])


def _merge(x2, mod, g_pre, g_post, ys, w_branch, w_merge, b_merge, w_out, layer, rows_per_mod, tm=256):
    m = x2.shape[0]
    row = lambda i: (i, 0)
    fixed2 = lambda i: (0, 0)
    return pl.pallas_call(
        _merge_kernel,
        grid=(m // tm,),
        in_specs=[
            pl.BlockSpec((tm, D_MODEL), row),
            pl.BlockSpec((1, 3, D_MODEL), lambda i: ((i * tm) // rows_per_mod, 0, 0)),
            pl.BlockSpec((1, D_MODEL), fixed2),
            pl.BlockSpec((1, D_MODEL), fixed2),
            pl.BlockSpec((tm, BR_W), row),
            pl.BlockSpec((tm, BR_W), row),
            pl.BlockSpec((tm, BR_W), row),
            pl.BlockSpec((tm, BR_W), row),
            pl.BlockSpec((None, N_BRANCH, BR_W, D_MODEL), lambda i: (layer, 0, 0, 0)),
            pl.BlockSpec((None, D_MODEL, N_BRANCH * D_MODEL), lambda i: (layer, 0, 0)),
            pl.BlockSpec((1, N_BRANCH * D_MODEL), fixed2),
            pl.BlockSpec((None, D_MODEL, D_MODEL), lambda i: (layer, 0, 0)),
        ],
        out_specs=pl.BlockSpec((tm, D_MODEL), row),
        out_shape=jax.ShapeDtypeStruct((m, D_MODEL), F32),
        compiler_params=_cparams("parallel"),
        name="merge",
    )(x2, mod, g_pre, g_post, *ys, w_branch, w_merge, b_merge, w_out)


def _softmax_rows(s):
    p = jnp.exp(s - jnp.max(s, axis=-1, keepdims=True))
    return p, jnp.sum(p, axis=-1, keepdims=True)


def _head_cols(h):
    return slice(h * D_HEAD, (h + 1) * D_HEAD)


def _stack_heads(ref):
    return ref[0]


def _ctx_nat_kernel(*refs, aliased):
    q_ref, k_ref, v_ref, g_ref, y_ref, kv_ref = refs[1:] if aliased else refs
    scale = D_HEAD ** -0.5
    q, k, v = (_stack_heads(r) for r in (q_ref, k_ref, v_ref))
    p, l = _softmax_rows(_bmm_nt(q, k) * scale)
    o = _bmm(p, v) / l
    for h in range(N_HEAD):
        sl = _head_cols(h)
        y_ref[0, :, sl] = (o[h] * _silu(g_ref[0, h])).astype(y_ref.dtype)
        kv_ref[0, 0, 0, h] = k[h]
        kv_ref[0, 0, 1, h] = v[h]


def _map_masks():
    lane = lax.broadcasted_iota(jnp.int32, (1, D_HEAD), 1)
    first = (lane < DQK_D).astype(F32)
    return first, 1.0 - first


def _ctx_diff_kernel(*refs, aliased, out_scale):
    lam_ref, q_ref, k_ref, v_ref, g_ref, gn_ref, y_ref, kv_ref = refs[1:] if aliased else refs
    scale = DQK_D ** -0.5
    m1, m2 = _map_masks()
    q, k, v = (_stack_heads(r) for r in (q_ref, k_ref, v_ref))
    p, l = _softmax_rows(_bmm_nt(jnp.concatenate([q * m1, q * m2], axis=0), jnp.concatenate([k, k], axis=0)) * scale)
    pn = p / l
    a = pn[:N_HEAD] - lam_ref[...] * pn[N_HEAD:]
    o = _rms(_bmm(a, v), gn_ref[...]) * out_scale
    for h in range(N_HEAD):
        sl = _head_cols(h)
        y_ref[0, :, sl] = (o[h] * _silu(g_ref[0, h])).astype(y_ref.dtype)
        kv_ref[0, 0, 0, h] = k[h]
        kv_ref[0, 0, 1, h] = v[h]


def _ctx_attention(proj3, lam, diff_norm, lam_init, layer, nat_cache, diff_cache):
    b, _, l, _ = proj3.shape
    blk = lambda c: pl.BlockSpec((1, N_HEAD, l, D_HEAD), lambda i, c=c: (i, c, 0, 0))
    y_spec = pl.BlockSpec((1, l, BR_W), lambda i: (i, 0, 0))
    kv_spec = pl.BlockSpec((1, 1, 2, N_HEAD, l, D_HEAD), lambda i: (i, layer, 0, 0, 0, 0))
    out_shape = [jax.ShapeDtypeStruct((b, l, BR_W), BF16),
                 jax.ShapeDtypeStruct((b, DEPTH, 2, N_HEAD, l, D_HEAD), F32)]
    aliased = nat_cache is not None
    cache_specs = [pl.BlockSpec(memory_space=pl.ANY)] if aliased else []
    aliases = {0: 1} if aliased else {}
    yc, nat_cache = pl.pallas_call(
        functools.partial(_ctx_nat_kernel, aliased=aliased),
        grid=(b,),
        in_specs=cache_specs + [blk(8), blk(9), blk(10), blk(11)],
        out_specs=[y_spec, kv_spec],
        out_shape=out_shape,
        input_output_aliases=aliases,
        compiler_params=_cparams("parallel"),
        name="ctx_nat",
    )(*([nat_cache] if aliased else []), proj3, proj3, proj3, proj3)
    yd, diff_cache = pl.pallas_call(
        functools.partial(_ctx_diff_kernel, aliased=aliased, out_scale=1.0 - lam_init),
        grid=(b,),
        in_specs=cache_specs + [pl.BlockSpec((1, 1), lambda i: (0, 0)),
                                blk(12), blk(13), blk(14), blk(15),
                                pl.BlockSpec((1, D_HEAD), lambda i: (0, 0))],
        out_specs=[y_spec, kv_spec],
        out_shape=out_shape,
        input_output_aliases=aliases,
        compiler_params=_cparams("parallel"),
        name="ctx_diff",
    )(*([diff_cache] if aliased else []), lam, proj3, proj3, proj3, proj3, diff_norm)
    return yc, yd, nat_cache, diff_cache


def _nat_bias_table(rpb):
    cols = np.arange(GRID_W)
    start = np.clip(cols - WIN_C // 2, 0, GRID_W - WIN_C)
    inside = (cols[None, :] >= start[:, None]) & (cols[None, :] < start[:, None] + WIN_C)
    dc = cols[None, :] - cols[:, None] + (WIN_C - 1)
    onehot = ((dc[None] == np.arange(2 * WIN_C - 1)[:, None, None]) & inside[None]).astype(np.float32)
    t = jnp.einsum('hdx,xck->hdck', rpb.astype(F32), jnp.asarray(onehot), precision=lax.Precision.HIGHEST)
    t = jnp.where(jnp.asarray(inside)[None, None], t, NEG_INF)
    tab = jnp.stack([t[:, WIN_R - 1 - off:2 * WIN_R - 1 - off] for off in range(WIN_R)], axis=1)
    return tab.transpose(0, 1, 3, 2, 4).reshape(rpb.shape[0], WIN_R, GRID_W, WIN_R * GRID_W)


def _lat_nat_kernel(q_ref, k_ref, v_ref, g_ref, ckv_ref, bias_ref, y_ref, kb_scr, vb_scr, *, rb):
    scale = D_HEAD ** -0.5
    rows = q_ref.shape[1] // GRID_W
    win = WIN_R * GRID_W
    kb_scr[...] = k_ref[0].astype(BF16)
    vb_scr[...] = v_ref[0].astype(BF16)
    ck = ckv_ref[0, 0, 0, 0].astype(BF16)
    cv = ckv_ref[0, 0, 1, 0].astype(BF16)

    def row_block(i, carry):
        q0 = pl.multiple_of(i * (rb * GRID_W), rb * GRID_W)
        qrows = pl.ds(q0, rb * GRID_W)
        q = q_ref[0, qrows, :].astype(BF16)
        kw, vw, bias = [], [], []
        for j in range(rb):
            r = i * rb + j
            rs = jnp.clip(r - WIN_R // 2, 0, rows - WIN_R)
            wrows = pl.ds(pl.multiple_of(rs * GRID_W, GRID_W), win)
            kw.append(kb_scr[wrows, :])
            vw.append(vb_scr[wrows, :])
            bias.append(bias_ref[0, r - rs])
        q3 = q.reshape(rb, GRID_W, D_HEAD)
        s_lat = _bmm_nt(q3, jnp.stack(kw)) * scale + jnp.stack(bias)
        s_ctx = (_dot_nt(q, ck) * scale).reshape(rb, GRID_W, ck.shape[0])
        m = jnp.maximum(jnp.max(s_lat, axis=-1, keepdims=True), jnp.max(s_ctx, axis=-1, keepdims=True))
        p_lat = jnp.exp(s_lat - m)
        p_ctx = jnp.exp(s_ctx - m)
        l = jnp.sum(p_lat, axis=-1, keepdims=True) + jnp.sum(p_ctx, axis=-1, keepdims=True)
        o_ctx = _dot(p_ctx.reshape(rb * GRID_W, ck.shape[0]), cv).reshape(rb, GRID_W, D_HEAD)
        o = ((_bmm(p_lat, jnp.stack(vw)) + o_ctx) / l).reshape(rb * GRID_W, D_HEAD)
        y_ref[0, qrows, :] = (o * _silu(g_ref[0, qrows, :])).astype(y_ref.dtype)
        return carry

    lax.fori_loop(0, rows // rb, row_block, 0)


def _lat_nat(proj3, cache_nat_kv, layer, bias_tab):
    b, _, l, _ = proj3.shape
    past = cache_nat_kv.shape[4]
    blk = lambda c: pl.BlockSpec((1, None, l, D_HEAD), lambda i, h, c=c: (i, c + h, 0, 0))
    return pl.pallas_call(
        functools.partial(_lat_nat_kernel, rb=8),
        grid=(b, N_HEAD),
        in_specs=[blk(32), blk(36), blk(40), blk(44),
                  pl.BlockSpec((1, 1, 2, 1, past, D_HEAD), lambda i, h: (i, layer, 0, h, 0, 0)),
                  pl.BlockSpec((1, WIN_R, GRID_W, WIN_R * GRID_W), lambda i, h: (h, 0, 0, 0))],
        out_specs=pl.BlockSpec((1, l, D_HEAD), lambda i, h: (i, 0, h)),
        out_shape=jax.ShapeDtypeStruct((b, l, BR_W), BF16),
        scratch_shapes=[pltpu.VMEM((l, D_HEAD), BF16), pltpu.VMEM((l, D_HEAD), BF16)],
        compiler_params=_cparams("parallel", "parallel"),
        name="lat_nat",
    )(proj3, proj3, proj3, proj3, cache_nat_kv, bias_tab)


def _rope_tables(l):
    half = DQK_D // 2
    nf = half // 2
    t = jnp.arange(l)
    row = (t // GRID_W).astype(F32)
    col = (t % GRID_W).astype(F32)
    inv = ROPE_BASE ** (-jnp.arange(nf, dtype=F32) / nf)
    ang = jnp.concatenate([row[:, None] * inv, col[:, None] * inv], axis=-1)
    cos, sin = jnp.cos(ang), jnp.sin(ang)
    zero = jnp.zeros_like(sin)
    tile2 = lambda a, b: jnp.concatenate([a, b, a, b], axis=-1)
    return tile2(cos, cos), tile2(-sin, zero), tile2(zero, sin)


def _rope(x, cos, sin_a, sin_b):
    return x * cos + pltpu.roll(x, 96, 1) * sin_a + pltpu.roll(x, 32, 1) * sin_b


def _lat_diff_kernel(lam_ref, q_ref, k_ref, v_ref, g_ref, ckv_ref, gn_ref,
                     cq_ref, saq_ref, sbq_ref, ck_ref, sak_ref, sbk_ref,
                     y_ref, ks_scr, vt_scr, *, out_scale, prep_rows, key_block, ahead):
    scale = DQK_D ** -0.5
    l = k_ref.shape[1]

    @pl.when(pl.program_id(2) == 0)
    def _():
        def prep(i, carry):
            rows = pl.ds(pl.multiple_of(i * prep_rows, prep_rows), prep_rows)
            kr = _rope(k_ref[0, rows, :], ck_ref[rows, :], sak_ref[rows, :], sbk_ref[rows, :])
            ks_scr[rows, :] = kr.astype(BF16)
            vt_scr[:, rows] = v_ref[0, rows, :].T.astype(BF16)
            return carry

        lax.fori_loop(0, l // prep_rows, prep, 0)
        ks_scr[l:, :] = ckv_ref[0, 0, 0, 0].astype(BF16)
        vt_scr[:, l:] = ckv_ref[0, 0, 1, 0].T.astype(BF16)

    q = _rope(q_ref[0], cq_ref[...], saq_ref[...], sbq_ref[...]) * (scale * math.log2(math.e))
    m1, m2 = _map_masks()
    tq = q.shape[0]
    qm = jnp.concatenate([q * m1, q * m2], axis=0).astype(BF16)
    m = l_sum = acc = None
    n_blk = ks_scr.shape[0] // key_block
    block = lambda blk: slice(blk * key_block, (blk + 1) * key_block)
    scores = [_dot_nt(ks_scr[block(b), :], qm) for b in range(min(ahead, n_blk))]
    for blk in range(n_blk):
        rows = block(blk)
        s = scores.pop(0)
        if blk + ahead < n_blk:
            scores.append(_dot_nt(ks_scr[block(blk + ahead), :], qm))
        m_blk = jnp.max(s, axis=0, keepdims=True)
        if blk == 0:
            m = m_blk
            p = jnp.exp2(s - m)
            l_sum = jnp.sum(p, axis=0, keepdims=True)
            acc = _dot(vt_scr[:, rows], p)
        else:
            m_new = jnp.maximum(m, m_blk)
            alpha = jnp.exp2(m - m_new)
            p = jnp.exp2(s - m_new)
            l_sum = alpha * l_sum + jnp.sum(p, axis=0, keepdims=True)
            acc = alpha * acc + _dot(vt_scr[:, rows], p)
            m = m_new
    out = acc / l_sum
    d = out[:, :tq] - lam_ref[...] * out[:, tq:]
    d = d * lax.rsqrt(jnp.mean(d * d, axis=0, keepdims=True) + EPS)
    o = d.T * gn_ref[...] * out_scale
    y_ref[0] = (o * _silu(g_ref[0])).astype(y_ref.dtype)


def _lat_diff(proj3, cache_diff_kv, layer, lam, diff_norm, lam_init, rope_tabs, tq=1024):
    b, _, l, _ = proj3.shape
    past = cache_diff_kv.shape[4]
    qblk = lambda c: pl.BlockSpec((1, None, tq, D_HEAD), lambda i, h, j, c=c: (i, c + h, j, 0))
    full = lambda c: pl.BlockSpec((1, None, l, D_HEAD), lambda i, h, j, c=c: (i, c + h, 0, 0))
    tq_tab = pl.BlockSpec((tq, D_HEAD), lambda i, h, j: (j, 0))
    full_tab = pl.BlockSpec((l, D_HEAD), lambda i, h, j: (0, 0))
    return pl.pallas_call(
        functools.partial(_lat_diff_kernel, out_scale=1.0 - lam_init, prep_rows=512, key_block=512, ahead=2),
        grid=(b, N_HEAD, l // tq),
        in_specs=[pl.BlockSpec((1, 1), lambda i, h, j: (0, 0)),
                  qblk(48), full(52), full(56), qblk(60),
                  pl.BlockSpec((1, 1, 2, 1, past, D_HEAD), lambda i, h, j: (i, layer, 0, h, 0, 0)),
                  pl.BlockSpec((1, D_HEAD), lambda i, h, j: (0, 0)),
                  tq_tab, tq_tab, tq_tab, full_tab, full_tab, full_tab],
        out_specs=pl.BlockSpec((1, tq, D_HEAD), lambda i, h, j: (i, j, h)),
        out_shape=jax.ShapeDtypeStruct((b, l, BR_W), BF16),
        scratch_shapes=[pltpu.VMEM((l + past, D_HEAD), BF16), pltpu.VMEM((D_HEAD, l + past), BF16)],
        compiler_params=_cparams("parallel", "parallel", "arbitrary"),
        name="lat_diff",
    )(lam, proj3, proj3, proj3, proj3, cache_diff_kv, diff_norm, *rope_tabs, *rope_tabs)


def _dwconv3(x, w_ref):
    l = x.shape[0]
    row = lax.broadcasted_iota(jnp.int32, x.shape, 0)
    prev = jnp.where(row == 0, 0.0, pltpu.roll(x, 1, 0))
    nxt = jnp.where(row == l - 1, 0.0, pltpu.roll(x, l - 1, 0))
    return prev * w_ref[0:1, :] + x * w_ref[1:2, :] + nxt * w_ref[2:3, :]


def _hy_pre_kernel(x_ref, above_ref, below_ref, w_ref, o_ref, ob_ref):
    t, n_t = pl.program_id(1), pl.num_programs(1)
    bt, cblocks, rows, _ = x_ref.shape
    row = lax.broadcasted_iota(jnp.int32, (rows, D_HEAD), 0)
    for bb in range(bt):
        for c in range(cblocks):
            cols = _head_cols(c)
            x = x_ref[bb, c]
            before = jnp.where(t == 0, 0.0, above_ref[bb, c, SUBLANES - 1:SUBLANES, :])
            after = jnp.where(t == n_t - 1, 0.0, below_ref[bb, c, 0:1, :])
            prev = jnp.where(row == 0, before, pltpu.roll(x, 1, 0))
            nxt = jnp.where(row == rows - 1, after, pltpu.roll(x, rows - 1, 0))
            y = prev * w_ref[0:1, cols] + x * w_ref[1:2, cols] + nxt * w_ref[2:3, cols]
            o_ref[bb, :, cols] = y
            ob_ref[bb, :, cols] = y.astype(BF16)


def _hy_pre(proj3, conv_w, tl=1024):
    b, _, l, _ = proj3.shape
    tl = min(tl, l)
    bt = _seqs_per_step(b, l)
    n = 3
    cb = BR_W // D_HEAD
    col0 = 4
    groups = tl // SUBLANES
    last_group = l // SUBLANES - 1
    spec = pl.BlockSpec((bt, tl, BR_W), lambda i, t, j: (i, t, j))
    return pl.pallas_call(
        _hy_pre_kernel,
        grid=(b // bt, l // tl, n),
        in_specs=[pl.BlockSpec((bt, cb, tl, D_HEAD), lambda i, t, j: (i, col0 + j, t, 0)),
                  pl.BlockSpec((bt, cb, SUBLANES, D_HEAD),
                               lambda i, t, j: (i, col0 + j, jnp.maximum(t * groups - 1, 0), 0)),
                  pl.BlockSpec((bt, cb, SUBLANES, D_HEAD),
                               lambda i, t, j: (i, col0 + j, jnp.minimum((t + 1) * groups, last_group), 0)),
                  pl.BlockSpec((3, BR_W), lambda i, t, j: (0, j))],
        out_specs=[spec, spec],
        out_shape=[jax.ShapeDtypeStruct((b, l, 3 * BR_W), F32),
                   jax.ShapeDtypeStruct((b, l, 3 * BR_W), BF16)],
        compiler_params=_cparams("parallel", "parallel", "parallel"),
        name="hy_pre",
    )(proj3, proj3, proj3, conv_w)


def _dot_hi(a, b):
    return jnp.dot(a, b, preferred_element_type=F32, precision=lax.Precision.HIGHEST)


def _hy_filter_kernel(feat_ref, dist_ref, w1_ref, b1_ref, w2_ref, b2_ref, w3_ref, b3_ref, dec_ref, o_ref):
    hid = jnp.sin(_dot_hi(feat_ref[...], w1_ref[...]) + b1_ref[...])
    hid = jnp.sin(_dot_hi(hid, w2_ref[...]) + b2_ref[...])
    dist = dist_ref[...]
    for j in range(o_ref.shape[1] // D_HEAD):
        cols = slice(j * D_HEAD, (j + 1) * D_HEAD)
        filt = _dot_hi(hid, w3_ref[:, cols]) + b3_ref[:, cols]
        o_ref[:, cols] = (filt * jnp.exp(-dist * jnp.abs(dec_ref[:, cols]))).astype(o_ref.dtype)


def _hy_filter(l, w1, b1, w2, b2, w3, b3, decay):
    pos = jnp.arange(l, dtype=F32)
    t = pos / l
    ang = (2.0 * math.pi) * t[:, None] * jnp.arange(1, HY_BANDS + 1, dtype=F32)
    feat = jnp.concatenate([t[:, None], jnp.cos(ang), jnp.sin(ang)], axis=-1)
    dist = jnp.broadcast_to((jnp.abs(pos - l // 2) / l)[:, None], (l, D_HEAD))
    pad = D_HEAD
    emb, ff = w1.shape
    feat = jnp.pad(feat, ((0, 0), (0, pad - emb)))
    w1p = jnp.pad(w1, ((0, pad - emb), (0, pad - ff)))
    w2p = jnp.pad(w2, ((0, pad - ff), (0, pad - ff)))
    w3p = jnp.pad(w3, ((0, pad - ff), (0, 0)))
    b1p = jnp.pad(b1, (0, pad - ff)).reshape(1, pad)
    b2p = jnp.pad(b2, (0, pad - ff)).reshape(1, pad)
    tl = min(l, 256)
    n = 2 * BR_W
    fixed = lambda shape: pl.BlockSpec(shape, lambda i: (0, 0))
    return pl.pallas_call(
        _hy_filter_kernel,
        grid=(l // tl,),
        in_specs=[pl.BlockSpec((tl, pad), lambda i: (i, 0)),
                  pl.BlockSpec((tl, D_HEAD), lambda i: (i, 0)),
                  fixed((pad, pad)), fixed((1, pad)), fixed((pad, pad)), fixed((1, pad)),
                  fixed((pad, n)), fixed((1, n)), fixed((1, n))],
        out_specs=pl.BlockSpec((tl, n), lambda i: (i, 0)),
        out_shape=jax.ShapeDtypeStruct((l, n), BF16),
        compiler_params=_cparams("parallel"),
        name="hy_filter",
    )(feat, dist, w1p, b1p, w2p, b2p, w3p, b3.reshape(1, n), decay.reshape(1, n))


def _dft_matrices(l):
    n = 2 * l
    k = jnp.arange(l, dtype=jnp.int32)
    t = jnp.arange(l, dtype=jnp.int32)
    tp = t + l // 2
    split = 1 << (max(l.bit_length() - 1, 0) // 2)

    def tables(rows, cols):
        def table(r):
            ang = (2.0 * math.pi / n) * ((r[:, None] * cols[None, :]) % n).astype(F32)
            return jnp.cos(ang), jnp.sin(ang)
        return (*table(rows[::split]), *table(rows[:split] - rows[0]))

    alt = jnp.where(t % 2 == 0, 1.0, -1.0).astype(F32).reshape(1, l)
    wk = (jnp.where(k == 0, 1.0, 2.0).astype(F32) / n).reshape(1, l)
    out = pl.pallas_call(
        functools.partial(_dft_gen_kernel, split=split, l=l),
        grid=(l // split,),
        in_specs=[pl.BlockSpec((l // split, l), lambda i: (0, 0))] * 2 + [pl.BlockSpec((split, l), lambda i: (0, 0))] * 2
        + [pl.BlockSpec((l // split, l), lambda i: (0, 0))] * 2 + [pl.BlockSpec((split, l), lambda i: (0, 0))] * 2
        + [pl.BlockSpec((1, l), lambda i: (0, 0))] * 2,
        out_specs=[pl.BlockSpec((split, l), lambda i: (i, 0))] * 4,
        out_shape=[jax.ShapeDtypeStruct((l, l), BF16)] * 4,
        compiler_params=_cparams("parallel"),
        name="dft_gen",
    )(*tables(k, t), *tables(tp, k), alt, wk)
    return (out[0], out[1]), (out[2], out[3])


def _dft_gen_kernel(ch_ref, sh_ref, cl_ref, sl_ref, chi_ref, shi_ref, cli_ref, sli_ref, alt_ref, wk_ref,
                    fc_ref, fs_ref, ic_ref, is_ref, *, split, l):
    i = pl.program_id(0)

    def cos_sin(c_hi, s_hi, c_lo, s_lo):
        ch, sh = c_hi[pl.ds(i, 1), :], s_hi[pl.ds(i, 1), :]
        return ch * c_lo[...] - sh * s_lo[...], sh * c_lo[...] + ch * s_lo[...]

    row = lax.broadcasted_iota(jnp.int32, (split, l), 0) + i * split
    col = lax.broadcasted_iota(jnp.int32, (split, l), 1)
    c, s = cos_sin(ch_ref, sh_ref, cl_ref, sl_ref)
    fc_ref[...] = c.astype(BF16)
    fs_ref[...] = jnp.where(row == 0, alt_ref[...], -s).astype(BF16)
    c, s = cos_sin(chi_ref, shi_ref, cli_ref, sli_ref)
    wk = wk_ref[...]
    alt_i = jnp.where(row % 2 == 0, 1.0, -1.0) * (1.0 / (2 * l))
    ic_ref[...] = (c * wk).astype(BF16)
    is_ref[...] = jnp.where(col == 0, alt_i, -s * wk).astype(BF16)


def _seqs_per_step(b, l, rows=2048):
    bt = max(1, min(b, rows // l))
    while b % bt:
        bt -= 1
    return bt


def _dft_fwd_kernel(fc_ref, fs_ref, x_ref, *rest, with_filter, tm):
    for bb in range(x_ref.shape[0]):
        x = x_ref[bb]
        ur = jnp.dot(fc_ref[...], x, preferred_element_type=F32)
        ui = jnp.dot(fs_ref[...], x, preferred_element_type=F32)
        if not with_filter:
            zr_ref, zi_ref = rest
            zr_ref[bb] = ur
            zi_ref[bb] = ui
            continue
        hr_ref, hi_ref, zr_ref, zi_ref = rest
        hr, hi = hr_ref[0], hi_ref[0]
        row0 = (lax.broadcasted_iota(jnp.int32, ur.shape, 0) + pl.program_id(0) * tm) == 0
        zr_ref[bb] = (ur * hr - jnp.where(row0, 0.0, ui * hi)).astype(zr_ref.dtype)
        zi_ref[bb] = jnp.where(row0, ui * hi, ur * hi + ui * hr).astype(zi_ref.dtype)


def _dft_fwd(fwd, x, x_col0, c, spec_h=None, h_col0=0, tm=512, tn=512):
    b, l, _ = x.shape
    tm = min(tm, l)
    bt = _seqs_per_step(b, l)
    xo, ho = x_col0 // tn, h_col0 // tn
    out_dtype = F32 if spec_h is None else BF16
    fspec = pl.BlockSpec((tm, l), lambda i, bb, j: (i, 0))
    in_specs = [fspec, fspec, pl.BlockSpec((bt, l, tn), lambda i, bb, j: (bb, 0, xo + j))]
    args = [*fwd, x]
    if spec_h is not None:
        hspec = pl.BlockSpec((1, tm, tn), lambda i, bb, j: (0, i, ho + j))
        in_specs += [hspec, hspec]
        args += list(spec_h)
    ospec = pl.BlockSpec((bt, tm, tn), lambda i, bb, j: (bb, i, j))
    return pl.pallas_call(
        functools.partial(_dft_fwd_kernel, with_filter=spec_h is not None, tm=tm),
        grid=(l // tm, b // bt, c // tn),
        in_specs=in_specs,
        out_specs=[ospec, ospec],
        out_shape=[jax.ShapeDtypeStruct((b, l, c), out_dtype)] * 2,
        compiler_params=_cparams("parallel", "parallel", "parallel"),
        name="dft_fwd",
    )(*args)


def _dft_inv_kernel(ic_ref, is_ref, zr_ref, zi_ref, u_ref, m_ref, skip_ref, *rest, with_gate):
    for bb in range(zr_ref.shape[0]):
        y = (jnp.dot(ic_ref[...], zr_ref[bb], preferred_element_type=F32)
             + jnp.dot(is_ref[...], zi_ref[bb], preferred_element_type=F32))
        z = m_ref[bb] * (y + u_ref[bb] * skip_ref[...])
        if with_gate:
            g_ref, o_ref = rest
            gate = jnp.concatenate([g_ref[bb, c] for c in range(g_ref.shape[1])], axis=-1)
            o_ref[bb] = (z * _silu(gate)).astype(o_ref.dtype)
        else:
            o_ref, ob_ref = rest
            o_ref[bb] = z
            ob_ref[bb] = z.astype(BF16)


def _dft_inv(inv, zr, zi, u, u_col0, mul, mul_col0, skip, gate=None, gate_col0=0, tm=512, tn=512):
    b, l, c = zr.shape
    tm = min(tm, l)
    bt = _seqs_per_step(b, l)
    win = lambda col0: pl.BlockSpec((bt, tm, tn), lambda i, bb, j, o=col0 // tn: (bb, i, o + j))
    zspec = pl.BlockSpec((bt, l, tn), lambda i, bb, j: (bb, 0, j))
    fspec = pl.BlockSpec((tm, l), lambda i, bb, j: (i, 0))
    in_specs = [fspec, fspec, zspec, zspec,
                win(u_col0), win(mul_col0), pl.BlockSpec((1, tn), lambda i, bb, j: (0, j))]
    args = [*inv, zr, zi, u, mul, skip]
    ospec = pl.BlockSpec((bt, tm, tn), lambda i, bb, j: (bb, i, j))
    if gate is not None:
        in_specs.append(pl.BlockSpec((bt, tn // D_HEAD, tm, D_HEAD),
                                     lambda i, bb, j, o=gate_col0 // tn: (bb, o + j, i, 0)))
        args.append(gate)
        out_specs, out_shape = ospec, jax.ShapeDtypeStruct((b, l, c), BF16)
    else:
        out_specs = [ospec, ospec]
        out_shape = [jax.ShapeDtypeStruct((b, l, c), F32), jax.ShapeDtypeStruct((b, l, c), BF16)]
    return pl.pallas_call(
        functools.partial(_dft_inv_kernel, with_gate=gate is not None),
        grid=(l // tm, b // bt, c // tn),
        in_specs=in_specs,
        out_specs=out_specs,
        out_shape=out_shape,
        compiler_params=_cparams("parallel", "parallel", "parallel"),
        name="dft_inv",
    )(*args)


def _hyena(proj3, p, dft):
    l = proj3.shape[2]
    fwd, inv = dft
    filt = _hy_filter(l, p['hy_w1'], p['hy_b1'], p['hy_w2'], p['hy_b2'], p['hy_w3'], p['hy_b3'], p['hy_decay'])
    filt_b = filt[None]
    spec_h = _dft_fwd(fwd, filt_b, 0, 2 * BR_W)
    pre, pre_b = _hy_pre(proj3, p['hy_conv'])
    skip = p['hy_skip'].astype(F32)
    zr, zi = _dft_fwd(fwd, pre_b, 0, BR_W, spec_h, 0)
    z1, z1_b = _dft_inv(inv, zr, zi, pre, 0, pre, BR_W, skip[0:1])
    zr, zi = _dft_fwd(fwd, z1_b, 0, BR_W, spec_h, BR_W)
    return _dft_inv(inv, zr, zi, z1, 0, pre, 2 * BR_W, skip[1:2], gate=proj3, gate_col0=7 * BR_W)


def _softplus(x):
    return jnp.maximum(x, 0.0) + jnp.log1p(jnp.exp(-jnp.abs(x)))


def _split_bf16(x, parts):
    out = []
    for _ in range(parts - 1):
        piece = x.astype(BF16)
        out.append(piece)
        x = x - piece.astype(F32)
    out.append(x.astype(BF16))
    return out


def _bmm(a, b, hi=False):
    mm = lambda x, y: jnp.einsum('nij,njk->nik', x, y, preferred_element_type=F32)
    if not hi:
        return mm(a.astype(BF16), b.astype(BF16))
    (a1, a2), (b1, b2) = _split_bf16(a, 2), _split_bf16(b, 2)
    return mm(a1, b1) + (mm(a1, b2) + mm(a2, b1))


def _bmm_nt(a, b):
    return jnp.einsum('nid,njd->nij', a.astype(BF16), b.astype(BF16), preferred_element_type=F32)


TRI_BASE = 4


def _unit_tri_inverse(a, ri, ci):
    same = lambda w: (ri // w) == (ci // w)
    eye = (ri == ci).astype(F32)
    x = -jnp.where(same(TRI_BASE), a, 0.0)
    p = eye + x
    for _ in range(TRI_BASE.bit_length() - 2):
        x = _bmm(x, x, hi=True)
        p = p + _bmm(p, x, hi=True)
    w = TRI_BASE
    while w < a.shape[-1]:
        off = jnp.where(same(2 * w) & ~same(w), a, 0.0)
        p = p - _bmm(p, _bmm(off, p))
        w *= 2
    return p


def _gdn_prepare(q, k, v, ab, a_row, dt_row, head0, group):
    n, c, _ = q.shape
    two = lambda x: jnp.concatenate([x, x], axis=0)
    q, k, v, ab = two(q), two(k), two(v), two(ab)
    back3 = lambda shape: lax.broadcasted_iota(jnp.int32, shape, 0) >= n
    lane = lax.broadcasted_iota(jnp.int32, ab.shape, 2)
    bidx = lax.broadcasted_iota(jnp.int32, ab.shape, 0)
    head = head0 + jnp.where(bidx >= n, bidx - n, bidx) // group
    base = jnp.where(bidx >= n, 2 * N_HEAD, 0) + head
    g_all = -a_row * _softplus(ab + dt_row)
    g = jnp.sum(jnp.where(lane == base, g_all, 0.0), axis=2, keepdims=True)
    beta = jnp.sum(jnp.where(lane == base + N_HEAD, _sigmoid(ab), 0.0), axis=2, keepdims=True)

    sq = (2 * n, c, c)
    ri = lax.broadcasted_iota(jnp.int32, sq, 1)
    ci = lax.broadcasted_iota(jnp.int32, sq, 2)
    ahead = jnp.where(back3(sq), ci - ri, ri - ci)
    incl = ahead >= 0
    strict = ahead > 0
    tri = jnp.where(incl, 1.0, 0.0).astype(BF16)
    gc = sum(jnp.einsum('nij,njk->nik', tri, piece, preferred_element_type=F32)
             for piece in _split_bf16(jnp.broadcast_to(g, q.shape), 3))
    gc_row = jnp.swapaxes(gc, 1, 2)[:, :c, :]
    total = jnp.where(back3((2 * n, 1, D_HEAD)), gc[:, 0:1, :], gc[:, c - 1:c, :])
    decay = jnp.where(incl, jnp.exp(jnp.where(incl, gc[:, :, :c] - gc_row, 0.0)), 0.0)

    kb = k * beta
    a = jnp.where(strict, _bmm_nt(kb, k) * decay, 0.0)
    t = _unit_tri_inverse(a, ri, ci)
    e = jnp.exp(gc)
    u = _bmm(t, v * beta)
    w = _bmm(t, kb * e)
    a_intra = jnp.where(incl, _bmm_nt(q, k) * decay, 0.0)
    return (u, w.astype(BF16), (q * e).astype(BF16), (k * jnp.exp(total - gc)).astype(BF16),
            a_intra.astype(BF16), jnp.exp(total))


def _gdn_kernel(*refs, aliased, has_s0, group):
    if aliased:
        refs = refs[1:]
    if has_s0:
        (q_ref, k_ref, v_ref, z_ref, ab_ref, wq_ref, wk_ref, wv_ref, arow_ref, dt_ref, gn_ref, s0_ref,
         y_ref, sf_ref, qn, kn, vn, u_s, w_s, qd_s, kd_s, ai_s, gl_s) = refs
    else:
        (q_ref, k_ref, v_ref, z_ref, ab_ref, wq_ref, wk_ref, wv_ref, arow_ref, dt_ref, gn_ref,
         y_ref, sf_ref, qn, kn, vn, u_s, w_s, qd_s, kd_s, ai_s, gl_s) = refs
    _, heads, l, _ = q_ref.shape
    head0 = pl.program_id(1) * heads
    n_chunks = l // CHUNK
    hcols = lambda hh: slice(hh * D_HEAD, (hh + 1) * D_HEAD)

    def l2n(x):
        return x * lax.rsqrt(jnp.sum(x * x, axis=-1, keepdims=True) + EPS)

    for hh in range(heads):
        cols = hcols(hh)
        qn[:, cols] = l2n(_silu(_dwconv3(q_ref[0, hh], wq_ref.at[:, cols]))) * (D_HEAD ** -0.5)
        kn[:, cols] = l2n(_silu(_dwconv3(k_ref[0, hh], wk_ref.at[:, cols])))
        vn[:, cols] = _silu(_dwconv3(v_ref[0, hh], wv_ref.at[:, cols]))

    a_row, dt_row = arow_ref[...], dt_ref[...]

    def prepare(gi, carry):
        span = group * CHUNK
        rows = pl.ds(pl.multiple_of(gi * span, span), span)
        chunks = lambda x: x.reshape(group, CHUNK, x.shape[-1])
        per_head = lambda ref: jnp.concatenate([chunks(ref[rows, hcols(hh)]) for hh in range(heads)], axis=0)
        ab = chunks(ab_ref[0, rows, :])
        u, w, qd, kd, ai, gl = _gdn_prepare(per_head(qn), per_head(kn), per_head(vn),
                                            jnp.concatenate([ab] * heads, axis=0), a_row, dt_row, head0, group)
        for d in range(2):
            for hh in range(heads):
                cols = hcols(hh)
                part = slice((d * heads + hh) * group, (d * heads + hh + 1) * group)
                u_s[d, rows, cols] = u[part].reshape(span, D_HEAD)
                w_s[d, rows, cols] = w[part].reshape(span, D_HEAD)
                qd_s[d, rows, cols] = qd[part].reshape(span, D_HEAD)
                kd_s[d, rows, cols] = kd[part].reshape(span, D_HEAD)
                ai_s[d, hh, rows, :] = ai[part].reshape(span, CHUNK)
                gl_s[d, hh, pl.ds(gi * group, group)] = jnp.broadcast_to(gl[part], (group,) + gl_s.shape[3:])
        return carry

    lax.fori_loop(0, n_chunks // group, prepare, 0)

    def scan(i, s):
        where = [(hh, d, pl.ds(pl.multiple_of(chunk * CHUNK, CHUNK), CHUNK), chunk)
                 for hh in range(heads) for d, chunk in ((0, i), (1, n_chunks - 1 - i))]
        gather = lambda ref: jnp.stack([ref[d, rows, hcols(hh)] for hh, d, rows, _ in where])
        a_intra = jnp.stack([ai_s[d, hh, rows, :] for hh, d, rows, _ in where])
        decay = jnp.stack([gl_s[d, hh, chunk][0:1, :] for hh, d, _, chunk in where])
        sb = s.astype(BF16)
        v_new = gather(u_s) - _bmm(gather(w_s), sb)
        vb = v_new.astype(BF16)
        o = _bmm(gather(qd_s), sb) + _bmm(a_intra, vb)
        for idx, (hh, d, rows, _) in enumerate(where):
            u_s[d, rows, hcols(hh)] = o[idx]
        return s * decay + jnp.einsum('nik,niv->nkv', gather(kd_s), vb, preferred_element_type=F32)

    if has_s0:
        init = jnp.stack([s0_ref[0, 0, d, hh] for hh in range(heads) for d in range(2)])
    else:
        init = jnp.zeros((2 * heads, D_HEAD, D_HEAD), F32)
    final = lax.fori_loop(0, n_chunks, scan, init)
    for hh in range(heads):
        cols = hcols(hh)
        sf_ref[0, 0, 0, hh] = final[2 * hh]
        sf_ref[0, 0, 1, hh] = final[2 * hh + 1]
        y_ref[0, :, cols] = (_rms(u_s[0, :, cols] + u_s[1, :, cols], gn_ref[...])
                             * _silu(z_ref[0, hh])).astype(y_ref.dtype)


def _gdn(proj3, ab3, conv_w, a_log, dt_bias, norm_g, layer, state=None, new_state=None):
    b, _, l, _ = proj3.shape
    depth_out, layer_out = (1, 0) if state is not None else (DEPTH, layer)
    aliased = new_state is not None
    lanes = jnp.zeros((2, 2 * N_HEAD), F32).at[:, :N_HEAD].set(1.0)
    a_row = jnp.pad((jnp.exp(a_log.astype(F32))[:, None, :] * lanes.reshape(2, 2, N_HEAD)).reshape(1, -1),
                    ((0, 0), (0, AB_PAD - 4 * N_HEAD)))
    dt_row = jnp.pad((dt_bias.astype(F32)[:, None, :] * lanes.reshape(2, 2, N_HEAD)).reshape(1, -1),
                     ((0, 0), (0, AB_PAD - 4 * N_HEAD)))
    hps = N_HEAD if l <= 512 else 1
    wid = hps * D_HEAD
    n_hb = N_HEAD // hps
    blk = lambda c: pl.BlockSpec((1, hps, l, D_HEAD), lambda i, h, c=c: (i, c * n_hb + h, 0, 0))
    wblk = lambda c: pl.BlockSpec((3, wid), lambda i, h, c=c: (0, c * n_hb + h))
    row = pl.BlockSpec((1, D_HEAD), lambda i, h: (0, 0))
    in_specs = [blk(0), blk(1), blk(2), blk(3),
                pl.BlockSpec((1, l, AB_PAD), lambda i, h: (i, 0, 0)),
                wblk(0), wblk(1), wblk(2), row, row, row]
    args = [proj3, proj3, proj3, proj3, ab3, conv_w, conv_w, conv_w, a_row, dt_row, norm_g]
    if aliased:
        in_specs.insert(0, pl.BlockSpec(memory_space=pl.ANY))
        args.insert(0, new_state)
    if state is not None:
        in_specs.append(pl.BlockSpec((1, 1, 2, hps, D_HEAD, D_HEAD), lambda i, h: (i, layer, 0, h, 0, 0)))
        args.append(state)
    return pl.pallas_call(
        functools.partial(_gdn_kernel, aliased=aliased, has_s0=state is not None, group=min(512, l) // CHUNK),
        grid=(b, n_hb),
        in_specs=in_specs,
        out_specs=[pl.BlockSpec((1, l, wid), lambda i, h: (i, 0, h)),
                   pl.BlockSpec((1, 1, 2, hps, D_HEAD, D_HEAD), lambda i, h: (i, layer_out, 0, h, 0, 0))],
        out_shape=[jax.ShapeDtypeStruct((b, l, BR_W), BF16),
                   jax.ShapeDtypeStruct((b, depth_out, 2, N_HEAD, D_HEAD, D_HEAD), F32)],
        input_output_aliases={0: 1} if aliased else {},
        scratch_shapes=[pltpu.VMEM((l, wid), F32)] * 3
        + [pltpu.VMEM((2, l, wid), F32)] + [pltpu.VMEM((2, l, wid), BF16)] * 3
        + [pltpu.VMEM((2, hps, l, CHUNK), BF16), pltpu.VMEM((2, hps, l // CHUNK, 8, D_HEAD), F32)],
        compiler_params=_cparams("parallel", "parallel"),
        name="gdn",
    )(*args)


def _mod_kernel(c_ref, w_ref, b_ref, o_ref):
    o_ref[...] = _dot_hi(_silu(c_ref[...]), w_ref[...]) + b_ref[...]


def _modulation(cond, w_mod, b_mod, layer, tn=512):
    n = cond.shape[0]
    rows = 8
    out = pl.pallas_call(
        _mod_kernel,
        grid=(3 * D_MODEL // tn,),
        in_specs=[pl.BlockSpec((rows, D_MODEL), lambda j: (0, 0)),
                  pl.BlockSpec((None, D_MODEL, tn), lambda j: (layer, 0, j)),
                  pl.BlockSpec((1, tn), lambda j: (0, j))],
        out_specs=pl.BlockSpec((rows, tn), lambda j: (0, j)),
        out_shape=jax.ShapeDtypeStruct((rows, 3 * D_MODEL), F32),
        compiler_params=_cparams("parallel"),
        name="modulation",
    )(jnp.pad(cond.astype(F32), ((0, rows - n), (0, 0))), w_mod, b_mod.reshape(1, -1))
    return out[:n].reshape(n, 3, D_MODEL)


def _split_w_in(w_in):
    n_a = 4 * BR_W + 4 * N_HEAD
    w_in = w_in.astype(BF16)
    main = jnp.concatenate([w_in[..., :4 * BR_W], w_in[..., n_a:]], axis=-1)
    ab = jnp.pad(w_in[..., 4 * BR_W:n_a], ((0, 0),) * (w_in.ndim - 1) + ((0, AB_PAD - 4 * N_HEAD),))
    return main, ab


def _trunk_layer(x3, cond, p, big, layer, dft, latent, new_outputs=(None, None, None)):
    b, l, _ = x3.shape
    x2 = x3.reshape(b * l, D_MODEL)
    mod = _modulation(cond, big['w_mod'], p['b_mod'], layer)
    rows_per_mod = l if mod.shape[0] == b else b * l
    g_pre = p['g_pre'].reshape(1, D_MODEL)
    proj3, ab = _inproj(x2, mod, g_pre, big['w_main'], big['w_ab'], layer, rows_per_mod, l)
    ab3 = ab.reshape(b, l, AB_PAD)

    lam_init = 0.8 - 0.6 * math.exp(-0.3 * layer)
    lam_p = p['diff_lam'].astype(F32)
    lam = (jnp.exp(jnp.sum(lam_p[0] * lam_p[1])) - jnp.exp(jnp.sum(lam_p[2] * lam_p[3])) + lam_init).reshape(1, 1)
    diff_norm = p['diff_norm'].reshape(1, D_HEAD)
    gdn_args = (proj3, ab3, p['gdn_conv'], p['gdn_a_log'], p['gdn_dt_bias'], p['gdn_norm'].reshape(1, D_HEAD), layer)

    yb = _hyena(proj3, p, dft)
    if latent is None:
        new_state, nat_cache, diff_cache = new_outputs
        ya, new_state = _gdn(*gdn_args, new_state=new_state)
        yc, yd, nat_cache, diff_cache = _ctx_attention(proj3, lam, diff_norm, lam_init, layer, nat_cache, diff_cache)
        extras = (new_state, nat_cache, diff_cache)
    else:
        ya, _ = _gdn(*gdn_args, state=latent['state_gdn'])
        yc = _lat_nat(proj3, latent['cache_nat_kv'], layer, _nat_bias_table(p['nat_rpb']))
        yd = _lat_diff(proj3, latent['cache_diff_kv'], layer, lam, diff_norm, lam_init, latent['rope'])
        extras = None

    ys = [t.reshape(b * l, BR_W) for t in (ya, yb, yc, yd)]
    out = _merge(x2, mod, g_pre, p['g_post'].reshape(1, D_MODEL), ys, big['w_branch'], big['w_merge'],
                 p['b_merge'].reshape(1, -1).astype(F32), big['w_out'], layer, rows_per_mod)
    return out.reshape(b, l, D_MODEL), extras


def kernel(x_prompt, x_sample, state_gdn, cache_nat_kv, cache_diff_kv, c, c_ctx,
           w_mod, b_mod, g_pre, g_post, w_in, gdn_conv, gdn_a_log, gdn_dt_bias, gdn_norm,
           hy_conv, hy_w1, hy_b1, hy_w2, hy_b2, hy_w3, hy_b3, hy_decay, hy_skip,
           nat_rpb, diff_lam, diff_norm, w_branch, w_merge, b_merge, w_out):
    small = {
        'b_mod': b_mod, 'g_pre': g_pre, 'g_post': g_post,
        'gdn_conv': gdn_conv, 'gdn_a_log': gdn_a_log, 'gdn_dt_bias': gdn_dt_bias, 'gdn_norm': gdn_norm,
        'hy_conv': hy_conv, 'hy_w1': hy_w1, 'hy_b1': hy_b1, 'hy_w2': hy_w2, 'hy_b2': hy_b2,
        'hy_w3': hy_w3, 'hy_b3': hy_b3, 'hy_decay': hy_decay, 'hy_skip': hy_skip,
        'nat_rpb': nat_rpb, 'diff_lam': diff_lam, 'diff_norm': diff_norm, 'b_merge': b_merge,
    }
    layers = [{name: arr[i] for name, arr in small.items()} for i in range(DEPTH)]
    w_main, w_ab = _split_w_in(w_in)
    big = {'w_mod': w_mod.astype(F32), 'w_main': w_main, 'w_ab': w_ab, 'w_branch': w_branch.astype(BF16),
           'w_merge': w_merge.astype(BF16), 'w_out': w_out.astype(BF16)}

    y_prompt = x_prompt
    dft_ctx = _dft_matrices(x_prompt.shape[1])
    outputs = (None, None, None)
    for i, p in enumerate(layers):
        y_prompt, outputs = _trunk_layer(y_prompt, c_ctx.reshape(1, D_MODEL), p, big, i, dft_ctx, None, outputs)
    new_state, nat_cache, diff_cache = outputs

    y_sample = x_sample
    dft_lat = _dft_matrices(x_sample.shape[1])
    latent = {'state_gdn': state_gdn, 'cache_nat_kv': cache_nat_kv, 'cache_diff_kv': cache_diff_kv,
              'rope': _rope_tables(x_sample.shape[1])}
    for i, p in enumerate(layers):
        y_sample, _ = _trunk_layer(y_sample, c, p, big, i, dft_lat, latent)

    return (y_prompt, y_sample, new_state, nat_cache, diff_cache)
```

```python
import functools
import math

import jax
import jax.numpy as jnp
import numpy as np
from jax import lax
from jax.experimental import pallas as pl
from jax.experimental.pallas import tpu as pltpu

F32 = jnp.float32
BF16 = jnp.bfloat16

D_MODEL = 1024
DEPTH = 2
GRID_W = 64
N_BRANCH = 4
BR_W = 512
N_HEAD = 4
D_HEAD = 128
SUBLANES = 8
CHUNK = 128
HY_BANDS = 16
WIN_R = 8
WIN_C = 16
DQK_D = 64
ROPE_BASE = 10000.0
EPS = 1e-6
N_MAIN = 4 * 4 * BR_W
AB_PAD = 128
NEG_INF = -1e30

VMEM_LIMIT = 48 * 1024 * 1024


def _cparams(*sem):
    return pltpu.CompilerParams(dimension_semantics=sem, vmem_limit_bytes=VMEM_LIMIT)


def _silu(x):
    return x * (1.0 / (1.0 + jnp.exp(-x)))


def _sigmoid(x):
    return 1.0 / (1.0 + jnp.exp(-x))


def _rms(x, g):
    return x * lax.rsqrt(jnp.mean(x * x, axis=-1, keepdims=True) + EPS) * g


def _dot(a, b):
    return jnp.dot(a.astype(BF16), b.astype(BF16), preferred_element_type=F32)


def _dot_nt(a, b):
    return lax.dot_general(a.astype(BF16), b.astype(BF16), (((1,), (1,)), ((), ())),
                           preferred_element_type=F32)


def _dot_tn(a, b):
    return lax.dot_general(a.astype(BF16), b.astype(BF16), (((0,), (0,)), ((), ())),
                           preferred_element_type=F32)


def _prenorm(x, g_pre, mod_ref):
    return _rms(x, g_pre) * (1.0 + mod_ref[0, 1:2, :]) + mod_ref[0, 0:1, :]


def _inproj_kernel(x_ref, mod_ref, gpre_ref, w_ref, wab_ref, proj_ref, ab_ref, h_scr):
    @pl.when(pl.program_id(1) == 0)
    def _():
        h = _prenorm(x_ref[...], gpre_ref[...], mod_ref).astype(BF16)
        h_scr[...] = h
        ab_ref[...] = jnp.dot(h, wab_ref[...], preferred_element_type=F32)

    acc = jnp.dot(h_scr[...], w_ref[...], preferred_element_type=F32)
    seqs, cblocks, rows, _ = proj_ref.shape
    for sq in range(seqs):
        for c in range(cblocks):
            proj_ref[sq, c] = acc[sq * rows:(sq + 1) * rows, c * D_HEAD:(c + 1) * D_HEAD]


def _inproj(x2, mod, g_pre, w_main, w_ab, layer, rows_per_mod, l, tm=1024, tn=2048):
    m = x2.shape[0]
    tm = math.gcd(tm, rows_per_mod)
    if tm >= l:
        proj_spec = pl.BlockSpec((tm // l, tn // D_HEAD, l, D_HEAD), lambda i, j: (i, j, 0, 0))
    else:
        per = l // tm
        proj_spec = pl.BlockSpec((1, tn // D_HEAD, tm, D_HEAD), lambda i, j: (i // per, j, i % per, 0))
    return pl.pallas_call(
        _inproj_kernel,
        grid=(m // tm, N_MAIN // tn),
        in_specs=[
            pl.BlockSpec((tm, D_MODEL), lambda i, j: (i, 0)),
            pl.BlockSpec((1, 3, D_MODEL), lambda i, j: ((i * tm) // rows_per_mod, 0, 0)),
            pl.BlockSpec((1, D_MODEL), lambda i, j: (0, 0)),
            pl.BlockSpec((None, D_MODEL, tn), lambda i, j: (layer, 0, j)),
            pl.BlockSpec((None, D_MODEL, AB_PAD), lambda i, j: (layer, 0, 0)),
        ],
        out_specs=[
            proj_spec,
            pl.BlockSpec((tm, AB_PAD), lambda i, j: (i, 0)),
        ],
        out_shape=[jax.ShapeDtypeStruct((m // l, N_MAIN // D_HEAD, l, D_HEAD), F32),
                   jax.ShapeDtypeStruct((m, AB_PAD), F32)],
        scratch_shapes=[pltpu.VMEM((tm, D_MODEL), BF16)],
        compiler_params=_cparams("parallel", "arbitrary"),
        name="inproj",
    )(x2, mod, g_pre, w_main, w_ab)


def _merge_kernel(x_ref, mod_ref, gpre_ref, gpost_ref, ya_ref, yb_ref, yc_ref, yd_ref,
                  wbr_ref, wmg_ref, bmg_ref, wout_ref, o_ref):
    x = x_ref[...]
    h = _prenorm(x, gpre_ref[...], mod_ref).astype(BF16)
    acc = None
    for k, y_ref in enumerate((ya_ref, yb_ref, yc_ref, yd_ref)):
        cols = slice(k * D_MODEL, (k + 1) * D_MODEL)
        gate = _sigmoid(jnp.dot(h, wmg_ref[:, cols], preferred_element_type=F32) + bmg_ref[:, cols])
        br = jnp.dot(y_ref[...], wbr_ref[k], preferred_element_type=F32)
        acc = gate * br if acc is None else acc + gate * br
    y = jnp.dot(acc.astype(BF16), wout_ref[...], preferred_element_type=F32)
    o_ref[...] = x + mod_ref[0, 2:3, :] * _rms(y, gpost_ref[...])


def _merge(x2, mod, g_pre, g_post, ys, w_branch, w_merge, b_merge, w_out, layer, rows_per_mod, tm=256):
    m = x2.shape[0]
    row = lambda i: (i, 0)
    fixed2 = lambda i: (0, 0)
    return pl.pallas_call(
        _merge_kernel,
        grid=(m // tm,),
        in_specs=[
            pl.BlockSpec((tm, D_MODEL), row),
            pl.BlockSpec((1, 3, D_MODEL), lambda i: ((i * tm) // rows_per_mod, 0, 0)),
            pl.BlockSpec((1, D_MODEL), fixed2),
            pl.BlockSpec((1, D_MODEL), fixed2),
            pl.BlockSpec((tm, BR_W), row),
            pl.BlockSpec((tm, BR_W), row),
            pl.BlockSpec((tm, BR_W), row),
            pl.BlockSpec((tm, BR_W), row),
            pl.BlockSpec((None, N_BRANCH, BR_W, D_MODEL), lambda i: (layer, 0, 0, 0)),
            pl.BlockSpec((None, D_MODEL, N_BRANCH * D_MODEL), lambda i: (layer, 0, 0)),
            pl.BlockSpec((1, N_BRANCH * D_MODEL), fixed2),
            pl.BlockSpec((None, D_MODEL, D_MODEL), lambda i: (layer, 0, 0)),
        ],
        out_specs=pl.BlockSpec((tm, D_MODEL), row),
        out_shape=jax.ShapeDtypeStruct((m, D_MODEL), F32),
        compiler_params=_cparams("parallel"),
        name="merge",
    )(x2, mod, g_pre, g_post, *ys, w_branch, w_merge, b_merge, w_out)


def _softmax_rows(s):
    p = jnp.exp(s - jnp.max(s, axis=-1, keepdims=True))
    return p, jnp.sum(p, axis=-1, keepdims=True)


def _head_cols(h):
    return slice(h * D_HEAD, (h + 1) * D_HEAD)


def _stack_heads(ref):
    return ref[...].reshape(ref.shape[0] * ref.shape[1], *ref.shape[2:])


def _ctx_nat_kernel(*refs, aliased):
    q_ref, k_ref, v_ref, g_ref, y_ref, kv_ref = refs[1:] if aliased else refs
    scale = D_HEAD ** -0.5
    q, k, v = (_stack_heads(r) for r in (q_ref, k_ref, v_ref))
    p, l = _softmax_rows(_bmm_nt(q, k) * scale)
    o = _bmm(p, v) / l
    for bb in range(q_ref.shape[0]):
        for h in range(N_HEAD):
            n = bb * N_HEAD + h
            y_ref[bb, :, _head_cols(h)] = (o[n] * _silu(g_ref[bb, h])).astype(y_ref.dtype)
            kv_ref[bb, 0, 0, h] = k[n]
            kv_ref[bb, 0, 1, h] = v[n]


def _map_masks():
    lane = lax.broadcasted_iota(jnp.int32, (1, D_HEAD), 1)
    first = (lane < DQK_D).astype(F32)
    return first, 1.0 - first


def _ctx_diff_kernel(*refs, aliased, out_scale):
    lam_ref, q_ref, k_ref, v_ref, g_ref, gn_ref, y_ref, kv_ref = refs[1:] if aliased else refs
    scale = DQK_D ** -0.5
    m1, m2 = _map_masks()
    q, k, v = (_stack_heads(r) for r in (q_ref, k_ref, v_ref))
    nb = q.shape[0]
    p, l = _softmax_rows(_bmm_nt(jnp.concatenate([q * m1, q * m2], axis=0), jnp.concatenate([k, k], axis=0)) * scale)
    pn = p / l
    a = pn[:nb] - lam_ref[...] * pn[nb:]
    o = _rms(_bmm(a, v), gn_ref[...]) * out_scale
    for bb in range(q_ref.shape[0]):
        for h in range(N_HEAD):
            n = bb * N_HEAD + h
            y_ref[bb, :, _head_cols(h)] = (o[n] * _silu(g_ref[bb, h])).astype(y_ref.dtype)
            kv_ref[bb, 0, 0, h] = k[n]
            kv_ref[bb, 0, 1, h] = v[n]


def _ctx_attention(proj3, lam, diff_norm, lam_init, layer, nat_cache, diff_cache):
    b, _, l, _ = proj3.shape
    bt = _seqs_per_step(b, l, rows=512)
    blk = lambda c: pl.BlockSpec((bt, N_HEAD, l, D_HEAD), lambda i, c=c: (i, c, 0, 0))
    y_spec = pl.BlockSpec((bt, l, BR_W), lambda i: (i, 0, 0))
    kv_spec = pl.BlockSpec((bt, 1, 2, N_HEAD, l, D_HEAD), lambda i: (i, layer, 0, 0, 0, 0))
    out_shape = [jax.ShapeDtypeStruct((b, l, BR_W), BF16),
                 jax.ShapeDtypeStruct((b, DEPTH, 2, N_HEAD, l, D_HEAD), F32)]
    aliased = nat_cache is not None
    cache_specs = [pl.BlockSpec(memory_space=pl.ANY)] if aliased else []
    aliases = {0: 1} if aliased else {}
    yc, nat_cache = pl.pallas_call(
        functools.partial(_ctx_nat_kernel, aliased=aliased),
        grid=(b // bt,),
        in_specs=cache_specs + [blk(8), blk(9), blk(10), blk(11)],
        out_specs=[y_spec, kv_spec],
        out_shape=out_shape,
        input_output_aliases=aliases,
        compiler_params=_cparams("parallel"),
        name="ctx_nat",
    )(*([nat_cache] if aliased else []), proj3, proj3, proj3, proj3)
    yd, diff_cache = pl.pallas_call(
        functools.partial(_ctx_diff_kernel, aliased=aliased, out_scale=1.0 - lam_init),
        grid=(b // bt,),
        in_specs=cache_specs + [pl.BlockSpec((1, 1), lambda i: (0, 0)),
                                blk(12), blk(13), blk(14), blk(15),
                                pl.BlockSpec((1, D_HEAD), lambda i: (0, 0))],
        out_specs=[y_spec, kv_spec],
        out_shape=out_shape,
        input_output_aliases=aliases,
        compiler_params=_cparams("parallel"),
        name="ctx_diff",
    )(*([diff_cache] if aliased else []), lam, proj3, proj3, proj3, proj3, diff_norm)
    return yc, yd, nat_cache, diff_cache


def _nat_bias_table(rpb):
    cols = np.arange(GRID_W)
    start = np.clip(cols - WIN_C // 2, 0, GRID_W - WIN_C)
    inside = (cols[None, :] >= start[:, None]) & (cols[None, :] < start[:, None] + WIN_C)
    dc = cols[None, :] - cols[:, None] + (WIN_C - 1)
    onehot = ((dc[None] == np.arange(2 * WIN_C - 1)[:, None, None]) & inside[None]).astype(np.float32)
    t = jnp.einsum('hdx,xck->hdck', rpb.astype(F32), jnp.asarray(onehot), precision=lax.Precision.HIGHEST)
    t = jnp.where(jnp.asarray(inside)[None, None], t, NEG_INF)
    tab = jnp.stack([t[:, WIN_R - 1 - off:2 * WIN_R - 1 - off] for off in range(WIN_R)], axis=1)
    return tab.transpose(0, 1, 3, 2, 4).reshape(rpb.shape[0], WIN_R, GRID_W, WIN_R * GRID_W)


def _lat_nat_kernel(q_ref, k_ref, v_ref, g_ref, ckv_ref, bias_ref, y_ref, kb_scr, vb_scr, *, rb):
    scale = D_HEAD ** -0.5
    rows = q_ref.shape[1] // GRID_W
    win = WIN_R * GRID_W
    kb_scr[...] = k_ref[0].astype(BF16)
    vb_scr[...] = v_ref[0].astype(BF16)
    ck = ckv_ref[0, 0, 0, 0].astype(BF16)
    cv = ckv_ref[0, 0, 1, 0].astype(BF16)

    def row_block(i, carry):
        q0 = pl.multiple_of(i * (rb * GRID_W), rb * GRID_W)
        qrows = pl.ds(q0, rb * GRID_W)
        q = q_ref[0, qrows, :].astype(BF16)
        kw, vw, bias = [], [], []
        for j in range(rb):
            r = i * rb + j
            rs = jnp.clip(r - WIN_R // 2, 0, rows - WIN_R)
            wrows = pl.ds(pl.multiple_of(rs * GRID_W, GRID_W), win)
            kw.append(kb_scr[wrows, :])
            vw.append(vb_scr[wrows, :])
            bias.append(bias_ref[0, r - rs])
        q3 = q.reshape(rb, GRID_W, D_HEAD)
        s_lat = _bmm_nt(q3, jnp.stack(kw)) * scale + jnp.stack(bias)
        s_ctx = (_dot_nt(q, ck) * scale).reshape(rb, GRID_W, ck.shape[0])
        m = jnp.maximum(jnp.max(s_lat, axis=-1, keepdims=True), jnp.max(s_ctx, axis=-1, keepdims=True))
        p_lat = jnp.exp(s_lat - m)
        p_ctx = jnp.exp(s_ctx - m)
        l = jnp.sum(p_lat, axis=-1, keepdims=True) + jnp.sum(p_ctx, axis=-1, keepdims=True)
        o_ctx = _dot(p_ctx.reshape(rb * GRID_W, ck.shape[0]), cv).reshape(rb, GRID_W, D_HEAD)
        o = ((_bmm(p_lat, jnp.stack(vw)) + o_ctx) / l).reshape(rb * GRID_W, D_HEAD)
        y_ref[0, qrows, :] = (o * _silu(g_ref[0, qrows, :])).astype(y_ref.dtype)
        return carry

    lax.fori_loop(0, rows // rb, row_block, 0)


def _lat_nat(proj3, cache_nat_kv, layer, bias_tab):
    b, _, l, _ = proj3.shape
    past = cache_nat_kv.shape[4]
    blk = lambda c: pl.BlockSpec((1, None, l, D_HEAD), lambda i, h, c=c: (i, c + h, 0, 0))
    return pl.pallas_call(
        functools.partial(_lat_nat_kernel, rb=16),
        grid=(b, N_HEAD),
        in_specs=[blk(32), blk(36), blk(40), blk(44),
                  pl.BlockSpec((1, 1, 2, 1, past, D_HEAD), lambda i, h: (i, layer, 0, h, 0, 0)),
                  pl.BlockSpec((1, WIN_R, GRID_W, WIN_R * GRID_W), lambda i, h: (h, 0, 0, 0))],
        out_specs=pl.BlockSpec((1, l, D_HEAD), lambda i, h: (i, 0, h)),
        out_shape=jax.ShapeDtypeStruct((b, l, BR_W), BF16),
        scratch_shapes=[pltpu.VMEM((l, D_HEAD), BF16), pltpu.VMEM((l, D_HEAD), BF16)],
        compiler_params=_cparams("parallel", "parallel"),
        name="lat_nat",
    )(proj3, proj3, proj3, proj3, cache_nat_kv, bias_tab)


def _rope_tables(l):
    half = DQK_D // 2
    nf = half // 2
    t = jnp.arange(l)
    row = (t // GRID_W).astype(F32)
    col = (t % GRID_W).astype(F32)
    inv = ROPE_BASE ** (-jnp.arange(nf, dtype=F32) / nf)
    ang = jnp.concatenate([row[:, None] * inv, col[:, None] * inv], axis=-1)
    cos, sin = jnp.cos(ang), jnp.sin(ang)
    zero = jnp.zeros_like(sin)
    tile2 = lambda a, b: jnp.concatenate([a, b, a, b], axis=-1)
    return tile2(cos, cos), tile2(-sin, zero), tile2(zero, sin)


def _rope(x, cos, sin_a, sin_b):
    return x * cos + pltpu.roll(x, 96, 1) * sin_a + pltpu.roll(x, 32, 1) * sin_b


def _lat_diff_kernel(lam_ref, q_ref, k_ref, v_ref, g_ref, ckv_ref, gn_ref,
                     cq_ref, saq_ref, sbq_ref, ck_ref, sak_ref, sbk_ref,
                     y_ref, ks_scr, vt_scr, *, out_scale, prep_rows, key_block, ahead):
    scale = DQK_D ** -0.5
    l = k_ref.shape[1]

    @pl.when(pl.program_id(2) == 0)
    def _():
        def prep(i, carry):
            rows = pl.ds(pl.multiple_of(i * prep_rows, prep_rows), prep_rows)
            kr = _rope(k_ref[0, rows, :], ck_ref[rows, :], sak_ref[rows, :], sbk_ref[rows, :])
            ks_scr[rows, :] = kr.astype(BF16)
            vt_scr[:, rows] = v_ref[0, rows, :].T.astype(BF16)
            return carry

        lax.fori_loop(0, l // prep_rows, prep, 0)
        ks_scr[l:, :] = ckv_ref[0, 0, 0, 0].astype(BF16)
        vt_scr[:, l:] = ckv_ref[0, 0, 1, 0].T.astype(BF16)

    q = _rope(q_ref[0], cq_ref[...], saq_ref[...], sbq_ref[...]) * (scale * math.log2(math.e))
    m1, m2 = _map_masks()
    tq = q.shape[0]
    qm = jnp.concatenate([q * m1, q * m2], axis=0).astype(BF16)
    m = l_sum = acc = None
    n_blk = ks_scr.shape[0] // key_block
    block = lambda blk: slice(blk * key_block, (blk + 1) * key_block)
    scores = [_dot_nt(ks_scr[block(b), :], qm) for b in range(min(ahead, n_blk))]
    for blk in range(n_blk):
        rows = block(blk)
        s = scores.pop(0)
        if blk + ahead < n_blk:
            scores.append(_dot_nt(ks_scr[block(blk + ahead), :], qm))
        m_blk = jnp.max(s, axis=0, keepdims=True)
        if blk == 0:
            m = m_blk
            p = jnp.exp2(s - m)
            l_sum = jnp.sum(p, axis=0, keepdims=True)
            acc = _dot(vt_scr[:, rows], p)
        else:
            m_new = jnp.maximum(m, m_blk)
            alpha = jnp.exp2(m - m_new)
            p = jnp.exp2(s - m_new)
            l_sum = alpha * l_sum + jnp.sum(p, axis=0, keepdims=True)
            acc = alpha * acc + _dot(vt_scr[:, rows], p)
            m = m_new
    out = acc / l_sum
    d = out[:, :tq] - lam_ref[...] * out[:, tq:]
    d = d * lax.rsqrt(jnp.mean(d * d, axis=0, keepdims=True) + EPS)
    o = d.T * gn_ref[...] * out_scale
    y_ref[0] = (o * _silu(g_ref[0])).astype(y_ref.dtype)


def _lat_diff(proj3, cache_diff_kv, layer, lam, diff_norm, lam_init, rope_tabs, tq=1024):
    b, _, l, _ = proj3.shape
    past = cache_diff_kv.shape[4]
    qblk = lambda c: pl.BlockSpec((1, None, tq, D_HEAD), lambda i, h, j, c=c: (i, c + h, j, 0))
    full = lambda c: pl.BlockSpec((1, None, l, D_HEAD), lambda i, h, j, c=c: (i, c + h, 0, 0))
    tq_tab = pl.BlockSpec((tq, D_HEAD), lambda i, h, j: (j, 0))
    full_tab = pl.BlockSpec((l, D_HEAD), lambda i, h, j: (0, 0))
    return pl.pallas_call(
        functools.partial(_lat_diff_kernel, out_scale=1.0 - lam_init, prep_rows=512, key_block=512, ahead=2),
        grid=(b, N_HEAD, l // tq),
        in_specs=[pl.BlockSpec((1, 1), lambda i, h, j: (0, 0)),
                  qblk(48), full(52), full(56), qblk(60),
                  pl.BlockSpec((1, 1, 2, 1, past, D_HEAD), lambda i, h, j: (i, layer, 0, h, 0, 0)),
                  pl.BlockSpec((1, D_HEAD), lambda i, h, j: (0, 0)),
                  tq_tab, tq_tab, tq_tab, full_tab, full_tab, full_tab],
        out_specs=pl.BlockSpec((1, tq, D_HEAD), lambda i, h, j: (i, j, h)),
        out_shape=jax.ShapeDtypeStruct((b, l, BR_W), BF16),
        scratch_shapes=[pltpu.VMEM((l + past, D_HEAD), BF16), pltpu.VMEM((D_HEAD, l + past), BF16)],
        compiler_params=_cparams("parallel", "parallel", "arbitrary"),
        name="lat_diff",
    )(lam, proj3, proj3, proj3, proj3, cache_diff_kv, diff_norm, *rope_tabs, *rope_tabs)


def _dwconv3(x, w_ref):
    l = x.shape[0]
    row = lax.broadcasted_iota(jnp.int32, x.shape, 0)
    prev = jnp.where(row == 0, 0.0, pltpu.roll(x, 1, 0))
    nxt = jnp.where(row == l - 1, 0.0, pltpu.roll(x, l - 1, 0))
    return prev * w_ref[0:1, :] + x * w_ref[1:2, :] + nxt * w_ref[2:3, :]


def _hy_pre_kernel(x_ref, above_ref, below_ref, w_ref, o_ref, ob_ref):
    t, n_t = pl.program_id(1), pl.num_programs(1)
    bt, cblocks, rows, _ = x_ref.shape
    row = lax.broadcasted_iota(jnp.int32, (rows, D_HEAD), 0)
    for bb in range(bt):
        for c in range(cblocks):
            cols = _head_cols(c)
            x = x_ref[bb, c]
            before = jnp.where(t == 0, 0.0, above_ref[bb, c, SUBLANES - 1:SUBLANES, :])
            after = jnp.where(t == n_t - 1, 0.0, below_ref[bb, c, 0:1, :])
            prev = jnp.where(row == 0, before, pltpu.roll(x, 1, 0))
            nxt = jnp.where(row == rows - 1, after, pltpu.roll(x, rows - 1, 0))
            y = prev * w_ref[0:1, cols] + x * w_ref[1:2, cols] + nxt * w_ref[2:3, cols]
            o_ref[bb, :, cols] = y
            ob_ref[bb, :, cols] = y.astype(BF16)


def _hy_pre(proj3, conv_w, tl=1024):
    b, _, l, _ = proj3.shape
    tl = min(tl, l)
    bt = _seqs_per_step(b, l)
    n = 3
    cb = BR_W // D_HEAD
    col0 = 4
    groups = tl // SUBLANES
    last_group = l // SUBLANES - 1
    spec = pl.BlockSpec((bt, tl, BR_W), lambda i, t, j: (i, t, j))
    return pl.pallas_call(
        _hy_pre_kernel,
        grid=(b // bt, l // tl, n),
        in_specs=[pl.BlockSpec((bt, cb, tl, D_HEAD), lambda i, t, j: (i, col0 + j, t, 0)),
                  pl.BlockSpec((bt, cb, SUBLANES, D_HEAD),
                               lambda i, t, j: (i, col0 + j, jnp.maximum(t * groups - 1, 0), 0)),
                  pl.BlockSpec((bt, cb, SUBLANES, D_HEAD),
                               lambda i, t, j: (i, col0 + j, jnp.minimum((t + 1) * groups, last_group), 0)),
                  pl.BlockSpec((3, BR_W), lambda i, t, j: (0, j))],
        out_specs=[spec, spec],
        out_shape=[jax.ShapeDtypeStruct((b, l, 3 * BR_W), F32),
                   jax.ShapeDtypeStruct((b, l, 3 * BR_W), BF16)],
        compiler_params=_cparams("parallel", "parallel", "parallel"),
        name="hy_pre",
    )(proj3, proj3, proj3, conv_w)


def _dot_hi(a, b):
    return jnp.dot(a, b, preferred_element_type=F32, precision=lax.Precision.HIGHEST)


def _hy_filter_kernel(feat_ref, dist_ref, w1_ref, b1_ref, w2_ref, b2_ref, w3_ref, b3_ref, dec_ref, o_ref):
    hid = jnp.sin(_dot_hi(feat_ref[...], w1_ref[...]) + b1_ref[...])
    hid = jnp.sin(_dot_hi(hid, w2_ref[...]) + b2_ref[...])
    dist = dist_ref[...]
    for j in range(o_ref.shape[1] // D_HEAD):
        cols = slice(j * D_HEAD, (j + 1) * D_HEAD)
        filt = _dot_hi(hid, w3_ref[:, cols]) + b3_ref[:, cols]
        o_ref[:, cols] = (filt * jnp.exp(-dist * jnp.abs(dec_ref[:, cols]))).astype(o_ref.dtype)


def _hy_filter(l, w1, b1, w2, b2, w3, b3, decay):
    pos = jnp.arange(l, dtype=F32)
    t = pos / l
    ang = (2.0 * math.pi) * t[:, None] * jnp.arange(1, HY_BANDS + 1, dtype=F32)
    feat = jnp.concatenate([t[:, None], jnp.cos(ang), jnp.sin(ang)], axis=-1)
    dist = jnp.broadcast_to((jnp.abs(pos - l // 2) / l)[:, None], (l, D_HEAD))
    pad = D_HEAD
    emb, ff = w1.shape
    feat = jnp.pad(feat, ((0, 0), (0, pad - emb)))
    w1p = jnp.pad(w1, ((0, pad - emb), (0, pad - ff)))
    w2p = jnp.pad(w2, ((0, pad - ff), (0, pad - ff)))
    w3p = jnp.pad(w3, ((0, pad - ff), (0, 0)))
    b1p = jnp.pad(b1, (0, pad - ff)).reshape(1, pad)
    b2p = jnp.pad(b2, (0, pad - ff)).reshape(1, pad)
    tl = min(l, 256)
    n = 2 * BR_W
    fixed = lambda shape: pl.BlockSpec(shape, lambda i: (0, 0))
    return pl.pallas_call(
        _hy_filter_kernel,
        grid=(l // tl,),
        in_specs=[pl.BlockSpec((tl, pad), lambda i: (i, 0)),
                  pl.BlockSpec((tl, D_HEAD), lambda i: (i, 0)),
                  fixed((pad, pad)), fixed((1, pad)), fixed((pad, pad)), fixed((1, pad)),
                  fixed((pad, n)), fixed((1, n)), fixed((1, n))],
        out_specs=pl.BlockSpec((tl, n), lambda i: (i, 0)),
        out_shape=jax.ShapeDtypeStruct((l, n), BF16),
        compiler_params=_cparams("parallel"),
        name="hy_filter",
    )(feat, dist, w1p, b1p, w2p, b2p, w3p, b3.reshape(1, n), decay.reshape(1, n))


def _dft_matrices(l):
    n = 2 * l
    k = jnp.arange(l, dtype=jnp.int32)
    t = jnp.arange(l, dtype=jnp.int32)
    tp = t + l // 2
    split = 1 << (max(l.bit_length() - 1, 0) // 2)

    def tables(rows, cols):
        def table(r):
            ang = (2.0 * math.pi / n) * ((r[:, None] * cols[None, :]) % n).astype(F32)
            return jnp.cos(ang), jnp.sin(ang)
        return (*table(rows[::split]), *table(rows[:split] - rows[0]))

    alt = jnp.where(t % 2 == 0, 1.0, -1.0).astype(F32).reshape(1, l)
    wk = (jnp.where(k == 0, 1.0, 2.0).astype(F32) / n).reshape(1, l)
    out = pl.pallas_call(
        functools.partial(_dft_gen_kernel, split=split, l=l),
        grid=(l // split,),
        in_specs=[pl.BlockSpec((l // split, l), lambda i: (0, 0))] * 2 + [pl.BlockSpec((split, l), lambda i: (0, 0))] * 2
        + [pl.BlockSpec((l // split, l), lambda i: (0, 0))] * 2 + [pl.BlockSpec((split, l), lambda i: (0, 0))] * 2
        + [pl.BlockSpec((1, l), lambda i: (0, 0))] * 2,
        out_specs=[pl.BlockSpec((split, l), lambda i: (i, 0))] * 4,
        out_shape=[jax.ShapeDtypeStruct((l, l), BF16)] * 4,
        compiler_params=_cparams("parallel"),
        name="dft_gen",
    )(*tables(k, t), *tables(tp, k), alt, wk)
    return (out[0], out[1]), (out[2], out[3])


def _dft_gen_kernel(ch_ref, sh_ref, cl_ref, sl_ref, chi_ref, shi_ref, cli_ref, sli_ref, alt_ref, wk_ref,
                    fc_ref, fs_ref, ic_ref, is_ref, *, split, l):
    i = pl.program_id(0)

    def cos_sin(c_hi, s_hi, c_lo, s_lo):
        ch, sh = c_hi[pl.ds(i, 1), :], s_hi[pl.ds(i, 1), :]
        return ch * c_lo[...] - sh * s_lo[...], sh * c_lo[...] + ch * s_lo[...]

    row = lax.broadcasted_iota(jnp.int32, (split, l), 0) + i * split
    col = lax.broadcasted_iota(jnp.int32, (split, l), 1)
    c, s = cos_sin(ch_ref, sh_ref, cl_ref, sl_ref)
    fc_ref[...] = c.astype(BF16)
    fs_ref[...] = jnp.where(row == 0, alt_ref[...], -s).astype(BF16)
    c, s = cos_sin(chi_ref, shi_ref, cli_ref, sli_ref)
    wk = wk_ref[...]
    alt_i = jnp.where(row % 2 == 0, 1.0, -1.0) * (1.0 / (2 * l))
    ic_ref[...] = (c * wk).astype(BF16)
    is_ref[...] = jnp.where(col == 0, alt_i, -s * wk).astype(BF16)


def _seqs_per_step(b, l, rows=2048):
    bt = max(1, min(b, rows // l))
    while b % bt:
        bt -= 1
    return bt


def _dft_fwd_kernel(fc_ref, fs_ref, x_ref, *rest, with_filter, tm):
    for bb in range(x_ref.shape[0]):
        x = x_ref[bb]
        ur = jnp.dot(fc_ref[...], x, preferred_element_type=F32)
        ui = jnp.dot(fs_ref[...], x, preferred_element_type=F32)
        if not with_filter:
            zr_ref, zi_ref = rest
            zr_ref[bb] = ur
            zi_ref[bb] = ui
            continue
        hr_ref, hi_ref, zr_ref, zi_ref = rest
        hr, hi = hr_ref[0], hi_ref[0]
        row0 = (lax.broadcasted_iota(jnp.int32, ur.shape, 0) + pl.program_id(0) * tm) == 0
        zr_ref[bb] = (ur * hr - jnp.where(row0, 0.0, ui * hi)).astype(zr_ref.dtype)
        zi_ref[bb] = jnp.where(row0, ui * hi, ur * hi + ui * hr).astype(zi_ref.dtype)


def _dft_fwd(fwd, x, x_col0, c, spec_h=None, h_col0=0, tm=512, tn=512):
    b, l, _ = x.shape
    tm = min(tm, l)
    bt = _seqs_per_step(b, l)
    xo, ho = x_col0 // tn, h_col0 // tn
    out_dtype = F32 if spec_h is None else BF16
    fspec = pl.BlockSpec((tm, l), lambda i, bb, j: (i, 0))
    in_specs = [fspec, fspec, pl.BlockSpec((bt, l, tn), lambda i, bb, j: (bb, 0, xo + j))]
    args = [*fwd, x]
    if spec_h is not None:
        hspec = pl.BlockSpec((1, tm, tn), lambda i, bb, j: (0, i, ho + j))
        in_specs += [hspec, hspec]
        args += list(spec_h)
    ospec = pl.BlockSpec((bt, tm, tn), lambda i, bb, j: (bb, i, j))
    return pl.pallas_call(
        functools.partial(_dft_fwd_kernel, with_filter=spec_h is not None, tm=tm),
        grid=(l // tm, b // bt, c // tn),
        in_specs=in_specs,
        out_specs=[ospec, ospec],
        out_shape=[jax.ShapeDtypeStruct((b, l, c), out_dtype)] * 2,
        compiler_params=_cparams("parallel", "parallel", "parallel"),
        name="dft_fwd",
    )(*args)


def _dft_inv_kernel(ic_ref, is_ref, zr_ref, zi_ref, u_ref, m_ref, skip_ref, *rest, with_gate):
    for bb in range(zr_ref.shape[0]):
        y = (jnp.dot(ic_ref[...], zr_ref[bb], preferred_element_type=F32)
             + jnp.dot(is_ref[...], zi_ref[bb], preferred_element_type=F32))
        z = m_ref[bb] * (y + u_ref[bb] * skip_ref[...])
        if with_gate:
            g_ref, o_ref = rest
            gate = jnp.concatenate([g_ref[bb, c] for c in range(g_ref.shape[1])], axis=-1)
            o_ref[bb] = (z * _silu(gate)).astype(o_ref.dtype)
        else:
            o_ref, ob_ref = rest
            o_ref[bb] = z
            ob_ref[bb] = z.astype(BF16)


def _dft_inv(inv, zr, zi, u, u_col0, mul, mul_col0, skip, gate=None, gate_col0=0, tm=512, tn=512):
    b, l, c = zr.shape
    tm = min(tm, l)
    bt = _seqs_per_step(b, l)
    win = lambda col0: pl.BlockSpec((bt, tm, tn), lambda i, bb, j, o=col0 // tn: (bb, i, o + j))
    zspec = pl.BlockSpec((bt, l, tn), lambda i, bb, j: (bb, 0, j))
    fspec = pl.BlockSpec((tm, l), lambda i, bb, j: (i, 0))
    in_specs = [fspec, fspec, zspec, zspec,
                win(u_col0), win(mul_col0), pl.BlockSpec((1, tn), lambda i, bb, j: (0, j))]
    args = [*inv, zr, zi, u, mul, skip]
    ospec = pl.BlockSpec((bt, tm, tn), lambda i, bb, j: (bb, i, j))
    if gate is not None:
        in_specs.append(pl.BlockSpec((bt, tn // D_HEAD, tm, D_HEAD),
                                     lambda i, bb, j, o=gate_col0 // tn: (bb, o + j, i, 0)))
        args.append(gate)
        out_specs, out_shape = ospec, jax.ShapeDtypeStruct((b, l, c), BF16)
    else:
        out_specs = [ospec, ospec]
        out_shape = [jax.ShapeDtypeStruct((b, l, c), F32), jax.ShapeDtypeStruct((b, l, c), BF16)]
    return pl.pallas_call(
        functools.partial(_dft_inv_kernel, with_gate=gate is not None),
        grid=(l // tm, b // bt, c // tn),
        in_specs=in_specs,
        out_specs=out_specs,
        out_shape=out_shape,
        compiler_params=_cparams("parallel", "parallel", "parallel"),
        name="dft_inv",
    )(*args)


def _hyena(proj3, p, dft):
    l = proj3.shape[2]
    fwd, inv = dft
    filt = _hy_filter(l, p['hy_w1'], p['hy_b1'], p['hy_w2'], p['hy_b2'], p['hy_w3'], p['hy_b3'], p['hy_decay'])
    filt_b = filt[None]
    spec_h = _dft_fwd(fwd, filt_b, 0, 2 * BR_W)
    pre, pre_b = _hy_pre(proj3, p['hy_conv'])
    skip = p['hy_skip'].astype(F32)
    zr, zi = _dft_fwd(fwd, pre_b, 0, BR_W, spec_h, 0)
    z1, z1_b = _dft_inv(inv, zr, zi, pre, 0, pre, BR_W, skip[0:1])
    zr, zi = _dft_fwd(fwd, z1_b, 0, BR_W, spec_h, BR_W)
    return _dft_inv(inv, zr, zi, z1, 0, pre, 2 * BR_W, skip[1:2], gate=proj3, gate_col0=7 * BR_W)


def _softplus(x):
    return jnp.maximum(x, 0.0) + jnp.log1p(jnp.exp(-jnp.abs(x)))


def _split_bf16(x, parts):
    out = []
    for _ in range(parts - 1):
        piece = x.astype(BF16)
        out.append(piece)
        x = x - piece.astype(F32)
    out.append(x.astype(BF16))
    return out


def _bmm(a, b, hi=False):
    mm = lambda x, y: jnp.einsum('nij,njk->nik', x, y, preferred_element_type=F32)
    if not hi:
        return mm(a.astype(BF16), b.astype(BF16))
    (a1, a2), (b1, b2) = _split_bf16(a, 2), _split_bf16(b, 2)
    return mm(a1, b1) + (mm(a1, b2) + mm(a2, b1))


def _bmm_nt(a, b):
    return jnp.einsum('nid,njd->nij', a.astype(BF16), b.astype(BF16), preferred_element_type=F32)


TRI_BASE = 4


def _unit_tri_inverse(a, ri, ci):
    same = lambda w: (ri // w) == (ci // w)
    eye = (ri == ci).astype(F32)
    x = -jnp.where(same(TRI_BASE), a, 0.0)
    p = eye + x
    for _ in range(TRI_BASE.bit_length() - 2):
        x = _bmm(x, x, hi=True)
        p = p + _bmm(p, x, hi=True)
    w = TRI_BASE
    while w < a.shape[-1]:
        off = jnp.where(same(2 * w) & ~same(w), a, 0.0)
        p = p - _bmm(p, _bmm(off, p))
        w *= 2
    return p


def _gdn_prepare(q, k, v, ab, a_row, dt_row, head0, group):
    n, c, _ = q.shape
    two = lambda x: jnp.concatenate([x, x], axis=0)
    q, k, v, ab = two(q), two(k), two(v), two(ab)
    back3 = lambda shape: lax.broadcasted_iota(jnp.int32, shape, 0) >= n
    lane = lax.broadcasted_iota(jnp.int32, ab.shape, 2)
    bidx = lax.broadcasted_iota(jnp.int32, ab.shape, 0)
    head = head0 + jnp.where(bidx >= n, bidx - n, bidx) // group
    base = jnp.where(bidx >= n, 2 * N_HEAD, 0) + head
    g_all = -a_row * _softplus(ab + dt_row)
    g = jnp.sum(jnp.where(lane == base, g_all, 0.0), axis=2, keepdims=True)
    beta = jnp.sum(jnp.where(lane == base + N_HEAD, _sigmoid(ab), 0.0), axis=2, keepdims=True)

    sq = (2 * n, c, c)
    ri = lax.broadcasted_iota(jnp.int32, sq, 1)
    ci = lax.broadcasted_iota(jnp.int32, sq, 2)
    ahead = jnp.where(back3(sq), ci - ri, ri - ci)
    incl = ahead >= 0
    strict = ahead > 0
    tri = jnp.where(incl, 1.0, 0.0).astype(BF16)
    gc = sum(jnp.einsum('nij,njk->nik', tri, piece, preferred_element_type=F32)
             for piece in _split_bf16(jnp.broadcast_to(g, q.shape), 3))
    gc_row = jnp.swapaxes(gc, 1, 2)[:, :c, :]
    total = jnp.where(back3((2 * n, 1, D_HEAD)), gc[:, 0:1, :], gc[:, c - 1:c, :])
    decay = jnp.where(incl, jnp.exp(jnp.where(incl, gc[:, :, :c] - gc_row, 0.0)), 0.0)

    kb = k * beta
    a = jnp.where(strict, _bmm_nt(kb, k) * decay, 0.0)
    t = _unit_tri_inverse(a, ri, ci)
    e = jnp.exp(gc)
    u = _bmm(t, v * beta)
    w = _bmm(t, kb * e)
    a_intra = jnp.where(incl, _bmm_nt(q, k) * decay, 0.0)
    return (u, w.astype(BF16), (q * e).astype(BF16), (k * jnp.exp(total - gc)).astype(BF16),
            a_intra.astype(BF16), jnp.exp(total))


def _gdn_kernel(*refs, aliased, has_s0, group):
    if aliased:
        refs = refs[1:]
    if has_s0:
        (q_ref, k_ref, v_ref, z_ref, ab_ref, wq_ref, wk_ref, wv_ref, arow_ref, dt_ref, gn_ref, s0_ref,
         y_ref, sf_ref, qn, kn, vn, u_s, w_s, qd_s, kd_s, ai_s, gl_s) = refs
    else:
        (q_ref, k_ref, v_ref, z_ref, ab_ref, wq_ref, wk_ref, wv_ref, arow_ref, dt_ref, gn_ref,
         y_ref, sf_ref, qn, kn, vn, u_s, w_s, qd_s, kd_s, ai_s, gl_s) = refs
    _, heads, l, _ = q_ref.shape
    head0 = pl.program_id(1) * heads
    n_chunks = l // CHUNK
    hcols = lambda hh: slice(hh * D_HEAD, (hh + 1) * D_HEAD)

    def l2n(x):
        return x * lax.rsqrt(jnp.sum(x * x, axis=-1, keepdims=True) + EPS)

    for hh in range(heads):
        cols = hcols(hh)
        qn[:, cols] = l2n(_silu(_dwconv3(q_ref[0, hh], wq_ref.at[:, cols]))) * (D_HEAD ** -0.5)
        kn[:, cols] = l2n(_silu(_dwconv3(k_ref[0, hh], wk_ref.at[:, cols])))
        vn[:, cols] = _silu(_dwconv3(v_ref[0, hh], wv_ref.at[:, cols]))

    a_row, dt_row = arow_ref[...], dt_ref[...]

    def prepare(gi, carry):
        span = group * CHUNK
        rows = pl.ds(pl.multiple_of(gi * span, span), span)
        chunks = lambda x: x.reshape(group, CHUNK, x.shape[-1])
        per_head = lambda ref: jnp.concatenate([chunks(ref[rows, hcols(hh)]) for hh in range(heads)], axis=0)
        ab = chunks(ab_ref[0, rows, :])
        u, w, qd, kd, ai, gl = _gdn_prepare(per_head(qn), per_head(kn), per_head(vn),
                                            jnp.concatenate([ab] * heads, axis=0), a_row, dt_row, head0, group)
        for d in range(2):
            for hh in range(heads):
                cols = hcols(hh)
                part = slice((d * heads + hh) * group, (d * heads + hh + 1) * group)
                u_s[d, rows, cols] = u[part].reshape(span, D_HEAD)
                w_s[d, rows, cols] = w[part].reshape(span, D_HEAD)
                qd_s[d, rows, cols] = qd[part].reshape(span, D_HEAD)
                kd_s[d, rows, cols] = kd[part].reshape(span, D_HEAD)
                ai_s[d, hh, rows, :] = ai[part].reshape(span, CHUNK)
                gl_s[d, hh, pl.ds(gi * group, group)] = jnp.broadcast_to(gl[part], (group,) + gl_s.shape[3:])
        return carry

    lax.fori_loop(0, n_chunks // group, prepare, 0)

    def scan(i, s):
        where = [(hh, d, pl.ds(pl.multiple_of(chunk * CHUNK, CHUNK), CHUNK), chunk)
                 for hh in range(heads) for d, chunk in ((0, i), (1, n_chunks - 1 - i))]
        gather = lambda ref: jnp.stack([ref[d, rows, hcols(hh)] for hh, d, rows, _ in where])
        a_intra = jnp.stack([ai_s[d, hh, rows, :] for hh, d, rows, _ in where])
        decay = jnp.stack([gl_s[d, hh, chunk][0:1, :] for hh, d, _, chunk in where])
        sb = s.astype(BF16)
        v_new = gather(u_s) - _bmm(gather(w_s), sb)
        vb = v_new.astype(BF16)
        o = _bmm(gather(qd_s), sb) + _bmm(a_intra, vb)
        for idx, (hh, d, rows, _) in enumerate(where):
            u_s[d, rows, hcols(hh)] = o[idx]
        return s * decay + jnp.einsum('nik,niv->nkv', gather(kd_s), vb, preferred_element_type=F32)

    if has_s0:
        init = jnp.stack([s0_ref[0, 0, d, hh] for hh in range(heads) for d in range(2)])
    else:
        init = jnp.zeros((2 * heads, D_HEAD, D_HEAD), F32)
    final = lax.fori_loop(0, n_chunks, scan, init)
    for hh in range(heads):
        cols = hcols(hh)
        sf_ref[0, 0, 0, hh] = final[2 * hh]
        sf_ref[0, 0, 1, hh] = final[2 * hh + 1]
        y_ref[0, :, cols] = (_rms(u_s[0, :, cols] + u_s[1, :, cols], gn_ref[...])
                             * _silu(z_ref[0, hh])).astype(y_ref.dtype)


def _gdn(proj3, ab3, conv_w, a_log, dt_bias, norm_g, layer, state=None, new_state=None):
    b, _, l, _ = proj3.shape
    depth_out, layer_out = (1, 0) if state is not None else (DEPTH, layer)
    aliased = new_state is not None
    lanes = jnp.zeros((2, 2 * N_HEAD), F32).at[:, :N_HEAD].set(1.0)
    a_row = jnp.pad((jnp.exp(a_log.astype(F32))[:, None, :] * lanes.reshape(2, 2, N_HEAD)).reshape(1, -1),
                    ((0, 0), (0, AB_PAD - 4 * N_HEAD)))
    dt_row = jnp.pad((dt_bias.astype(F32)[:, None, :] * lanes.reshape(2, 2, N_HEAD)).reshape(1, -1),
                     ((0, 0), (0, AB_PAD - 4 * N_HEAD)))
    hps = N_HEAD if l <= 512 else 1
    wid = hps * D_HEAD
    n_hb = N_HEAD // hps
    blk = lambda c: pl.BlockSpec((1, hps, l, D_HEAD), lambda i, h, c=c: (i, c * n_hb + h, 0, 0))
    wblk = lambda c: pl.BlockSpec((3, wid), lambda i, h, c=c: (0, c * n_hb + h))
    row = pl.BlockSpec((1, D_HEAD), lambda i, h: (0, 0))
    in_specs = [blk(0), blk(1), blk(2), blk(3),
                pl.BlockSpec((1, l, AB_PAD), lambda i, h: (i, 0, 0)),
                wblk(0), wblk(1), wblk(2), row, row, row]
    args = [proj3, proj3, proj3, proj3, ab3, conv_w, conv_w, conv_w, a_row, dt_row, norm_g]
    if aliased:
        in_specs.insert(0, pl.BlockSpec(memory_space=pl.ANY))
        args.insert(0, new_state)
    if state is not None:
        in_specs.append(pl.BlockSpec((1, 1, 2, hps, D_HEAD, D_HEAD), lambda i, h: (i, layer, 0, h, 0, 0)))
        args.append(state)
    return pl.pallas_call(
        functools.partial(_gdn_kernel, aliased=aliased, has_s0=state is not None, group=min(512, l) // CHUNK),
        grid=(b, n_hb),
        in_specs=in_specs,
        out_specs=[pl.BlockSpec((1, l, wid), lambda i, h: (i, 0, h)),
                   pl.BlockSpec((1, 1, 2, hps, D_HEAD, D_HEAD), lambda i, h: (i, layer_out, 0, h, 0, 0))],
        out_shape=[jax.ShapeDtypeStruct((b, l, BR_W), BF16),
                   jax.ShapeDtypeStruct((b, depth_out, 2, N_HEAD, D_HEAD, D_HEAD), F32)],
        input_output_aliases={0: 1} if aliased else {},
        scratch_shapes=[pltpu.VMEM((l, wid), F32)] * 3
        + [pltpu.VMEM((2, l, wid), F32)] + [pltpu.VMEM((2, l, wid), BF16)] * 3
        + [pltpu.VMEM((2, hps, l, CHUNK), BF16), pltpu.VMEM((2, hps, l // CHUNK, 8, D_HEAD), F32)],
        compiler_params=_cparams("parallel", "parallel"),
        name="gdn",
    )(*args)


def _mod_kernel(c_ref, w_ref, b_ref, o_ref):
    o_ref[...] = _dot_hi(_silu(c_ref[...]), w_ref[...]) + b_ref[...]


def _modulation(cond, w_mod, b_mod, layer, tn=512):
    n = cond.shape[0]
    rows = 8
    out = pl.pallas_call(
        _mod_kernel,
        grid=(3 * D_MODEL // tn,),
        in_specs=[pl.BlockSpec((rows, D_MODEL), lambda j: (0, 0)),
                  pl.BlockSpec((None, D_MODEL, tn), lambda j: (layer, 0, j)),
                  pl.BlockSpec((1, tn), lambda j: (0, j))],
        out_specs=pl.BlockSpec((rows, tn), lambda j: (0, j)),
        out_shape=jax.ShapeDtypeStruct((rows, 3 * D_MODEL), F32),
        compiler_params=_cparams("parallel"),
        name="modulation",
    )(jnp.pad(cond.astype(F32), ((0, rows - n), (0, 0))), w_mod, b_mod.reshape(1, -1))
    return out[:n].reshape(n, 3, D_MODEL)


def _split_w_in(w_in):
    n_a = 4 * BR_W + 4 * N_HEAD
    w_in = w_in.astype(BF16)
    main = jnp.concatenate([w_in[..., :4 * BR_W], w_in[..., n_a:]], axis=-1)
    ab = jnp.pad(w_in[..., 4 * BR_W:n_a], ((0, 0),) * (w_in.ndim - 1) + ((0, AB_PAD - 4 * N_HEAD),))
    return main, ab


def _trunk_layer(x3, cond, p, big, layer, dft, latent, new_outputs=(None, None, None)):
    b, l, _ = x3.shape
    x2 = x3.reshape(b * l, D_MODEL)
    mod = _modulation(cond, big['w_mod'], p['b_mod'], layer)
    rows_per_mod = l if mod.shape[0] == b else b * l
    g_pre = p['g_pre'].reshape(1, D_MODEL)
    proj3, ab = _inproj(x2, mod, g_pre, big['w_main'], big['w_ab'], layer, rows_per_mod, l)
    ab3 = ab.reshape(b, l, AB_PAD)

    lam_init = 0.8 - 0.6 * math.exp(-0.3 * layer)
    lam_p = p['diff_lam'].astype(F32)
    lam = (jnp.exp(jnp.sum(lam_p[0] * lam_p[1])) - jnp.exp(jnp.sum(lam_p[2] * lam_p[3])) + lam_init).reshape(1, 1)
    diff_norm = p['diff_norm'].reshape(1, D_HEAD)
    gdn_args = (proj3, ab3, p['gdn_conv'], p['gdn_a_log'], p['gdn_dt_bias'], p['gdn_norm'].reshape(1, D_HEAD), layer)

    yb = _hyena(proj3, p, dft)
    if latent is None:
        new_state, nat_cache, diff_cache = new_outputs
        ya, new_state = _gdn(*gdn_args, new_state=new_state)
        yc, yd, nat_cache, diff_cache = _ctx_attention(proj3, lam, diff_norm, lam_init, layer, nat_cache, diff_cache)
        extras = (new_state, nat_cache, diff_cache)
    else:
        ya, _ = _gdn(*gdn_args, state=latent['state_gdn'])
        yc = _lat_nat(proj3, latent['cache_nat_kv'], layer, _nat_bias_table(p['nat_rpb']))
        yd = _lat_diff(proj3, latent['cache_diff_kv'], layer, lam, diff_norm, lam_init, latent['rope'])
        extras = None

    ys = [t.reshape(b * l, BR_W) for t in (ya, yb, yc, yd)]
    out = _merge(x2, mod, g_pre, p['g_post'].reshape(1, D_MODEL), ys, big['w_branch'], big['w_merge'],
                 p['b_merge'].reshape(1, -1).astype(F32), big['w_out'], layer, rows_per_mod)
    return out.reshape(b, l, D_MODEL), extras


def kernel(x_prompt, x_sample, state_gdn, cache_nat_kv, cache_diff_kv, c, c_ctx,
           w_mod, b_mod, g_pre, g_post, w_in, gdn_conv, gdn_a_log, gdn_dt_bias, gdn_norm,
           hy_conv, hy_w1, hy_b1, hy_w2, hy_b2, hy_w3, hy_b3, hy_decay, hy_skip,
           nat_rpb, diff_lam, diff_norm, w_branch, w_merge, b_merge, w_out):
    small = {
        'b_mod': b_mod, 'g_pre': g_pre, 'g_post': g_post,
        'gdn_conv': gdn_conv, 'gdn_a_log': gdn_a_log, 'gdn_dt_bias': gdn_dt_bias, 'gdn_norm': gdn_norm,
        'hy_conv': hy_conv, 'hy_w1': hy_w1, 'hy_b1': hy_b1, 'hy_w2': hy_w2, 'hy_b2': hy_b2,
        'hy_w3': hy_w3, 'hy_b3': hy_b3, 'hy_decay': hy_decay, 'hy_skip': hy_skip,
        'nat_rpb': nat_rpb, 'diff_lam': diff_lam, 'diff_norm': diff_norm, 'b_merge': b_merge,
    }
    layers = [{name: arr[i] for name, arr in small.items()} for i in range(DEPTH)]
    w_main, w_ab = _split_w_in(w_in)
    big = {'w_mod': w_mod.astype(F32), 'w_main': w_main, 'w_ab': w_ab, 'w_branch': w_branch.astype(BF16),
           'w_merge': w_merge.astype(BF16), 'w_out': w_out.astype(BF16)}

    y_prompt = x_prompt
    dft_ctx = _dft_matrices(x_prompt.shape[1])
    outputs = (None, None, None)
    for i, p in enumerate(layers):
        y_prompt, outputs = _trunk_layer(y_prompt, c_ctx.reshape(1, D_MODEL), p, big, i, dft_ctx, None, outputs)
    new_state, nat_cache, diff_cache = outputs

    y_sample = x_sample
    dft_lat = _dft_matrices(x_sample.shape[1])
    latent = {'state_gdn': state_gdn, 'cache_nat_kv': cache_nat_kv, 'cache_diff_kv': cache_diff_kv,
              'rope': _rope_tables(x_sample.shape[1])}
    for i, p in enumerate(layers):
        y_sample, _ = _trunk_layer(y_sample, c, p, big, i, dft_lat, latent)

    return (y_prompt, y_sample, new_state, nat_cache, diff_cache)
```

```python
import functools
import math

import jax
import jax.numpy as jnp
import numpy as np
from jax import lax
from jax.experimental import pallas as pl
from jax.experimental.pallas import tpu as pltpu

F32 = jnp.float32
BF16 = jnp.bfloat16

D_MODEL = 1024
DEPTH = 2
GRID_W = 64
N_BRANCH = 4
BR_W = 512
N_HEAD = 4
D_HEAD = 128
SUBLANES = 8
CHUNK = 128
HY_BANDS = 16
WIN_R = 8
WIN_C = 16
DQK_D = 64
ROPE_BASE = 10000.0
EPS = 1e-6
N_MAIN = 4 * 4 * BR_W
AB_PAD = 128
NEG_INF = -1e30

VMEM_LIMIT = 48 * 1024 * 1024


def _cparams(*sem):
    return pltpu.CompilerParams(dimension_semantics=sem, vmem_limit_bytes=VMEM_LIMIT)


def _silu(x):
    return x * (1.0 / (1.0 + jnp.exp(-x)))


def _sigmoid(x):
    return 1.0 / (1.0 + jnp.exp(-x))


def _rms(x, g):
    return x * lax.rsqrt(jnp.mean(x * x, axis=-1, keepdims=True) + EPS) * g


def _dot(a, b):
    return jnp.dot(a.astype(BF16), b.astype(BF16), preferred_element_type=F32)


def _dot_nt(a, b):
    return lax.dot_general(a.astype(BF16), b.astype(BF16), (((1,), (1,)), ((), ())),
                           preferred_element_type=F32)


def _dot_tn(a, b):
    return lax.dot_general(a.astype(BF16), b.astype(BF16), (((0,), (0,)), ((), ())),
                           preferred_element_type=F32)


def _prenorm(x, g_pre, mod_ref):
    return _rms(x, g_pre) * (1.0 + mod_ref[0, 1:2, :]) + mod_ref[0, 0:1, :]


def _inproj_kernel(x_ref, mod_ref, gpre_ref, w_ref, wab_ref, proj_ref, ab_ref, h_scr):
    @pl.when(pl.program_id(1) == 0)
    def _():
        h = _prenorm(x_ref[...], gpre_ref[...], mod_ref).astype(BF16)
        h_scr[...] = h
        ab_ref[...] = jnp.dot(h, wab_ref[...], preferred_element_type=F32)

    acc = jnp.dot(h_scr[...], w_ref[...], preferred_element_type=F32)
    seqs, cblocks, rows, _ = proj_ref.shape
    for sq in range(seqs):
        for c in range(cblocks):
            proj_ref[sq, c] = acc[sq * rows:(sq + 1) * rows, c * D_HEAD:(c + 1) * D_HEAD]


def _inproj(x2, mod, g_pre, w_main, w_ab, layer, rows_per_mod, l, tm=1024, tn=2048):
    m = x2.shape[0]
    tm = math.gcd(tm, rows_per_mod)
    if tm >= l:
        proj_spec = pl.BlockSpec((tm // l, tn // D_HEAD, l, D_HEAD), lambda i, j: (i, j, 0, 0))
    else:
        per = l // tm
        proj_spec = pl.BlockSpec((1, tn // D_HEAD, tm, D_HEAD), lambda i, j: (i // per, j, i % per, 0))
    return pl.pallas_call(
        _inproj_kernel,
        grid=(m // tm, N_MAIN // tn),
        in_specs=[
            pl.BlockSpec((tm, D_MODEL), lambda i, j: (i, 0)),
            pl.BlockSpec((1, 3, D_MODEL), lambda i, j: ((i * tm) // rows_per_mod, 0, 0)),
            pl.BlockSpec((1, D_MODEL), lambda i, j: (0, 0)),
            pl.BlockSpec((None, D_MODEL, tn), lambda i, j: (layer, 0, j)),
            pl.BlockSpec((None, D_MODEL, AB_PAD), lambda i, j: (layer, 0, 0)),
        ],
        out_specs=[
            proj_spec,
            pl.BlockSpec((tm, AB_PAD), lambda i, j: (i, 0)),
        ],
        out_shape=[jax.ShapeDtypeStruct((m // l, N_MAIN // D_HEAD, l, D_HEAD), F32),
                   jax.ShapeDtypeStruct((m, AB_PAD), F32)],
        scratch_shapes=[pltpu.VMEM((tm, D_MODEL), BF16)],
        compiler_params=_cparams("parallel", "arbitrary"),
        name="inproj",
    )(x2, mod, g_pre, w_main, w_ab)


def _merge_kernel(x_ref, mod_ref, gpre_ref, gpost_ref, ya_ref, yb_ref, yc_ref, yd_ref,
                  wbr_ref, wmg_ref, bmg_ref, wout_ref, o_ref):
    x = x_ref[...]
    h = _prenorm(x, gpre_ref[...], mod_ref).astype(BF16)
    acc = None
    for k, y_ref in enumerate((ya_ref, yb_ref, yc_ref, yd_ref)):
        cols = slice(k * D_MODEL, (k + 1) * D_MODEL)
        gate = _sigmoid(jnp.dot(h, wmg_ref[:, cols], preferred_element_type=F32) + bmg_ref[:, cols])
        br = jnp.dot(y_ref[...], wbr_ref[k], preferred_element_type=F32)
        acc = gate * br if acc is None else acc + gate * br
    y = jnp.dot(acc.astype(BF16), wout_ref[...], preferred_element_type=F32)
    o_ref[...] = x + mod_ref[0, 2:3, :] * _rms(y, gpost_ref[...])


def _merge(x2, mod, g_pre, g_post, ys, w_branch, w_merge, b_merge, w_out, layer, rows_per_mod, tm=256):
    m = x2.shape[0]
    row = lambda i: (i, 0)
    fixed2 = lambda i: (0, 0)
    return pl.pallas_call(
        _merge_kernel,
        grid=(m // tm,),
        in_specs=[
            pl.BlockSpec((tm, D_MODEL), row),
            pl.BlockSpec((1, 3, D_MODEL), lambda i: ((i * tm) // rows_per_mod, 0, 0)),
            pl.BlockSpec((1, D_MODEL), fixed2),
            pl.BlockSpec((1, D_MODEL), fixed2),
            pl.BlockSpec((tm, BR_W), row),
            pl.BlockSpec((tm, BR_W), row),
            pl.BlockSpec((tm, BR_W), row),
            pl.BlockSpec((tm, BR_W), row),
            pl.BlockSpec((None, N_BRANCH, BR_W, D_MODEL), lambda i: (layer, 0, 0, 0)),
            pl.BlockSpec((None, D_MODEL, N_BRANCH * D_MODEL), lambda i: (layer, 0, 0)),
            pl.BlockSpec((1, N_BRANCH * D_MODEL), fixed2),
            pl.BlockSpec((None, D_MODEL, D_MODEL), lambda i: (layer, 0, 0)),
        ],
        out_specs=pl.BlockSpec((tm, D_MODEL), row),
        out_shape=jax.ShapeDtypeStruct((m, D_MODEL), F32),
        compiler_params=_cparams("parallel"),
        name="merge",
    )(x2, mod, g_pre, g_post, *ys, w_branch, w_merge, b_merge, w_out)


def _softmax_rows(s):
    p = jnp.exp(s - jnp.max(s, axis=-1, keepdims=True))
    return p, jnp.sum(p, axis=-1, keepdims=True)


def _head_cols(h):
    return slice(h * D_HEAD, (h + 1) * D_HEAD)


def _stack_heads(ref):
    return ref[...].reshape(ref.shape[0] * ref.shape[1], *ref.shape[2:])


def _ctx_nat_kernel(*refs, aliased):
    q_ref, k_ref, v_ref, g_ref, y_ref, kv_ref = refs[1:] if aliased else refs
    scale = D_HEAD ** -0.5
    q, k, v = (_stack_heads(r) for r in (q_ref, k_ref, v_ref))
    p, l = _softmax_rows(_bmm_nt(q, k) * scale)
    o = _bmm(p, v) / l
    for bb in range(q_ref.shape[0]):
        for h in range(N_HEAD):
            n = bb * N_HEAD + h
            y_ref[bb, :, _head_cols(h)] = (o[n] * _silu(g_ref[bb, h])).astype(y_ref.dtype)
            kv_ref[bb, 0, 0, h] = k[n]
            kv_ref[bb, 0, 1, h] = v[n]


def _map_masks():
    lane = lax.broadcasted_iota(jnp.int32, (1, D_HEAD), 1)
    first = (lane < DQK_D).astype(F32)
    return first, 1.0 - first


def _ctx_diff_kernel(*refs, aliased, out_scale):
    lam_ref, q_ref, k_ref, v_ref, g_ref, gn_ref, y_ref, kv_ref = refs[1:] if aliased else refs
    scale = DQK_D ** -0.5
    m1, m2 = _map_masks()
    q, k, v = (_stack_heads(r) for r in (q_ref, k_ref, v_ref))
    nb = q.shape[0]
    p, l = _softmax_rows(_bmm_nt(jnp.concatenate([q * m1, q * m2], axis=0), jnp.concatenate([k, k], axis=0)) * scale)
    pn = p / l
    a = pn[:nb] - lam_ref[...] * pn[nb:]
    o = _rms(_bmm(a, v), gn_ref[...]) * out_scale
    for bb in range(q_ref.shape[0]):
        for h in range(N_HEAD):
            n = bb * N_HEAD + h
            y_ref[bb, :, _head_cols(h)] = (o[n] * _silu(g_ref[bb, h])).astype(y_ref.dtype)
            kv_ref[bb, 0, 0, h] = k[n]
            kv_ref[bb, 0, 1, h] = v[n]


def _ctx_attention(proj3, lam, diff_norm, lam_init, layer, nat_cache, diff_cache):
    b, _, l, _ = proj3.shape
    bt = _seqs_per_step(b, l, rows=1024)
    blk = lambda c: pl.BlockSpec((bt, N_HEAD, l, D_HEAD), lambda i, c=c: (i, c, 0, 0))
    y_spec = pl.BlockSpec((bt, l, BR_W), lambda i: (i, 0, 0))
    kv_spec = pl.BlockSpec((bt, 1, 2, N_HEAD, l, D_HEAD), lambda i: (i, layer, 0, 0, 0, 0))
    out_shape = [jax.ShapeDtypeStruct((b, l, BR_W), BF16),
                 jax.ShapeDtypeStruct((b, DEPTH, 2, N_HEAD, l, D_HEAD), F32)]
    aliased = nat_cache is not None
    cache_specs = [pl.BlockSpec(memory_space=pl.ANY)] if aliased else []
    aliases = {0: 1} if aliased else {}
    yc, nat_cache = pl.pallas_call(
        functools.partial(_ctx_nat_kernel, aliased=aliased),
        grid=(b // bt,),
        in_specs=cache_specs + [blk(8), blk(9), blk(10), blk(11)],
        out_specs=[y_spec, kv_spec],
        out_shape=out_shape,
        input_output_aliases=aliases,
        compiler_params=_cparams("parallel"),
        name="ctx_nat",
    )(*([nat_cache] if aliased else []), proj3, proj3, proj3, proj3)
    yd, diff_cache = pl.pallas_call(
        functools.partial(_ctx_diff_kernel, aliased=aliased, out_scale=1.0 - lam_init),
        grid=(b // bt,),
        in_specs=cache_specs + [pl.BlockSpec((1, 1), lambda i: (0, 0)),
                                blk(12), blk(13), blk(14), blk(15),
                                pl.BlockSpec((1, D_HEAD), lambda i: (0, 0))],
        out_specs=[y_spec, kv_spec],
        out_shape=out_shape,
        input_output_aliases=aliases,
        compiler_params=_cparams("parallel"),
        name="ctx_diff",
    )(*([diff_cache] if aliased else []), lam, proj3, proj3, proj3, proj3, diff_norm)
    return yc, yd, nat_cache, diff_cache


def _nat_bias_table(rpb):
    cols = np.arange(GRID_W)
    start = np.clip(cols - WIN_C // 2, 0, GRID_W - WIN_C)
    inside = (cols[None, :] >= start[:, None]) & (cols[None, :] < start[:, None] + WIN_C)
    dc = cols[None, :] - cols[:, None] + (WIN_C - 1)
    onehot = ((dc[None] == np.arange(2 * WIN_C - 1)[:, None, None]) & inside[None]).astype(np.float32)
    t = jnp.einsum('hdx,xck->hdck', rpb.astype(F32), jnp.asarray(onehot), precision=lax.Precision.HIGHEST)
    t = jnp.where(jnp.asarray(inside)[None, None], t, NEG_INF)
    tab = jnp.stack([t[:, WIN_R - 1 - off:2 * WIN_R - 1 - off] for off in range(WIN_R)], axis=1)
    return tab.transpose(0, 1, 3, 2, 4).reshape(rpb.shape[0], WIN_R, GRID_W, WIN_R * GRID_W)


def _lat_nat_kernel(q_ref, k_ref, v_ref, g_ref, ckv_ref, bias_ref, y_ref, kb_scr, vb_scr, *, rb):
    scale = D_HEAD ** -0.5
    rows = q_ref.shape[1] // GRID_W
    win = WIN_R * GRID_W
    kb_scr[...] = k_ref[0].astype(BF16)
    vb_scr[...] = v_ref[0].astype(BF16)
    ck = ckv_ref[0, 0, 0, 0].astype(BF16)
    cv = ckv_ref[0, 0, 1, 0].astype(BF16)

    def row_block(i, carry):
        q0 = pl.multiple_of(i * (rb * GRID_W), rb * GRID_W)
        qrows = pl.ds(q0, rb * GRID_W)
        q = q_ref[0, qrows, :].astype(BF16)
        kw, vw, bias = [], [], []
        for j in range(rb):
            r = i * rb + j
            rs = jnp.clip(r - WIN_R // 2, 0, rows - WIN_R)
            wrows = pl.ds(pl.multiple_of(rs * GRID_W, GRID_W), win)
            kw.append(kb_scr[wrows, :])
            vw.append(vb_scr[wrows, :])
            bias.append(bias_ref[0, r - rs])
        q3 = q.reshape(rb, GRID_W, D_HEAD)
        s_lat = _bmm_nt(q3, jnp.stack(kw)) * scale + jnp.stack(bias)
        s_ctx = (_dot_nt(q, ck) * scale).reshape(rb, GRID_W, ck.shape[0])
        m = jnp.maximum(jnp.max(s_lat, axis=-1, keepdims=True), jnp.max(s_ctx, axis=-1, keepdims=True))
        p_lat = jnp.exp(s_lat - m)
        p_ctx = jnp.exp(s_ctx - m)
        l = jnp.sum(p_lat, axis=-1, keepdims=True) + jnp.sum(p_ctx, axis=-1, keepdims=True)
        o_ctx = _dot(p_ctx.reshape(rb * GRID_W, ck.shape[0]), cv).reshape(rb, GRID_W, D_HEAD)
        o = ((_bmm(p_lat, jnp.stack(vw)) + o_ctx) / l).reshape(rb * GRID_W, D_HEAD)
        y_ref[0, qrows, :] = (o * _silu(g_ref[0, qrows, :])).astype(y_ref.dtype)
        return carry

    lax.fori_loop(0, rows // rb, row_block, 0)


def _lat_nat(proj3, cache_nat_kv, layer, bias_tab):
    b, _, l, _ = proj3.shape
    past = cache_nat_kv.shape[4]
    blk = lambda c: pl.BlockSpec((1, None, l, D_HEAD), lambda i, h, c=c: (i, c + h, 0, 0))
    return pl.pallas_call(
        functools.partial(_lat_nat_kernel, rb=math.gcd(32, l // GRID_W)),
        grid=(b, N_HEAD),
        in_specs=[blk(32), blk(36), blk(40), blk(44),
                  pl.BlockSpec((1, 1, 2, 1, past, D_HEAD), lambda i, h: (i, layer, 0, h, 0, 0)),
                  pl.BlockSpec((1, WIN_R, GRID_W, WIN_R * GRID_W), lambda i, h: (h, 0, 0, 0))],
        out_specs=pl.BlockSpec((1, l, D_HEAD), lambda i, h: (i, 0, h)),
        out_shape=jax.ShapeDtypeStruct((b, l, BR_W), BF16),
        scratch_shapes=[pltpu.VMEM((l, D_HEAD), BF16), pltpu.VMEM((l, D_HEAD), BF16)],
        compiler_params=_cparams("parallel", "parallel"),
        name="lat_nat",
    )(proj3, proj3, proj3, proj3, cache_nat_kv, bias_tab)


def _rope_tables(l):
    half = DQK_D // 2
    nf = half // 2
    t = jnp.arange(l)
    row = (t // GRID_W).astype(F32)
    col = (t % GRID_W).astype(F32)
    inv = ROPE_BASE ** (-jnp.arange(nf, dtype=F32) / nf)
    ang = jnp.concatenate([row[:, None] * inv, col[:, None] * inv], axis=-1)
    cos, sin = jnp.cos(ang), jnp.sin(ang)
    zero = jnp.zeros_like(sin)
    tile2 = lambda a, b: jnp.concatenate([a, b, a, b], axis=-1)
    return tile2(cos, cos), tile2(-sin, zero), tile2(zero, sin)


def _rope(x, cos, sin_a, sin_b):
    return x * cos + pltpu.roll(x, 96, 1) * sin_a + pltpu.roll(x, 32, 1) * sin_b


def _lat_diff_kernel(lam_ref, q_ref, k_ref, v_ref, g_ref, ckv_ref, gn_ref,
                     cq_ref, saq_ref, sbq_ref, ck_ref, sak_ref, sbk_ref,
                     y_ref, ks_scr, vt_scr, *, out_scale, prep_rows, key_block, ahead):
    scale = DQK_D ** -0.5
    l = k_ref.shape[1]

    @pl.when(pl.program_id(2) == 0)
    def _():
        def prep(i, carry):
            rows = pl.ds(pl.multiple_of(i * prep_rows, prep_rows), prep_rows)
            kr = _rope(k_ref[0, rows, :], ck_ref[rows, :], sak_ref[rows, :], sbk_ref[rows, :])
            ks_scr[rows, :] = kr.astype(BF16)
            vt_scr[:, rows] = v_ref[0, rows, :].T.astype(BF16)
            return carry

        lax.fori_loop(0, l // prep_rows, prep, 0)
        ks_scr[l:, :] = ckv_ref[0, 0, 0, 0].astype(BF16)
        vt_scr[:, l:] = ckv_ref[0, 0, 1, 0].T.astype(BF16)

    q = _rope(q_ref[0], cq_ref[...], saq_ref[...], sbq_ref[...]) * (scale * math.log2(math.e))
    m1, m2 = _map_masks()
    tq = q.shape[0]
    qm = jnp.concatenate([q * m1, q * m2], axis=0).astype(BF16)
    m = l_sum = acc = None
    n_blk = ks_scr.shape[0] // key_block
    block = lambda blk: slice(blk * key_block, (blk + 1) * key_block)
    scores = [_dot_nt(ks_scr[block(b), :], qm) for b in range(min(ahead, n_blk))]
    for blk in range(n_blk):
        rows = block(blk)
        s = scores.pop(0)
        if blk + ahead < n_blk:
            scores.append(_dot_nt(ks_scr[block(blk + ahead), :], qm))
        m_blk = jnp.max(s, axis=0, keepdims=True)
        if blk == 0:
            m = m_blk
            p = jnp.exp2(s - m)
            l_sum = jnp.sum(p, axis=0, keepdims=True)
            acc = _dot(vt_scr[:, rows], p)
        else:
            m_new = jnp.maximum(m, m_blk)
            alpha = jnp.exp2(m - m_new)
            p = jnp.exp2(s - m_new)
            l_sum = alpha * l_sum + jnp.sum(p, axis=0, keepdims=True)
            acc = alpha * acc + _dot(vt_scr[:, rows], p)
            m = m_new
    out = acc / l_sum
    d = out[:, :tq] - lam_ref[...] * out[:, tq:]
    d = d * lax.rsqrt(jnp.mean(d * d, axis=0, keepdims=True) + EPS)
    o = d.T * gn_ref[...] * out_scale
    y_ref[0] = (o * _silu(g_ref[0])).astype(y_ref.dtype)


def _lat_diff(proj3, cache_diff_kv, layer, lam, diff_norm, lam_init, rope_tabs, tq=1024):
    b, _, l, _ = proj3.shape
    past = cache_diff_kv.shape[4]
    qblk = lambda c: pl.BlockSpec((1, None, tq, D_HEAD), lambda i, h, j, c=c: (i, c + h, j, 0))
    full = lambda c: pl.BlockSpec((1, None, l, D_HEAD), lambda i, h, j, c=c: (i, c + h, 0, 0))
    tq_tab = pl.BlockSpec((tq, D_HEAD), lambda i, h, j: (j, 0))
    full_tab = pl.BlockSpec((l, D_HEAD), lambda i, h, j: (0, 0))
    return pl.pallas_call(
        functools.partial(_lat_diff_kernel, out_scale=1.0 - lam_init, prep_rows=512, key_block=512, ahead=2),
        grid=(b, N_HEAD, l // tq),
        in_specs=[pl.BlockSpec((1, 1), lambda i, h, j: (0, 0)),
                  qblk(48), full(52), full(56), qblk(60),
                  pl.BlockSpec((1, 1, 2, 1, past, D_HEAD), lambda i, h, j: (i, layer, 0, h, 0, 0)),
                  pl.BlockSpec((1, D_HEAD), lambda i, h, j: (0, 0)),
                  tq_tab, tq_tab, tq_tab, full_tab, full_tab, full_tab],
        out_specs=pl.BlockSpec((1, tq, D_HEAD), lambda i, h, j: (i, j, h)),
        out_shape=jax.ShapeDtypeStruct((b, l, BR_W), BF16),
        scratch_shapes=[pltpu.VMEM((l + past, D_HEAD), BF16), pltpu.VMEM((D_HEAD, l + past), BF16)],
        compiler_params=_cparams("parallel", "parallel", "arbitrary"),
        name="lat_diff",
    )(lam, proj3, proj3, proj3, proj3, cache_diff_kv, diff_norm, *rope_tabs, *rope_tabs)


def _dwconv3(x, w_ref):
    l = x.shape[0]
    row = lax.broadcasted_iota(jnp.int32, x.shape, 0)
    prev = jnp.where(row == 0, 0.0, pltpu.roll(x, 1, 0))
    nxt = jnp.where(row == l - 1, 0.0, pltpu.roll(x, l - 1, 0))
    return prev * w_ref[0:1, :] + x * w_ref[1:2, :] + nxt * w_ref[2:3, :]


def _hy_pre_kernel(x_ref, above_ref, below_ref, w_ref, o_ref, ob_ref):
    t, n_t = pl.program_id(1), pl.num_programs(1)
    bt, cblocks, rows, _ = x_ref.shape
    row = lax.broadcasted_iota(jnp.int32, (rows, D_HEAD), 0)
    for bb in range(bt):
        for c in range(cblocks):
            cols = _head_cols(c)
            x = x_ref[bb, c]
            before = jnp.where(t == 0, 0.0, above_ref[bb, c, SUBLANES - 1:SUBLANES, :])
            after = jnp.where(t == n_t - 1, 0.0, below_ref[bb, c, 0:1, :])
            prev = jnp.where(row == 0, before, pltpu.roll(x, 1, 0))
            nxt = jnp.where(row == rows - 1, after, pltpu.roll(x, rows - 1, 0))
            y = prev * w_ref[0:1, cols] + x * w_ref[1:2, cols] + nxt * w_ref[2:3, cols]
            o_ref[bb, :, cols] = y
            ob_ref[bb, :, cols] = y.astype(BF16)


def _hy_pre(proj3, conv_w, tl=1024):
    b, _, l, _ = proj3.shape
    tl = min(tl, l)
    bt = _seqs_per_step(b, l)
    n = 3
    cb = BR_W // D_HEAD
    col0 = 4
    groups = tl // SUBLANES
    last_group = l // SUBLANES - 1
    spec = pl.BlockSpec((bt, tl, BR_W), lambda i, t, j: (i, t, j))
    return pl.pallas_call(
        _hy_pre_kernel,
        grid=(b // bt, l // tl, n),
        in_specs=[pl.BlockSpec((bt, cb, tl, D_HEAD), lambda i, t, j: (i, col0 + j, t, 0)),
                  pl.BlockSpec((bt, cb, SUBLANES, D_HEAD),
                               lambda i, t, j: (i, col0 + j, jnp.maximum(t * groups - 1, 0), 0)),
                  pl.BlockSpec((bt, cb, SUBLANES, D_HEAD),
                               lambda i, t, j: (i, col0 + j, jnp.minimum((t + 1) * groups, last_group), 0)),
                  pl.BlockSpec((3, BR_W), lambda i, t, j: (0, j))],
        out_specs=[spec, spec],
        out_shape=[jax.ShapeDtypeStruct((b, l, 3 * BR_W), F32),
                   jax.ShapeDtypeStruct((b, l, 3 * BR_W), BF16)],
        compiler_params=_cparams("parallel", "parallel", "parallel"),
        name="hy_pre",
    )(proj3, proj3, proj3, conv_w)


def _dot_hi(a, b):
    return jnp.dot(a, b, preferred_element_type=F32, precision=lax.Precision.HIGHEST)


def _hy_filter_kernel(feat_ref, dist_ref, w1_ref, b1_ref, w2_ref, b2_ref, w3_ref, b3_ref, dec_ref, o_ref):
    hid = jnp.sin(_dot_hi(feat_ref[...], w1_ref[...]) + b1_ref[...])
    hid = jnp.sin(_dot_hi(hid, w2_ref[...]) + b2_ref[...])
    dist = dist_ref[...]
    for j in range(o_ref.shape[1] // D_HEAD):
        cols = slice(j * D_HEAD, (j + 1) * D_HEAD)
        filt = _dot_hi(hid, w3_ref[:, cols]) + b3_ref[:, cols]
        o_ref[:, cols] = (filt * jnp.exp(-dist * jnp.abs(dec_ref[:, cols]))).astype(o_ref.dtype)


def _hy_filter(l, w1, b1, w2, b2, w3, b3, decay):
    pos = jnp.arange(l, dtype=F32)
    t = pos / l
    ang = (2.0 * math.pi) * t[:, None] * jnp.arange(1, HY_BANDS + 1, dtype=F32)
    feat = jnp.concatenate([t[:, None], jnp.cos(ang), jnp.sin(ang)], axis=-1)
    dist = jnp.broadcast_to((jnp.abs(pos - l // 2) / l)[:, None], (l, D_HEAD))
    pad = D_HEAD
    emb, ff = w1.shape
    feat = jnp.pad(feat, ((0, 0), (0, pad - emb)))
    w1p = jnp.pad(w1, ((0, pad - emb), (0, pad - ff)))
    w2p = jnp.pad(w2, ((0, pad - ff), (0, pad - ff)))
    w3p = jnp.pad(w3, ((0, pad - ff), (0, 0)))
    b1p = jnp.pad(b1, (0, pad - ff)).reshape(1, pad)
    b2p = jnp.pad(b2, (0, pad - ff)).reshape(1, pad)
    tl = min(l, 256)
    n = 2 * BR_W
    fixed = lambda shape: pl.BlockSpec(shape, lambda i: (0, 0))
    return pl.pallas_call(
        _hy_filter_kernel,
        grid=(l // tl,),
        in_specs=[pl.BlockSpec((tl, pad), lambda i: (i, 0)),
                  pl.BlockSpec((tl, D_HEAD), lambda i: (i, 0)),
                  fixed((pad, pad)), fixed((1, pad)), fixed((pad, pad)), fixed((1, pad)),
                  fixed((pad, n)), fixed((1, n)), fixed((1, n))],
        out_specs=pl.BlockSpec((tl, n), lambda i: (i, 0)),
        out_shape=jax.ShapeDtypeStruct((l, n), BF16),
        compiler_params=_cparams("parallel"),
        name="hy_filter",
    )(feat, dist, w1p, b1p, w2p, b2p, w3p, b3.reshape(1, n), decay.reshape(1, n))


def _dft_matrices(l):
    n = 2 * l
    k = jnp.arange(l, dtype=jnp.int32)
    t = jnp.arange(l, dtype=jnp.int32)
    tp = t + l // 2
    split = 1 << (max(l.bit_length() - 1, 0) // 2)

    def tables(rows, cols):
        def table(r):
            ang = (2.0 * math.pi / n) * ((r[:, None] * cols[None, :]) % n).astype(F32)
            return jnp.cos(ang), jnp.sin(ang)
        return (*table(rows[::split]), *table(rows[:split] - rows[0]))

    alt = jnp.where(t % 2 == 0, 1.0, -1.0).astype(F32).reshape(1, l)
    wk = (jnp.where(k == 0, 1.0, 2.0).astype(F32) / n).reshape(1, l)
    out = pl.pallas_call(
        functools.partial(_dft_gen_kernel, split=split, l=l),
        grid=(l // split,),
        in_specs=[pl.BlockSpec((l // split, l), lambda i: (0, 0))] * 2 + [pl.BlockSpec((split, l), lambda i: (0, 0))] * 2
        + [pl.BlockSpec((l // split, l), lambda i: (0, 0))] * 2 + [pl.BlockSpec((split, l), lambda i: (0, 0))] * 2
        + [pl.BlockSpec((1, l), lambda i: (0, 0))] * 2,
        out_specs=[pl.BlockSpec((split, l), lambda i: (i, 0))] * 4,
        out_shape=[jax.ShapeDtypeStruct((l, l), BF16)] * 4,
        compiler_params=_cparams("parallel"),
        name="dft_gen",
    )(*tables(k, t), *tables(tp, k), alt, wk)
    return (out[0], out[1]), (out[2], out[3])


def _dft_gen_kernel(ch_ref, sh_ref, cl_ref, sl_ref, chi_ref, shi_ref, cli_ref, sli_ref, alt_ref, wk_ref,
                    fc_ref, fs_ref, ic_ref, is_ref, *, split, l):
    i = pl.program_id(0)

    def cos_sin(c_hi, s_hi, c_lo, s_lo):
        ch, sh = c_hi[pl.ds(i, 1), :], s_hi[pl.ds(i, 1), :]
        return ch * c_lo[...] - sh * s_lo[...], sh * c_lo[...] + ch * s_lo[...]

    row = lax.broadcasted_iota(jnp.int32, (split, l), 0) + i * split
    col = lax.broadcasted_iota(jnp.int32, (split, l), 1)
    c, s = cos_sin(ch_ref, sh_ref, cl_ref, sl_ref)
    fc_ref[...] = c.astype(BF16)
    fs_ref[...] = jnp.where(row == 0, alt_ref[...], -s).astype(BF16)
    c, s = cos_sin(chi_ref, shi_ref, cli_ref, sli_ref)
    wk = wk_ref[...]
    alt_i = jnp.where(row % 2 == 0, 1.0, -1.0) * (1.0 / (2 * l))
    ic_ref[...] = (c * wk).astype(BF16)
    is_ref[...] = jnp.where(col == 0, alt_i, -s * wk).astype(BF16)


def _seqs_per_step(b, l, rows=2048):
    bt = max(1, min(b, rows // l))
    while b % bt:
        bt -= 1
    return bt


def _dft_fwd_kernel(fc_ref, fs_ref, x_ref, *rest, with_filter, tm):
    for bb in range(x_ref.shape[0]):
        x = x_ref[bb]
        ur = jnp.dot(fc_ref[...], x, preferred_element_type=F32)
        ui = jnp.dot(fs_ref[...], x, preferred_element_type=F32)
        if not with_filter:
            zr_ref, zi_ref = rest
            zr_ref[bb] = ur
            zi_ref[bb] = ui
            continue
        hr_ref, hi_ref, zr_ref, zi_ref = rest
        hr, hi = hr_ref[0], hi_ref[0]
        row0 = (lax.broadcasted_iota(jnp.int32, ur.shape, 0) + pl.program_id(0) * tm) == 0
        zr_ref[bb] = (ur * hr - jnp.where(row0, 0.0, ui * hi)).astype(zr_ref.dtype)
        zi_ref[bb] = jnp.where(row0, ui * hi, ur * hi + ui * hr).astype(zi_ref.dtype)


def _dft_fwd(fwd, x, x_col0, c, spec_h=None, h_col0=0, tm=512, tn=512):
    b, l, _ = x.shape
    tm = min(tm, l)
    bt = _seqs_per_step(b, l)
    xo, ho = x_col0 // tn, h_col0 // tn
    out_dtype = F32 if spec_h is None else BF16
    fspec = pl.BlockSpec((tm, l), lambda i, bb, j: (i, 0))
    in_specs = [fspec, fspec, pl.BlockSpec((bt, l, tn), lambda i, bb, j: (bb, 0, xo + j))]
    args = [*fwd, x]
    if spec_h is not None:
        hspec = pl.BlockSpec((1, tm, tn), lambda i, bb, j: (0, i, ho + j))
        in_specs += [hspec, hspec]
        args += list(spec_h)
    ospec = pl.BlockSpec((bt, tm, tn), lambda i, bb, j: (bb, i, j))
    return pl.pallas_call(
        functools.partial(_dft_fwd_kernel, with_filter=spec_h is not None, tm=tm),
        grid=(l // tm, b // bt, c // tn),
        in_specs=in_specs,
        out_specs=[ospec, ospec],
        out_shape=[jax.ShapeDtypeStruct((b, l, c), out_dtype)] * 2,
        compiler_params=_cparams("parallel", "parallel", "parallel"),
        name="dft_fwd",
    )(*args)


def _dft_inv_kernel(ic_ref, is_ref, zr_ref, zi_ref, u_ref, m_ref, skip_ref, *rest, with_gate):
    for bb in range(zr_ref.shape[0]):
        y = (jnp.dot(ic_ref[...], zr_ref[bb], preferred_element_type=F32)
             + jnp.dot(is_ref[...], zi_ref[bb], preferred_element_type=F32))
        z = m_ref[bb] * (y + u_ref[bb] * skip_ref[...])
        if with_gate:
            g_ref, o_ref = rest
            gate = jnp.concatenate([g_ref[bb, c] for c in range(g_ref.shape[1])], axis=-1)
            o_ref[bb] = (z * _silu(gate)).astype(o_ref.dtype)
        else:
            o_ref, ob_ref = rest
            o_ref[bb] = z
            ob_ref[bb] = z.astype(BF16)


def _dft_inv(inv, zr, zi, u, u_col0, mul, mul_col0, skip, gate=None, gate_col0=0, tm=512, tn=512):
    b, l, c = zr.shape
    tm = min(tm, l)
    bt = _seqs_per_step(b, l)
    win = lambda col0: pl.BlockSpec((bt, tm, tn), lambda i, bb, j, o=col0 // tn: (bb, i, o + j))
    zspec = pl.BlockSpec((bt, l, tn), lambda i, bb, j: (bb, 0, j))
    fspec = pl.BlockSpec((tm, l), lambda i, bb, j: (i, 0))
    in_specs = [fspec, fspec, zspec, zspec,
                win(u_col0), win(mul_col0), pl.BlockSpec((1, tn), lambda i, bb, j: (0, j))]
    args = [*inv, zr, zi, u, mul, skip]
    ospec = pl.BlockSpec((bt, tm, tn), lambda i, bb, j: (bb, i, j))
    if gate is not None:
        in_specs.append(pl.BlockSpec((bt, tn // D_HEAD, tm, D_HEAD),
                                     lambda i, bb, j, o=gate_col0 // tn: (bb, o + j, i, 0)))
        args.append(gate)
        out_specs, out_shape = ospec, jax.ShapeDtypeStruct((b, l, c), BF16)
    else:
        out_specs = [ospec, ospec]
        out_shape = [jax.ShapeDtypeStruct((b, l, c), F32), jax.ShapeDtypeStruct((b, l, c), BF16)]
    return pl.pallas_call(
        functools.partial(_dft_inv_kernel, with_gate=gate is not None),
        grid=(l // tm, b // bt, c // tn),
        in_specs=in_specs,
        out_specs=out_specs,
        out_shape=out_shape,
        compiler_params=_cparams("parallel", "parallel", "parallel"),
        name="dft_inv",
    )(*args)


def _hyena(proj3, p, dft):
    l = proj3.shape[2]
    fwd, inv = dft
    filt = _hy_filter(l, p['hy_w1'], p['hy_b1'], p['hy_w2'], p['hy_b2'], p['hy_w3'], p['hy_b3'], p['hy_decay'])
    filt_b = filt[None]
    spec_h = _dft_fwd(fwd, filt_b, 0, 2 * BR_W)
    pre, pre_b = _hy_pre(proj3, p['hy_conv'])
    skip = p['hy_skip'].astype(F32)
    zr, zi = _dft_fwd(fwd, pre_b, 0, BR_W, spec_h, 0)
    z1, z1_b = _dft_inv(inv, zr, zi, pre, 0, pre, BR_W, skip[0:1])
    zr, zi = _dft_fwd(fwd, z1_b, 0, BR_W, spec_h, BR_W)
    return _dft_inv(inv, zr, zi, z1, 0, pre, 2 * BR_W, skip[1:2], gate=proj3, gate_col0=7 * BR_W)


def _softplus(x):
    return jnp.maximum(x, 0.0) + jnp.log1p(jnp.exp(-jnp.abs(x)))


def _split_bf16(x, parts):
    out = []
    for _ in range(parts - 1):
        piece = x.astype(BF16)
        out.append(piece)
        x = x - piece.astype(F32)
    out.append(x.astype(BF16))
    return out


def _bmm(a, b, hi=False):
    mm = lambda x, y: jnp.einsum('nij,njk->nik', x, y, preferred_element_type=F32)
    if not hi:
        return mm(a.astype(BF16), b.astype(BF16))
    (a1, a2), (b1, b2) = _split_bf16(a, 2), _split_bf16(b, 2)
    return mm(a1, b1) + (mm(a1, b2) + mm(a2, b1))


def _bmm_nt(a, b):
    return jnp.einsum('nid,njd->nij', a.astype(BF16), b.astype(BF16), preferred_element_type=F32)


TRI_BASE = 4


def _unit_tri_inverse(a, ri, ci):
    same = lambda w: (ri // w) == (ci // w)
    eye = (ri == ci).astype(F32)
    x = -jnp.where(same(TRI_BASE), a, 0.0)
    p = eye + x
    for _ in range(TRI_BASE.bit_length() - 2):
        x = _bmm(x, x, hi=True)
        p = p + _bmm(p, x, hi=True)
    w = TRI_BASE
    while w < a.shape[-1]:
        off = jnp.where(same(2 * w) & ~same(w), a, 0.0)
        p = p - _bmm(p, _bmm(off, p))
        w *= 2
    return p


def _gdn_prepare(q, k, v, ab, a_row, dt_row, head0, group):
    n, c, _ = q.shape
    two = lambda x: jnp.concatenate([x, x], axis=0)
    q, k, v, ab = two(q), two(k), two(v), two(ab)
    back3 = lambda shape: lax.broadcasted_iota(jnp.int32, shape, 0) >= n
    lane = lax.broadcasted_iota(jnp.int32, ab.shape, 2)
    bidx = lax.broadcasted_iota(jnp.int32, ab.shape, 0)
    head = head0 + jnp.where(bidx >= n, bidx - n, bidx) // group
    base = jnp.where(bidx >= n, 2 * N_HEAD, 0) + head
    g_all = -a_row * _softplus(ab + dt_row)
    g = jnp.sum(jnp.where(lane == base, g_all, 0.0), axis=2, keepdims=True)
    beta = jnp.sum(jnp.where(lane == base + N_HEAD, _sigmoid(ab), 0.0), axis=2, keepdims=True)

    sq = (2 * n, c, c)
    ri = lax.broadcasted_iota(jnp.int32, sq, 1)
    ci = lax.broadcasted_iota(jnp.int32, sq, 2)
    ahead = jnp.where(back3(sq), ci - ri, ri - ci)
    incl = ahead >= 0
    strict = ahead > 0
    tri = jnp.where(incl, 1.0, 0.0).astype(BF16)
    gc = sum(jnp.einsum('nij,njk->nik', tri, piece, preferred_element_type=F32)
             for piece in _split_bf16(jnp.broadcast_to(g, q.shape), 3))
    gc_row = jnp.swapaxes(gc, 1, 2)[:, :c, :]
    total = jnp.where(back3((2 * n, 1, D_HEAD)), gc[:, 0:1, :], gc[:, c - 1:c, :])
    decay = jnp.where(incl, jnp.exp(jnp.where(incl, gc[:, :, :c] - gc_row, 0.0)), 0.0)

    kb = k * beta
    a = jnp.where(strict, _bmm_nt(kb, k) * decay, 0.0)
    t = _unit_tri_inverse(a, ri, ci)
    e = jnp.exp(gc)
    u = _bmm(t, v * beta)
    w = _bmm(t, kb * e)
    a_intra = jnp.where(incl, _bmm_nt(q, k) * decay, 0.0)
    return (u, w.astype(BF16), (q * e).astype(BF16), (k * jnp.exp(total - gc)).astype(BF16),
            a_intra.astype(BF16), jnp.exp(total))


def _gdn_kernel(*refs, aliased, has_s0, group):
    if aliased:
        refs = refs[1:]
    if has_s0:
        (q_ref, k_ref, v_ref, z_ref, ab_ref, wq_ref, wk_ref, wv_ref, arow_ref, dt_ref, gn_ref, s0_ref,
         y_ref, sf_ref, qn, kn, vn, u_s, w_s, qd_s, kd_s, ai_s, gl_s) = refs
    else:
        (q_ref, k_ref, v_ref, z_ref, ab_ref, wq_ref, wk_ref, wv_ref, arow_ref, dt_ref, gn_ref,
         y_ref, sf_ref, qn, kn, vn, u_s, w_s, qd_s, kd_s, ai_s, gl_s) = refs
    _, heads, l, _ = q_ref.shape
    head0 = pl.program_id(1) * heads
    n_chunks = l // CHUNK
    hcols = lambda hh: slice(hh * D_HEAD, (hh + 1) * D_HEAD)

    def l2n(x):
        return x * lax.rsqrt(jnp.sum(x * x, axis=-1, keepdims=True) + EPS)

    for hh in range(heads):
        cols = hcols(hh)
        qn[:, cols] = l2n(_silu(_dwconv3(q_ref[0, hh], wq_ref.at[:, cols]))) * (D_HEAD ** -0.5)
        kn[:, cols] = l2n(_silu(_dwconv3(k_ref[0, hh], wk_ref.at[:, cols])))
        vn[:, cols] = _silu(_dwconv3(v_ref[0, hh], wv_ref.at[:, cols]))

    a_row, dt_row = arow_ref[...], dt_ref[...]

    def prepare(gi, carry):
        span = group * CHUNK
        rows = pl.ds(pl.multiple_of(gi * span, span), span)
        chunks = lambda x: x.reshape(group, CHUNK, x.shape[-1])
        per_head = lambda ref: jnp.concatenate([chunks(ref[rows, hcols(hh)]) for hh in range(heads)], axis=0)
        ab = chunks(ab_ref[0, rows, :])
        u, w, qd, kd, ai, gl = _gdn_prepare(per_head(qn), per_head(kn), per_head(vn),
                                            jnp.concatenate([ab] * heads, axis=0), a_row, dt_row, head0, group)
        for d in range(2):
            for hh in range(heads):
                cols = hcols(hh)
                part = slice((d * heads + hh) * group, (d * heads + hh + 1) * group)
                u_s[d, rows, cols] = u[part].reshape(span, D_HEAD)
                w_s[d, rows, cols] = w[part].reshape(span, D_HEAD)
                qd_s[d, rows, cols] = qd[part].reshape(span, D_HEAD)
                kd_s[d, rows, cols] = kd[part].reshape(span, D_HEAD)
                ai_s[d, hh, rows, :] = ai[part].reshape(span, CHUNK)
                gl_s[d, hh, pl.ds(gi * group, group)] = jnp.broadcast_to(gl[part], (group,) + gl_s.shape[3:])
        return carry

    lax.fori_loop(0, n_chunks // group, prepare, 0)

    def scan(i, s):
        where = [(hh, d, pl.ds(pl.multiple_of(chunk * CHUNK, CHUNK), CHUNK), chunk)
                 for hh in range(heads) for d, chunk in ((0, i), (1, n_chunks - 1 - i))]
        gather = lambda ref: jnp.stack([ref[d, rows, hcols(hh)] for hh, d, rows, _ in where])
        a_intra = jnp.stack([ai_s[d, hh, rows, :] for hh, d, rows, _ in where])
        decay = jnp.stack([gl_s[d, hh, chunk][0:1, :] for hh, d, _, chunk in where])
        sb = s.astype(BF16)
        v_new = gather(u_s) - _bmm(gather(w_s), sb)
        vb = v_new.astype(BF16)
        o = _bmm(gather(qd_s), sb) + _bmm(a_intra, vb)
        for idx, (hh, d, rows, _) in enumerate(where):
            u_s[d, rows, hcols(hh)] = o[idx]
        return s * decay + jnp.einsum('nik,niv->nkv', gather(kd_s), vb, preferred_element_type=F32)

    if has_s0:
        init = jnp.stack([s0_ref[0, 0, d, hh] for hh in range(heads) for d in range(2)])
    else:
        init = jnp.zeros((2 * heads, D_HEAD, D_HEAD), F32)
    final = lax.fori_loop(0, n_chunks, scan, init)
    for hh in range(heads):
        cols = hcols(hh)
        sf_ref[0, 0, 0, hh] = final[2 * hh]
        sf_ref[0, 0, 1, hh] = final[2 * hh + 1]
        y_ref[0, :, cols] = (_rms(u_s[0, :, cols] + u_s[1, :, cols], gn_ref[...])
                             * _silu(z_ref[0, hh])).astype(y_ref.dtype)


def _gdn(proj3, ab3, conv_w, a_log, dt_bias, norm_g, layer, state=None, new_state=None):
    b, _, l, _ = proj3.shape
    depth_out, layer_out = (1, 0) if state is not None else (DEPTH, layer)
    aliased = new_state is not None
    lanes = jnp.zeros((2, 2 * N_HEAD), F32).at[:, :N_HEAD].set(1.0)
    a_row = jnp.pad((jnp.exp(a_log.astype(F32))[:, None, :] * lanes.reshape(2, 2, N_HEAD)).reshape(1, -1),
                    ((0, 0), (0, AB_PAD - 4 * N_HEAD)))
    dt_row = jnp.pad((dt_bias.astype(F32)[:, None, :] * lanes.reshape(2, 2, N_HEAD)).reshape(1, -1),
                     ((0, 0), (0, AB_PAD - 4 * N_HEAD)))
    hps = N_HEAD if l <= 512 else 1
    wid = hps * D_HEAD
    n_hb = N_HEAD // hps
    blk = lambda c: pl.BlockSpec((1, hps, l, D_HEAD), lambda i, h, c=c: (i, c * n_hb + h, 0, 0))
    wblk = lambda c: pl.BlockSpec((3, wid), lambda i, h, c=c: (0, c * n_hb + h))
    row = pl.BlockSpec((1, D_HEAD), lambda i, h: (0, 0))
    in_specs = [blk(0), blk(1), blk(2), blk(3),
                pl.BlockSpec((1, l, AB_PAD), lambda i, h: (i, 0, 0)),
                wblk(0), wblk(1), wblk(2), row, row, row]
    args = [proj3, proj3, proj3, proj3, ab3, conv_w, conv_w, conv_w, a_row, dt_row, norm_g]
    if aliased:
        in_specs.insert(0, pl.BlockSpec(memory_space=pl.ANY))
        args.insert(0, new_state)
    if state is not None:
        in_specs.append(pl.BlockSpec((1, 1, 2, hps, D_HEAD, D_HEAD), lambda i, h: (i, layer, 0, h, 0, 0)))
        args.append(state)
    return pl.pallas_call(
        functools.partial(_gdn_kernel, aliased=aliased, has_s0=state is not None, group=min(512, l) // CHUNK),
        grid=(b, n_hb),
        in_specs=in_specs,
        out_specs=[pl.BlockSpec((1, l, wid), lambda i, h: (i, 0, h)),
                   pl.BlockSpec((1, 1, 2, hps, D_HEAD, D_HEAD), lambda i, h: (i, layer_out, 0, h, 0, 0))],
        out_shape=[jax.ShapeDtypeStruct((b, l, BR_W), BF16),
                   jax.ShapeDtypeStruct((b, depth_out, 2, N_HEAD, D_HEAD, D_HEAD), F32)],
        input_output_aliases={0: 1} if aliased else {},
        scratch_shapes=[pltpu.VMEM((l, wid), F32)] * 3
        + [pltpu.VMEM((2, l, wid), F32)] + [pltpu.VMEM((2, l, wid), BF16)] * 3
        + [pltpu.VMEM((2, hps, l, CHUNK), BF16), pltpu.VMEM((2, hps, l // CHUNK, 8, D_HEAD), F32)],
        compiler_params=_cparams("parallel", "parallel"),
        name="gdn",
    )(*args)


def _mod_kernel(c_ref, w_ref, b_ref, o_ref):
    o_ref[...] = _dot_hi(_silu(c_ref[...]), w_ref[...]) + b_ref[...]


def _modulation(cond, w_mod, b_mod, layer, tn=512):
    n = cond.shape[0]
    rows = 8
    out = pl.pallas_call(
        _mod_kernel,
        grid=(3 * D_MODEL // tn,),
        in_specs=[pl.BlockSpec((rows, D_MODEL), lambda j: (0, 0)),
                  pl.BlockSpec((None, D_MODEL, tn), lambda j: (layer, 0, j)),
                  pl.BlockSpec((1, tn), lambda j: (0, j))],
        out_specs=pl.BlockSpec((rows, tn), lambda j: (0, j)),
        out_shape=jax.ShapeDtypeStruct((rows, 3 * D_MODEL), F32),
        compiler_params=_cparams("parallel"),
        name="modulation",
    )(jnp.pad(cond.astype(F32), ((0, rows - n), (0, 0))), w_mod, b_mod.reshape(1, -1))
    return out[:n].reshape(n, 3, D_MODEL)


def _split_w_in(w_in):
    n_a = 4 * BR_W + 4 * N_HEAD
    w_in = w_in.astype(BF16)
    main = jnp.concatenate([w_in[..., :4 * BR_W], w_in[..., n_a:]], axis=-1)
    ab = jnp.pad(w_in[..., 4 * BR_W:n_a], ((0, 0),) * (w_in.ndim - 1) + ((0, AB_PAD - 4 * N_HEAD),))
    return main, ab


def _trunk_layer(x3, cond, p, big, layer, dft, latent, new_outputs=(None, None, None)):
    b, l, _ = x3.shape
    x2 = x3.reshape(b * l, D_MODEL)
    mod = _modulation(cond, big['w_mod'], p['b_mod'], layer)
    rows_per_mod = l if mod.shape[0] == b else b * l
    g_pre = p['g_pre'].reshape(1, D_MODEL)
    proj3, ab = _inproj(x2, mod, g_pre, big['w_main'], big['w_ab'], layer, rows_per_mod, l)
    ab3 = ab.reshape(b, l, AB_PAD)

    lam_init = 0.8 - 0.6 * math.exp(-0.3 * layer)
    lam_p = p['diff_lam'].astype(F32)
    lam = (jnp.exp(jnp.sum(lam_p[0] * lam_p[1])) - jnp.exp(jnp.sum(lam_p[2] * lam_p[3])) + lam_init).reshape(1, 1)
    diff_norm = p['diff_norm'].reshape(1, D_HEAD)
    gdn_args = (proj3, ab3, p['gdn_conv'], p['gdn_a_log'], p['gdn_dt_bias'], p['gdn_norm'].reshape(1, D_HEAD), layer)

    yb = _hyena(proj3, p, dft)
    if latent is None:
        new_state, nat_cache, diff_cache = new_outputs
        ya, new_state = _gdn(*gdn_args, new_state=new_state)
        yc, yd, nat_cache, diff_cache = _ctx_attention(proj3, lam, diff_norm, lam_init, layer, nat_cache, diff_cache)
        extras = (new_state, nat_cache, diff_cache)
    else:
        ya, _ = _gdn(*gdn_args, state=latent['state_gdn'])
        yc = _lat_nat(proj3, latent['cache_nat_kv'], layer, _nat_bias_table(p['nat_rpb']))
        yd = _lat_diff(proj3, latent['cache_diff_kv'], layer, lam, diff_norm, lam_init, latent['rope'])
        extras = None

    ys = [t.reshape(b * l, BR_W) for t in (ya, yb, yc, yd)]
    out = _merge(x2, mod, g_pre, p['g_post'].reshape(1, D_MODEL), ys, big['w_branch'], big['w_merge'],
                 p['b_merge'].reshape(1, -1).astype(F32), big['w_out'], layer, rows_per_mod)
    return out.reshape(b, l, D_MODEL), extras


def kernel(x_prompt, x_sample, state_gdn, cache_nat_kv, cache_diff_kv, c, c_ctx,
           w_mod, b_mod, g_pre, g_post, w_in, gdn_conv, gdn_a_log, gdn_dt_bias, gdn_norm,
           hy_conv, hy_w1, hy_b1, hy_w2, hy_b2, hy_w3, hy_b3, hy_decay, hy_skip,
           nat_rpb, diff_lam, diff_norm, w_branch, w_merge, b_merge, w_out):
    small = {
        'b_mod': b_mod, 'g_pre': g_pre, 'g_post': g_post,
        'gdn_conv': gdn_conv, 'gdn_a_log': gdn_a_log, 'gdn_dt_bias': gdn_dt_bias, 'gdn_norm': gdn_norm,
        'hy_conv': hy_conv, 'hy_w1': hy_w1, 'hy_b1': hy_b1, 'hy_w2': hy_w2, 'hy_b2': hy_b2,
        'hy_w3': hy_w3, 'hy_b3': hy_b3, 'hy_decay': hy_decay, 'hy_skip': hy_skip,
        'nat_rpb': nat_rpb, 'diff_lam': diff_lam, 'diff_norm': diff_norm, 'b_merge': b_merge,
    }
    layers = [{name: arr[i] for name, arr in small.items()} for i in range(DEPTH)]
    w_main, w_ab = _split_w_in(w_in)
    big = {'w_mod': w_mod.astype(F32), 'w_main': w_main, 'w_ab': w_ab, 'w_branch': w_branch.astype(BF16),
           'w_merge': w_merge.astype(BF16), 'w_out': w_out.astype(BF16)}

    y_prompt = x_prompt
    dft_ctx = _dft_matrices(x_prompt.shape[1])
    outputs = (None, None, None)
    for i, p in enumerate(layers):
        y_prompt, outputs = _trunk_layer(y_prompt, c_ctx.reshape(1, D_MODEL), p, big, i, dft_ctx, None, outputs)
    new_state, nat_cache, diff_cache = outputs

    y_sample = x_sample
    dft_lat = _dft_matrices(x_sample.shape[1])
    latent = {'state_gdn': state_gdn, 'cache_nat_kv': cache_nat_kv, 'cache_diff_kv': cache_diff_kv,
              'rope': _rope_tables(x_sample.shape[1])}
    for i, p in enumerate(layers):
        y_sample, _ = _trunk_layer(y_sample, c, p, big, i, dft_lat, latent)

    return (y_prompt, y_sample, new_state, nat_cache, diff_cache)
```

```python
import functools
import math

import jax
import jax.numpy as jnp
import numpy as np
from jax import lax
from jax.experimental import pallas as pl
from jax.experimental.pallas import tpu as pltpu

F32 = jnp.float32
BF16 = jnp.bfloat16

D_MODEL = 1024
DEPTH = 2
GRID_W = 64
N_BRANCH = 4
BR_W = 512
N_HEAD = 4
D_HEAD = 128
SUBLANES = 8
CHUNK = 128
HY_BANDS = 16
WIN_R = 8
WIN_C = 16
DQK_D = 64
ROPE_BASE = 10000.0
EPS = 1e-6
N_MAIN = 4 * 4 * BR_W
AB_PAD = 128
NEG_INF = -1e30

VMEM_LIMIT = 48 * 1024 * 1024


def _cparams(*sem):
    return pltpu.CompilerParams(dimension_semantics=sem, vmem_limit_bytes=VMEM_LIMIT)


def _silu(x):
    return x * (1.0 / (1.0 + jnp.exp(-x)))


def _sigmoid(x):
    return 1.0 / (1.0 + jnp.exp(-x))


def _rms(x, g):
    return x * lax.rsqrt(jnp.mean(x * x, axis=-1, keepdims=True) + EPS) * g


def _dot(a, b):
    return jnp.dot(a.astype(BF16), b.astype(BF16), preferred_element_type=F32)


def _dot_nt(a, b):
    return lax.dot_general(a.astype(BF16), b.astype(BF16), (((1,), (1,)), ((), ())),
                           preferred_element_type=F32)


def _dot_tn(a, b):
    return lax.dot_general(a.astype(BF16), b.astype(BF16), (((0,), (0,)), ((), ())),
                           preferred_element_type=F32)


def _prenorm(x, g_pre, mod_ref):
    return _rms(x, g_pre) * (1.0 + mod_ref[0, 1:2, :]) + mod_ref[0, 0:1, :]


def _inproj_kernel(x_ref, mod_ref, gpre_ref, w_ref, wab_ref, proj_ref, ab_ref, h_scr):
    @pl.when(pl.program_id(1) == 0)
    def _():
        h = _prenorm(x_ref[...], gpre_ref[...], mod_ref).astype(BF16)
        h_scr[...] = h
        ab_ref[...] = jnp.dot(h, wab_ref[...], preferred_element_type=F32)

    acc = jnp.dot(h_scr[...], w_ref[...], preferred_element_type=F32)
    seqs, cblocks, rows, _ = proj_ref.shape
    for sq in range(seqs):
        for c in range(cblocks):
            proj_ref[sq, c] = acc[sq * rows:(sq + 1) * rows, c * D_HEAD:(c + 1) * D_HEAD]


def _inproj(x2, mod, g_pre, w_main, w_ab, layer, rows_per_mod, l, tm=1024, tn=2048):
    m = x2.shape[0]
    tm = math.gcd(tm, rows_per_mod)
    if tm >= l:
        proj_spec = pl.BlockSpec((tm // l, tn // D_HEAD, l, D_HEAD), lambda i, j: (i, j, 0, 0))
    else:
        per = l // tm
        proj_spec = pl.BlockSpec((1, tn // D_HEAD, tm, D_HEAD), lambda i, j: (i // per, j, i % per, 0))
    return pl.pallas_call(
        _inproj_kernel,
        grid=(m // tm, N_MAIN // tn),
        in_specs=[
            pl.BlockSpec((tm, D_MODEL), lambda i, j: (i, 0)),
            pl.BlockSpec((1, 3, D_MODEL), lambda i, j: ((i * tm) // rows_per_mod, 0, 0)),
            pl.BlockSpec((1, D_MODEL), lambda i, j: (0, 0)),
            pl.BlockSpec((None, D_MODEL, tn), lambda i, j: (layer, 0, j)),
            pl.BlockSpec((None, D_MODEL, AB_PAD), lambda i, j: (layer, 0, 0)),
        ],
        out_specs=[
            proj_spec,
            pl.BlockSpec((tm, AB_PAD), lambda i, j: (i, 0)),
        ],
        out_shape=[jax.ShapeDtypeStruct((m // l, N_MAIN // D_HEAD, l, D_HEAD), F32),
                   jax.ShapeDtypeStruct((m, AB_PAD), F32)],
        scratch_shapes=[pltpu.VMEM((tm, D_MODEL), BF16)],
        compiler_params=_cparams("parallel", "arbitrary"),
        name="inproj",
    )(x2, mod, g_pre, w_main, w_ab)


def _merge_kernel(x_ref, mod_ref, gpre_ref, gpost_ref, ya_ref, yb_ref, yc_ref, yd_ref,
                  wbr_ref, wmg_ref, bmg_ref, wout_ref, o_ref):
    x = x_ref[...]
    h = _prenorm(x, gpre_ref[...], mod_ref).astype(BF16)
    acc = None
    for k, y_ref in enumerate((ya_ref, yb_ref, yc_ref, yd_ref)):
        cols = slice(k * D_MODEL, (k + 1) * D_MODEL)
        gate = _sigmoid(jnp.dot(h, wmg_ref[:, cols], preferred_element_type=F32) + bmg_ref[:, cols])
        br = jnp.dot(y_ref[...], wbr_ref[k], preferred_element_type=F32)
        acc = gate * br if acc is None else acc + gate * br
    y = jnp.dot(acc.astype(BF16), wout_ref[...], preferred_element_type=F32)
    o_ref[...] = x + mod_ref[0, 2:3, :] * _rms(y, gpost_ref[...])


def _merge(x2, mod, g_pre, g_post, ys, w_branch, w_merge, b_merge, w_out, layer, rows_per_mod, tm=512):
    m = x2.shape[0]
    row = lambda i: (i, 0)
    fixed2 = lambda i: (0, 0)
    return pl.pallas_call(
        _merge_kernel,
        grid=(m // tm,),
        in_specs=[
            pl.BlockSpec((tm, D_MODEL), row),
            pl.BlockSpec((1, 3, D_MODEL), lambda i: ((i * tm) // rows_per_mod, 0, 0)),
            pl.BlockSpec((1, D_MODEL), fixed2),
            pl.BlockSpec((1, D_MODEL), fixed2),
            pl.BlockSpec((tm, BR_W), row),
            pl.BlockSpec((tm, BR_W), row),
            pl.BlockSpec((tm, BR_W), row),
            pl.BlockSpec((tm, BR_W), row),
            pl.BlockSpec((None, N_BRANCH, BR_W, D_MODEL), lambda i: (layer, 0, 0, 0)),
            pl.BlockSpec((None, D_MODEL, N_BRANCH * D_MODEL), lambda i: (layer, 0, 0)),
            pl.BlockSpec((1, N_BRANCH * D_MODEL), fixed2),
            pl.BlockSpec((None, D_MODEL, D_MODEL), lambda i: (layer, 0, 0)),
        ],
        out_specs=pl.BlockSpec((tm, D_MODEL), row),
        out_shape=jax.ShapeDtypeStruct((m, D_MODEL), F32),
        compiler_params=_cparams("parallel"),
        name="merge",
    )(x2, mod, g_pre, g_post, *ys, w_branch, w_merge, b_merge, w_out)


def _softmax_rows(s):
    p = jnp.exp(s - jnp.max(s, axis=-1, keepdims=True))
    return p, jnp.sum(p, axis=-1, keepdims=True)


def _head_cols(h):
    return slice(h * D_HEAD, (h + 1) * D_HEAD)


def _stack_heads(ref):
    return ref[...].reshape(ref.shape[0] * ref.shape[1], *ref.shape[2:])


def _ctx_nat_kernel(*refs, aliased):
    q_ref, k_ref, v_ref, g_ref, y_ref, kv_ref = refs[1:] if aliased else refs
    scale = D_HEAD ** -0.5
    q, k, v = (_stack_heads(r) for r in (q_ref, k_ref, v_ref))
    p, l = _softmax_rows(_bmm_nt(q, k) * scale)
    o = _bmm(p, v) / l
    for bb in range(q_ref.shape[0]):
        for h in range(N_HEAD):
            n = bb * N_HEAD + h
            y_ref[bb, :, _head_cols(h)] = (o[n] * _silu(g_ref[bb, h])).astype(y_ref.dtype)
            kv_ref[bb, 0, 0, h] = k[n]
            kv_ref[bb, 0, 1, h] = v[n]


def _map_masks():
    lane = lax.broadcasted_iota(jnp.int32, (1, D_HEAD), 1)
    first = (lane < DQK_D).astype(F32)
    return first, 1.0 - first


def _ctx_diff_kernel(*refs, aliased, out_scale):
    lam_ref, q_ref, k_ref, v_ref, g_ref, gn_ref, y_ref, kv_ref = refs[1:] if aliased else refs
    scale = DQK_D ** -0.5
    m1, m2 = _map_masks()
    q, k, v = (_stack_heads(r) for r in (q_ref, k_ref, v_ref))
    nb = q.shape[0]
    p, l = _softmax_rows(_bmm_nt(jnp.concatenate([q * m1, q * m2], axis=0), jnp.concatenate([k, k], axis=0)) * scale)
    pn = p / l
    a = pn[:nb] - lam_ref[...] * pn[nb:]
    o = _rms(_bmm(a, v), gn_ref[...]) * out_scale
    for bb in range(q_ref.shape[0]):
        for h in range(N_HEAD):
            n = bb * N_HEAD + h
            y_ref[bb, :, _head_cols(h)] = (o[n] * _silu(g_ref[bb, h])).astype(y_ref.dtype)
            kv_ref[bb, 0, 0, h] = k[n]
            kv_ref[bb, 0, 1, h] = v[n]


def _ctx_attention(proj3, lam, diff_norm, lam_init, layer, nat_cache, diff_cache):
    b, _, l, _ = proj3.shape
    bt = _seqs_per_step(b, l, rows=1024)
    blk = lambda c: pl.BlockSpec((bt, N_HEAD, l, D_HEAD), lambda i, c=c: (i, c, 0, 0))
    y_spec = pl.BlockSpec((bt, l, BR_W), lambda i: (i, 0, 0))
    kv_spec = pl.BlockSpec((bt, 1, 2, N_HEAD, l, D_HEAD), lambda i: (i, layer, 0, 0, 0, 0))
    out_shape = [jax.ShapeDtypeStruct((b, l, BR_W), BF16),
                 jax.ShapeDtypeStruct((b, DEPTH, 2, N_HEAD, l, D_HEAD), F32)]
    aliased = nat_cache is not None
    cache_specs = [pl.BlockSpec(memory_space=pl.ANY)] if aliased else []
    aliases = {0: 1} if aliased else {}
    yc, nat_cache = pl.pallas_call(
        functools.partial(_ctx_nat_kernel, aliased=aliased),
        grid=(b // bt,),
        in_specs=cache_specs + [blk(8), blk(9), blk(10), blk(11)],
        out_specs=[y_spec, kv_spec],
        out_shape=out_shape,
        input_output_aliases=aliases,
        compiler_params=_cparams("parallel"),
        name="ctx_nat",
    )(*([nat_cache] if aliased else []), proj3, proj3, proj3, proj3)
    yd, diff_cache = pl.pallas_call(
        functools.partial(_ctx_diff_kernel, aliased=aliased, out_scale=1.0 - lam_init),
        grid=(b // bt,),
        in_specs=cache_specs + [pl.BlockSpec((1, 1), lambda i: (0, 0)),
                                blk(12), blk(13), blk(14), blk(15),
                                pl.BlockSpec((1, D_HEAD), lambda i: (0, 0))],
        out_specs=[y_spec, kv_spec],
        out_shape=out_shape,
        input_output_aliases=aliases,
        compiler_params=_cparams("parallel"),
        name="ctx_diff",
    )(*([diff_cache] if aliased else []), lam, proj3, proj3, proj3, proj3, diff_norm)
    return yc, yd, nat_cache, diff_cache


def _nat_bias_table(rpb):
    cols = np.arange(GRID_W)
    start = np.clip(cols - WIN_C // 2, 0, GRID_W - WIN_C)
    inside = (cols[None, :] >= start[:, None]) & (cols[None, :] < start[:, None] + WIN_C)
    dc = cols[None, :] - cols[:, None] + (WIN_C - 1)
    onehot = ((dc[None] == np.arange(2 * WIN_C - 1)[:, None, None]) & inside[None]).astype(np.float32)
    t = jnp.einsum('hdx,xck->hdck', rpb.astype(F32), jnp.asarray(onehot), precision=lax.Precision.HIGHEST)
    t = jnp.where(jnp.asarray(inside)[None, None], t, NEG_INF)
    tab = jnp.stack([t[:, WIN_R - 1 - off:2 * WIN_R - 1 - off] for off in range(WIN_R)], axis=1)
    return tab.transpose(0, 1, 3, 2, 4).reshape(rpb.shape[0], WIN_R, GRID_W, WIN_R * GRID_W)


def _lat_nat_kernel(q_ref, k_ref, v_ref, g_ref, ckv_ref, bias_ref, y_ref, kb_scr, vb_scr, *, rb):
    scale = D_HEAD ** -0.5
    rows = q_ref.shape[1] // GRID_W
    win = WIN_R * GRID_W
    kb_scr[...] = k_ref[0].astype(BF16)
    vb_scr[...] = v_ref[0].astype(BF16)
    ck = ckv_ref[0, 0, 0, 0].astype(BF16)
    cv = ckv_ref[0, 0, 1, 0].astype(BF16)

    def row_block(i, carry):
        q0 = pl.multiple_of(i * (rb * GRID_W), rb * GRID_W)
        qrows = pl.ds(q0, rb * GRID_W)
        q = q_ref[0, qrows, :].astype(BF16)
        kw, vw, bias = [], [], []
        for j in range(rb):
            r = i * rb + j
            rs = jnp.clip(r - WIN_R // 2, 0, rows - WIN_R)
            wrows = pl.ds(pl.multiple_of(rs * GRID_W, GRID_W), win)
            kw.append(kb_scr[wrows, :])
            vw.append(vb_scr[wrows, :])
            bias.append(bias_ref[0, r - rs])
        q3 = q.reshape(rb, GRID_W, D_HEAD)
        s_lat = _bmm_nt(q3, jnp.stack(kw)) * scale + jnp.stack(bias)
        s_ctx = (_dot_nt(q, ck) * scale).reshape(rb, GRID_W, ck.shape[0])
        m = jnp.maximum(jnp.max(s_lat, axis=-1, keepdims=True), jnp.max(s_ctx, axis=-1, keepdims=True))
        p_lat = jnp.exp(s_lat - m)
        p_ctx = jnp.exp(s_ctx - m)
        l = jnp.sum(p_lat, axis=-1, keepdims=True) + jnp.sum(p_ctx, axis=-1, keepdims=True)
        o_ctx = _dot(p_ctx.reshape(rb * GRID_W, ck.shape[0]), cv).reshape(rb, GRID_W, D_HEAD)
        o = ((_bmm(p_lat, jnp.stack(vw)) + o_ctx) / l).reshape(rb * GRID_W, D_HEAD)
        y_ref[0, qrows, :] = (o * _silu(g_ref[0, qrows, :])).astype(y_ref.dtype)
        return carry

    lax.fori_loop(0, rows // rb, row_block, 0)


def _lat_nat(proj3, cache_nat_kv, layer, bias_tab):
    b, _, l, _ = proj3.shape
    past = cache_nat_kv.shape[4]
    blk = lambda c: pl.BlockSpec((1, None, l, D_HEAD), lambda i, h, c=c: (i, c + h, 0, 0))
    return pl.pallas_call(
        functools.partial(_lat_nat_kernel, rb=math.gcd(32, l // GRID_W)),
        grid=(b, N_HEAD),
        in_specs=[blk(32), blk(36), blk(40), blk(44),
                  pl.BlockSpec((1, 1, 2, 1, past, D_HEAD), lambda i, h: (i, layer, 0, h, 0, 0)),
                  pl.BlockSpec((1, WIN_R, GRID_W, WIN_R * GRID_W), lambda i, h: (h, 0, 0, 0))],
        out_specs=pl.BlockSpec((1, l, D_HEAD), lambda i, h: (i, 0, h)),
        out_shape=jax.ShapeDtypeStruct((b, l, BR_W), BF16),
        scratch_shapes=[pltpu.VMEM((l, D_HEAD), BF16), pltpu.VMEM((l, D_HEAD), BF16)],
        compiler_params=_cparams("parallel", "parallel"),
        name="lat_nat",
    )(proj3, proj3, proj3, proj3, cache_nat_kv, bias_tab)


def _rope_tables(l):
    half = DQK_D // 2
    nf = half // 2
    t = jnp.arange(l)
    row = (t // GRID_W).astype(F32)
    col = (t % GRID_W).astype(F32)
    inv = ROPE_BASE ** (-jnp.arange(nf, dtype=F32) / nf)
    ang = jnp.concatenate([row[:, None] * inv, col[:, None] * inv], axis=-1)
    cos, sin = jnp.cos(ang), jnp.sin(ang)
    zero = jnp.zeros_like(sin)
    tile2 = lambda a, b: jnp.concatenate([a, b, a, b], axis=-1)
    return tile2(cos, cos), tile2(-sin, zero), tile2(zero, sin)


def _rope(x, cos, sin_a, sin_b):
    return x * cos + pltpu.roll(x, 96, 1) * sin_a + pltpu.roll(x, 32, 1) * sin_b


def _lat_diff_kernel(lam_ref, q_ref, k_ref, v_ref, g_ref, ckv_ref, gn_ref,
                     cq_ref, saq_ref, sbq_ref, ck_ref, sak_ref, sbk_ref,
                     y_ref, ks_scr, vt_scr, *, out_scale, prep_rows, key_block, ahead):
    scale = DQK_D ** -0.5
    l = k_ref.shape[1]

    @pl.when(pl.program_id(2) == 0)
    def _():
        def prep(i, carry):
            rows = pl.ds(pl.multiple_of(i * prep_rows, prep_rows), prep_rows)
            kr = _rope(k_ref[0, rows, :], ck_ref[rows, :], sak_ref[rows, :], sbk_ref[rows, :])
            ks_scr[rows, :] = kr.astype(BF16)
            vt_scr[:, rows] = v_ref[0, rows, :].T.astype(BF16)
            return carry

        lax.fori_loop(0, l // prep_rows, prep, 0)
        ks_scr[l:, :] = ckv_ref[0, 0, 0, 0].astype(BF16)
        vt_scr[:, l:] = ckv_ref[0, 0, 1, 0].T.astype(BF16)

    q = _rope(q_ref[0], cq_ref[...], saq_ref[...], sbq_ref[...]) * (scale * math.log2(math.e))
    m1, m2 = _map_masks()
    tq = q.shape[0]
    qm = jnp.concatenate([q * m1, q * m2], axis=0).astype(BF16)
    m = l_sum = acc = None
    n_blk = ks_scr.shape[0] // key_block
    block = lambda blk: slice(blk * key_block, (blk + 1) * key_block)
    scores = [_dot_nt(ks_scr[block(b), :], qm) for b in range(min(ahead, n_blk))]
    for blk in range(n_blk):
        rows = block(blk)
        s = scores.pop(0)
        if blk + ahead < n_blk:
            scores.append(_dot_nt(ks_scr[block(blk + ahead), :], qm))
        m_blk = jnp.max(s, axis=0, keepdims=True)
        if blk == 0:
            m = m_blk
            p = jnp.exp2(s - m)
            l_sum = jnp.sum(p, axis=0, keepdims=True)
            acc = _dot(vt_scr[:, rows], p)
        else:
            m_new = jnp.maximum(m, m_blk)
            alpha = jnp.exp2(m - m_new)
            p = jnp.exp2(s - m_new)
            l_sum = alpha * l_sum + jnp.sum(p, axis=0, keepdims=True)
            acc = alpha * acc + _dot(vt_scr[:, rows], p)
            m = m_new
    out = acc / l_sum
    d = out[:, :tq] - lam_ref[...] * out[:, tq:]
    d = d * lax.rsqrt(jnp.mean(d * d, axis=0, keepdims=True) + EPS)
    o = d.T * gn_ref[...] * out_scale
    y_ref[0] = (o * _silu(g_ref[0])).astype(y_ref.dtype)


def _lat_diff(proj3, cache_diff_kv, layer, lam, diff_norm, lam_init, rope_tabs, tq=1024):
    b, _, l, _ = proj3.shape
    past = cache_diff_kv.shape[4]
    qblk = lambda c: pl.BlockSpec((1, None, tq, D_HEAD), lambda i, h, j, c=c: (i, c + h, j, 0))
    full = lambda c: pl.BlockSpec((1, None, l, D_HEAD), lambda i, h, j, c=c: (i, c + h, 0, 0))
    tq_tab = pl.BlockSpec((tq, D_HEAD), lambda i, h, j: (j, 0))
    full_tab = pl.BlockSpec((l, D_HEAD), lambda i, h, j: (0, 0))
    return pl.pallas_call(
        functools.partial(_lat_diff_kernel, out_scale=1.0 - lam_init, prep_rows=512, key_block=512, ahead=2),
        grid=(b, N_HEAD, l // tq),
        in_specs=[pl.BlockSpec((1, 1), lambda i, h, j: (0, 0)),
                  qblk(48), full(52), full(56), qblk(60),
                  pl.BlockSpec((1, 1, 2, 1, past, D_HEAD), lambda i, h, j: (i, layer, 0, h, 0, 0)),
                  pl.BlockSpec((1, D_HEAD), lambda i, h, j: (0, 0)),
                  tq_tab, tq_tab, tq_tab, full_tab, full_tab, full_tab],
        out_specs=pl.BlockSpec((1, tq, D_HEAD), lambda i, h, j: (i, j, h)),
        out_shape=jax.ShapeDtypeStruct((b, l, BR_W), BF16),
        scratch_shapes=[pltpu.VMEM((l + past, D_HEAD), BF16), pltpu.VMEM((D_HEAD, l + past), BF16)],
        compiler_params=_cparams("parallel", "parallel", "arbitrary"),
        name="lat_diff",
    )(lam, proj3, proj3, proj3, proj3, cache_diff_kv, diff_norm, *rope_tabs, *rope_tabs)


def _dwconv3(x, w_ref):
    l = x.shape[0]
    row = lax.broadcasted_iota(jnp.int32, x.shape, 0)
    prev = jnp.where(row == 0, 0.0, pltpu.roll(x, 1, 0))
    nxt = jnp.where(row == l - 1, 0.0, pltpu.roll(x, l - 1, 0))
    return prev * w_ref[0:1, :] + x * w_ref[1:2, :] + nxt * w_ref[2:3, :]


def _hy_pre_kernel(x_ref, above_ref, below_ref, w_ref, o_ref, ob_ref):
    t, n_t = pl.program_id(1), pl.num_programs(1)
    bt, cblocks, rows, _ = x_ref.shape
    row = lax.broadcasted_iota(jnp.int32, (rows, D_HEAD), 0)
    for bb in range(bt):
        for c in range(cblocks):
            cols = _head_cols(c)
            x = x_ref[bb, c]
            before = jnp.where(t == 0, 0.0, above_ref[bb, c, SUBLANES - 1:SUBLANES, :])
            after = jnp.where(t == n_t - 1, 0.0, below_ref[bb, c, 0:1, :])
            prev = jnp.where(row == 0, before, pltpu.roll(x, 1, 0))
            nxt = jnp.where(row == rows - 1, after, pltpu.roll(x, rows - 1, 0))
            y = prev * w_ref[0:1, cols] + x * w_ref[1:2, cols] + nxt * w_ref[2:3, cols]
            o_ref[bb, :, cols] = y
            ob_ref[bb, :, cols] = y.astype(BF16)


def _hy_pre(proj3, conv_w, tl=2048):
    b, _, l, _ = proj3.shape
    tl = min(tl, l)
    bt = _seqs_per_step(b, l)
    n = 3
    cb = BR_W // D_HEAD
    col0 = 4
    groups = tl // SUBLANES
    last_group = l // SUBLANES - 1
    spec = pl.BlockSpec((bt, tl, BR_W), lambda i, t, j: (i, t, j))
    return pl.pallas_call(
        _hy_pre_kernel,
        grid=(b // bt, l // tl, n),
        in_specs=[pl.BlockSpec((bt, cb, tl, D_HEAD), lambda i, t, j: (i, col0 + j, t, 0)),
                  pl.BlockSpec((bt, cb, SUBLANES, D_HEAD),
                               lambda i, t, j: (i, col0 + j, jnp.maximum(t * groups - 1, 0), 0)),
                  pl.BlockSpec((bt, cb, SUBLANES, D_HEAD),
                               lambda i, t, j: (i, col0 + j, jnp.minimum((t + 1) * groups, last_group), 0)),
                  pl.BlockSpec((3, BR_W), lambda i, t, j: (0, j))],
        out_specs=[spec, spec],
        out_shape=[jax.ShapeDtypeStruct((b, l, 3 * BR_W), F32),
                   jax.ShapeDtypeStruct((b, l, 3 * BR_W), BF16)],
        compiler_params=_cparams("parallel", "parallel", "parallel"),
        name="hy_pre",
    )(proj3, proj3, proj3, conv_w)


def _dot_hi(a, b):
    return jnp.dot(a, b, preferred_element_type=F32, precision=lax.Precision.HIGHEST)


def _hy_filter_kernel(feat_ref, dist_ref, w1_ref, b1_ref, w2_ref, b2_ref, w3_ref, b3_ref, dec_ref, o_ref):
    hid = jnp.sin(_dot_hi(feat_ref[...], w1_ref[...]) + b1_ref[...])
    hid = jnp.sin(_dot_hi(hid, w2_ref[...]) + b2_ref[...])
    dist = dist_ref[...]
    for j in range(o_ref.shape[1] // D_HEAD):
        cols = slice(j * D_HEAD, (j + 1) * D_HEAD)
        filt = _dot_hi(hid, w3_ref[:, cols]) + b3_ref[:, cols]
        o_ref[:, cols] = (filt * jnp.exp(-dist * jnp.abs(dec_ref[:, cols]))).astype(o_ref.dtype)


def _hy_filter(l, w1, b1, w2, b2, w3, b3, decay):
    pos = jnp.arange(l, dtype=F32)
    t = pos / l
    ang = (2.0 * math.pi) * t[:, None] * jnp.arange(1, HY_BANDS + 1, dtype=F32)
    feat = jnp.concatenate([t[:, None], jnp.cos(ang), jnp.sin(ang)], axis=-1)
    dist = jnp.broadcast_to((jnp.abs(pos - l // 2) / l)[:, None], (l, D_HEAD))
    pad = D_HEAD
    emb, ff = w1.shape
    feat = jnp.pad(feat, ((0, 0), (0, pad - emb)))
    w1p = jnp.pad(w1, ((0, pad - emb), (0, pad - ff)))
    w2p = jnp.pad(w2, ((0, pad - ff), (0, pad - ff)))
    w3p = jnp.pad(w3, ((0, pad - ff), (0, 0)))
    b1p = jnp.pad(b1, (0, pad - ff)).reshape(1, pad)
    b2p = jnp.pad(b2, (0, pad - ff)).reshape(1, pad)
    tl = min(l, 256)
    n = 2 * BR_W
    fixed = lambda shape: pl.BlockSpec(shape, lambda i: (0, 0))
    return pl.pallas_call(
        _hy_filter_kernel,
        grid=(l // tl,),
        in_specs=[pl.BlockSpec((tl, pad), lambda i: (i, 0)),
                  pl.BlockSpec((tl, D_HEAD), lambda i: (i, 0)),
                  fixed((pad, pad)), fixed((1, pad)), fixed((pad, pad)), fixed((1, pad)),
                  fixed((pad, n)), fixed((1, n)), fixed((1, n))],
        out_specs=pl.BlockSpec((tl, n), lambda i: (i, 0)),
        out_shape=jax.ShapeDtypeStruct((l, n), BF16),
        compiler_params=_cparams("parallel"),
        name="hy_filter",
    )(feat, dist, w1p, b1p, w2p, b2p, w3p, b3.reshape(1, n), decay.reshape(1, n))


def _dft_matrices(l):
    n = 2 * l
    k = jnp.arange(l, dtype=jnp.int32)
    t = jnp.arange(l, dtype=jnp.int32)
    tp = t + l // 2
    split = 1 << (max(l.bit_length() - 1, 0) // 2)

    def tables(rows, cols):
        def table(r):
            ang = (2.0 * math.pi / n) * ((r[:, None] * cols[None, :]) % n).astype(F32)
            return jnp.cos(ang), jnp.sin(ang)
        return (*table(rows[::split]), *table(rows[:split] - rows[0]))

    alt = jnp.where(t % 2 == 0, 1.0, -1.0).astype(F32).reshape(1, l)
    wk = (jnp.where(k == 0, 1.0, 2.0).astype(F32) / n).reshape(1, l)
    out = pl.pallas_call(
        functools.partial(_dft_gen_kernel, split=split, l=l),
        grid=(l // split,),
        in_specs=[pl.BlockSpec((l // split, l), lambda i: (0, 0))] * 2 + [pl.BlockSpec((split, l), lambda i: (0, 0))] * 2
        + [pl.BlockSpec((l // split, l), lambda i: (0, 0))] * 2 + [pl.BlockSpec((split, l), lambda i: (0, 0))] * 2
        + [pl.BlockSpec((1, l), lambda i: (0, 0))] * 2,
        out_specs=[pl.BlockSpec((split, l), lambda i: (i, 0))] * 4,
        out_shape=[jax.ShapeDtypeStruct((l, l), BF16)] * 4,
        compiler_params=_cparams("parallel"),
        name="dft_gen",
    )(*tables(k, t), *tables(tp, k), alt, wk)
    return (out[0], out[1]), (out[2], out[3])


def _dft_gen_kernel(ch_ref, sh_ref, cl_ref, sl_ref, chi_ref, shi_ref, cli_ref, sli_ref, alt_ref, wk_ref,
                    fc_ref, fs_ref, ic_ref, is_ref, *, split, l):
    i = pl.program_id(0)

    def cos_sin(c_hi, s_hi, c_lo, s_lo):
        ch, sh = c_hi[pl.ds(i, 1), :], s_hi[pl.ds(i, 1), :]
        return ch * c_lo[...] - sh * s_lo[...], sh * c_lo[...] + ch * s_lo[...]

    row = lax.broadcasted_iota(jnp.int32, (split, l), 0) + i * split
    col = lax.broadcasted_iota(jnp.int32, (split, l), 1)
    c, s = cos_sin(ch_ref, sh_ref, cl_ref, sl_ref)
    fc_ref[...] = c.astype(BF16)
    fs_ref[...] = jnp.where(row == 0, alt_ref[...], -s).astype(BF16)
    c, s = cos_sin(chi_ref, shi_ref, cli_ref, sli_ref)
    wk = wk_ref[...]
    alt_i = jnp.where(row % 2 == 0, 1.0, -1.0) * (1.0 / (2 * l))
    ic_ref[...] = (c * wk).astype(BF16)
    is_ref[...] = jnp.where(col == 0, alt_i, -s * wk).astype(BF16)


def _seqs_per_step(b, l, rows=2048):
    bt = max(1, min(b, rows // l))
    while b % bt:
        bt -= 1
    return bt


def _dft_fwd_kernel(fc_ref, fs_ref, x_ref, *rest, with_filter, tm):
    for bb in range(x_ref.shape[0]):
        x = x_ref[bb]
        ur = jnp.dot(fc_ref[...], x, preferred_element_type=F32)
        ui = jnp.dot(fs_ref[...], x, preferred_element_type=F32)
        if not with_filter:
            zr_ref, zi_ref = rest
            zr_ref[bb] = ur
            zi_ref[bb] = ui
            continue
        hr_ref, hi_ref, zr_ref, zi_ref = rest
        hr, hi = hr_ref[0], hi_ref[0]
        row0 = (lax.broadcasted_iota(jnp.int32, ur.shape, 0) + pl.program_id(0) * tm) == 0
        zr_ref[bb] = (ur * hr - jnp.where(row0, 0.0, ui * hi)).astype(zr_ref.dtype)
        zi_ref[bb] = jnp.where(row0, ui * hi, ur * hi + ui * hr).astype(zi_ref.dtype)


def _dft_fwd(fwd, x, x_col0, c, spec_h=None, h_col0=0, tm=512, tn=512):
    b, l, _ = x.shape
    tm = min(tm, l)
    bt = _seqs_per_step(b, l)
    xo, ho = x_col0 // tn, h_col0 // tn
    out_dtype = F32 if spec_h is None else BF16
    fspec = pl.BlockSpec((tm, l), lambda i, bb, j: (i, 0))
    in_specs = [fspec, fspec, pl.BlockSpec((bt, l, tn), lambda i, bb, j: (bb, 0, xo + j))]
    args = [*fwd, x]
    if spec_h is not None:
        hspec = pl.BlockSpec((1, tm, tn), lambda i, bb, j: (0, i, ho + j))
        in_specs += [hspec, hspec]
        args += list(spec_h)
    ospec = pl.BlockSpec((bt, tm, tn), lambda i, bb, j: (bb, i, j))
    return pl.pallas_call(
        functools.partial(_dft_fwd_kernel, with_filter=spec_h is not None, tm=tm),
        grid=(l // tm, b // bt, c // tn),
        in_specs=in_specs,
        out_specs=[ospec, ospec],
        out_shape=[jax.ShapeDtypeStruct((b, l, c), out_dtype)] * 2,
        compiler_params=_cparams("parallel", "parallel", "parallel"),
        name="dft_fwd",
    )(*args)


def _dft_inv_kernel(ic_ref, is_ref, zr_ref, zi_ref, u_ref, m_ref, skip_ref, *rest, with_gate):
    for bb in range(zr_ref.shape[0]):
        y = (jnp.dot(ic_ref[...], zr_ref[bb], preferred_element_type=F32)
             + jnp.dot(is_ref[...], zi_ref[bb], preferred_element_type=F32))
        z = m_ref[bb] * (y + u_ref[bb] * skip_ref[...])
        if with_gate:
            g_ref, o_ref = rest
            gate = jnp.concatenate([g_ref[bb, c] for c in range(g_ref.shape[1])], axis=-1)
            o_ref[bb] = (z * _silu(gate)).astype(o_ref.dtype)
        else:
            o_ref, ob_ref = rest
            o_ref[bb] = z
            ob_ref[bb] = z.astype(BF16)


def _dft_inv(inv, zr, zi, u, u_col0, mul, mul_col0, skip, gate=None, gate_col0=0, tm=512, tn=512):
    b, l, c = zr.shape
    tm = min(tm, l)
    bt = _seqs_per_step(b, l)
    win = lambda col0: pl.BlockSpec((bt, tm, tn), lambda i, bb, j, o=col0 // tn: (bb, i, o + j))
    zspec = pl.BlockSpec((bt, l, tn), lambda i, bb, j: (bb, 0, j))
    fspec = pl.BlockSpec((tm, l), lambda i, bb, j: (i, 0))
    in_specs = [fspec, fspec, zspec, zspec,
                win(u_col0), win(mul_col0), pl.BlockSpec((1, tn), lambda i, bb, j: (0, j))]
    args = [*inv, zr, zi, u, mul, skip]
    ospec = pl.BlockSpec((bt, tm, tn), lambda i, bb, j: (bb, i, j))
    if gate is not None:
        in_specs.append(pl.BlockSpec((bt, tn // D_HEAD, tm, D_HEAD),
                                     lambda i, bb, j, o=gate_col0 // tn: (bb, o + j, i, 0)))
        args.append(gate)
        out_specs, out_shape = ospec, jax.ShapeDtypeStruct((b, l, c), BF16)
    else:
        out_specs = [ospec, ospec]
        out_shape = [jax.ShapeDtypeStruct((b, l, c), F32), jax.ShapeDtypeStruct((b, l, c), BF16)]
    return pl.pallas_call(
        functools.partial(_dft_inv_kernel, with_gate=gate is not None),
        grid=(l // tm, b // bt, c // tn),
        in_specs=in_specs,
        out_specs=out_specs,
        out_shape=out_shape,
        compiler_params=_cparams("parallel", "parallel", "parallel"),
        name="dft_inv",
    )(*args)


def _hyena(proj3, p, dft):
    l = proj3.shape[2]
    fwd, inv = dft
    filt = _hy_filter(l, p['hy_w1'], p['hy_b1'], p['hy_w2'], p['hy_b2'], p['hy_w3'], p['hy_b3'], p['hy_decay'])
    filt_b = filt[None]
    spec_h = _dft_fwd(fwd, filt_b, 0, 2 * BR_W)
    pre, pre_b = _hy_pre(proj3, p['hy_conv'])
    skip = p['hy_skip'].astype(F32)
    zr, zi = _dft_fwd(fwd, pre_b, 0, BR_W, spec_h, 0)
    z1, z1_b = _dft_inv(inv, zr, zi, pre, 0, pre, BR_W, skip[0:1])
    zr, zi = _dft_fwd(fwd, z1_b, 0, BR_W, spec_h, BR_W)
    return _dft_inv(inv, zr, zi, z1, 0, pre, 2 * BR_W, skip[1:2], gate=proj3, gate_col0=7 * BR_W)


def _softplus(x):
    return jnp.maximum(x, 0.0) + jnp.log1p(jnp.exp(-jnp.abs(x)))


def _split_bf16(x, parts):
    out = []
    for _ in range(parts - 1):
        piece = x.astype(BF16)
        out.append(piece)
        x = x - piece.astype(F32)
    out.append(x.astype(BF16))
    return out


def _bmm(a, b, hi=False):
    mm = lambda x, y: jnp.einsum('nij,njk->nik', x, y, preferred_element_type=F32)
    if not hi:
        return mm(a.astype(BF16), b.astype(BF16))
    (a1, a2), (b1, b2) = _split_bf16(a, 2), _split_bf16(b, 2)
    return mm(a1, b1) + (mm(a1, b2) + mm(a2, b1))


def _bmm_nt(a, b):
    return jnp.einsum('nid,njd->nij', a.astype(BF16), b.astype(BF16), preferred_element_type=F32)


TRI_BASE = 4


def _unit_tri_inverse(a, ri, ci):
    same = lambda w: (ri // w) == (ci // w)
    eye = (ri == ci).astype(F32)
    x = -jnp.where(same(TRI_BASE), a, 0.0)
    p = eye + x
    for _ in range(TRI_BASE.bit_length() - 2):
        x = _bmm(x, x, hi=True)
        p = p + _bmm(p, x, hi=True)
    w = TRI_BASE
    while w < a.shape[-1]:
        off = jnp.where(same(2 * w) & ~same(w), a, 0.0)
        p = p - _bmm(p, _bmm(off, p))
        w *= 2
    return p


def _gdn_prepare(q, k, v, ab, a_row, dt_row, head0, group):
    n, c, _ = q.shape
    two = lambda x: jnp.concatenate([x, x], axis=0)
    q, k, v, ab = two(q), two(k), two(v), two(ab)
    back3 = lambda shape: lax.broadcasted_iota(jnp.int32, shape, 0) >= n
    lane = lax.broadcasted_iota(jnp.int32, ab.shape, 2)
    bidx = lax.broadcasted_iota(jnp.int32, ab.shape, 0)
    head = head0 + jnp.where(bidx >= n, bidx - n, bidx) // group
    base = jnp.where(bidx >= n, 2 * N_HEAD, 0) + head
    g_all = -a_row * _softplus(ab + dt_row)
    g = jnp.sum(jnp.where(lane == base, g_all, 0.0), axis=2, keepdims=True)
    beta = jnp.sum(jnp.where(lane == base + N_HEAD, _sigmoid(ab), 0.0), axis=2, keepdims=True)

    sq = (2 * n, c, c)
    ri = lax.broadcasted_iota(jnp.int32, sq, 1)
    ci = lax.broadcasted_iota(jnp.int32, sq, 2)
    ahead = jnp.where(back3(sq), ci - ri, ri - ci)
    incl = ahead >= 0
    strict = ahead > 0
    tri = jnp.where(incl, 1.0, 0.0).astype(BF16)
    gc = sum(jnp.einsum('nij,njk->nik', tri, piece, preferred_element_type=F32)
             for piece in _split_bf16(jnp.broadcast_to(g, q.shape), 3))
    gc_row = jnp.swapaxes(gc, 1, 2)[:, :c, :]
    total = jnp.where(back3((2 * n, 1, D_HEAD)), gc[:, 0:1, :], gc[:, c - 1:c, :])
    decay = jnp.where(incl, jnp.exp(jnp.where(incl, gc[:, :, :c] - gc_row, 0.0)), 0.0)

    kb = k * beta
    a = jnp.where(strict, _bmm_nt(kb, k) * decay, 0.0)
    t = _unit_tri_inverse(a, ri, ci)
    e = jnp.exp(gc)
    u = _bmm(t, v * beta)
    w = _bmm(t, kb * e)
    a_intra = jnp.where(incl, _bmm_nt(q, k) * decay, 0.0)
    return (u, w.astype(BF16), (q * e).astype(BF16), (k * jnp.exp(total - gc)).astype(BF16),
            a_intra.astype(BF16), jnp.exp(total))


def _gdn_kernel(*refs, aliased, has_s0, group):
    if aliased:
        refs = refs[1:]
    if has_s0:
        (q_ref, k_ref, v_ref, z_ref, ab_ref, wq_ref, wk_ref, wv_ref, arow_ref, dt_ref, gn_ref, s0_ref,
         y_ref, sf_ref, qn, kn, vn, u_s, w_s, qd_s, kd_s, ai_s, gl_s) = refs
    else:
        (q_ref, k_ref, v_ref, z_ref, ab_ref, wq_ref, wk_ref, wv_ref, arow_ref, dt_ref, gn_ref,
         y_ref, sf_ref, qn, kn, vn, u_s, w_s, qd_s, kd_s, ai_s, gl_s) = refs
    _, heads, l, _ = q_ref.shape
    head0 = pl.program_id(1) * heads
    n_chunks = l // CHUNK
    hcols = lambda hh: slice(hh * D_HEAD, (hh + 1) * D_HEAD)

    def l2n(x):
        return x * lax.rsqrt(jnp.sum(x * x, axis=-1, keepdims=True) + EPS)

    for hh in range(heads):
        cols = hcols(hh)
        qn[:, cols] = l2n(_silu(_dwconv3(q_ref[0, hh], wq_ref.at[:, cols]))) * (D_HEAD ** -0.5)
        kn[:, cols] = l2n(_silu(_dwconv3(k_ref[0, hh], wk_ref.at[:, cols])))
        vn[:, cols] = _silu(_dwconv3(v_ref[0, hh], wv_ref.at[:, cols]))

    a_row, dt_row = arow_ref[...], dt_ref[...]

    def prepare(gi, carry):
        span = group * CHUNK
        rows = pl.ds(pl.multiple_of(gi * span, span), span)
        chunks = lambda x: x.reshape(group, CHUNK, x.shape[-1])
        per_head = lambda ref: jnp.concatenate([chunks(ref[rows, hcols(hh)]) for hh in range(heads)], axis=0)
        ab = chunks(ab_ref[0, rows, :])
        u, w, qd, kd, ai, gl = _gdn_prepare(per_head(qn), per_head(kn), per_head(vn),
                                            jnp.concatenate([ab] * heads, axis=0), a_row, dt_row, head0, group)
        for d in range(2):
            for hh in range(heads):
                cols = hcols(hh)
                part = slice((d * heads + hh) * group, (d * heads + hh + 1) * group)
                u_s[d, rows, cols] = u[part].reshape(span, D_HEAD)
                w_s[d, rows, cols] = w[part].reshape(span, D_HEAD)
                qd_s[d, rows, cols] = qd[part].reshape(span, D_HEAD)
                kd_s[d, rows, cols] = kd[part].reshape(span, D_HEAD)
                ai_s[d, hh, rows, :] = ai[part].reshape(span, CHUNK)
                gl_s[d, hh, pl.ds(gi * group, group)] = jnp.broadcast_to(gl[part], (group,) + gl_s.shape[3:])
        return carry

    lax.fori_loop(0, n_chunks // group, prepare, 0)

    def scan(i, s):
        where = [(hh, d, pl.ds(pl.multiple_of(chunk * CHUNK, CHUNK), CHUNK), chunk)
                 for hh in range(heads) for d, chunk in ((0, i), (1, n_chunks - 1 - i))]
        gather = lambda ref: jnp.stack([ref[d, rows, hcols(hh)] for hh, d, rows, _ in where])
        a_intra = jnp.stack([ai_s[d, hh, rows, :] for hh, d, rows, _ in where])
        decay = jnp.stack([gl_s[d, hh, chunk][0:1, :] for hh, d, _, chunk in where])
        sb = s.astype(BF16)
        v_new = gather(u_s) - _bmm(gather(w_s), sb)
        vb = v_new.astype(BF16)
        o = _bmm(gather(qd_s), sb) + _bmm(a_intra, vb)
        for idx, (hh, d, rows, _) in enumerate(where):
            u_s[d, rows, hcols(hh)] = o[idx]
        return s * decay + jnp.einsum('nik,niv->nkv', gather(kd_s), vb, preferred_element_type=F32)

    if has_s0:
        init = jnp.stack([s0_ref[0, 0, d, hh] for hh in range(heads) for d in range(2)])
    else:
        init = jnp.zeros((2 * heads, D_HEAD, D_HEAD), F32)
    final = lax.fori_loop(0, n_chunks, scan, init)
    for hh in range(heads):
        cols = hcols(hh)
        sf_ref[0, 0, 0, hh] = final[2 * hh]
        sf_ref[0, 0, 1, hh] = final[2 * hh + 1]
        y_ref[0, :, cols] = (_rms(u_s[0, :, cols] + u_s[1, :, cols], gn_ref[...])
                             * _silu(z_ref[0, hh])).astype(y_ref.dtype)


def _gdn(proj3, ab3, conv_w, a_log, dt_bias, norm_g, layer, state=None, new_state=None):
    b, _, l, _ = proj3.shape
    depth_out, layer_out = (1, 0) if state is not None else (DEPTH, layer)
    aliased = new_state is not None
    lanes = jnp.zeros((2, 2 * N_HEAD), F32).at[:, :N_HEAD].set(1.0)
    a_row = jnp.pad((jnp.exp(a_log.astype(F32))[:, None, :] * lanes.reshape(2, 2, N_HEAD)).reshape(1, -1),
                    ((0, 0), (0, AB_PAD - 4 * N_HEAD)))
    dt_row = jnp.pad((dt_bias.astype(F32)[:, None, :] * lanes.reshape(2, 2, N_HEAD)).reshape(1, -1),
                     ((0, 0), (0, AB_PAD - 4 * N_HEAD)))
    hps = N_HEAD if l <= 512 else 1
    wid = hps * D_HEAD
    n_hb = N_HEAD // hps
    blk = lambda c: pl.BlockSpec((1, hps, l, D_HEAD), lambda i, h, c=c: (i, c * n_hb + h, 0, 0))
    wblk = lambda c: pl.BlockSpec((3, wid), lambda i, h, c=c: (0, c * n_hb + h))
    row = pl.BlockSpec((1, D_HEAD), lambda i, h: (0, 0))
    in_specs = [blk(0), blk(1), blk(2), blk(3),
                pl.BlockSpec((1, l, AB_PAD), lambda i, h: (i, 0, 0)),
                wblk(0), wblk(1), wblk(2), row, row, row]
    args = [proj3, proj3, proj3, proj3, ab3, conv_w, conv_w, conv_w, a_row, dt_row, norm_g]
    if aliased:
        in_specs.insert(0, pl.BlockSpec(memory_space=pl.ANY))
        args.insert(0, new_state)
    if state is not None:
        in_specs.append(pl.BlockSpec((1, 1, 2, hps, D_HEAD, D_HEAD), lambda i, h: (i, layer, 0, h, 0, 0)))
        args.append(state)
    return pl.pallas_call(
        functools.partial(_gdn_kernel, aliased=aliased, has_s0=state is not None, group=min(512, l) // CHUNK),
        grid=(b, n_hb),
        in_specs=in_specs,
        out_specs=[pl.BlockSpec((1, l, wid), lambda i, h: (i, 0, h)),
                   pl.BlockSpec((1, 1, 2, hps, D_HEAD, D_HEAD), lambda i, h: (i, layer_out, 0, h, 0, 0))],
        out_shape=[jax.ShapeDtypeStruct((b, l, BR_W), BF16),
                   jax.ShapeDtypeStruct((b, depth_out, 2, N_HEAD, D_HEAD, D_HEAD), F32)],
        input_output_aliases={0: 1} if aliased else {},
        scratch_shapes=[pltpu.VMEM((l, wid), F32)] * 3
        + [pltpu.VMEM((2, l, wid), F32)] + [pltpu.VMEM((2, l, wid), BF16)] * 3
        + [pltpu.VMEM((2, hps, l, CHUNK), BF16), pltpu.VMEM((2, hps, l // CHUNK, 8, D_HEAD), F32)],
        compiler_params=_cparams("parallel", "parallel"),
        name="gdn",
    )(*args)


def _mod_kernel(c_ref, w_ref, b_ref, o_ref):
    o_ref[...] = _dot_hi(_silu(c_ref[...]), w_ref[...]) + b_ref[...]


def _modulation(cond, w_mod, b_mod, layer, tn=512):
    n = cond.shape[0]
    rows = 8
    out = pl.pallas_call(
        _mod_kernel,
        grid=(3 * D_MODEL // tn,),
        in_specs=[pl.BlockSpec((rows, D_MODEL), lambda j: (0, 0)),
                  pl.BlockSpec((None, D_MODEL, tn), lambda j: (layer, 0, j)),
                  pl.BlockSpec((1, tn), lambda j: (0, j))],
        out_specs=pl.BlockSpec((rows, tn), lambda j: (0, j)),
        out_shape=jax.ShapeDtypeStruct((rows, 3 * D_MODEL), F32),
        compiler_params=_cparams("parallel"),
        name="modulation",
    )(jnp.pad(cond.astype(F32), ((0, rows - n), (0, 0))), w_mod, b_mod.reshape(1, -1))
    return out[:n].reshape(n, 3, D_MODEL)


def _split_w_in(w_in):
    n_a = 4 * BR_W + 4 * N_HEAD
    w_in = w_in.astype(BF16)
    main = jnp.concatenate([w_in[..., :4 * BR_W], w_in[..., n_a:]], axis=-1)
    ab = jnp.pad(w_in[..., 4 * BR_W:n_a], ((0, 0),) * (w_in.ndim - 1) + ((0, AB_PAD - 4 * N_HEAD),))
    return main, ab


def _trunk_layer(x3, cond, p, big, layer, dft, latent, new_outputs=(None, None, None)):
    b, l, _ = x3.shape
    x2 = x3.reshape(b * l, D_MODEL)
    mod = _modulation(cond, big['w_mod'], p['b_mod'], layer)
    rows_per_mod = l if mod.shape[0] == b else b * l
    g_pre = p['g_pre'].reshape(1, D_MODEL)
    proj3, ab = _inproj(x2, mod, g_pre, big['w_main'], big['w_ab'], layer, rows_per_mod, l)
    ab3 = ab.reshape(b, l, AB_PAD)

    lam_init = 0.8 - 0.6 * math.exp(-0.3 * layer)
    lam_p = p['diff_lam'].astype(F32)
    lam = (jnp.exp(jnp.sum(lam_p[0] * lam_p[1])) - jnp.exp(jnp.sum(lam_p[2] * lam_p[3])) + lam_init).reshape(1, 1)
    diff_norm = p['diff_norm'].reshape(1, D_HEAD)
    gdn_args = (proj3, ab3, p['gdn_conv'], p['gdn_a_log'], p['gdn_dt_bias'], p['gdn_norm'].reshape(1, D_HEAD), layer)

    yb = _hyena(proj3, p, dft)
    if latent is None:
        new_state, nat_cache, diff_cache = new_outputs
        ya, new_state = _gdn(*gdn_args, new_state=new_state)
        yc, yd, nat_cache, diff_cache = _ctx_attention(proj3, lam, diff_norm, lam_init, layer, nat_cache, diff_cache)
        extras = (new_state, nat_cache, diff_cache)
    else:
        ya, _ = _gdn(*gdn_args, state=latent['state_gdn'])
        yc = _lat_nat(proj3, latent['cache_nat_kv'], layer, _nat_bias_table(p['nat_rpb']))
        yd = _lat_diff(proj3, latent['cache_diff_kv'], layer, lam, diff_norm, lam_init, latent['rope'])
        extras = None

    ys = [t.reshape(b * l, BR_W) for t in (ya, yb, yc, yd)]
    out = _merge(x2, mod, g_pre, p['g_post'].reshape(1, D_MODEL), ys, big['w_branch'], big['w_merge'],
                 p['b_merge'].reshape(1, -1).astype(F32), big['w_out'], layer, rows_per_mod)
    return out.reshape(b, l, D_MODEL), extras


def kernel(x_prompt, x_sample, state_gdn, cache_nat_kv, cache_diff_kv, c, c_ctx,
           w_mod, b_mod, g_pre, g_post, w_in, gdn_conv, gdn_a_log, gdn_dt_bias, gdn_norm,
           hy_conv, hy_w1, hy_b1, hy_w2, hy_b2, hy_w3, hy_b3, hy_decay, hy_skip,
           nat_rpb, diff_lam, diff_norm, w_branch, w_merge, b_merge, w_out):
    small = {
        'b_mod': b_mod, 'g_pre': g_pre, 'g_post': g_post,
        'gdn_conv': gdn_conv, 'gdn_a_log': gdn_a_log, 'gdn_dt_bias': gdn_dt_bias, 'gdn_norm': gdn_norm,
        'hy_conv': hy_conv, 'hy_w1': hy_w1, 'hy_b1': hy_b1, 'hy_w2': hy_w2, 'hy_b2': hy_b2,
        'hy_w3': hy_w3, 'hy_b3': hy_b3, 'hy_decay': hy_decay, 'hy_skip': hy_skip,
        'nat_rpb': nat_rpb, 'diff_lam': diff_lam, 'diff_norm': diff_norm, 'b_merge': b_merge,
    }
    layers = [{name: arr[i] for name, arr in small.items()} for i in range(DEPTH)]
    w_main, w_ab = _split_w_in(w_in)
    big = {'w_mod': w_mod.astype(F32), 'w_main': w_main, 'w_ab': w_ab, 'w_branch': w_branch.astype(BF16),
           'w_merge': w_merge.astype(BF16), 'w_out': w_out.astype(BF16)}

    y_prompt = x_prompt
    dft_ctx = _dft_matrices(x_prompt.shape[1])
    outputs = (None, None, None)
    for i, p in enumerate(layers):
        y_prompt, outputs = _trunk_layer(y_prompt, c_ctx.reshape(1, D_MODEL), p, big, i, dft_ctx, None, outputs)
    new_state, nat_cache, diff_cache = outputs

    y_sample = x_sample
    dft_lat = _dft_matrices(x_sample.shape[1])
    latent = {'state_gdn': state_gdn, 'cache_nat_kv': cache_nat_kv, 'cache_diff_kv': cache_diff_kv,
              'rope': _rope_tables(x_sample.shape[1])}
    for i, p in enumerate(layers):
        y_sample, _ = _trunk_layer(y_sample, c, p, big, i, dft_lat, latent)

    return (y_prompt, y_sample, new_state, nat_cache, diff_cache)
```

```python
import functools
import math

import jax
import jax.numpy as jnp
import numpy as np
from jax import lax
from jax.experimental import pallas as pl
from jax.experimental.pallas import tpu as pltpu

F32 = jnp.float32
BF16 = jnp.bfloat16

D_MODEL = 1024
DEPTH = 2
GRID_W = 64
N_BRANCH = 4
BR_W = 512
N_HEAD = 4
D_HEAD = 128
SUBLANES = 8
CHUNK = 128
HY_BANDS = 16
WIN_R = 8
WIN_C = 16
DQK_D = 64
ROPE_BASE = 10000.0
EPS = 1e-6
N_MAIN = 4 * 4 * BR_W
AB_PAD = 128
NEG_INF = -1e30

VMEM_LIMIT = 48 * 1024 * 1024


def _cparams(*sem):
    return pltpu.CompilerParams(dimension_semantics=sem, vmem_limit_bytes=VMEM_LIMIT)


def _silu(x):
    return x * (1.0 / (1.0 + jnp.exp(-x)))


def _sigmoid(x):
    return 1.0 / (1.0 + jnp.exp(-x))


def _rms(x, g):
    return x * lax.rsqrt(jnp.mean(x * x, axis=-1, keepdims=True) + EPS) * g


def _dot(a, b):
    return jnp.dot(a.astype(BF16), b.astype(BF16), preferred_element_type=F32)


def _dot_nt(a, b):
    return lax.dot_general(a.astype(BF16), b.astype(BF16), (((1,), (1,)), ((), ())),
                           preferred_element_type=F32)


def _dot_tn(a, b):
    return lax.dot_general(a.astype(BF16), b.astype(BF16), (((0,), (0,)), ((), ())),
                           preferred_element_type=F32)


def _prenorm(x, g_pre, mod_ref):
    return _rms(x, g_pre) * (1.0 + mod_ref[0, 1:2, :]) + mod_ref[0, 0:1, :]


def _inproj_kernel(x_ref, mod_ref, gpre_ref, w_ref, wab_ref, proj_ref, ab_ref, h_scr):
    @pl.when(pl.program_id(1) == 0)
    def _():
        h = _prenorm(x_ref[...], gpre_ref[...], mod_ref).astype(BF16)
        h_scr[...] = h
        ab_ref[...] = jnp.dot(h, wab_ref[...], preferred_element_type=F32)

    acc = jnp.dot(h_scr[...], w_ref[...], preferred_element_type=F32)
    seqs, cblocks, rows, _ = proj_ref.shape
    for sq in range(seqs):
        for c in range(cblocks):
            proj_ref[sq, c] = acc[sq * rows:(sq + 1) * rows, c * D_HEAD:(c + 1) * D_HEAD]


def _inproj(x2, mod, g_pre, w_main, w_ab, layer, rows_per_mod, l, tm=1024, tn=2048):
    m = x2.shape[0]
    tm = math.gcd(tm, rows_per_mod)
    if tm >= l:
        proj_spec = pl.BlockSpec((tm // l, tn // D_HEAD, l, D_HEAD), lambda i, j: (i, j, 0, 0))
    else:
        per = l // tm
        proj_spec = pl.BlockSpec((1, tn // D_HEAD, tm, D_HEAD), lambda i, j: (i // per, j, i % per, 0))
    return pl.pallas_call(
        _inproj_kernel,
        grid=(m // tm, N_MAIN // tn),
        in_specs=[
            pl.BlockSpec((tm, D_MODEL), lambda i, j: (i, 0)),
            pl.BlockSpec((1, 3, D_MODEL), lambda i, j: ((i * tm) // rows_per_mod, 0, 0)),
            pl.BlockSpec((1, D_MODEL), lambda i, j: (0, 0)),
            pl.BlockSpec((None, D_MODEL, tn), lambda i, j: (layer, 0, j)),
            pl.BlockSpec((None, D_MODEL, AB_PAD), lambda i, j: (layer, 0, 0)),
        ],
        out_specs=[
            proj_spec,
            pl.BlockSpec((tm, AB_PAD), lambda i, j: (i, 0)),
        ],
        out_shape=[jax.ShapeDtypeStruct((m // l, N_MAIN // D_HEAD, l, D_HEAD), F32),
                   jax.ShapeDtypeStruct((m, AB_PAD), F32)],
        scratch_shapes=[pltpu.VMEM((tm, D_MODEL), BF16)],
        compiler_params=_cparams("parallel", "arbitrary"),
        name="inproj",
    )(x2, mod, g_pre, w_main, w_ab)


def _merge_kernel(x_ref, mod_ref, gpre_ref, gpost_ref, ya_ref, yb_ref, yc_ref, yd_ref,
                  wbr_ref, wmg_ref, bmg_ref, wout_ref, o_ref):
    x = x_ref[...]
    h = _prenorm(x, gpre_ref[...], mod_ref).astype(BF16)
    acc = None
    for k, y_ref in enumerate((ya_ref, yb_ref, yc_ref, yd_ref)):
        cols = slice(k * D_MODEL, (k + 1) * D_MODEL)
        gate = _sigmoid(jnp.dot(h, wmg_ref[:, cols], preferred_element_type=F32) + bmg_ref[:, cols])
        br = jnp.dot(y_ref[...], wbr_ref[k], preferred_element_type=F32)
        acc = gate * br if acc is None else acc + gate * br
    y = jnp.dot(acc.astype(BF16), wout_ref[...], preferred_element_type=F32)
    o_ref[...] = x + mod_ref[0, 2:3, :] * _rms(y, gpost_ref[...])


def _merge(x2, mod, g_pre, g_post, ys, w_branch, w_merge, b_merge, w_out, layer, rows_per_mod, tm=512):
    m = x2.shape[0]
    row = lambda i: (i, 0)
    fixed2 = lambda i: (0, 0)
    return pl.pallas_call(
        _merge_kernel,
        grid=(m // tm,),
        in_specs=[
            pl.BlockSpec((tm, D_MODEL), row),
            pl.BlockSpec((1, 3, D_MODEL), lambda i: ((i * tm) // rows_per_mod, 0, 0)),
            pl.BlockSpec((1, D_MODEL), fixed2),
            pl.BlockSpec((1, D_MODEL), fixed2),
            pl.BlockSpec((tm, BR_W), row),
            pl.BlockSpec((tm, BR_W), row),
            pl.BlockSpec((tm, BR_W), row),
            pl.BlockSpec((tm, BR_W), row),
            pl.BlockSpec((None, N_BRANCH, BR_W, D_MODEL), lambda i: (layer, 0, 0, 0)),
            pl.BlockSpec((None, D_MODEL, N_BRANCH * D_MODEL), lambda i: (layer, 0, 0)),
            pl.BlockSpec((1, N_BRANCH * D_MODEL), fixed2),
            pl.BlockSpec((None, D_MODEL, D_MODEL), lambda i: (layer, 0, 0)),
        ],
        out_specs=pl.BlockSpec((tm, D_MODEL), row),
        out_shape=jax.ShapeDtypeStruct((m, D_MODEL), F32),
        compiler_params=_cparams("parallel"),
        name="merge",
    )(x2, mod, g_pre, g_post, *ys, w_branch, w_merge, b_merge, w_out)


def _softmax_rows(s):
    p = jnp.exp(s - jnp.max(s, axis=-1, keepdims=True))
    return p, jnp.sum(p, axis=-1, keepdims=True)


def _head_cols(h):
    return slice(h * D_HEAD, (h + 1) * D_HEAD)


def _stack_heads(ref):
    return ref[...].reshape(ref.shape[0] * ref.shape[1], *ref.shape[2:])


def _ctx_nat_kernel(*refs, aliased):
    q_ref, k_ref, v_ref, g_ref, y_ref, kv_ref = refs[1:] if aliased else refs
    scale = D_HEAD ** -0.5
    q, k, v = (_stack_heads(r) for r in (q_ref, k_ref, v_ref))
    p, l = _softmax_rows(_bmm_nt(q, k) * scale)
    o = _bmm(p, v) / l
    for bb in range(q_ref.shape[0]):
        for h in range(N_HEAD):
            n = bb * N_HEAD + h
            y_ref[bb, :, _head_cols(h)] = (o[n] * _silu(g_ref[bb, h])).astype(y_ref.dtype)
            kv_ref[bb, 0, 0, h] = k[n]
            kv_ref[bb, 0, 1, h] = v[n]


def _map_masks():
    lane = lax.broadcasted_iota(jnp.int32, (1, D_HEAD), 1)
    first = (lane < DQK_D).astype(F32)
    return first, 1.0 - first


def _ctx_diff_kernel(*refs, aliased, out_scale):
    lam_ref, q_ref, k_ref, v_ref, g_ref, gn_ref, y_ref, kv_ref = refs[1:] if aliased else refs
    scale = DQK_D ** -0.5
    m1, m2 = _map_masks()
    q, k, v = (_stack_heads(r) for r in (q_ref, k_ref, v_ref))
    nb = q.shape[0]
    p, l = _softmax_rows(_bmm_nt(jnp.concatenate([q * m1, q * m2], axis=0), jnp.concatenate([k, k], axis=0)) * scale)
    pn = p / l
    a = pn[:nb] - lam_ref[...] * pn[nb:]
    o = _rms(_bmm(a, v), gn_ref[...]) * out_scale
    for bb in range(q_ref.shape[0]):
        for h in range(N_HEAD):
            n = bb * N_HEAD + h
            y_ref[bb, :, _head_cols(h)] = (o[n] * _silu(g_ref[bb, h])).astype(y_ref.dtype)
            kv_ref[bb, 0, 0, h] = k[n]
            kv_ref[bb, 0, 1, h] = v[n]


def _ctx_attention(proj3, lam, diff_norm, lam_init, layer, nat_cache, diff_cache):
    b, _, l, _ = proj3.shape
    bt = _seqs_per_step(b, l, rows=1024)
    blk = lambda c: pl.BlockSpec((bt, N_HEAD, l, D_HEAD), lambda i, c=c: (i, c, 0, 0))
    y_spec = pl.BlockSpec((bt, l, BR_W), lambda i: (i, 0, 0))
    kv_spec = pl.BlockSpec((bt, 1, 2, N_HEAD, l, D_HEAD), lambda i: (i, layer, 0, 0, 0, 0))
    out_shape = [jax.ShapeDtypeStruct((b, l, BR_W), BF16),
                 jax.ShapeDtypeStruct((b, DEPTH, 2, N_HEAD, l, D_HEAD), F32)]
    aliased = nat_cache is not None
    cache_specs = [pl.BlockSpec(memory_space=pl.ANY)] if aliased else []
    aliases = {0: 1} if aliased else {}
    yc, nat_cache = pl.pallas_call(
        functools.partial(_ctx_nat_kernel, aliased=aliased),
        grid=(b // bt,),
        in_specs=cache_specs + [blk(8), blk(9), blk(10), blk(11)],
        out_specs=[y_spec, kv_spec],
        out_shape=out_shape,
        input_output_aliases=aliases,
        compiler_params=_cparams("parallel"),
        name="ctx_nat",
    )(*([nat_cache] if aliased else []), proj3, proj3, proj3, proj3)
    yd, diff_cache = pl.pallas_call(
        functools.partial(_ctx_diff_kernel, aliased=aliased, out_scale=1.0 - lam_init),
        grid=(b // bt,),
        in_specs=cache_specs + [pl.BlockSpec((1, 1), lambda i: (0, 0)),
                                blk(12), blk(13), blk(14), blk(15),
                                pl.BlockSpec((1, D_HEAD), lambda i: (0, 0))],
        out_specs=[y_spec, kv_spec],
        out_shape=out_shape,
        input_output_aliases=aliases,
        compiler_params=_cparams("parallel"),
        name="ctx_diff",
    )(*([diff_cache] if aliased else []), lam, proj3, proj3, proj3, proj3, diff_norm)
    return yc, yd, nat_cache, diff_cache


def _nat_bias_table(rpb):
    cols = np.arange(GRID_W)
    start = np.clip(cols - WIN_C // 2, 0, GRID_W - WIN_C)
    inside = (cols[None, :] >= start[:, None]) & (cols[None, :] < start[:, None] + WIN_C)
    dc = cols[None, :] - cols[:, None] + (WIN_C - 1)
    onehot = ((dc[None] == np.arange(2 * WIN_C - 1)[:, None, None]) & inside[None]).astype(np.float32)
    t = jnp.einsum('hdx,xck->hdck', rpb.astype(F32), jnp.asarray(onehot), precision=lax.Precision.HIGHEST)
    t = jnp.where(jnp.asarray(inside)[None, None], t, NEG_INF)
    tab = jnp.stack([t[:, WIN_R - 1 - off:2 * WIN_R - 1 - off] for off in range(WIN_R)], axis=1)
    return tab.transpose(0, 1, 3, 2, 4).reshape(rpb.shape[0], WIN_R, GRID_W, WIN_R * GRID_W)


def _lat_nat_kernel(q_ref, k_ref, v_ref, g_ref, ckv_ref, bias_ref, y_ref, kb_scr, vb_scr, *, rb):
    scale = D_HEAD ** -0.5
    rows = q_ref.shape[1] // GRID_W
    win = WIN_R * GRID_W
    kb_scr[...] = k_ref[0].astype(BF16)
    vb_scr[...] = v_ref[0].astype(BF16)
    ck = ckv_ref[0, 0, 0, 0].astype(BF16)
    cv = ckv_ref[0, 0, 1, 0].astype(BF16)

    def row_block(i, carry):
        q0 = pl.multiple_of(i * (rb * GRID_W), rb * GRID_W)
        qrows = pl.ds(q0, rb * GRID_W)
        q = q_ref[0, qrows, :].astype(BF16)
        kw, vw, bias = [], [], []
        for j in range(rb):
            r = i * rb + j
            rs = jnp.clip(r - WIN_R // 2, 0, rows - WIN_R)
            wrows = pl.ds(pl.multiple_of(rs * GRID_W, GRID_W), win)
            kw.append(kb_scr[wrows, :])
            vw.append(vb_scr[wrows, :])
            bias.append(bias_ref[0, r - rs])
        q3 = q.reshape(rb, GRID_W, D_HEAD)
        s_lat = _bmm_nt(q3, jnp.stack(kw)) * scale + jnp.stack(bias)
        s_ctx = (_dot_nt(q, ck) * scale).reshape(rb, GRID_W, ck.shape[0])
        m = jnp.maximum(jnp.max(s_lat, axis=-1, keepdims=True), jnp.max(s_ctx, axis=-1, keepdims=True))
        p_lat = jnp.exp(s_lat - m)
        p_ctx = jnp.exp(s_ctx - m)
        l = jnp.sum(p_lat, axis=-1, keepdims=True) + jnp.sum(p_ctx, axis=-1, keepdims=True)
        o_ctx = _dot(p_ctx.reshape(rb * GRID_W, ck.shape[0]), cv).reshape(rb, GRID_W, D_HEAD)
        o = ((_bmm(p_lat, jnp.stack(vw)) + o_ctx) / l).reshape(rb * GRID_W, D_HEAD)
        y_ref[0, qrows, :] = (o * _silu(g_ref[0, qrows, :])).astype(y_ref.dtype)
        return carry

    lax.fori_loop(0, rows // rb, row_block, 0)


def _lat_nat(proj3, cache_nat_kv, layer, bias_tab):
    b, _, l, _ = proj3.shape
    past = cache_nat_kv.shape[4]
    blk = lambda c: pl.BlockSpec((1, None, l, D_HEAD), lambda i, h, c=c: (i, c + h, 0, 0))
    return pl.pallas_call(
        functools.partial(_lat_nat_kernel, rb=math.gcd(32, l // GRID_W)),
        grid=(b, N_HEAD),
        in_specs=[blk(32), blk(36), blk(40), blk(44),
                  pl.BlockSpec((1, 1, 2, 1, past, D_HEAD), lambda i, h: (i, layer, 0, h, 0, 0)),
                  pl.BlockSpec((1, WIN_R, GRID_W, WIN_R * GRID_W), lambda i, h: (h, 0, 0, 0))],
        out_specs=pl.BlockSpec((1, l, D_HEAD), lambda i, h: (i, 0, h)),
        out_shape=jax.ShapeDtypeStruct((b, l, BR_W), BF16),
        scratch_shapes=[pltpu.VMEM((l, D_HEAD), BF16), pltpu.VMEM((l, D_HEAD), BF16)],
        compiler_params=_cparams("parallel", "parallel"),
        name="lat_nat",
    )(proj3, proj3, proj3, proj3, cache_nat_kv, bias_tab)


def _rope_tables(l):
    half = DQK_D // 2
    nf = half // 2
    t = jnp.arange(l)
    row = (t // GRID_W).astype(F32)
    col = (t % GRID_W).astype(F32)
    inv = ROPE_BASE ** (-jnp.arange(nf, dtype=F32) / nf)
    ang = jnp.concatenate([row[:, None] * inv, col[:, None] * inv], axis=-1)
    cos, sin = jnp.cos(ang), jnp.sin(ang)
    zero = jnp.zeros_like(sin)
    tile2 = lambda a, b: jnp.concatenate([a, b, a, b], axis=-1)
    return tile2(cos, cos), tile2(-sin, zero), tile2(zero, sin)


def _rope(x, cos, sin_a, sin_b):
    return x * cos + pltpu.roll(x, 96, 1) * sin_a + pltpu.roll(x, 32, 1) * sin_b


def _lat_diff_kernel(lam_ref, q_ref, k_ref, v_ref, g_ref, ckv_ref, gn_ref,
                     cq_ref, saq_ref, sbq_ref, ck_ref, sak_ref, sbk_ref,
                     y_ref, ks_scr, vt_scr, *, out_scale, prep_rows, key_block, ahead):
    scale = DQK_D ** -0.5
    l = k_ref.shape[1]

    @pl.when(pl.program_id(2) == 0)
    def _():
        def prep(i, carry):
            rows = pl.ds(pl.multiple_of(i * prep_rows, prep_rows), prep_rows)
            kr = _rope(k_ref[0, rows, :], ck_ref[rows, :], sak_ref[rows, :], sbk_ref[rows, :])
            ks_scr[rows, :] = kr.astype(BF16)
            vt_scr[:, rows] = v_ref[0, rows, :].T.astype(BF16)
            return carry

        lax.fori_loop(0, l // prep_rows, prep, 0)
        ks_scr[l:, :] = ckv_ref[0, 0, 0, 0].astype(BF16)
        vt_scr[:, l:] = ckv_ref[0, 0, 1, 0].T.astype(BF16)

    q = _rope(q_ref[0], cq_ref[...], saq_ref[...], sbq_ref[...]) * (scale * math.log2(math.e))
    m1, m2 = _map_masks()
    tq = q.shape[0]
    qm = jnp.concatenate([q * m1, q * m2], axis=0).astype(BF16)
    m = l_sum = acc = None
    n_blk = ks_scr.shape[0] // key_block
    block = lambda blk: slice(blk * key_block, (blk + 1) * key_block)
    scores = [_dot_nt(ks_scr[block(b), :], qm) for b in range(min(ahead, n_blk))]
    for blk in range(n_blk):
        rows = block(blk)
        s = scores.pop(0)
        if blk + ahead < n_blk:
            scores.append(_dot_nt(ks_scr[block(blk + ahead), :], qm))
        m_blk = jnp.max(s, axis=0, keepdims=True)
        if blk == 0:
            m = m_blk
            p = jnp.exp2(s - m)
            l_sum = jnp.sum(p, axis=0, keepdims=True)
            acc = _dot(vt_scr[:, rows], p)
        else:
            m_new = jnp.maximum(m, m_blk)
            alpha = jnp.exp2(m - m_new)
            p = jnp.exp2(s - m_new)
            l_sum = alpha * l_sum + jnp.sum(p, axis=0, keepdims=True)
            acc = alpha * acc + _dot(vt_scr[:, rows], p)
            m = m_new
    out = acc / l_sum
    d = out[:, :tq] - lam_ref[...] * out[:, tq:]
    d = d * lax.rsqrt(jnp.mean(d * d, axis=0, keepdims=True) + EPS)
    o = d.T * gn_ref[...] * out_scale
    y_ref[0] = (o * _silu(g_ref[0])).astype(y_ref.dtype)


def _lat_diff(proj3, cache_diff_kv, layer, lam, diff_norm, lam_init, rope_tabs, tq=1024):
    b, _, l, _ = proj3.shape
    past = cache_diff_kv.shape[4]
    qblk = lambda c: pl.BlockSpec((1, None, tq, D_HEAD), lambda i, h, j, c=c: (i, c + h, j, 0))
    full = lambda c: pl.BlockSpec((1, None, l, D_HEAD), lambda i, h, j, c=c: (i, c + h, 0, 0))
    tq_tab = pl.BlockSpec((tq, D_HEAD), lambda i, h, j: (j, 0))
    full_tab = pl.BlockSpec((l, D_HEAD), lambda i, h, j: (0, 0))
    return pl.pallas_call(
        functools.partial(_lat_diff_kernel, out_scale=1.0 - lam_init, prep_rows=512, key_block=512, ahead=3),
        grid=(b, N_HEAD, l // tq),
        in_specs=[pl.BlockSpec((1, 1), lambda i, h, j: (0, 0)),
                  qblk(48), full(52), full(56), qblk(60),
                  pl.BlockSpec((1, 1, 2, 1, past, D_HEAD), lambda i, h, j: (i, layer, 0, h, 0, 0)),
                  pl.BlockSpec((1, D_HEAD), lambda i, h, j: (0, 0)),
                  tq_tab, tq_tab, tq_tab, full_tab, full_tab, full_tab],
        out_specs=pl.BlockSpec((1, tq, D_HEAD), lambda i, h, j: (i, j, h)),
        out_shape=jax.ShapeDtypeStruct((b, l, BR_W), BF16),
        scratch_shapes=[pltpu.VMEM((l + past, D_HEAD), BF16), pltpu.VMEM((D_HEAD, l + past), BF16)],
        compiler_params=_cparams("parallel", "parallel", "arbitrary"),
        name="lat_diff",
    )(lam, proj3, proj3, proj3, proj3, cache_diff_kv, diff_norm, *rope_tabs, *rope_tabs)


def _dwconv3(x, w_ref):
    l = x.shape[0]
    row = lax.broadcasted_iota(jnp.int32, x.shape, 0)
    prev = jnp.where(row == 0, 0.0, pltpu.roll(x, 1, 0))
    nxt = jnp.where(row == l - 1, 0.0, pltpu.roll(x, l - 1, 0))
    return prev * w_ref[0:1, :] + x * w_ref[1:2, :] + nxt * w_ref[2:3, :]


def _hy_pre_kernel(x_ref, above_ref, below_ref, w_ref, o_ref, ob_ref):
    t, n_t = pl.program_id(1), pl.num_programs(1)
    bt, cblocks, rows, _ = x_ref.shape
    row = lax.broadcasted_iota(jnp.int32, (rows, D_HEAD), 0)
    for bb in range(bt):
        for c in range(cblocks):
            cols = _head_cols(c)
            x = x_ref[bb, c]
            before = jnp.where(t == 0, 0.0, above_ref[bb, c, SUBLANES - 1:SUBLANES, :])
            after = jnp.where(t == n_t - 1, 0.0, below_ref[bb, c, 0:1, :])
            prev = jnp.where(row == 0, before, pltpu.roll(x, 1, 0))
            nxt = jnp.where(row == rows - 1, after, pltpu.roll(x, rows - 1, 0))
            y = prev * w_ref[0:1, cols] + x * w_ref[1:2, cols] + nxt * w_ref[2:3, cols]
            o_ref[bb, :, cols] = y
            ob_ref[bb, :, cols] = y.astype(BF16)


def _hy_pre(proj3, conv_w, tl=2048):
    b, _, l, _ = proj3.shape
    tl = min(tl, l)
    bt = _seqs_per_step(b, l)
    n = 3
    cb = BR_W // D_HEAD
    col0 = 4
    groups = tl // SUBLANES
    last_group = l // SUBLANES - 1
    spec = pl.BlockSpec((bt, tl, BR_W), lambda i, t, j: (i, t, j))
    return pl.pallas_call(
        _hy_pre_kernel,
        grid=(b // bt, l // tl, n),
        in_specs=[pl.BlockSpec((bt, cb, tl, D_HEAD), lambda i, t, j: (i, col0 + j, t, 0)),
                  pl.BlockSpec((bt, cb, SUBLANES, D_HEAD),
                               lambda i, t, j: (i, col0 + j, jnp.maximum(t * groups - 1, 0), 0)),
                  pl.BlockSpec((bt, cb, SUBLANES, D_HEAD),
                               lambda i, t, j: (i, col0 + j, jnp.minimum((t + 1) * groups, last_group), 0)),
                  pl.BlockSpec((3, BR_W), lambda i, t, j: (0, j))],
        out_specs=[spec, spec],
        out_shape=[jax.ShapeDtypeStruct((b, l, 3 * BR_W), F32),
                   jax.ShapeDtypeStruct((b, l, 3 * BR_W), BF16)],
        compiler_params=_cparams("parallel", "parallel", "parallel"),
        name="hy_pre",
    )(proj3, proj3, proj3, conv_w)


def _dot_hi(a, b):
    return jnp.dot(a, b, preferred_element_type=F32, precision=lax.Precision.HIGHEST)


def _hy_filter_kernel(feat_ref, dist_ref, w1_ref, b1_ref, w2_ref, b2_ref, w3_ref, b3_ref, dec_ref, o_ref):
    hid = jnp.sin(_dot_hi(feat_ref[...], w1_ref[...]) + b1_ref[...])
    hid = jnp.sin(_dot_hi(hid, w2_ref[...]) + b2_ref[...])
    dist = dist_ref[...]
    for j in range(o_ref.shape[1] // D_HEAD):
        cols = slice(j * D_HEAD, (j + 1) * D_HEAD)
        filt = _dot_hi(hid, w3_ref[:, cols]) + b3_ref[:, cols]
        o_ref[:, cols] = (filt * jnp.exp(-dist * jnp.abs(dec_ref[:, cols]))).astype(o_ref.dtype)


def _hy_filter(l, w1, b1, w2, b2, w3, b3, decay):
    pos = jnp.arange(l, dtype=F32)
    t = pos / l
    ang = (2.0 * math.pi) * t[:, None] * jnp.arange(1, HY_BANDS + 1, dtype=F32)
    feat = jnp.concatenate([t[:, None], jnp.cos(ang), jnp.sin(ang)], axis=-1)
    dist = jnp.broadcast_to((jnp.abs(pos - l // 2) / l)[:, None], (l, D_HEAD))
    pad = D_HEAD
    emb, ff = w1.shape
    feat = jnp.pad(feat, ((0, 0), (0, pad - emb)))
    w1p = jnp.pad(w1, ((0, pad - emb), (0, pad - ff)))
    w2p = jnp.pad(w2, ((0, pad - ff), (0, pad - ff)))
    w3p = jnp.pad(w3, ((0, pad - ff), (0, 0)))
    b1p = jnp.pad(b1, (0, pad - ff)).reshape(1, pad)
    b2p = jnp.pad(b2, (0, pad - ff)).reshape(1, pad)
    tl = min(l, 256)
    n = 2 * BR_W
    fixed = lambda shape: pl.BlockSpec(shape, lambda i: (0, 0))
    return pl.pallas_call(
        _hy_filter_kernel,
        grid=(l // tl,),
        in_specs=[pl.BlockSpec((tl, pad), lambda i: (i, 0)),
                  pl.BlockSpec((tl, D_HEAD), lambda i: (i, 0)),
                  fixed((pad, pad)), fixed((1, pad)), fixed((pad, pad)), fixed((1, pad)),
                  fixed((pad, n)), fixed((1, n)), fixed((1, n))],
        out_specs=pl.BlockSpec((tl, n), lambda i: (i, 0)),
        out_shape=jax.ShapeDtypeStruct((l, n), BF16),
        compiler_params=_cparams("parallel"),
        name="hy_filter",
    )(feat, dist, w1p, b1p, w2p, b2p, w3p, b3.reshape(1, n), decay.reshape(1, n))


def _dft_matrices(l):
    n = 2 * l
    k = jnp.arange(l, dtype=jnp.int32)
    t = jnp.arange(l, dtype=jnp.int32)
    tp = t + l // 2
    split = 1 << (max(l.bit_length() - 1, 0) // 2)

    def tables(rows, cols):
        def table(r):
            ang = (2.0 * math.pi / n) * ((r[:, None] * cols[None, :]) % n).astype(F32)
            return jnp.cos(ang), jnp.sin(ang)
        return (*table(rows[::split]), *table(rows[:split] - rows[0]))

    alt = jnp.where(t % 2 == 0, 1.0, -1.0).astype(F32).reshape(1, l)
    wk = (jnp.where(k == 0, 1.0, 2.0).astype(F32) / n).reshape(1, l)
    out = pl.pallas_call(
        functools.partial(_dft_gen_kernel, split=split, l=l),
        grid=(l // split,),
        in_specs=[pl.BlockSpec((l // split, l), lambda i: (0, 0))] * 2 + [pl.BlockSpec((split, l), lambda i: (0, 0))] * 2
        + [pl.BlockSpec((l // split, l), lambda i: (0, 0))] * 2 + [pl.BlockSpec((split, l), lambda i: (0, 0))] * 2
        + [pl.BlockSpec((1, l), lambda i: (0, 0))] * 2,
        out_specs=[pl.BlockSpec((split, l), lambda i: (i, 0))] * 4,
        out_shape=[jax.ShapeDtypeStruct((l, l), BF16)] * 4,
        compiler_params=_cparams("parallel"),
        name="dft_gen",
    )(*tables(k, t), *tables(tp, k), alt, wk)
    return (out[0], out[1]), (out[2], out[3])


def _dft_gen_kernel(ch_ref, sh_ref, cl_ref, sl_ref, chi_ref, shi_ref, cli_ref, sli_ref, alt_ref, wk_ref,
                    fc_ref, fs_ref, ic_ref, is_ref, *, split, l):
    i = pl.program_id(0)

    def cos_sin(c_hi, s_hi, c_lo, s_lo):
        ch, sh = c_hi[pl.ds(i, 1), :], s_hi[pl.ds(i, 1), :]
        return ch * c_lo[...] - sh * s_lo[...], sh * c_lo[...] + ch * s_lo[...]

    row = lax.broadcasted_iota(jnp.int32, (split, l), 0) + i * split
    col = lax.broadcasted_iota(jnp.int32, (split, l), 1)
    c, s = cos_sin(ch_ref, sh_ref, cl_ref, sl_ref)
    fc_ref[...] = c.astype(BF16)
    fs_ref[...] = jnp.where(row == 0, alt_ref[...], -s).astype(BF16)
    c, s = cos_sin(chi_ref, shi_ref, cli_ref, sli_ref)
    wk = wk_ref[...]
    alt_i = jnp.where(row % 2 == 0, 1.0, -1.0) * (1.0 / (2 * l))
    ic_ref[...] = (c * wk).astype(BF16)
    is_ref[...] = jnp.where(col == 0, alt_i, -s * wk).astype(BF16)


def _seqs_per_step(b, l, rows=2048):
    bt = max(1, min(b, rows // l))
    while b % bt:
        bt -= 1
    return bt


def _dft_fwd_kernel(fc_ref, fs_ref, x_ref, *rest, with_filter, tm):
    for bb in range(x_ref.shape[0]):
        x = x_ref[bb]
        ur = jnp.dot(fc_ref[...], x, preferred_element_type=F32)
        ui = jnp.dot(fs_ref[...], x, preferred_element_type=F32)
        if not with_filter:
            zr_ref, zi_ref = rest
            zr_ref[bb] = ur
            zi_ref[bb] = ui
            continue
        hr_ref, hi_ref, zr_ref, zi_ref = rest
        hr, hi = hr_ref[0], hi_ref[0]
        row0 = (lax.broadcasted_iota(jnp.int32, ur.shape, 0) + pl.program_id(0) * tm) == 0
        zr_ref[bb] = (ur * hr - jnp.where(row0, 0.0, ui * hi)).astype(zr_ref.dtype)
        zi_ref[bb] = jnp.where(row0, ui * hi, ur * hi + ui * hr).astype(zi_ref.dtype)


def _dft_fwd(fwd, x, x_col0, c, spec_h=None, h_col0=0, tm=512, tn=512):
    b, l, _ = x.shape
    tm = min(tm, l)
    bt = _seqs_per_step(b, l)
    xo, ho = x_col0 // tn, h_col0 // tn
    out_dtype = F32 if spec_h is None else BF16
    fspec = pl.BlockSpec((tm, l), lambda i, bb, j: (i, 0))
    in_specs = [fspec, fspec, pl.BlockSpec((bt, l, tn), lambda i, bb, j: (bb, 0, xo + j))]
    args = [*fwd, x]
    if spec_h is not None:
        hspec = pl.BlockSpec((1, tm, tn), lambda i, bb, j: (0, i, ho + j))
        in_specs += [hspec, hspec]
        args += list(spec_h)
    ospec = pl.BlockSpec((bt, tm, tn), lambda i, bb, j: (bb, i, j))
    return pl.pallas_call(
        functools.partial(_dft_fwd_kernel, with_filter=spec_h is not None, tm=tm),
        grid=(l // tm, b // bt, c // tn),
        in_specs=in_specs,
        out_specs=[ospec, ospec],
        out_shape=[jax.ShapeDtypeStruct((b, l, c), out_dtype)] * 2,
        compiler_params=_cparams("parallel", "parallel", "parallel"),
        name="dft_fwd",
    )(*args)


def _dft_inv_kernel(ic_ref, is_ref, zr_ref, zi_ref, u_ref, m_ref, skip_ref, *rest, with_gate):
    for bb in range(zr_ref.shape[0]):
        y = (jnp.dot(ic_ref[...], zr_ref[bb], preferred_element_type=F32)
             + jnp.dot(is_ref[...], zi_ref[bb], preferred_element_type=F32))
        z = m_ref[bb] * (y + u_ref[bb] * skip_ref[...])
        if with_gate:
            g_ref, o_ref = rest
            gate = jnp.concatenate([g_ref[bb, c] for c in range(g_ref.shape[1])], axis=-1)
            o_ref[bb] = (z * _silu(gate)).astype(o_ref.dtype)
        else:
            o_ref, ob_ref = rest
            o_ref[bb] = z
            ob_ref[bb] = z.astype(BF16)


def _dft_inv(inv, zr, zi, u, u_col0, mul, mul_col0, skip, gate=None, gate_col0=0, tm=512, tn=512):
    b, l, c = zr.shape
    tm = min(tm, l)
    bt = _seqs_per_step(b, l)
    win = lambda col0: pl.BlockSpec((bt, tm, tn), lambda i, bb, j, o=col0 // tn: (bb, i, o + j))
    zspec = pl.BlockSpec((bt, l, tn), lambda i, bb, j: (bb, 0, j))
    fspec = pl.BlockSpec((tm, l), lambda i, bb, j: (i, 0))
    in_specs = [fspec, fspec, zspec, zspec,
                win(u_col0), win(mul_col0), pl.BlockSpec((1, tn), lambda i, bb, j: (0, j))]
    args = [*inv, zr, zi, u, mul, skip]
    ospec = pl.BlockSpec((bt, tm, tn), lambda i, bb, j: (bb, i, j))
    if gate is not None:
        in_specs.append(pl.BlockSpec((bt, tn // D_HEAD, tm, D_HEAD),
                                     lambda i, bb, j, o=gate_col0 // tn: (bb, o + j, i, 0)))
        args.append(gate)
        out_specs, out_shape = ospec, jax.ShapeDtypeStruct((b, l, c), BF16)
    else:
        out_specs = [ospec, ospec]
        out_shape = [jax.ShapeDtypeStruct((b, l, c), F32), jax.ShapeDtypeStruct((b, l, c), BF16)]
    return pl.pallas_call(
        functools.partial(_dft_inv_kernel, with_gate=gate is not None),
        grid=(l // tm, b // bt, c // tn),
        in_specs=in_specs,
        out_specs=out_specs,
        out_shape=out_shape,
        compiler_params=_cparams("parallel", "parallel", "parallel"),
        name="dft_inv",
    )(*args)


def _hyena(proj3, p, dft):
    l = proj3.shape[2]
    fwd, inv = dft
    filt = _hy_filter(l, p['hy_w1'], p['hy_b1'], p['hy_w2'], p['hy_b2'], p['hy_w3'], p['hy_b3'], p['hy_decay'])
    filt_b = filt[None]
    spec_h = _dft_fwd(fwd, filt_b, 0, 2 * BR_W)
    pre, pre_b = _hy_pre(proj3, p['hy_conv'])
    skip = p['hy_skip'].astype(F32)
    zr, zi = _dft_fwd(fwd, pre_b, 0, BR_W, spec_h, 0)
    z1, z1_b = _dft_inv(inv, zr, zi, pre, 0, pre, BR_W, skip[0:1])
    zr, zi = _dft_fwd(fwd, z1_b, 0, BR_W, spec_h, BR_W)
    return _dft_inv(inv, zr, zi, z1, 0, pre, 2 * BR_W, skip[1:2], gate=proj3, gate_col0=7 * BR_W)


def _softplus(x):
    return jnp.maximum(x, 0.0) + jnp.log1p(jnp.exp(-jnp.abs(x)))


def _split_bf16(x, parts):
    out = []
    for _ in range(parts - 1):
        piece = x.astype(BF16)
        out.append(piece)
        x = x - piece.astype(F32)
    out.append(x.astype(BF16))
    return out


def _bmm(a, b, hi=False):
    mm = lambda x, y: jnp.einsum('nij,njk->nik', x, y, preferred_element_type=F32)
    if not hi:
        return mm(a.astype(BF16), b.astype(BF16))
    (a1, a2), (b1, b2) = _split_bf16(a, 2), _split_bf16(b, 2)
    return mm(a1, b1) + (mm(a1, b2) + mm(a2, b1))


def _bmm_nt(a, b):
    return jnp.einsum('nid,njd->nij', a.astype(BF16), b.astype(BF16), preferred_element_type=F32)


TRI_BASE = 4


def _unit_tri_inverse(a, ri, ci):
    same = lambda w: (ri // w) == (ci // w)
    eye = (ri == ci).astype(F32)
    x = -jnp.where(same(TRI_BASE), a, 0.0)
    p = eye + x
    for _ in range(TRI_BASE.bit_length() - 2):
        x = _bmm(x, x, hi=True)
        p = p + _bmm(p, x, hi=True)
    w = TRI_BASE
    while w < a.shape[-1]:
        off = jnp.where(same(2 * w) & ~same(w), a, 0.0)
        p = p - _bmm(p, _bmm(off, p))
        w *= 2
    return p


def _gdn_prepare(q, k, v, ab, a_row, dt_row, head0, group):
    n, c, _ = q.shape
    two = lambda x: jnp.concatenate([x, x], axis=0)
    q, k, v, ab = two(q), two(k), two(v), two(ab)
    back3 = lambda shape: lax.broadcasted_iota(jnp.int32, shape, 0) >= n
    lane = lax.broadcasted_iota(jnp.int32, ab.shape, 2)
    bidx = lax.broadcasted_iota(jnp.int32, ab.shape, 0)
    head = head0 + jnp.where(bidx >= n, bidx - n, bidx) // group
    base = jnp.where(bidx >= n, 2 * N_HEAD, 0) + head
    g_all = -a_row * _softplus(ab + dt_row)
    g = jnp.sum(jnp.where(lane == base, g_all, 0.0), axis=2, keepdims=True)
    beta = jnp.sum(jnp.where(lane == base + N_HEAD, _sigmoid(ab), 0.0), axis=2, keepdims=True)

    sq = (2 * n, c, c)
    ri = lax.broadcasted_iota(jnp.int32, sq, 1)
    ci = lax.broadcasted_iota(jnp.int32, sq, 2)
    ahead = jnp.where(back3(sq), ci - ri, ri - ci)
    incl = ahead >= 0
    strict = ahead > 0
    tri = jnp.where(incl, 1.0, 0.0).astype(BF16)
    gc = sum(jnp.einsum('nij,njk->nik', tri, piece, preferred_element_type=F32)
             for piece in _split_bf16(jnp.broadcast_to(g, q.shape), 3))
    gc_row = jnp.swapaxes(gc, 1, 2)[:, :c, :]
    total = jnp.where(back3((2 * n, 1, D_HEAD)), gc[:, 0:1, :], gc[:, c - 1:c, :])
    decay = jnp.where(incl, jnp.exp(jnp.where(incl, gc[:, :, :c] - gc_row, 0.0)), 0.0)

    kb = k * beta
    a = jnp.where(strict, _bmm_nt(kb, k) * decay, 0.0)
    t = _unit_tri_inverse(a, ri, ci)
    e = jnp.exp(gc)
    u = _bmm(t, v * beta)
    w = _bmm(t, kb * e)
    a_intra = jnp.where(incl, _bmm_nt(q, k) * decay, 0.0)
    return (u, w.astype(BF16), (q * e).astype(BF16), (k * jnp.exp(total - gc)).astype(BF16),
            a_intra.astype(BF16), jnp.exp(total))


def _gdn_kernel(*refs, aliased, has_s0, group):
    if aliased:
        refs = refs[1:]
    if has_s0:
        (q_ref, k_ref, v_ref, z_ref, ab_ref, wq_ref, wk_ref, wv_ref, arow_ref, dt_ref, gn_ref, s0_ref,
         y_ref, sf_ref, qn, kn, vn, u_s, w_s, qd_s, kd_s, ai_s, gl_s) = refs
    else:
        (q_ref, k_ref, v_ref, z_ref, ab_ref, wq_ref, wk_ref, wv_ref, arow_ref, dt_ref, gn_ref,
         y_ref, sf_ref, qn, kn, vn, u_s, w_s, qd_s, kd_s, ai_s, gl_s) = refs
    _, heads, l, _ = q_ref.shape
    head0 = pl.program_id(1) * heads
    n_chunks = l // CHUNK
    hcols = lambda hh: slice(hh * D_HEAD, (hh + 1) * D_HEAD)

    def l2n(x):
        return x * lax.rsqrt(jnp.sum(x * x, axis=-1, keepdims=True) + EPS)

    for hh in range(heads):
        cols = hcols(hh)
        qn[:, cols] = l2n(_silu(_dwconv3(q_ref[0, hh], wq_ref.at[:, cols]))) * (D_HEAD ** -0.5)
        kn[:, cols] = l2n(_silu(_dwconv3(k_ref[0, hh], wk_ref.at[:, cols])))
        vn[:, cols] = _silu(_dwconv3(v_ref[0, hh], wv_ref.at[:, cols]))

    a_row, dt_row = arow_ref[...], dt_ref[...]

    def prepare(gi, carry):
        span = group * CHUNK
        rows = pl.ds(pl.multiple_of(gi * span, span), span)
        chunks = lambda x: x.reshape(group, CHUNK, x.shape[-1])
        per_head = lambda ref: jnp.concatenate([chunks(ref[rows, hcols(hh)]) for hh in range(heads)], axis=0)
        ab = chunks(ab_ref[0, rows, :])
        u, w, qd, kd, ai, gl = _gdn_prepare(per_head(qn), per_head(kn), per_head(vn),
                                            jnp.concatenate([ab] * heads, axis=0), a_row, dt_row, head0, group)
        for d in range(2):
            for hh in range(heads):
                cols = hcols(hh)
                part = slice((d * heads + hh) * group, (d * heads + hh + 1) * group)
                u_s[d, rows, cols] = u[part].reshape(span, D_HEAD)
                w_s[d, rows, cols] = w[part].reshape(span, D_HEAD)
                qd_s[d, rows, cols] = qd[part].reshape(span, D_HEAD)
                kd_s[d, rows, cols] = kd[part].reshape(span, D_HEAD)
                ai_s[d, hh, rows, :] = ai[part].reshape(span, CHUNK)
                gl_s[d, hh, pl.ds(gi * group, group)] = jnp.broadcast_to(gl[part], (group,) + gl_s.shape[3:])
        return carry

    lax.fori_loop(0, n_chunks // group, prepare, 0)

    def scan(i, s):
        where = [(hh, d, pl.ds(pl.multiple_of(chunk * CHUNK, CHUNK), CHUNK), chunk)
                 for hh in range(heads) for d, chunk in ((0, i), (1, n_chunks - 1 - i))]
        gather = lambda ref: jnp.stack([ref[d, rows, hcols(hh)] for hh, d, rows, _ in where])
        a_intra = jnp.stack([ai_s[d, hh, rows, :] for hh, d, rows, _ in where])
        decay = jnp.stack([gl_s[d, hh, chunk][0:1, :] for hh, d, _, chunk in where])
        sb = s.astype(BF16)
        v_new = gather(u_s) - _bmm(gather(w_s), sb)
        vb = v_new.astype(BF16)
        o = _bmm(gather(qd_s), sb) + _bmm(a_intra, vb)
        for idx, (hh, d, rows, _) in enumerate(where):
            u_s[d, rows, hcols(hh)] = o[idx]
        return s * decay + jnp.einsum('nik,niv->nkv', gather(kd_s), vb, preferred_element_type=F32)

    if has_s0:
        init = jnp.stack([s0_ref[0, 0, d, hh] for hh in range(heads) for d in range(2)])
    else:
        init = jnp.zeros((2 * heads, D_HEAD, D_HEAD), F32)
    final = lax.fori_loop(0, n_chunks, scan, init)
    for hh in range(heads):
        cols = hcols(hh)
        sf_ref[0, 0, 0, hh] = final[2 * hh]
        sf_ref[0, 0, 1, hh] = final[2 * hh + 1]
        y_ref[0, :, cols] = (_rms(u_s[0, :, cols] + u_s[1, :, cols], gn_ref[...])
                             * _silu(z_ref[0, hh])).astype(y_ref.dtype)


def _gdn(proj3, ab3, conv_w, a_log, dt_bias, norm_g, layer, state=None, new_state=None):
    b, _, l, _ = proj3.shape
    depth_out, layer_out = (1, 0) if state is not None else (DEPTH, layer)
    aliased = new_state is not None
    lanes = jnp.zeros((2, 2 * N_HEAD), F32).at[:, :N_HEAD].set(1.0)
    a_row = jnp.pad((jnp.exp(a_log.astype(F32))[:, None, :] * lanes.reshape(2, 2, N_HEAD)).reshape(1, -1),
                    ((0, 0), (0, AB_PAD - 4 * N_HEAD)))
    dt_row = jnp.pad((dt_bias.astype(F32)[:, None, :] * lanes.reshape(2, 2, N_HEAD)).reshape(1, -1),
                     ((0, 0), (0, AB_PAD - 4 * N_HEAD)))
    hps = N_HEAD if l <= 512 else 1
    wid = hps * D_HEAD
    n_hb = N_HEAD // hps
    blk = lambda c: pl.BlockSpec((1, hps, l, D_HEAD), lambda i, h, c=c: (i, c * n_hb + h, 0, 0))
    wblk = lambda c: pl.BlockSpec((3, wid), lambda i, h, c=c: (0, c * n_hb + h))
    row = pl.BlockSpec((1, D_HEAD), lambda i, h: (0, 0))
    in_specs = [blk(0), blk(1), blk(2), blk(3),
                pl.BlockSpec((1, l, AB_PAD), lambda i, h: (i, 0, 0)),
                wblk(0), wblk(1), wblk(2), row, row, row]
    args = [proj3, proj3, proj3, proj3, ab3, conv_w, conv_w, conv_w, a_row, dt_row, norm_g]
    if aliased:
        in_specs.insert(0, pl.BlockSpec(memory_space=pl.ANY))
        args.insert(0, new_state)
    if state is not None:
        in_specs.append(pl.BlockSpec((1, 1, 2, hps, D_HEAD, D_HEAD), lambda i, h: (i, layer, 0, h, 0, 0)))
        args.append(state)
    return pl.pallas_call(
        functools.partial(_gdn_kernel, aliased=aliased, has_s0=state is not None, group=min(512, l) // CHUNK),
        grid=(b, n_hb),
        in_specs=in_specs,
        out_specs=[pl.BlockSpec((1, l, wid), lambda i, h: (i, 0, h)),
                   pl.BlockSpec((1, 1, 2, hps, D_HEAD, D_HEAD), lambda i, h: (i, layer_out, 0, h, 0, 0))],
        out_shape=[jax.ShapeDtypeStruct((b, l, BR_W), BF16),
                   jax.ShapeDtypeStruct((b, depth_out, 2, N_HEAD, D_HEAD, D_HEAD), F32)],
        input_output_aliases={0: 1} if aliased else {},
        scratch_shapes=[pltpu.VMEM((l, wid), F32)] * 3
        + [pltpu.VMEM((2, l, wid), F32)] + [pltpu.VMEM((2, l, wid), BF16)] * 3
        + [pltpu.VMEM((2, hps, l, CHUNK), BF16), pltpu.VMEM((2, hps, l // CHUNK, 8, D_HEAD), F32)],
        compiler_params=_cparams("parallel", "parallel"),
        name="gdn",
    )(*args)


def _mod_kernel(c_ref, w_ref, b_ref, o_ref):
    o_ref[...] = _dot_hi(_silu(c_ref[...]), w_ref[...]) + b_ref[...]


def _modulation(cond, w_mod, b_mod, layer, tn=512):
    n = cond.shape[0]
    rows = 8
    out = pl.pallas_call(
        _mod_kernel,
        grid=(3 * D_MODEL // tn,),
        in_specs=[pl.BlockSpec((rows, D_MODEL), lambda j: (0, 0)),
                  pl.BlockSpec((None, D_MODEL, tn), lambda j: (layer, 0, j)),
                  pl.BlockSpec((1, tn), lambda j: (0, j))],
        out_specs=pl.BlockSpec((rows, tn), lambda j: (0, j)),
        out_shape=jax.ShapeDtypeStruct((rows, 3 * D_MODEL), F32),
        compiler_params=_cparams("parallel"),
        name="modulation",
    )(jnp.pad(cond.astype(F32), ((0, rows - n), (0, 0))), w_mod, b_mod.reshape(1, -1))
    return out[:n].reshape(n, 3, D_MODEL)


def _split_w_in(w_in):
    n_a = 4 * BR_W + 4 * N_HEAD
    w_in = w_in.astype(BF16)
    main = jnp.concatenate([w_in[..., :4 * BR_W], w_in[..., n_a:]], axis=-1)
    ab = jnp.pad(w_in[..., 4 * BR_W:n_a], ((0, 0),) * (w_in.ndim - 1) + ((0, AB_PAD - 4 * N_HEAD),))
    return main, ab


def _trunk_layer(x3, cond, p, big, layer, dft, latent, new_outputs=(None, None, None)):
    b, l, _ = x3.shape
    x2 = x3.reshape(b * l, D_MODEL)
    mod = _modulation(cond, big['w_mod'], p['b_mod'], layer)
    rows_per_mod = l if mod.shape[0] == b else b * l
    g_pre = p['g_pre'].reshape(1, D_MODEL)
    proj3, ab = _inproj(x2, mod, g_pre, big['w_main'], big['w_ab'], layer, rows_per_mod, l)
    ab3 = ab.reshape(b, l, AB_PAD)

    lam_init = 0.8 - 0.6 * math.exp(-0.3 * layer)
    lam_p = p['diff_lam'].astype(F32)
    lam = (jnp.exp(jnp.sum(lam_p[0] * lam_p[1])) - jnp.exp(jnp.sum(lam_p[2] * lam_p[3])) + lam_init).reshape(1, 1)
    diff_norm = p['diff_norm'].reshape(1, D_HEAD)
    gdn_args = (proj3, ab3, p['gdn_conv'], p['gdn_a_log'], p['gdn_dt_bias'], p['gdn_norm'].reshape(1, D_HEAD), layer)

    yb = _hyena(proj3, p, dft)
    if latent is None:
        new_state, nat_cache, diff_cache = new_outputs
        ya, new_state = _gdn(*gdn_args, new_state=new_state)
        yc, yd, nat_cache, diff_cache = _ctx_attention(proj3, lam, diff_norm, lam_init, layer, nat_cache, diff_cache)
        extras = (new_state, nat_cache, diff_cache)
    else:
        ya, _ = _gdn(*gdn_args, state=latent['state_gdn'])
        yc = _lat_nat(proj3, latent['cache_nat_kv'], layer, _nat_bias_table(p['nat_rpb']))
        yd = _lat_diff(proj3, latent['cache_diff_kv'], layer, lam, diff_norm, lam_init, latent['rope'])
        extras = None

    ys = [t.reshape(b * l, BR_W) for t in (ya, yb, yc, yd)]
    out = _merge(x2, mod, g_pre, p['g_post'].reshape(1, D_MODEL), ys, big['w_branch'], big['w_merge'],
                 p['b_merge'].reshape(1, -1).astype(F32), big['w_out'], layer, rows_per_mod)
    return out.reshape(b, l, D_MODEL), extras


def kernel(x_prompt, x_sample, state_gdn, cache_nat_kv, cache_diff_kv, c, c_ctx,
           w_mod, b_mod, g_pre, g_post, w_in, gdn_conv, gdn_a_log, gdn_dt_bias, gdn_norm,
           hy_conv, hy_w1, hy_b1, hy_w2, hy_b2, hy_w3, hy_b3, hy_decay, hy_skip,
           nat_rpb, diff_lam, diff_norm, w_branch, w_merge, b_merge, w_out):
    small = {
        'b_mod': b_mod, 'g_pre': g_pre, 'g_post': g_post,
        'gdn_conv': gdn_conv, 'gdn_a_log': gdn_a_log, 'gdn_dt_bias': gdn_dt_bias, 'gdn_norm': gdn_norm,
        'hy_conv': hy_conv, 'hy_w1': hy_w1, 'hy_b1': hy_b1, 'hy_w2': hy_w2, 'hy_b2': hy_b2,
        'hy_w3': hy_w3, 'hy_b3': hy_b3, 'hy_decay': hy_decay, 'hy_skip': hy_skip,
        'nat_rpb': nat_rpb, 'diff_lam': diff_lam, 'diff_norm': diff_norm, 'b_merge': b_merge,
    }
    layers = [{name: arr[i] for name, arr in small.items()} for i in range(DEPTH)]
    w_main, w_ab = _split_w_in(w_in)
    big = {'w_mod': w_mod.astype(F32), 'w_main': w_main, 'w_ab': w_ab, 'w_branch': w_branch.astype(BF16),
           'w_merge': w_merge.astype(BF16), 'w_out': w_out.astype(BF16)}

    y_prompt = x_prompt
    dft_ctx = _dft_matrices(x_prompt.shape[1])
    outputs = (None, None, None)
    for i, p in enumerate(layers):
        y_prompt, outputs = _trunk_layer(y_prompt, c_ctx.reshape(1, D_MODEL), p, big, i, dft_ctx, None, outputs)
    new_state, nat_cache, diff_cache = outputs

    y_sample = x_sample
    dft_lat = _dft_matrices(x_sample.shape[1])
    latent = {'state_gdn': state_gdn, 'cache_nat_kv': cache_nat_kv, 'cache_diff_kv': cache_diff_kv,
              'rope': _rope_tables(x_sample.shape[1])}
    for i, p in enumerate(layers):
        y_sample, _ = _trunk_layer(y_sample, c, p, big, i, dft_lat, latent)

    return (y_prompt, y_sample, new_state, nat_cache, diff_cache)
```
